```python
import math
import jax, jax.numpy as jnp
from jax import lax
import numpy as np

D_MODEL = 1024
BATCH = 8
SEQ = 4096
DEPTH = 1

CONV_CH = D_MODEL // 2
CONV_WIDTH = 31
DA_HEADS = 4
DA_HEAD_DIM = 64
DA_V_DIM = 2 * DA_HEAD_DIM
ATTN_W = DA_HEADS * DA_V_DIM
N_BRANCH = 2
COL_CONV = 2 * CONV_CH
COL_Q = DA_HEADS * 2 * DA_HEAD_DIM
COL_K = DA_HEADS * 2 * DA_HEAD_DIM
COL_V = ATTN_W
COL_GATE = N_BRANCH * D_MODEL
IN_COLS = COL_CONV + COL_Q + COL_K + COL_V + COL_GATE
N_GROUPS = 4
EXPERTS_PER_GROUP = 8
N_EXPERTS = N_GROUPS * EXPERTS_PER_GROUP
TOP_K_IN_GROUP = 2
D_EXPERT = D_MODEL // 2

Q_BLOCK = 128
ROW_BLOCK = 128
EPS = 1e-6

kernel_name = "hybrid_conv_diffattn_hmoe_block"


def rms_norm(x, g):
    xf = x.astype(jnp.float32)
    y = xf * lax.rsqrt(jnp.mean(xf * xf, axis=-1, keepdims=True) + EPS)
    return (y * g.astype(jnp.float32)).astype(x.dtype)


def layer_norm(x, g, b):
    xf = x.astype(jnp.float32)
    mu = jnp.mean(xf, axis=-1, keepdims=True)
    var = jnp.mean(jnp.square(xf - mu), axis=-1, keepdims=True)
    y = (xf - mu) * lax.rsqrt(var + EPS)
    return (y * g.astype(jnp.float32) + b.astype(jnp.float32)).astype(x.dtype)


def conformer_conv(u, dw_w, dw_b, ln_g, ln_b):
    a, gate = jnp.split(u, 2, axis=-1)
    y = a * jax.nn.sigmoid(gate)
    y = lax.conv_general_dilated(
        y, dw_w[:, None, :].astype(y.dtype), window_strides=(1,),
        padding=[(CONV_WIDTH - 1, 0)],
        dimension_numbers=("NWC", "WIO", "NWC"),
        feature_group_count=CONV_CH) + dw_b
    y = layer_norm(y, ln_g, ln_b)
    return jax.nn.silu(y)


def diff_attention(q, k, v, lam, subln_g, lam_init):
    B, S = q.shape[0], q.shape[1]
    nb = S // Q_BLOCK
    scale = DA_HEAD_DIM ** -0.5
    qb = q.reshape(B, nb, Q_BLOCK, DA_HEADS, 2, DA_HEAD_DIM).transpose(1, 0, 2, 3, 4, 5)
    key_pos = jnp.arange(S)

    def block(args):
        qi, bi = args
        s = jnp.einsum("bqhcd,bkhcd->bhcqk", qi, k,
                       preferred_element_type=jnp.float32) * scale
        q_pos = bi * Q_BLOCK + jnp.arange(Q_BLOCK)
        mask = key_pos[None, :] <= q_pos[:, None]
        s = jnp.where(mask, s, -jnp.inf)
        p = jax.nn.softmax(s, axis=-1)
        a = (p[:, :, 0] - lam * p[:, :, 1]).astype(v.dtype)
        return jnp.einsum("bhqk,bkhe->bqhe", a, v)

    o = lax.map(block, (qb, jnp.arange(nb)))
    o = o.transpose(1, 0, 2, 3, 4).reshape(B, S, DA_HEADS, DA_V_DIM)
    o = rms_norm(o, subln_g) * (1.0 - lam_init)
    return o.reshape(B, S, ATTN_W)


def hierarchical_moe(h, w_rg, b_rg, w_re, b_re, w_g, w_u, w_d):
    B, S, D = h.shape
    T = B * S
    hf = h.reshape(T, D)
    g_logits = jnp.matmul(hf, w_rg, preferred_element_type=jnp.float32) + b_rg.astype(jnp.float32)
    g_prob = jax.nn.softmax(g_logits, axis=-1)
    g_sel = jnp.argmax(g_logits, axis=-1)
    g_w = jnp.take_along_axis(g_prob, g_sel[:, None], axis=1)
    e_logits = (jnp.matmul(hf, w_re, preferred_element_type=jnp.float32)
                + b_re.astype(jnp.float32)).reshape(T, N_GROUPS, EXPERTS_PER_GROUP)
    e_logits = jnp.take_along_axis(e_logits, g_sel[:, None, None], axis=1)[:, 0]
    e_prob = jax.nn.softmax(e_logits, axis=-1)
    top_p, top_i = lax.top_k(e_prob, TOP_K_IN_GROUP)
    wts = top_p / jnp.sum(top_p, axis=-1, keepdims=True) * g_w
    eidx = g_sel[:, None] * EXPERTS_PER_GROUP + top_i

    A = T * TOP_K_IN_GROUP
    e_flat = eidx.reshape(A).astype(jnp.int32)
    tok_flat = jnp.repeat(jnp.arange(T, dtype=jnp.int32), TOP_K_IN_GROUP)
    w_flat = wts.reshape(A).astype(h.dtype)
    order = jnp.argsort(e_flat, stable=True)
    e_s, tok_s, w_s = e_flat[order], tok_flat[order], w_flat[order]
    counts = jnp.zeros((N_EXPERTS,), jnp.int32).at[e_flat].add(1)
    padded = (counts + ROW_BLOCK - 1) // ROW_BLOCK * ROW_BLOCK
    start = jnp.cumsum(counts) - counts
    pad_end = jnp.cumsum(padded)
    pstart = pad_end - padded
    dest = pstart[e_s] + jnp.arange(A, dtype=jnp.int32) - start[e_s]
    n_rows = A + N_EXPERTS * ROW_BLOCK
    n_blk = n_rows // ROW_BLOCK
    row_tok = jnp.zeros((n_rows,), jnp.int32).at[dest].set(tok_s)
    row_w = jnp.zeros((n_rows,), h.dtype).at[dest].set(w_s)
    blk_exp = jnp.clip(jnp.searchsorted(pad_end, jnp.arange(n_blk, dtype=jnp.int32) * ROW_BLOCK,
                                        side="right"), 0, N_EXPERTS - 1)

    def expert_block(args):
        toks, wb, e = args
        xb = hf[toks]
        hid = jax.nn.silu(jnp.matmul(xb, w_g[e])) * jnp.matmul(xb, w_u[e])
        return jnp.matmul(hid, w_d[e]) * wb[:, None]

    yb = lax.map(expert_block, (row_tok.reshape(n_blk, ROW_BLOCK),
                                row_w.reshape(n_blk, ROW_BLOCK), blk_exp))
    y = jnp.zeros_like(hf).at[row_tok].add(yb.reshape(n_rows, D))
    return y.reshape(B, S, D)


def setup_inputs(seed: int = 0) -> dict:
    key = jax.random.key(seed)
    ks = jax.random.split(key, 26)
    f32 = jnp.float32

    def nrm(k, shape, scale):
        return jax.random.normal(k, shape, f32) * scale

    L, D, C = DEPTH, D_MODEL, CONV_CH
    return {
        "x": nrm(ks[0], (BATCH, SEQ, D), 1.0),
        "attn_norm_g": 1.0 + nrm(ks[1], (L, D), 0.05),
        "w_in": nrm(ks[2], (L, D, IN_COLS), D ** -0.5),
        "conv_dw_w": nrm(ks[3], (L, CONV_WIDTH, C), CONV_WIDTH ** -0.5),
        "conv_dw_b": nrm(ks[4], (L, C), 0.02),
        "conv_ln_g": 1.0 + nrm(ks[5], (L, C), 0.05),
        "conv_ln_b": nrm(ks[6], (L, C), 0.02),
        "w_conv_out": nrm(ks[7], (L, C, D), C ** -0.5),
        "q_norm_g": 1.0 + nrm(ks[8], (L, DA_HEAD_DIM), 0.05),
        "k_norm_g": 1.0 + nrm(ks[9], (L, DA_HEAD_DIM), 0.05),
        "lambda_q1": nrm(ks[10], (L, DA_HEAD_DIM), 0.1),
        "lambda_k1": nrm(ks[11], (L, DA_HEAD_DIM), 0.1),
        "lambda_q2": nrm(ks[12], (L, DA_HEAD_DIM), 0.1),
        "lambda_k2": nrm(ks[13], (L, DA_HEAD_DIM), 0.1),
        "subln_g": 1.0 + nrm(ks[14], (L, DA_V_DIM), 0.05),
        "w_attn_out": nrm(ks[15], (L, ATTN_W, D), ATTN_W ** -0.5),
        "w_out": nrm(ks[16], (L, D, D), D ** -0.5),
        "ffn_norm_g": 1.0 + nrm(ks[17], (L, D), 0.05),
        "w_router_group": nrm(ks[18], (L, D, N_GROUPS), D ** -0.5),
        "b_router_group": nrm(ks[19], (L, N_GROUPS), 0.01),
        "w_router_expert": nrm(ks[20], (L, D, N_EXPERTS), D ** -0.5),
        "b_router_expert": nrm(ks[21], (L, N_EXPERTS), 0.01),
        "w_gate_e": nrm(ks[22], (L, N_EXPERTS, D, D_EXPERT), D ** -0.5),
        "w_up_e": nrm(ks[23], (L, N_EXPERTS, D, D_EXPERT), D ** -0.5),
        "w_down_e": nrm(ks[24], (L, N_EXPERTS, D_EXPERT, D), D_EXPERT ** -0.5),
    }


def reference(x, attn_norm_g, w_in, conv_dw_w, conv_dw_b, conv_ln_g, conv_ln_b, w_conv_out,
              q_norm_g, k_norm_g, lambda_q1, lambda_k1, lambda_q2, lambda_k2, subln_g,
              w_attn_out, w_out, ffn_norm_g, w_router_group, b_router_group,
              w_router_expert, b_router_expert, w_gate_e, w_up_e, w_down_e):
    B, S, D = x.shape
    for l in range(DEPTH):
        lam_init = 0.8 - 0.6 * math.exp(-0.3 * l)
        h = rms_norm(x, attn_norm_g[l])
        proj = jnp.matmul(h, w_in[l])
        o0 = COL_CONV
        o1 = o0 + COL_Q
        o2 = o1 + COL_K
        o3 = o2 + COL_V
        u_conv = proj[..., :o0]
        q = proj[..., o0:o1].reshape(B, S, DA_HEADS, 2, DA_HEAD_DIM)
        k = proj[..., o1:o2].reshape(B, S, DA_HEADS, 2, DA_HEAD_DIM)
        v = proj[..., o2:o3].reshape(B, S, DA_HEADS, DA_V_DIM)
        gates = jax.nn.sigmoid(proj[..., o3:]).reshape(B, S, N_BRANCH, D)

        conv_out = conformer_conv(u_conv, conv_dw_w[l], conv_dw_b[l], conv_ln_g[l], conv_ln_b[l])

        q = rms_norm(q, q_norm_g[l])
        k = rms_norm(k, k_norm_g[l])
        lam = (jnp.exp(jnp.sum(lambda_q1[l].astype(jnp.float32) * lambda_k1[l].astype(jnp.float32)))
               - jnp.exp(jnp.sum(lambda_q2[l].astype(jnp.float32) * lambda_k2[l].astype(jnp.float32)))
               + lam_init)
        attn_out = diff_attention(q, k, v, lam, subln_g[l], lam_init)

        merged = (gates[:, :, 0] * jnp.matmul(conv_out, w_conv_out[l])
                  + gates[:, :, 1] * jnp.matmul(attn_out, w_attn_out[l]))
        x = x + jnp.matmul(merged, w_out[l])
        h2 = rms_norm(x, ffn_norm_g[l])
        x = x + hierarchical_moe(h2, w_router_group[l], b_router_group[l], w_router_expert[l],
                                 b_router_expert[l], w_gate_e[l], w_up_e[l], w_down_e[l])
    return x
```

```python
import functools
import math

import jax
import jax.numpy as jnp
from jax import lax
from jax.experimental import pallas as pl
from jax.experimental.pallas import tpu as pltpu

F32 = jnp.float32
BF16 = jnp.bfloat16

D_MODEL = 1024
CONV_CH = 512
CONV_WIDTH = 31
DA_HEADS = 4
DA_HEAD_DIM = 64
DA_V_DIM = 128
ATTN_W = 512
N_GROUPS = 4
EXPERTS_PER_GROUP = 8
N_EXPERTS = 32
D_EXPERT = 512
EPS = 1e-6
LAM_INIT = 0.8 - 0.6 * math.exp(-0.3 * 0)

COL_GLU = 2 * CONV_CH
COL_Q = 512
COL_K = 512
COL_V = 512
COL_GATE = 2 * D_MODEL
OFF_Q = COL_GLU
OFF_K = OFF_Q + COL_Q
OFF_V = OFF_K + COL_K
OFF_GATE = OFF_V + COL_V
IN_COLS = OFF_GATE + COL_GATE

ROUTER_LANES = 128
HALO = 32

TM_PROJ = 512
TS_CONV = 512
CONV_CHUNK = 64
TQ = 256
RB = 256
TM_COMB = 512

VMEM_LIMIT = 48 * 1024 * 1024


def _sigmoid(x):
    return 1.0 / (1.0 + jnp.exp(-x))


def _const_spec(shape):
    nd = len(shape)
    return pl.BlockSpec(shape, lambda *_: (0,) * nd, pipeline_mode=pl.Buffered(1))


def _group_mean_sq(t, bd_ref):
    sq = t * t
    hi = sq.astype(BF16)
    lo = (sq - hi.astype(F32)).astype(BF16)
    return (jnp.dot(hi, bd_ref[...], preferred_element_type=F32)
            + jnp.dot(lo, bd_ref[...], preferred_element_type=F32))


def _inproj_kernel(x_ref, g_ref, w_ref, qg_ref, kg_ref, bd_ref,
                   glu_ref, q_ref, k_ref, v_ref, gate_ref):
    x = x_ref[...]
    ms = jnp.mean(x * x, axis=-1, keepdims=True)
    h = (x * lax.rsqrt(ms + EPS) * g_ref[...]).astype(BF16)

    def proj(lo, width):
        return jnp.dot(h, w_ref[:, lo:lo + width], preferred_element_type=F32)

    a = proj(0, CONV_CH)
    gt = proj(CONV_CH, CONV_CH)
    glu_ref[...] = (a * _sigmoid(gt)).astype(BF16)

    q = proj(OFF_Q, COL_Q)
    qn = q * lax.rsqrt(_group_mean_sq(q, bd_ref) + EPS) * qg_ref[...]
    q_ref[...] = (qn * (DA_HEAD_DIM ** -0.5)).astype(BF16)

    k = proj(OFF_K, COL_K)
    kn = k * lax.rsqrt(_group_mean_sq(k, bd_ref) + EPS) * kg_ref[...]
    k_ref[...] = kn.astype(BF16)

    v_ref[...] = proj(OFF_V, COL_V).astype(BF16)

    for c in range(COL_GATE // 512):
        gate_ref[:, c * 512:(c + 1) * 512] = _sigmoid(proj(OFF_GATE + c * 512, 512)).astype(BF16)


def _in_projection(x2, g, w_bf, qg, kg, bd):
    T = x2.shape[0]
    tm = TM_PROJ
    row = lambda i: (i, 0)
    out_shape = (
        jax.ShapeDtypeStruct((T, CONV_CH), BF16),
        jax.ShapeDtypeStruct((T, COL_Q), BF16),
        jax.ShapeDtypeStruct((T, COL_K), BF16),
        jax.ShapeDtypeStruct((T, COL_V), BF16),
        jax.ShapeDtypeStruct((T, COL_GATE), BF16),
    )
    return pl.pallas_call(
        _inproj_kernel,
        grid=(T // tm,),
        in_specs=[
            pl.BlockSpec((tm, D_MODEL), row),
            _const_spec((1, D_MODEL)),
            _const_spec((D_MODEL, IN_COLS)),
            _const_spec((1, COL_Q)),
            _const_spec((1, COL_K)),
            _const_spec((COL_Q, COL_Q)),
        ],
        out_specs=(
            pl.BlockSpec((tm, CONV_CH), row),
            pl.BlockSpec((tm, COL_Q), row),
            pl.BlockSpec((tm, COL_K), row),
            pl.BlockSpec((tm, COL_V), row),
            pl.BlockSpec((tm, COL_GATE), row),
        ),
        out_shape=out_shape,
        compiler_params=pltpu.CompilerParams(
            dimension_semantics=("parallel",), vmem_limit_bytes=VMEM_LIMIT),
        name="in_projection",
    )(x2, g, w_bf, qg, kg, bd)


def _conv_kernel(cur_ref, halo_ref, w_ref, b_ref, lng_ref, lnb_ref, out_ref, buf_ref):
    i = pl.program_id(1)
    ts = cur_ref.shape[1]
    halo = halo_ref[0].astype(F32)
    buf_ref[0:HALO, :] = jnp.where(i > 0, halo, 0.0)
    buf_ref[HALO:HALO + ts, :] = cur_ref[0].astype(F32)

    first = HALO - (CONV_WIDTH - 1)
    for c in range(ts // CONV_CHUNK):
        r0 = c * CONV_CHUNK
        acc = jnp.broadcast_to(b_ref[...], (CONV_CHUNK, CONV_CH))
        for j in range(CONV_WIDTH):
            acc = acc + w_ref[j:j + 1, :] * buf_ref[r0 + first + j:r0 + first + j + CONV_CHUNK, :]
        mu = jnp.mean(acc, axis=-1, keepdims=True)
        d = acc - mu
        var = jnp.mean(d * d, axis=-1, keepdims=True)
        y = d * lax.rsqrt(var + EPS) * lng_ref[...] + lnb_ref[...]
        out_ref[0, r0:r0 + CONV_CHUNK, :] = (y * _sigmoid(y)).astype(BF16)


def _conv_branch(glu3, dw_w, dw_b, ln_g, ln_b):
    B, S, C = glu3.shape
    ts = TS_CONV
    per = ts // HALO
    return pl.pallas_call(
        _conv_kernel,
        grid=(B, S // ts),
        in_specs=[
            pl.BlockSpec((1, ts, C), lambda b, i: (b, i, 0)),
            pl.BlockSpec((1, HALO, C), lambda b, i: (b, jnp.maximum(i * per - 1, 0), 0)),
            _const_spec((CONV_WIDTH, C)),
            _const_spec((1, C)),
            _const_spec((1, C)),
            _const_spec((1, C)),
        ],
        out_specs=pl.BlockSpec((1, ts, C), lambda b, i: (b, i, 0)),
        out_shape=jax.ShapeDtypeStruct((B, S, C), BF16),
        scratch_shapes=[pltpu.VMEM((HALO + ts, C), F32)],
        compiler_params=pltpu.CompilerParams(
            dimension_semantics=("parallel", "parallel"), vmem_limit_bytes=VMEM_LIMIT),
        name="conv_branch",
    )(glu3, glu3, dw_w, dw_b, ln_g, ln_b)


def _attn_kernel(q_ref, k_ref, v_ref, lam_ref, sg_ref, o_ref, qs_ref, m_ref, l_ref, acc_ref):
    qi = pl.program_id(2)
    tq = q_ref.shape[1]
    q = q_ref[0]
    lane = lax.broadcasted_iota(jnp.int32, q.shape, 1)
    zero = jnp.zeros_like(q)
    qs_ref[0:tq, :] = jnp.where(lane < DA_HEAD_DIM, q, zero)
    qs_ref[tq:2 * tq, :] = jnp.where(lane >= DA_HEAD_DIM, q, zero)
    m_ref[...] = jnp.full(m_ref.shape, -jnp.inf, F32)
    l_ref[...] = jnp.zeros(l_ref.shape, F32)
    acc_ref[...] = jnp.zeros(acc_ref.shape, F32)

    def step(j, masked):
        start = pl.multiple_of(j * tq, tq)
        kj = k_ref[0, pl.ds(start, tq), :]
        vj = v_ref[0, pl.ds(start, tq), :]
        s = lax.dot_general(qs_ref[...], kj, (((1,), (1,)), ((), ())),
                            preferred_element_type=F32)
        if masked:
            r = lax.broadcasted_iota(jnp.int32, s.shape, 0)
            r = jnp.where(r >= tq, r - tq, r)
            c = lax.broadcasted_iota(jnp.int32, s.shape, 1)
            s = jnp.where(c <= r, s, -jnp.inf)
        m_old = m_ref[...]
        m_new = jnp.maximum(m_old, jnp.max(s, axis=-1, keepdims=True))
        alpha = jnp.exp(m_old - m_new)
        p = jnp.exp(s - m_new)
        l_ref[...] = alpha * l_ref[...] + jnp.sum(p, axis=-1, keepdims=True)
        acc_ref[...] = alpha * acc_ref[...] + jnp.dot(p.astype(BF16), vj,
                                                      preferred_element_type=F32)
        m_ref[...] = m_new

    def body(j, carry):
        step(j, False)
        return carry

    lax.fori_loop(0, qi, body, 0)
    step(qi, True)

    lam = (jnp.exp(jnp.sum(lam_ref[0:1, :] * lam_ref[1:2, :], axis=-1, keepdims=True))
           - jnp.exp(jnp.sum(lam_ref[2:3, :] * lam_ref[3:4, :], axis=-1, keepdims=True))
           + LAM_INIT)
    o1 = acc_ref[0:tq, :] / l_ref[0:tq, :]
    o2 = acc_ref[tq:2 * tq, :] / l_ref[tq:2 * tq, :]
    o = o1 - lam * o2
    ms = jnp.mean(o * o, axis=-1, keepdims=True)
    on = o * lax.rsqrt(ms + EPS) * sg_ref[...]
    o_ref[0] = (on * (1.0 - LAM_INIT)).astype(BF16)


def _diff_attention(q3, k3, v3, lam_vecs, subln_g):
    B, S, _ = q3.shape
    tq = TQ
    return pl.pallas_call(
        _attn_kernel,
        grid=(B, DA_HEADS, S // tq),
        in_specs=[
            pl.BlockSpec((1, tq, DA_V_DIM), lambda b, h, i: (b, i, h)),
            pl.BlockSpec((1, S, DA_V_DIM), lambda b, h, i: (b, 0, h)),
            pl.BlockSpec((1, S, DA_V_DIM), lambda b, h, i: (b, 0, h)),
            _const_spec((4, DA_HEAD_DIM)),
            _const_spec((1, DA_V_DIM)),
        ],
        out_specs=pl.BlockSpec((1, tq, DA_V_DIM), lambda b, h, i: (b, i, h)),
        out_shape=jax.ShapeDtypeStruct((B, S, ATTN_W), BF16),
        scratch_shapes=[
            pltpu.VMEM((2 * tq, DA_V_DIM), BF16),
            pltpu.VMEM((2 * tq, 1), F32),
            pltpu.VMEM((2 * tq, 1), F32),
            pltpu.VMEM((2 * tq, DA_V_DIM), F32),
        ],
        compiler_params=pltpu.CompilerParams(
            dimension_semantics=("parallel", "parallel", "parallel"),
            vmem_limit_bytes=VMEM_LIMIT),
        name="diff_attention",
    )(q3, k3, v3, lam_vecs, subln_g)


def _merge_kernel(conv_ref, attn_ref, gate_ref, x_ref, wc_ref, wa_ref, wo_ref, fg_ref,
                  wr_ref, br_ref, x1_ref, h2_ref, logit_ref):
    c = jnp.dot(conv_ref[...], wc_ref[...], preferred_element_type=F32)
    a = jnp.dot(attn_ref[...], wa_ref[...], preferred_element_type=F32)
    g0 = gate_ref[:, 0:D_MODEL].astype(F32)
    g1 = gate_ref[:, D_MODEL:2 * D_MODEL].astype(F32)
    merged = (g0 * c + g1 * a).astype(BF16)
    x1 = x_ref[...] + jnp.dot(merged, wo_ref[...], preferred_element_type=F32)
    x1_ref[...] = x1
    ms = jnp.mean(x1 * x1, axis=-1, keepdims=True)
    h2 = x1 * lax.rsqrt(ms + EPS) * fg_ref[...]
    h2_ref[...] = h2.astype(BF16)
    logit_ref[...] = jnp.dot(h2, wr_ref[...], preferred_element_type=F32,
                             precision=lax.Precision.HIGHEST) + br_ref[...]


def _merge(conv_act, attn_o, gates, x2, wc, wa, wo, fg, wr, br):
    T = x2.shape[0]
    tm = TM_PROJ
    row = lambda i: (i, 0)
    return pl.pallas_call(
        _merge_kernel,
        grid=(T // tm,),
        in_specs=[
            pl.BlockSpec((tm, CONV_CH), row),
            pl.BlockSpec((tm, ATTN_W), row),
            pl.BlockSpec((tm, COL_GATE), row),
            pl.BlockSpec((tm, D_MODEL), row),
            _const_spec((CONV_CH, D_MODEL)),
            _const_spec((ATTN_W, D_MODEL)),
            _const_spec((D_MODEL, D_MODEL)),
            _const_spec((1, D_MODEL)),
            _const_spec((D_MODEL, ROUTER_LANES)),
            _const_spec((1, ROUTER_LANES)),
        ],
        out_specs=(
            pl.BlockSpec((tm, D_MODEL), row),
            pl.BlockSpec((tm, D_MODEL), row),
            pl.BlockSpec((tm, ROUTER_LANES), row),
        ),
        out_shape=(
            jax.ShapeDtypeStruct((T, D_MODEL), F32),
            jax.ShapeDtypeStruct((T, D_MODEL), BF16),
            jax.ShapeDtypeStruct((T, ROUTER_LANES), F32),
        ),
        compiler_params=pltpu.CompilerParams(
            dimension_semantics=("parallel",), vmem_limit_bytes=VMEM_LIMIT),
        name="merge_out_router",
    )(conv_act, attn_o, gates, x2, wc, wa, wo, fg, wr, br)


def _expert_kernel(blk_exp_ref, n_used_ref, xs_ref, wg_ref, wu_ref, wd_ref, y_ref):
    b = pl.program_id(0)

    @pl.when(b < n_used_ref[0])
    def _():
        xb = xs_ref[...]
        g = jnp.dot(xb, wg_ref[0], preferred_element_type=F32)
        u = jnp.dot(xb, wu_ref[0], preferred_element_type=F32)
        hid = (g * _sigmoid(g) * u).astype(BF16)
        y_ref[...] = jnp.dot(hid, wd_ref[0], preferred_element_type=F32).astype(BF16)

    @pl.when(b >= n_used_ref[0])
    def _():
        y_ref[...] = jnp.zeros(y_ref.shape, BF16)


def _experts(blk_exp, n_used, xs, wg, wu, wd):
    n_rows = xs.shape[0]
    n_blk = n_rows // RB
    grid_spec = pltpu.PrefetchScalarGridSpec(
        num_scalar_prefetch=2,
        grid=(n_blk,),
        in_specs=[
            pl.BlockSpec((RB, D_MODEL), lambda b, be, nu: (b, 0)),
            pl.BlockSpec((1, D_MODEL, D_EXPERT), lambda b, be, nu: (be[b], 0, 0)),
            pl.BlockSpec((1, D_MODEL, D_EXPERT), lambda b, be, nu: (be[b], 0, 0)),
            pl.BlockSpec((1, D_EXPERT, D_MODEL), lambda b, be, nu: (be[b], 0, 0)),
        ],
        out_specs=pl.BlockSpec((RB, D_MODEL), lambda b, be, nu: (b, 0)),
    )
    return pl.pallas_call(
        _expert_kernel,
        grid_spec=grid_spec,
        out_shape=jax.ShapeDtypeStruct((n_rows, D_MODEL), BF16),
        compiler_params=pltpu.CompilerParams(
            dimension_semantics=("arbitrary",), vmem_limit_bytes=VMEM_LIMIT),
        name="expert_mlp",
    )(blk_exp, n_used, xs, wg, wu, wd)


def _combine_kernel(x1_ref, y_ref, w_ref, o_ref):
    y0 = y_ref[:, 0:D_MODEL].astype(F32)
    y1 = y_ref[:, D_MODEL:2 * D_MODEL].astype(F32)
    o_ref[...] = x1_ref[...] + (y0 * w_ref[:, 0:1] + y1 * w_ref[:, 1:2])


def _combine(x1, y2, wts):
    T = x1.shape[0]
    tm = TM_COMB
    row = lambda i: (i, 0)
    return pl.pallas_call(
        _combine_kernel,
        grid=(T // tm,),
        in_specs=[
            pl.BlockSpec((tm, D_MODEL), row),
            pl.BlockSpec((tm, 2 * D_MODEL), row),
            pl.BlockSpec((tm, 2), row),
        ],
        out_specs=pl.BlockSpec((tm, D_MODEL), row),
        out_shape=jax.ShapeDtypeStruct((T, D_MODEL), F32),
        compiler_params=pltpu.CompilerParams(
            dimension_semantics=("parallel",), vmem_limit_bytes=VMEM_LIMIT),
        name="moe_combine",
    )(x1, y2, wts)


def _route(logits):
    T = logits.shape[0]
    g_logits = logits[:, :N_GROUPS]
    e_logits = logits[:, N_GROUPS:N_GROUPS + N_EXPERTS].reshape(T, N_GROUPS, EXPERTS_PER_GROUP)
    g_prob = jax.nn.softmax(g_logits, axis=-1)
    g_sel = jnp.argmax(g_logits, axis=-1)
    g_w = jnp.take_along_axis(g_prob, g_sel[:, None], axis=1)
    e_sel = jnp.take_along_axis(e_logits, g_sel[:, None, None], axis=1)[:, 0]
    e_prob = jax.nn.softmax(e_sel, axis=-1)
    top_p, top_i = lax.top_k(e_prob, 2)
    wts = top_p / jnp.sum(top_p, axis=-1, keepdims=True) * g_w
    eidx = (g_sel[:, None] * EXPERTS_PER_GROUP + top_i).astype(jnp.int32)
    return eidx, wts


def _dispatch_plan(eidx):
    T = eidx.shape[0]
    A = 2 * T
    e_flat = eidx.reshape(A)
    onehot = (e_flat[:, None] == jnp.arange(N_EXPERTS, dtype=jnp.int32)[None, :]).astype(jnp.int32)
    csum = jnp.cumsum(onehot, axis=0)
    rank = jnp.take_along_axis(csum, e_flat[:, None], axis=1)[:, 0] - 1
    counts = csum[-1]
    padded = (counts + RB - 1) // RB * RB
    pad_end = jnp.cumsum(padded)
    pstart = pad_end - padded
    dest = pstart[e_flat] + rank
    n_blk = (A + N_EXPERTS * RB) // RB
    blk_exp = jnp.clip(jnp.searchsorted(pad_end, jnp.arange(n_blk, dtype=jnp.int32) * RB,
                                        side="right"), 0, N_EXPERTS - 1).astype(jnp.int32)
    n_used = (pad_end[-1] // RB).astype(jnp.int32).reshape(1)
    return dest.astype(jnp.int32), blk_exp, n_used


def kernel(x, attn_norm_g, w_in, conv_dw_w, conv_dw_b, conv_ln_g, conv_ln_b, w_conv_out,
           q_norm_g, k_norm_g, lambda_q1, lambda_k1, lambda_q2, lambda_k2, subln_g,
           w_attn_out, w_out, ffn_norm_g, w_router_group, b_router_group,
           w_router_expert, b_router_expert, w_gate_e, w_up_e, w_down_e):
    B, S, D = x.shape
    T = B * S
    l = 0
    x2 = x.reshape(T, D)

    reps = COL_Q // DA_HEAD_DIM
    qg = jnp.tile(q_norm_g[l], reps).reshape(1, COL_Q)
    kg = jnp.tile(k_norm_g[l], reps).reshape(1, COL_K)
    grp = jnp.arange(COL_Q, dtype=jnp.int32) // DA_HEAD_DIM
    bd = jnp.where(grp[:, None] == grp[None, :], 1.0 / DA_HEAD_DIM, 0.0).astype(BF16)

    glu, q, k, v, gates = _in_projection(
        x2, attn_norm_g[l].reshape(1, D), w_in[l].astype(BF16), qg, kg, bd)

    conv_act = _conv_branch(glu.reshape(B, S, CONV_CH), conv_dw_w[l],
                            conv_dw_b[l].reshape(1, CONV_CH), conv_ln_g[l].reshape(1, CONV_CH),
                            conv_ln_b[l].reshape(1, CONV_CH))

    lam_vecs = jnp.stack([lambda_q1[l], lambda_k1[l], lambda_q2[l], lambda_k2[l]]).astype(F32)
    attn_o = _diff_attention(q.reshape(B, S, COL_Q), k.reshape(B, S, COL_K),
                             v.reshape(B, S, COL_V), lam_vecs, subln_g[l].reshape(1, DA_V_DIM))

    wr = jnp.zeros((D, ROUTER_LANES), F32)
    wr = wr.at[:, :N_GROUPS].set(w_router_group[l]).at[:, N_GROUPS:N_GROUPS + N_EXPERTS].set(
        w_router_expert[l])
    br = jnp.zeros((1, ROUTER_LANES), F32)
    br = br.at[0, :N_GROUPS].set(b_router_group[l]).at[0, N_GROUPS:N_GROUPS + N_EXPERTS].set(
        b_router_expert[l])

    x1, h2, logits = _merge(conv_act.reshape(T, CONV_CH), attn_o.reshape(T, ATTN_W), gates, x2,
                            w_conv_out[l].astype(BF16), w_attn_out[l].astype(BF16),
                            w_out[l].astype(BF16), ffn_norm_g[l].reshape(1, D), wr, br)

    eidx, wts = _route(logits)
    dest, blk_exp, n_used = _dispatch_plan(eidx)
    n_rows = 2 * T + N_EXPERTS * RB
    tok = jnp.arange(2 * T, dtype=jnp.int32) // 2
    xs = jnp.zeros((n_rows, D), BF16).at[dest].set(h2[tok])

    yb = _experts(blk_exp, n_used, xs, w_gate_e[l].astype(BF16), w_up_e[l].astype(BF16),
                  w_down_e[l].astype(BF16))

    y2 = yb[dest].reshape(T, 2 * D)
    out = _combine(x1, y2, wts.astype(F32))
    return out.reshape(B, S, D)
```

```python
import functools
import math

import jax
import jax.numpy as jnp
from jax import lax
from jax.experimental import pallas as pl
from jax.experimental.pallas import tpu as pltpu

F32 = jnp.float32
BF16 = jnp.bfloat16

D_MODEL = 1024
CONV_CH = 512
CONV_WIDTH = 31
DA_HEADS = 4
DA_HEAD_DIM = 64
DA_V_DIM = 128
ATTN_W = 512
N_GROUPS = 4
EXPERTS_PER_GROUP = 8
N_EXPERTS = 32
D_EXPERT = 512
EPS = 1e-6
LAM_INIT = 0.8 - 0.6 * math.exp(-0.3 * 0)

COL_GLU = 2 * CONV_CH
COL_Q = 512
COL_K = 512
COL_V = 512
COL_GATE = 2 * D_MODEL
OFF_Q = COL_GLU
OFF_K = OFF_Q + COL_Q
OFF_V = OFF_K + COL_K
OFF_GATE = OFF_V + COL_V
IN_COLS = OFF_GATE + COL_GATE

ROUTER_LANES = 128
HALO = 32

TM_PROJ = 512
TS_CONV = 512
CONV_CHUNK = 64
ATT_TQ = 512
ATT_TK = 512
ATT_RC = 256
LANES = 128
RB = 256
TM_COMB = 512

VMEM_LIMIT = 48 * 1024 * 1024


def _sigmoid(x):
    return 1.0 / (1.0 + jnp.exp(-x))


def _const_spec(shape):
    nd = len(shape)
    return pl.BlockSpec(shape, lambda *_: (0,) * nd, pipeline_mode=pl.Buffered(1))


def _group_mean_sq(t, bd_ref):
    sq = t * t
    hi = sq.astype(BF16)
    lo = (sq - hi.astype(F32)).astype(BF16)
    return (jnp.dot(hi, bd_ref[...], preferred_element_type=F32)
            + jnp.dot(lo, bd_ref[...], preferred_element_type=F32))


def _inproj_kernel(x_ref, g_ref, w_ref, qg_ref, kg_ref, bd_ref,
                   glu_ref, q_ref, k_ref, v_ref, gate_ref):
    x = x_ref[...]
    ms = jnp.mean(x * x, axis=-1, keepdims=True)
    h = (x * lax.rsqrt(ms + EPS) * g_ref[...]).astype(BF16)

    def proj(lo, width):
        return jnp.dot(h, w_ref[:, lo:lo + width], preferred_element_type=F32)

    a = proj(0, CONV_CH)
    gt = proj(CONV_CH, CONV_CH)
    glu_ref[...] = (a * _sigmoid(gt)).astype(BF16)

    q = proj(OFF_Q, COL_Q)
    qn = q * lax.rsqrt(_group_mean_sq(q, bd_ref) + EPS) * qg_ref[...]
    q_ref[...] = (qn * (DA_HEAD_DIM ** -0.5)).astype(BF16)

    k = proj(OFF_K, COL_K)
    kn = k * lax.rsqrt(_group_mean_sq(k, bd_ref) + EPS) * kg_ref[...]
    k_ref[...] = kn.astype(BF16)

    v_ref[...] = proj(OFF_V, COL_V).astype(BF16)

    for c in range(COL_GATE // 512):
        gate_ref[:, c * 512:(c + 1) * 512] = _sigmoid(proj(OFF_GATE + c * 512, 512)).astype(BF16)


def _in_projection(x2, g, w_bf, qg, kg, bd):
    T = x2.shape[0]
    tm = TM_PROJ
    row = lambda i: (i, 0)
    out_shape = (
        jax.ShapeDtypeStruct((T, CONV_CH), BF16),
        jax.ShapeDtypeStruct((T, COL_Q), BF16),
        jax.ShapeDtypeStruct((T, COL_K), BF16),
        jax.ShapeDtypeStruct((T, COL_V), BF16),
        jax.ShapeDtypeStruct((T, COL_GATE), BF16),
    )
    return pl.pallas_call(
        _inproj_kernel,
        grid=(T // tm,),
        in_specs=[
            pl.BlockSpec((tm, D_MODEL), row),
            _const_spec((1, D_MODEL)),
            _const_spec((D_MODEL, IN_COLS)),
            _const_spec((1, COL_Q)),
            _const_spec((1, COL_K)),
            _const_spec((COL_Q, COL_Q)),
        ],
        out_specs=(
            pl.BlockSpec((tm, CONV_CH), row),
            pl.BlockSpec((tm, COL_Q), row),
            pl.BlockSpec((tm, COL_K), row),
            pl.BlockSpec((tm, COL_V), row),
            pl.BlockSpec((tm, COL_GATE), row),
        ),
        out_shape=out_shape,
        compiler_params=pltpu.CompilerParams(
            dimension_semantics=("parallel",), vmem_limit_bytes=VMEM_LIMIT),
        name="in_projection",
    )(x2, g, w_bf, qg, kg, bd)


def _conv_kernel(cur_ref, halo_ref, w_ref, b_ref, lng_ref, lnb_ref, out_ref, buf_ref):
    i = pl.program_id(1)
    ts = cur_ref.shape[1]
    halo = halo_ref[0].astype(F32)
    buf_ref[0:HALO, :] = jnp.where(i > 0, halo, 0.0)
    buf_ref[HALO:HALO + ts, :] = cur_ref[0].astype(F32)

    first = HALO - (CONV_WIDTH - 1)
    for c in range(ts // CONV_CHUNK):
        r0 = c * CONV_CHUNK
        acc = jnp.broadcast_to(b_ref[...], (CONV_CHUNK, CONV_CH))
        for j in range(CONV_WIDTH):
            acc = acc + w_ref[j:j + 1, :] * buf_ref[r0 + first + j:r0 + first + j + CONV_CHUNK, :]
        mu = jnp.mean(acc, axis=-1, keepdims=True)
        d = acc - mu
        var = jnp.mean(d * d, axis=-1, keepdims=True)
        y = d * lax.rsqrt(var + EPS) * lng_ref[...] + lnb_ref[...]
        out_ref[0, r0:r0 + CONV_CHUNK, :] = (y * _sigmoid(y)).astype(BF16)


def _conv_branch(glu3, dw_w, dw_b, ln_g, ln_b):
    B, S, C = glu3.shape
    ts = TS_CONV
    per = ts // HALO
    return pl.pallas_call(
        _conv_kernel,
        grid=(B, S // ts),
        in_specs=[
            pl.BlockSpec((1, ts, C), lambda b, i: (b, i, 0)),
            pl.BlockSpec((1, HALO, C), lambda b, i: (b, jnp.maximum(i * per - 1, 0), 0)),
            _const_spec((CONV_WIDTH, C)),
            _const_spec((1, C)),
            _const_spec((1, C)),
            _const_spec((1, C)),
        ],
        out_specs=pl.BlockSpec((1, ts, C), lambda b, i: (b, i, 0)),
        out_shape=jax.ShapeDtypeStruct((B, S, C), BF16),
        scratch_shapes=[pltpu.VMEM((HALO + ts, C), F32)],
        compiler_params=pltpu.CompilerParams(
            dimension_semantics=("parallel", "parallel"), vmem_limit_bytes=VMEM_LIMIT),
        name="conv_branch",
    )(glu3, glu3, dw_w, dw_b, ln_g, ln_b)


def _attn_kernel(q_ref, k_ref, v_ref, lam_ref, sg_ref, o_ref, qs_ref, m_ref, acc_ref, s_ref, *, tq, tk):
    qi = pl.program_id(2)
    q = q_ref[0]
    lane = lax.broadcasted_iota(jnp.int32, q.shape, 1)
    zero = jnp.zeros_like(q)
    qs_ref[0:tq, :] = jnp.where(lane < DA_HEAD_DIM, q, zero)
    qs_ref[tq:2 * tq, :] = jnp.where(lane >= DA_HEAD_DIM, q, zero)
    m_ref[...] = jnp.full(m_ref.shape, -jnp.inf, F32)
    acc_ref[...] = jnp.zeros(acc_ref.shape, F32)
    ones = jnp.ones((tk, LANES), BF16)
    rc = ATT_RC
    chunks = [slice(r0, r0 + rc) for r0 in range(0, 2 * tq, rc)]

    def scores(j, rows):
        kj = k_ref[0, pl.ds(pl.multiple_of(j * tk, tk), tk), :]
        return lax.dot_general(qs_ref[rows, :], kj, (((1,), (1,)), ((), ())),
                               preferred_element_type=F32)

    def step(j, masked, prefetch):
        vj = jnp.concatenate([v_ref[0, pl.ds(pl.multiple_of(j * tk, tk), tk), :], ones], axis=1)
        for rows in chunks:
            r_lo = rows.start % tq
            kw = min(tk, -(-(r_lo + rc) // (2 * LANES)) * (2 * LANES)) if masked else tk
            s = s_ref[rows, 0:kw]
            if prefetch:
                s_ref[rows, :] = scores(j + 1, rows)
            if masked:
                r = lax.broadcasted_iota(jnp.int32, s.shape, 0) + r_lo
                c = lax.broadcasted_iota(jnp.int32, s.shape, 1)
                s = jnp.where(c <= r, s, -jnp.inf)
            m_old = m_ref[rows, :]
            m_new = jnp.maximum(m_old, jnp.max(s, axis=-1, keepdims=True))
            alpha = jnp.exp(m_old - m_new)
            p = jnp.exp(s - jnp.tile(m_new, (1, kw // LANES)))
            pv = jnp.dot(p.astype(BF16), vj[0:kw, :], preferred_element_type=F32)
            acc_ref[rows, :] = jnp.tile(alpha, (1, 2)) * acc_ref[rows, :] + pv
            m_ref[rows, :] = m_new

    for rows in chunks:
        s_ref[rows, :] = scores(0, rows)

    def body(j, carry):
        step(j, False, True)
        return carry

    lax.fori_loop(0, qi, body, 0)
    step(qi, True, False)

    lam = (jnp.exp(jnp.sum(lam_ref[0:1, :] * lam_ref[1:2, :], axis=-1, keepdims=True))
           - jnp.exp(jnp.sum(lam_ref[2:3, :] * lam_ref[3:4, :], axis=-1, keepdims=True))
           + LAM_INIT)
    o1 = acc_ref[0:tq, 0:LANES] / acc_ref[0:tq, LANES:2 * LANES]
    o2 = acc_ref[tq:2 * tq, 0:LANES] / acc_ref[tq:2 * tq, LANES:2 * LANES]
    o = o1 - lam * o2
    ms = jnp.mean(o * o, axis=-1, keepdims=True)
    on = o * lax.rsqrt(ms + EPS) * sg_ref[...]
    o_ref[0] = (on * (1.0 - LAM_INIT)).astype(BF16)


def _diff_attention(q3, k3, v3, lam_vecs, subln_g):
    B, S, _ = q3.shape
    tq, tk = ATT_TQ, ATT_TK
    return pl.pallas_call(
        functools.partial(_attn_kernel, tq=tq, tk=tk),
        grid=(B, DA_HEADS, S // tq),
        in_specs=[
            pl.BlockSpec((1, tq, DA_V_DIM), lambda b, h, i: (b, i, h)),
            pl.BlockSpec((1, S, DA_V_DIM), lambda b, h, i: (b, 0, h)),
            pl.BlockSpec((1, S, DA_V_DIM), lambda b, h, i: (b, 0, h)),
            _const_spec((4, DA_HEAD_DIM)),
            _const_spec((1, DA_V_DIM)),
        ],
        out_specs=pl.BlockSpec((1, tq, DA_V_DIM), lambda b, h, i: (b, i, h)),
        out_shape=jax.ShapeDtypeStruct((B, S, ATTN_W), BF16),
        scratch_shapes=[
            pltpu.VMEM((2 * tq, DA_V_DIM), BF16),
            pltpu.VMEM((2 * tq, LANES), F32),
            pltpu.VMEM((2 * tq, 2 * LANES), F32),
            pltpu.VMEM((2 * tq, tk), F32),
        ],
        compiler_params=pltpu.CompilerParams(
            dimension_semantics=("parallel", "parallel", "parallel"),
            vmem_limit_bytes=VMEM_LIMIT),
        name="diff_attention",
    )(q3, k3, v3, lam_vecs, subln_g)


def _merge_kernel(conv_ref, attn_ref, gate_ref, x_ref, wc_ref, wa_ref, wo_ref, fg_ref,
                  wr_ref, br_ref, x1_ref, h2_ref, logit_ref):
    c = jnp.dot(conv_ref[...], wc_ref[...], preferred_element_type=F32)
    a = jnp.dot(attn_ref[...], wa_ref[...], preferred_element_type=F32)
    g0 = gate_ref[:, 0:D_MODEL].astype(F32)
    g1 = gate_ref[:, D_MODEL:2 * D_MODEL].astype(F32)
    merged = (g0 * c + g1 * a).astype(BF16)
    x1 = x_ref[...] + jnp.dot(merged, wo_ref[...], preferred_element_type=F32)
    x1_ref[...] = x1
    ms = jnp.mean(x1 * x1, axis=-1, keepdims=True)
    h2 = x1 * lax.rsqrt(ms + EPS) * fg_ref[...]
    h2_ref[...] = h2.astype(BF16)
    logit_ref[...] = jnp.dot(h2, wr_ref[...], preferred_element_type=F32,
                             precision=lax.Precision.HIGHEST) + br_ref[...]


def _merge(conv_act, attn_o, gates, x2, wc, wa, wo, fg, wr, br):
    T = x2.shape[0]
    tm = TM_PROJ
    row = lambda i: (i, 0)
    return pl.pallas_call(
        _merge_kernel,
        grid=(T // tm,),
        in_specs=[
            pl.BlockSpec((tm, CONV_CH), row),
            pl.BlockSpec((tm, ATTN_W), row),
            pl.BlockSpec((tm, COL_GATE), row),
            pl.BlockSpec((tm, D_MODEL), row),
            _const_spec((CONV_CH, D_MODEL)),
            _const_spec((ATTN_W, D_MODEL)),
            _const_spec((D_MODEL, D_MODEL)),
            _const_spec((1, D_MODEL)),
            _const_spec((D_MODEL, ROUTER_LANES)),
            _const_spec((1, ROUTER_LANES)),
        ],
        out_specs=(
            pl.BlockSpec((tm, D_MODEL), row),
            pl.BlockSpec((tm, D_MODEL), row),
            pl.BlockSpec((tm, ROUTER_LANES), row),
        ),
        out_shape=(
            jax.ShapeDtypeStruct((T, D_MODEL), F32),
            jax.ShapeDtypeStruct((T, D_MODEL), BF16),
            jax.ShapeDtypeStruct((T, ROUTER_LANES), F32),
        ),
        compiler_params=pltpu.CompilerParams(
            dimension_semantics=("parallel",), vmem_limit_bytes=VMEM_LIMIT),
        name="merge_out_router",
    )(conv_act, attn_o, gates, x2, wc, wa, wo, fg, wr, br)


def _expert_kernel(blk_exp_ref, n_used_ref, xs_ref, wg_ref, wu_ref, wd_ref, y_ref):
    b = pl.program_id(0)

    @pl.when(b < n_used_ref[0])
    def _():
        xb = xs_ref[...]
        g = jnp.dot(xb, wg_ref[0], preferred_element_type=F32)
        u = jnp.dot(xb, wu_ref[0], preferred_element_type=F32)
        hid = (g * _sigmoid(g) * u).astype(BF16)
        y_ref[...] = jnp.dot(hid, wd_ref[0], preferred_element_type=F32).astype(BF16)

    @pl.when(b >= n_used_ref[0])
    def _():
        y_ref[...] = jnp.zeros(y_ref.shape, BF16)


def _experts(blk_exp, n_used, xs, wg, wu, wd):
    n_rows = xs.shape[0]
    n_blk = n_rows // RB
    grid_spec = pltpu.PrefetchScalarGridSpec(
        num_scalar_prefetch=2,
        grid=(n_blk,),
        in_specs=[
            pl.BlockSpec((RB, D_MODEL), lambda b, be, nu: (b, 0)),
            pl.BlockSpec((1, D_MODEL, D_EXPERT), lambda b, be, nu: (be[b], 0, 0)),
            pl.BlockSpec((1, D_MODEL, D_EXPERT), lambda b, be, nu: (be[b], 0, 0)),
            pl.BlockSpec((1, D_EXPERT, D_MODEL), lambda b, be, nu: (be[b], 0, 0)),
        ],
        out_specs=pl.BlockSpec((RB, D_MODEL), lambda b, be, nu: (b, 0)),
    )
    return pl.pallas_call(
        _expert_kernel,
        grid_spec=grid_spec,
        out_shape=jax.ShapeDtypeStruct((n_rows, D_MODEL), BF16),
        compiler_params=pltpu.CompilerParams(
            dimension_semantics=("arbitrary",), vmem_limit_bytes=VMEM_LIMIT),
        name="expert_mlp",
    )(blk_exp, n_used, xs, wg, wu, wd)


def _combine_kernel(x1_ref, y_ref, w_ref, o_ref):
    y0 = y_ref[:, 0:D_MODEL].astype(F32)
    y1 = y_ref[:, D_MODEL:2 * D_MODEL].astype(F32)
    o_ref[...] = x1_ref[...] + (y0 * w_ref[:, 0:1] + y1 * w_ref[:, 1:2])


def _combine(x1, y2, wts):
    T = x1.shape[0]
    tm = TM_COMB
    row = lambda i: (i, 0)
    return pl.pallas_call(
        _combine_kernel,
        grid=(T // tm,),
        in_specs=[
            pl.BlockSpec((tm, D_MODEL), row),
            pl.BlockSpec((tm, 2 * D_MODEL), row),
            pl.BlockSpec((tm, 2), row),
        ],
        out_specs=pl.BlockSpec((tm, D_MODEL), row),
        out_shape=jax.ShapeDtypeStruct((T, D_MODEL), F32),
        compiler_params=pltpu.CompilerParams(
            dimension_semantics=("parallel",), vmem_limit_bytes=VMEM_LIMIT),
        name="moe_combine",
    )(x1, y2, wts)


def _route(logits):
    T = logits.shape[0]
    g_logits = logits[:, :N_GROUPS]
    e_logits = logits[:, N_GROUPS:N_GROUPS + N_EXPERTS].reshape(T, N_GROUPS, EXPERTS_PER_GROUP)
    g_prob = jax.nn.softmax(g_logits, axis=-1)
    g_sel = jnp.argmax(g_logits, axis=-1)
    g_w = jnp.take_along_axis(g_prob, g_sel[:, None], axis=1)
    e_sel = jnp.take_along_axis(e_logits, g_sel[:, None, None], axis=1)[:, 0]
    e_prob = jax.nn.softmax(e_sel, axis=-1)
    top_p, top_i = lax.top_k(e_prob, 2)
    wts = top_p / jnp.sum(top_p, axis=-1, keepdims=True) * g_w
    eidx = (g_sel[:, None] * EXPERTS_PER_GROUP + top_i).astype(jnp.int32)
    return eidx, wts


def _dispatch_plan(eidx):
    T = eidx.shape[0]
    A = 2 * T
    e_flat = eidx.reshape(A)
    onehot = (e_flat[:, None] == jnp.arange(N_EXPERTS, dtype=jnp.int32)[None, :]).astype(jnp.int32)
    csum = jnp.cumsum(onehot, axis=0)
    rank = jnp.take_along_axis(csum, e_flat[:, None], axis=1)[:, 0] - 1
    counts = csum[-1]
    padded = (counts + RB - 1) // RB * RB
    pad_end = jnp.cumsum(padded)
    pstart = pad_end - padded
    dest = pstart[e_flat] + rank
    n_blk = (A + N_EXPERTS * RB) // RB
    blk_exp = jnp.clip(jnp.searchsorted(pad_end, jnp.arange(n_blk, dtype=jnp.int32) * RB,
                                        side="right"), 0, N_EXPERTS - 1).astype(jnp.int32)
    n_used = (pad_end[-1] // RB).astype(jnp.int32).reshape(1)
    return dest.astype(jnp.int32), blk_exp, n_used


def kernel(x, attn_norm_g, w_in, conv_dw_w, conv_dw_b, conv_ln_g, conv_ln_b, w_conv_out,
           q_norm_g, k_norm_g, lambda_q1, lambda_k1, lambda_q2, lambda_k2, subln_g,
           w_attn_out, w_out, ffn_norm_g, w_router_group, b_router_group,
           w_router_expert, b_router_expert, w_gate_e, w_up_e, w_down_e):
    B, S, D = x.shape
    T = B * S
    l = 0
    x2 = x.reshape(T, D)

    reps = COL_Q // DA_HEAD_DIM
    qg = jnp.tile(q_norm_g[l], reps).reshape(1, COL_Q)
    kg = jnp.tile(k_norm_g[l], reps).reshape(1, COL_K)
    grp = jnp.arange(COL_Q, dtype=jnp.int32) // DA_HEAD_DIM
    bd = jnp.where(grp[:, None] == grp[None, :], 1.0 / DA_HEAD_DIM, 0.0).astype(BF16)

    glu, q, k, v, gates = _in_projection(
        x2, attn_norm_g[l].reshape(1, D), w_in[l].astype(BF16), qg, kg, bd)

    conv_act = _conv_branch(glu.reshape(B, S, CONV_CH), conv_dw_w[l],
                            conv_dw_b[l].reshape(1, CONV_CH), conv_ln_g[l].reshape(1, CONV_CH),
                            conv_ln_b[l].reshape(1, CONV_CH))

    lam_vecs = jnp.stack([lambda_q1[l], lambda_k1[l], lambda_q2[l], lambda_k2[l]]).astype(F32)
    attn_o = _diff_attention(q.reshape(B, S, COL_Q), k.reshape(B, S, COL_K),
                             v.reshape(B, S, COL_V), lam_vecs, subln_g[l].reshape(1, DA_V_DIM))

    wr = jnp.zeros((D, ROUTER_LANES), F32)
    wr = wr.at[:, :N_GROUPS].set(w_router_group[l]).at[:, N_GROUPS:N_GROUPS + N_EXPERTS].set(
        w_router_expert[l])
    br = jnp.zeros((1, ROUTER_LANES), F32)
    br = br.at[0, :N_GROUPS].set(b_router_group[l]).at[0, N_GROUPS:N_GROUPS + N_EXPERTS].set(
        b_router_expert[l])

    x1, h2, logits = _merge(conv_act.reshape(T, CONV_CH), attn_o.reshape(T, ATTN_W), gates, x2,
                            w_conv_out[l].astype(BF16), w_attn_out[l].astype(BF16),
                            w_out[l].astype(BF16), ffn_norm_g[l].reshape(1, D), wr, br)

    eidx, wts = _route(logits)
    dest, blk_exp, n_used = _dispatch_plan(eidx)
    n_rows = 2 * T + N_EXPERTS * RB
    tok = jnp.arange(2 * T, dtype=jnp.int32) // 2
    row_src = jnp.zeros((n_rows,), jnp.int32).at[dest].set(tok)
    xs = h2[row_src]

    yb = _experts(blk_exp, n_used, xs, w_gate_e[l].astype(BF16), w_up_e[l].astype(BF16),
                  w_down_e[l].astype(BF16))

    y2 = yb[dest].reshape(T, 2 * D)
    out = _combine(x1, y2, wts.astype(F32))
    return out.reshape(B, S, D)
```

```python
import functools
import math

import jax
import jax.numpy as jnp
from jax import lax
from jax.experimental import pallas as pl
from jax.experimental.pallas import tpu as pltpu

F32 = jnp.float32
BF16 = jnp.bfloat16

D_MODEL = 1024
CONV_CH = 512
CONV_WIDTH = 31
DA_HEADS = 4
DA_HEAD_DIM = 64
DA_V_DIM = 128
ATTN_W = 512
N_GROUPS = 4
EXPERTS_PER_GROUP = 8
N_EXPERTS = 32
D_EXPERT = 512
EPS = 1e-6
LAM_INIT = 0.8 - 0.6 * math.exp(-0.3 * 0)

COL_GLU = 2 * CONV_CH
COL_Q = 512
COL_K = 512
COL_V = 512
COL_GATE = 2 * D_MODEL
OFF_Q = COL_GLU
OFF_K = OFF_Q + COL_Q
OFF_V = OFF_K + COL_K
OFF_GATE = OFF_V + COL_V
IN_COLS = OFF_GATE + COL_GATE

ROUTER_LANES = 128
ROUTER_EXPERT_COL = 8
SUBLANES = 8
HALO = 32

TM_PROJ = 512
TS_CONV = 512
CONV_CHUNK = 64
ATT_TQ = 512
ATT_TK = 512
ATT_RC = 256
LANES = 128
RB = 256
TM_COMB = 512
TM_ROUTE = 512

VMEM_LIMIT = 48 * 1024 * 1024


def _sigmoid(x):
    return 1.0 / (1.0 + jnp.exp(-x))


def _const_spec(shape):
    nd = len(shape)
    return pl.BlockSpec(shape, lambda *_: (0,) * nd, pipeline_mode=pl.Buffered(1))


def _group_mean_sq(t, bd_ref):
    sq = t * t
    hi = sq.astype(BF16)
    lo = (sq - hi.astype(F32)).astype(BF16)
    return (jnp.dot(hi, bd_ref[...], preferred_element_type=F32)
            + jnp.dot(lo, bd_ref[...], preferred_element_type=F32))


def _inproj_kernel(x_ref, g_ref, w_ref, qg_ref, kg_ref, bd_ref,
                   glu_ref, q_ref, k_ref, v_ref, gate_ref):
    x = x_ref[...]
    ms = jnp.mean(x * x, axis=-1, keepdims=True)
    h = (x * lax.rsqrt(ms + EPS) * g_ref[...]).astype(BF16)

    def proj(lo, width):
        return jnp.dot(h, w_ref[:, lo:lo + width], preferred_element_type=F32)

    a = proj(0, CONV_CH)
    gt = proj(CONV_CH, CONV_CH)
    glu_ref[...] = (a * _sigmoid(gt)).astype(BF16)

    q = proj(OFF_Q, COL_Q)
    qn = q * lax.rsqrt(_group_mean_sq(q, bd_ref) + EPS) * qg_ref[...]
    q_ref[...] = (qn * (DA_HEAD_DIM ** -0.5)).astype(BF16)

    k = proj(OFF_K, COL_K)
    kn = k * lax.rsqrt(_group_mean_sq(k, bd_ref) + EPS) * kg_ref[...]
    k_ref[...] = kn.astype(BF16)

    v_ref[...] = proj(OFF_V, COL_V).astype(BF16)

    for c in range(COL_GATE // 512):
        gate_ref[:, c * 512:(c + 1) * 512] = _sigmoid(proj(OFF_GATE + c * 512, 512)).astype(BF16)


def _in_projection(x2, g, w_bf, qg, kg, bd):
    T = x2.shape[0]
    tm = TM_PROJ
    row = lambda i: (i, 0)
    out_shape = (
        jax.ShapeDtypeStruct((T, CONV_CH), BF16),
        jax.ShapeDtypeStruct((T, COL_Q), BF16),
        jax.ShapeDtypeStruct((T, COL_K), BF16),
        jax.ShapeDtypeStruct((T, COL_V), BF16),
        jax.ShapeDtypeStruct((T, COL_GATE), BF16),
    )
    return pl.pallas_call(
        _inproj_kernel,
        grid=(T // tm,),
        in_specs=[
            pl.BlockSpec((tm, D_MODEL), row),
            _const_spec((1, D_MODEL)),
            _const_spec((D_MODEL, IN_COLS)),
            _const_spec((1, COL_Q)),
            _const_spec((1, COL_K)),
            _const_spec((COL_Q, COL_Q)),
        ],
        out_specs=(
            pl.BlockSpec((tm, CONV_CH), row),
            pl.BlockSpec((tm, COL_Q), row),
            pl.BlockSpec((tm, COL_K), row),
            pl.BlockSpec((tm, COL_V), row),
            pl.BlockSpec((tm, COL_GATE), row),
        ),
        out_shape=out_shape,
        compiler_params=pltpu.CompilerParams(
            dimension_semantics=("parallel",), vmem_limit_bytes=VMEM_LIMIT),
        name="in_projection",
    )(x2, g, w_bf, qg, kg, bd)


def _conv_kernel(cur_ref, halo_ref, w_ref, b_ref, lng_ref, lnb_ref, out_ref, buf_ref):
    i = pl.program_id(1)
    ts = cur_ref.shape[1]
    halo = halo_ref[0].astype(F32)
    buf_ref[0:HALO, :] = jnp.where(i > 0, halo, 0.0)
    buf_ref[HALO:HALO + ts, :] = cur_ref[0].astype(F32)

    first = HALO - (CONV_WIDTH - 1)
    for c in range(ts // CONV_CHUNK):
        r0 = c * CONV_CHUNK
        acc = jnp.broadcast_to(b_ref[...], (CONV_CHUNK, CONV_CH))
        for j in range(CONV_WIDTH):
            acc = acc + w_ref[j:j + 1, :] * buf_ref[r0 + first + j:r0 + first + j + CONV_CHUNK, :]
        mu = jnp.mean(acc, axis=-1, keepdims=True)
        d = acc - mu
        var = jnp.mean(d * d, axis=-1, keepdims=True)
        y = d * lax.rsqrt(var + EPS) * lng_ref[...] + lnb_ref[...]
        out_ref[0, r0:r0 + CONV_CHUNK, :] = (y * _sigmoid(y)).astype(BF16)


def _conv_branch(glu3, dw_w, dw_b, ln_g, ln_b):
    B, S, C = glu3.shape
    ts = TS_CONV
    per = ts // HALO
    return pl.pallas_call(
        _conv_kernel,
        grid=(B, S // ts),
        in_specs=[
            pl.BlockSpec((1, ts, C), lambda b, i: (b, i, 0)),
            pl.BlockSpec((1, HALO, C), lambda b, i: (b, jnp.maximum(i * per - 1, 0), 0)),
            _const_spec((CONV_WIDTH, C)),
            _const_spec((1, C)),
            _const_spec((1, C)),
            _const_spec((1, C)),
        ],
        out_specs=pl.BlockSpec((1, ts, C), lambda b, i: (b, i, 0)),
        out_shape=jax.ShapeDtypeStruct((B, S, C), BF16),
        scratch_shapes=[pltpu.VMEM((HALO + ts, C), F32)],
        compiler_params=pltpu.CompilerParams(
            dimension_semantics=("parallel", "parallel"), vmem_limit_bytes=VMEM_LIMIT),
        name="conv_branch",
    )(glu3, glu3, dw_w, dw_b, ln_g, ln_b)


def _attn_kernel(q_ref, k_ref, v_ref, lam_ref, sg_ref, o_ref, qs_ref, m_ref, acc_ref, s_ref, *, tq, tk):
    qi = pl.program_id(2)
    q = q_ref[0]
    lane = lax.broadcasted_iota(jnp.int32, q.shape, 1)
    zero = jnp.zeros_like(q)
    qs_ref[0:tq, :] = jnp.where(lane < DA_HEAD_DIM, q, zero)
    qs_ref[tq:2 * tq, :] = jnp.where(lane >= DA_HEAD_DIM, q, zero)
    m_ref[...] = jnp.full(m_ref.shape, -jnp.inf, F32)
    acc_ref[...] = jnp.zeros(acc_ref.shape, F32)
    ones = jnp.ones((tk, LANES), BF16)
    rc = ATT_RC
    chunks = [slice(r0, r0 + rc) for r0 in range(0, 2 * tq, rc)]

    def scores(j, rows):
        kj = k_ref[0, pl.ds(pl.multiple_of(j * tk, tk), tk), :]
        return lax.dot_general(qs_ref[rows, :], kj, (((1,), (1,)), ((), ())),
                               preferred_element_type=F32)

    def step(j, masked, prefetch):
        vj = jnp.concatenate([v_ref[0, pl.ds(pl.multiple_of(j * tk, tk), tk), :], ones], axis=1)
        for rows in chunks:
            r_lo = rows.start % tq
            kw = min(tk, -(-(r_lo + rc) // (2 * LANES)) * (2 * LANES)) if masked else tk
            s = s_ref[rows, 0:kw]
            if prefetch:
                s_ref[rows, :] = scores(j + 1, rows)
            if masked:
                r = lax.broadcasted_iota(jnp.int32, s.shape, 0) + r_lo
                c = lax.broadcasted_iota(jnp.int32, s.shape, 1)
                s = jnp.where(c <= r, s, -jnp.inf)
            m_old = m_ref[rows, :]
            m_new = jnp.maximum(m_old, jnp.max(s, axis=-1, keepdims=True))
            alpha = jnp.exp(m_old - m_new)
            p = jnp.exp(s - jnp.tile(m_new, (1, kw // LANES)))
            pv = jnp.dot(p.astype(BF16), vj[0:kw, :], preferred_element_type=F32)
            acc_ref[rows, :] = jnp.tile(alpha, (1, 2)) * acc_ref[rows, :] + pv
            m_ref[rows, :] = m_new

    for rows in chunks:
        s_ref[rows, :] = scores(0, rows)

    def body(j, carry):
        step(j, False, True)
        return carry

    lax.fori_loop(0, qi, body, 0)
    step(qi, True, False)

    lam = (jnp.exp(jnp.sum(lam_ref[0:1, :] * lam_ref[1:2, :], axis=-1, keepdims=True))
           - jnp.exp(jnp.sum(lam_ref[2:3, :] * lam_ref[3:4, :], axis=-1, keepdims=True))
           + LAM_INIT)
    o1 = acc_ref[0:tq, 0:LANES] / acc_ref[0:tq, LANES:2 * LANES]
    o2 = acc_ref[tq:2 * tq, 0:LANES] / acc_ref[tq:2 * tq, LANES:2 * LANES]
    o = o1 - lam * o2
    ms = jnp.mean(o * o, axis=-1, keepdims=True)
    on = o * lax.rsqrt(ms + EPS) * sg_ref[...]
    o_ref[0] = (on * (1.0 - LAM_INIT)).astype(BF16)


def _diff_attention(q3, k3, v3, lam_vecs, subln_g):
    B, S, _ = q3.shape
    tq, tk = ATT_TQ, ATT_TK
    return pl.pallas_call(
        functools.partial(_attn_kernel, tq=tq, tk=tk),
        grid=(B, DA_HEADS, S // tq),
        in_specs=[
            pl.BlockSpec((1, tq, DA_V_DIM), lambda b, h, i: (b, i, h)),
            pl.BlockSpec((1, S, DA_V_DIM), lambda b, h, i: (b, 0, h)),
            pl.BlockSpec((1, S, DA_V_DIM), lambda b, h, i: (b, 0, h)),
            _const_spec((4, DA_HEAD_DIM)),
            _const_spec((1, DA_V_DIM)),
        ],
        out_specs=pl.BlockSpec((1, tq, DA_V_DIM), lambda b, h, i: (b, i, h)),
        out_shape=jax.ShapeDtypeStruct((B, S, ATTN_W), BF16),
        scratch_shapes=[
            pltpu.VMEM((2 * tq, DA_V_DIM), BF16),
            pltpu.VMEM((2 * tq, LANES), F32),
            pltpu.VMEM((2 * tq, 2 * LANES), F32),
            pltpu.VMEM((2 * tq, tk), F32),
        ],
        compiler_params=pltpu.CompilerParams(
            dimension_semantics=("parallel", "parallel", "parallel"),
            vmem_limit_bytes=VMEM_LIMIT),
        name="diff_attention",
    )(q3, k3, v3, lam_vecs, subln_g)


def _merge_kernel(conv_ref, attn_ref, gate_ref, x_ref, wc_ref, wa_ref, wo_ref, fg_ref,
                  wr_ref, br_ref, x1_ref, h2_ref, logit_ref):
    c = jnp.dot(conv_ref[...], wc_ref[...], preferred_element_type=F32)
    a = jnp.dot(attn_ref[...], wa_ref[...], preferred_element_type=F32)
    g0 = gate_ref[:, 0:D_MODEL].astype(F32)
    g1 = gate_ref[:, D_MODEL:2 * D_MODEL].astype(F32)
    merged = (g0 * c + g1 * a).astype(BF16)
    x1 = x_ref[...] + jnp.dot(merged, wo_ref[...], preferred_element_type=F32)
    x1_ref[...] = x1
    ms = jnp.mean(x1 * x1, axis=-1, keepdims=True)
    h2 = x1 * lax.rsqrt(ms + EPS) * fg_ref[...]
    h2_ref[...] = h2.astype(BF16)
    logit_ref[...] = jnp.dot(h2, wr_ref[...], preferred_element_type=F32,
                             precision=lax.Precision.HIGHEST) + br_ref[...]


def _merge(conv_act, attn_o, gates, x2, wc, wa, wo, fg, wr, br):
    T = x2.shape[0]
    tm = TM_PROJ
    row = lambda i: (i, 0)
    return pl.pallas_call(
        _merge_kernel,
        grid=(T // tm,),
        in_specs=[
            pl.BlockSpec((tm, CONV_CH), row),
            pl.BlockSpec((tm, ATTN_W), row),
            pl.BlockSpec((tm, COL_GATE), row),
            pl.BlockSpec((tm, D_MODEL), row),
            _const_spec((CONV_CH, D_MODEL)),
            _const_spec((ATTN_W, D_MODEL)),
            _const_spec((D_MODEL, D_MODEL)),
            _const_spec((1, D_MODEL)),
            _const_spec((D_MODEL, ROUTER_LANES)),
            _const_spec((1, ROUTER_LANES)),
        ],
        out_specs=(
            pl.BlockSpec((tm, D_MODEL), row),
            pl.BlockSpec((tm, D_MODEL), row),
            pl.BlockSpec((tm, ROUTER_LANES), row),
        ),
        out_shape=(
            jax.ShapeDtypeStruct((T, D_MODEL), F32),
            jax.ShapeDtypeStruct((T, D_MODEL), BF16),
            jax.ShapeDtypeStruct((T, ROUTER_LANES), F32),
        ),
        compiler_params=pltpu.CompilerParams(
            dimension_semantics=("parallel",), vmem_limit_bytes=VMEM_LIMIT),
        name="merge_out_router",
    )(conv_act, attn_o, gates, x2, wc, wa, wo, fg, wr, br)


def _route_kernel(logit_ref, tri_ref, low_ref, dest_ref, wts_ref, blk_ref, seg_ref,
                  cnt_ref, base_ref, *, n_blk_lanes):
    ph = pl.program_id(0)
    i = pl.program_id(1)
    tm = logit_ref.shape[0]
    lt = logit_ref[...].T
    row8 = lax.broadcasted_iota(jnp.int32, (SUBLANES, tm), 0)

    g = jnp.where(row8 < N_GROUPS, lt[0:SUBLANES, :], -jnp.inf)
    g_max = jnp.max(g, axis=0, keepdims=True)
    g_sum = jnp.sum(jnp.exp(g - g_max), axis=0, keepdims=True)
    g_sel = jnp.min(jnp.where(g == g_max, row8, SUBLANES), axis=0, keepdims=True)
    g_w = 1.0 / g_sum

    e_sel = jnp.zeros((EXPERTS_PER_GROUP, tm), F32)
    for gi in range(N_GROUPS):
        lo = ROUTER_EXPERT_COL + gi * EXPERTS_PER_GROUP
        e_sel = jnp.where(g_sel == gi, lt[lo:lo + EXPERTS_PER_GROUP, :], e_sel)
    e_max = jnp.max(e_sel, axis=0, keepdims=True)
    e_exp = jnp.exp(e_sel - e_max)
    prob = e_exp / jnp.sum(e_exp, axis=0, keepdims=True)
    p1 = jnp.max(prob, axis=0, keepdims=True)
    i1 = jnp.min(jnp.where(prob == p1, row8, SUBLANES), axis=0, keepdims=True)
    rest = jnp.where(row8 == i1, -1.0, prob)
    p2 = jnp.max(rest, axis=0, keepdims=True)
    i2 = jnp.min(jnp.where(rest == p2, row8, SUBLANES), axis=0, keepdims=True)
    denom = p1 + p2
    e0 = g_sel * EXPERTS_PER_GROUP + i1
    e1 = g_sel * EXPERTS_PER_GROUP + i2

    row32 = lax.broadcasted_iota(jnp.int32, (N_EXPERTS, tm), 0)
    oh0 = row32 == e0
    oh1 = row32 == e1
    ohs = jnp.where(oh0 | oh1, 1.0, 0.0)
    tile_cnt = jnp.sum(ohs, axis=1, keepdims=True)

    @pl.when((ph == 0) & (i == 0))
    def _():
        cnt_ref[...] = jnp.zeros(cnt_ref.shape, F32)

    @pl.when(ph == 0)
    def _():
        cnt_ref[...] += tile_cnt

    @pl.when((ph == 1) & (i == 0))
    def _():
        cnt = cnt_ref[...]
        padded = jnp.ceil(cnt * (1.0 / RB)) * RB
        pstart = jnp.dot(low_ref[...], padded, preferred_element_type=F32,
                         precision=lax.Precision.HIGHEST)
        start = jnp.dot(low_ref[...], cnt, preferred_element_type=F32,
                        precision=lax.Precision.HIGHEST)
        base_ref[...] = pstart
        pad_end = pstart + padded
        lane_row = lax.broadcasted_iota(jnp.int32, (N_EXPERTS, n_blk_lanes), 1).astype(F32) * RB
        before = jnp.where(pad_end[:, 0:1] <= lane_row, 1.0, 0.0)
        blk = jnp.minimum(jnp.sum(before, axis=0, keepdims=True), N_EXPERTS - 1.0)
        blk_ref[...] = blk.astype(jnp.int32)
        seg_ref[0] = cnt.astype(jnp.int32)
        seg_ref[1] = pstart.astype(jnp.int32)
        seg_ref[2] = start.astype(jnp.int32)
        seg_ref[3] = pad_end.astype(jnp.int32)

    @pl.when(ph == 1)
    def _():
        before = jnp.dot(ohs.astype(BF16), tri_ref[...], preferred_element_type=F32)
        pos = base_ref[:, 0:1] + before
        d0 = jnp.sum(jnp.where(oh0, pos, 0.0), axis=0, keepdims=True)
        d1 = jnp.sum(jnp.where(oh1, pos, 0.0), axis=0, keepdims=True)
        dest_ref[0:1, :] = d0.astype(jnp.int32)
        dest_ref[1:2, :] = d1.astype(jnp.int32)
        wts_ref[0:1, :] = p1 / denom * g_w
        wts_ref[1:2, :] = p2 / denom * g_w
        base_ref[...] += tile_cnt


def _route(logits):
    T = logits.shape[0]
    tm = TM_ROUTE
    n_rows = 2 * T + N_EXPERTS * RB
    n_blk = n_rows // RB
    n_blk_lanes = -(-n_blk // LANES) * LANES
    r = jnp.arange(tm, dtype=jnp.int32)
    tri = (r[:, None] < r[None, :]).astype(BF16)
    e = jnp.arange(N_EXPERTS, dtype=jnp.int32)
    low = (e[None, :] < e[:, None]).astype(F32)
    dest, wts, blk, seg = pl.pallas_call(
        functools.partial(_route_kernel, n_blk_lanes=n_blk_lanes),
        grid=(2, T // tm),
        in_specs=[
            pl.BlockSpec((tm, ROUTER_LANES), lambda ph, i: (i, 0)),
            _const_spec((tm, tm)),
            _const_spec((N_EXPERTS, N_EXPERTS)),
        ],
        out_specs=(
            pl.BlockSpec((2, tm), lambda ph, i: (0, i * ph)),
            pl.BlockSpec((2, tm), lambda ph, i: (0, i * ph)),
            pl.BlockSpec((1, n_blk_lanes), lambda ph, i: (0, 0)),
            pl.BlockSpec((4, N_EXPERTS, LANES), lambda ph, i: (0, 0, 0)),
        ),
        out_shape=(
            jax.ShapeDtypeStruct((2, T), jnp.int32),
            jax.ShapeDtypeStruct((2, T), F32),
            jax.ShapeDtypeStruct((1, n_blk_lanes), jnp.int32),
            jax.ShapeDtypeStruct((4, N_EXPERTS, LANES), jnp.int32),
        ),
        scratch_shapes=[pltpu.VMEM((N_EXPERTS, LANES), F32), pltpu.VMEM((N_EXPERTS, LANES), F32)],
        compiler_params=pltpu.CompilerParams(
            dimension_semantics=("arbitrary", "arbitrary"), vmem_limit_bytes=VMEM_LIMIT),
        name="route_plan",
    )(logits, tri, low)
    return dest, wts, blk, seg


def _expert_kernel(blk_exp_ref, n_used_ref, xs_ref, wg_ref, wu_ref, wd_ref, y_ref,
                   wg_bf, wu_bf, wd_bf):
    b = pl.program_id(0)
    used = b < n_used_ref[0]
    new_expert = (b == 0) | (blk_exp_ref[b] != blk_exp_ref[jnp.maximum(b - 1, 0)])

    @pl.when(used & new_expert)
    def _():
        wg_bf[...] = wg_ref[0].astype(BF16)
        wu_bf[...] = wu_ref[0].astype(BF16)
        wd_bf[...] = wd_ref[0].astype(BF16)

    @pl.when(used)
    def _():
        xb = xs_ref[...]
        g = jnp.dot(xb, wg_bf[...], preferred_element_type=F32)
        u = jnp.dot(xb, wu_bf[...], preferred_element_type=F32)
        hid = (g * _sigmoid(g) * u).astype(BF16)
        y_ref[...] = jnp.dot(hid, wd_bf[...], preferred_element_type=F32).astype(BF16)

    @pl.when(jnp.logical_not(used))
    def _():
        y_ref[...] = jnp.zeros(y_ref.shape, BF16)


def _experts(blk_exp, n_used, xs, wg, wu, wd):
    n_rows = xs.shape[0]
    n_blk = n_rows // RB
    grid_spec = pltpu.PrefetchScalarGridSpec(
        num_scalar_prefetch=2,
        grid=(n_blk,),
        in_specs=[
            pl.BlockSpec((RB, D_MODEL), lambda b, be, nu: (b, 0)),
            pl.BlockSpec((1, D_MODEL, D_EXPERT), lambda b, be, nu: (be[b], 0, 0)),
            pl.BlockSpec((1, D_MODEL, D_EXPERT), lambda b, be, nu: (be[b], 0, 0)),
            pl.BlockSpec((1, D_EXPERT, D_MODEL), lambda b, be, nu: (be[b], 0, 0)),
        ],
        out_specs=pl.BlockSpec((RB, D_MODEL), lambda b, be, nu: (b, 0)),
        scratch_shapes=[
            pltpu.VMEM((D_MODEL, D_EXPERT), BF16),
            pltpu.VMEM((D_MODEL, D_EXPERT), BF16),
            pltpu.VMEM((D_EXPERT, D_MODEL), BF16),
        ],
    )
    return pl.pallas_call(
        _expert_kernel,
        grid_spec=grid_spec,
        out_shape=jax.ShapeDtypeStruct((n_rows, D_MODEL), BF16),
        compiler_params=pltpu.CompilerParams(
            dimension_semantics=("arbitrary",), vmem_limit_bytes=VMEM_LIMIT),
        name="expert_mlp",
    )(blk_exp, n_used, xs, wg, wu, wd)


def _combine_kernel(x1_ref, y0_ref, y1_ref, w_ref, o_ref):
    o_ref[...] = x1_ref[...] + (y0_ref[...].astype(F32) * w_ref[:, 0:1]
                                + y1_ref[...].astype(F32) * w_ref[:, 1:2])


def _combine(x1, y0, y1, wts):
    T = x1.shape[0]
    tm = TM_COMB
    row = lambda i: (i, 0)
    return pl.pallas_call(
        _combine_kernel,
        grid=(T // tm,),
        in_specs=[
            pl.BlockSpec((tm, D_MODEL), row),
            pl.BlockSpec((tm, D_MODEL), row),
            pl.BlockSpec((tm, D_MODEL), row),
            pl.BlockSpec((tm, 2), row),
        ],
        out_specs=pl.BlockSpec((tm, D_MODEL), row),
        out_shape=jax.ShapeDtypeStruct((T, D_MODEL), F32),
        compiler_params=pltpu.CompilerParams(
            dimension_semantics=("parallel",), vmem_limit_bytes=VMEM_LIMIT),
        name="moe_combine",
    )(x1, y0, y1, wts)


def _sorted_row_sources(dest, blk, seg, n_rows):
    T = dest.shape[1]
    a = jnp.arange(2 * T, dtype=jnp.int32)
    _, tok_sorted = lax.sort_key_val(dest.T.reshape(2 * T), a // 2)
    cnt, pstart, start = seg[0, :, 0], seg[1, :, 0], seg[2, :, 0]
    r = jnp.arange(n_rows, dtype=jnp.int32)
    e_r = blk[0, :n_rows // RB][r // RB]
    off = r - pstart[e_r]
    src = tok_sorted[jnp.clip(start[e_r] + off, 0, 2 * T - 1)]
    return jnp.where(off < cnt[e_r], src, 0)


def kernel(x, attn_norm_g, w_in, conv_dw_w, conv_dw_b, conv_ln_g, conv_ln_b, w_conv_out,
           q_norm_g, k_norm_g, lambda_q1, lambda_k1, lambda_q2, lambda_k2, subln_g,
           w_attn_out, w_out, ffn_norm_g, w_router_group, b_router_group,
           w_router_expert, b_router_expert, w_gate_e, w_up_e, w_down_e):
    B, S, D = x.shape
    T = B * S
    l = 0
    x2 = x.reshape(T, D)

    reps = COL_Q // DA_HEAD_DIM
    qg = jnp.tile(q_norm_g[l], reps).reshape(1, COL_Q)
    kg = jnp.tile(k_norm_g[l], reps).reshape(1, COL_K)
    grp = jnp.arange(COL_Q, dtype=jnp.int32) // DA_HEAD_DIM
    bd = jnp.where(grp[:, None] == grp[None, :], 1.0 / DA_HEAD_DIM, 0.0).astype(BF16)

    glu, q, k, v, gates = _in_projection(
        x2, attn_norm_g[l].reshape(1, D), w_in[l].astype(BF16), qg, kg, bd)

    conv_act = _conv_branch(glu.reshape(B, S, CONV_CH), conv_dw_w[l],
                            conv_dw_b[l].reshape(1, CONV_CH), conv_ln_g[l].reshape(1, CONV_CH),
                            conv_ln_b[l].reshape(1, CONV_CH))

    lam_vecs = jnp.stack([lambda_q1[l], lambda_k1[l], lambda_q2[l], lambda_k2[l]]).astype(F32)
    attn_o = _diff_attention(q.reshape(B, S, COL_Q), k.reshape(B, S, COL_K),
                             v.reshape(B, S, COL_V), lam_vecs, subln_g[l].reshape(1, DA_V_DIM))

    e_lo, e_hi = ROUTER_EXPERT_COL, ROUTER_EXPERT_COL + N_EXPERTS
    wr = jnp.zeros((D, ROUTER_LANES), F32)
    wr = wr.at[:, :N_GROUPS].set(w_router_group[l]).at[:, e_lo:e_hi].set(w_router_expert[l])
    br = jnp.zeros((1, ROUTER_LANES), F32)
    br = br.at[0, :N_GROUPS].set(b_router_group[l]).at[0, e_lo:e_hi].set(b_router_expert[l])

    x1, h2, logits = _merge(conv_act.reshape(T, CONV_CH), attn_o.reshape(T, ATTN_W), gates, x2,
                            w_conv_out[l].astype(BF16), w_attn_out[l].astype(BF16),
                            w_out[l].astype(BF16), ffn_norm_g[l].reshape(1, D), wr, br)

    dest, wts, blk, seg = _route(logits)
    n_rows = 2 * T + N_EXPERTS * RB
    n_blk = n_rows // RB
    xs = h2[_sorted_row_sources(dest, blk, seg, n_rows)]
    n_used = seg[3, N_EXPERTS - 1, 0:1] // RB

    yb = _experts(blk[0, :n_blk], n_used, xs, w_gate_e[l], w_up_e[l], w_down_e[l])

    out = _combine(x1, yb[dest[0]], yb[dest[1]], wts.T)
    return out.reshape(B, S, D)
```

```python
import functools
import math

import jax
import jax.numpy as jnp
from jax import lax
from jax.experimental import pallas as pl
from jax.experimental.pallas import tpu as pltpu

F32 = jnp.float32
BF16 = jnp.bfloat16

D_MODEL = 1024
CONV_CH = 512
CONV_WIDTH = 31
DA_HEADS = 4
DA_HEAD_DIM = 64
DA_V_DIM = 128
ATTN_W = 512
N_GROUPS = 4
EXPERTS_PER_GROUP = 8
N_EXPERTS = 32
D_EXPERT = 512
EPS = 1e-6
LAM_INIT = 0.8 - 0.6 * math.exp(-0.3 * 0)

COL_GLU = 2 * CONV_CH
COL_Q = 512
COL_K = 512
COL_V = 512
COL_GATE = 2 * D_MODEL
OFF_Q = COL_GLU
OFF_K = OFF_Q + COL_Q
OFF_V = OFF_K + COL_K
OFF_GATE = OFF_V + COL_V
IN_COLS = OFF_GATE + COL_GATE

ROUTER_LANES = 128
ROUTER_EXPERT_COL = 8
SUBLANES = 8
NORM_BLOCK = 256
HALO = 32

TM_PROJ = 512
TS_CONV = 512
CONV_CHUNK = 64
ATT_TQ = 512
ATT_TK = 512
ATT_RC = 256
LANES = 128
RB = 512
TM_COMB = 512
TM_ROUTE = 512

VMEM_LIMIT = 48 * 1024 * 1024


def _sigmoid(x):
    return 1.0 / (1.0 + jnp.exp(-x))


def _const_spec(shape):
    nd = len(shape)
    return pl.BlockSpec(shape, lambda *_: (0,) * nd, pipeline_mode=pl.Buffered(1))


def _group_mean_sq(t, bd_ref):
    sq = t * t
    hi = sq.astype(BF16)
    lo = (sq - hi.astype(F32)).astype(BF16)
    w = bd_ref.shape[0]
    parts = [jnp.dot(hi[:, c:c + w], bd_ref[...], preferred_element_type=F32)
             + jnp.dot(lo[:, c:c + w], bd_ref[...], preferred_element_type=F32)
             for c in range(0, t.shape[1], w)]
    return jnp.concatenate(parts, axis=1)


def _inproj_kernel(x_ref, g_ref, w_ref, qg_ref, kg_ref, bd_ref,
                   glu_ref, q_ref, k_ref, v_ref, gate_ref):
    x = x_ref[...]
    ms = jnp.mean(x * x, axis=-1, keepdims=True)
    h = (x * lax.rsqrt(ms + EPS) * g_ref[...]).astype(BF16)

    def proj(lo, width):
        return jnp.dot(h, w_ref[:, lo:lo + width], preferred_element_type=F32)

    a = proj(0, CONV_CH)
    gt = proj(CONV_CH, CONV_CH)
    glu_ref[...] = (a * _sigmoid(gt)).astype(BF16)

    q = proj(OFF_Q, COL_Q)
    qn = q * lax.rsqrt(_group_mean_sq(q, bd_ref) + EPS) * qg_ref[...]
    q_ref[...] = (qn * (DA_HEAD_DIM ** -0.5)).astype(BF16)

    k = proj(OFF_K, COL_K)
    kn = k * lax.rsqrt(_group_mean_sq(k, bd_ref) + EPS) * kg_ref[...]
    k_ref[...] = kn.astype(BF16)

    v_ref[...] = proj(OFF_V, COL_V).astype(BF16)

    for c in range(COL_GATE // 512):
        gate_ref[:, c * 512:(c + 1) * 512] = _sigmoid(proj(OFF_GATE + c * 512, 512)).astype(BF16)


def _in_projection(x2, g, w_bf, qg, kg, bd):
    T = x2.shape[0]
    tm = TM_PROJ
    row = lambda i: (i, 0)
    out_shape = (
        jax.ShapeDtypeStruct((T, CONV_CH), BF16),
        jax.ShapeDtypeStruct((T, COL_Q), BF16),
        jax.ShapeDtypeStruct((T, COL_K), BF16),
        jax.ShapeDtypeStruct((T, COL_V), BF16),
        jax.ShapeDtypeStruct((T, COL_GATE), BF16),
    )
    return pl.pallas_call(
        _inproj_kernel,
        grid=(T // tm,),
        in_specs=[
            pl.BlockSpec((tm, D_MODEL), row),
            _const_spec((1, D_MODEL)),
            _const_spec((D_MODEL, IN_COLS)),
            _const_spec((1, COL_Q)),
            _const_spec((1, COL_K)),
            _const_spec((NORM_BLOCK, NORM_BLOCK)),
        ],
        out_specs=(
            pl.BlockSpec((tm, CONV_CH), row),
            pl.BlockSpec((tm, COL_Q), row),
            pl.BlockSpec((tm, COL_K), row),
            pl.BlockSpec((tm, COL_V), row),
            pl.BlockSpec((tm, COL_GATE), row),
        ),
        out_shape=out_shape,
        compiler_params=pltpu.CompilerParams(
            dimension_semantics=("parallel",), vmem_limit_bytes=VMEM_LIMIT),
        name="in_projection",
    )(x2, g, w_bf, qg, kg, bd)


def _conv_kernel(cur_ref, halo_ref, w_ref, b_ref, lng_ref, lnb_ref, out_ref, buf_ref):
    i = pl.program_id(1)
    ts = cur_ref.shape[1]
    halo = halo_ref[0].astype(F32)
    buf_ref[0:HALO, :] = jnp.where(i > 0, halo, 0.0)
    buf_ref[HALO:HALO + ts, :] = cur_ref[0].astype(F32)

    first = HALO - (CONV_WIDTH - 1)
    for c in range(ts // CONV_CHUNK):
        r0 = c * CONV_CHUNK
        acc = jnp.broadcast_to(b_ref[...], (CONV_CHUNK, CONV_CH))
        for j in range(CONV_WIDTH):
            acc = acc + w_ref[j:j + 1, :] * buf_ref[r0 + first + j:r0 + first + j + CONV_CHUNK, :]
        mu = jnp.mean(acc, axis=-1, keepdims=True)
        d = acc - mu
        var = jnp.mean(d * d, axis=-1, keepdims=True)
        y = d * lax.rsqrt(var + EPS) * lng_ref[...] + lnb_ref[...]
        out_ref[0, r0:r0 + CONV_CHUNK, :] = (y * _sigmoid(y)).astype(BF16)


def _conv_branch(glu3, dw_w, dw_b, ln_g, ln_b):
    B, S, C = glu3.shape
    ts = TS_CONV
    per = ts // HALO
    return pl.pallas_call(
        _conv_kernel,
        grid=(B, S // ts),
        in_specs=[
            pl.BlockSpec((1, ts, C), lambda b, i: (b, i, 0)),
            pl.BlockSpec((1, HALO, C), lambda b, i: (b, jnp.maximum(i * per - 1, 0), 0)),
            _const_spec((CONV_WIDTH, C)),
            _const_spec((1, C)),
            _const_spec((1, C)),
            _const_spec((1, C)),
        ],
        out_specs=pl.BlockSpec((1, ts, C), lambda b, i: (b, i, 0)),
        out_shape=jax.ShapeDtypeStruct((B, S, C), BF16),
        scratch_shapes=[pltpu.VMEM((HALO + ts, C), F32)],
        compiler_params=pltpu.CompilerParams(
            dimension_semantics=("parallel", "parallel"), vmem_limit_bytes=VMEM_LIMIT),
        name="conv_branch",
    )(glu3, glu3, dw_w, dw_b, ln_g, ln_b)


def _attn_kernel(q_ref, k_ref, v_ref, lam_ref, sg_ref, o_ref, qs_ref, m_ref, acc_ref, s_ref, *, tq, tk):
    qi = pl.program_id(2)
    q = q_ref[0]
    lane = lax.broadcasted_iota(jnp.int32, q.shape, 1)
    zero = jnp.zeros_like(q)
    qs_ref[0:tq, :] = jnp.where(lane < DA_HEAD_DIM, q, zero)
    qs_ref[tq:2 * tq, :] = jnp.where(lane >= DA_HEAD_DIM, q, zero)
    m_ref[...] = jnp.full(m_ref.shape, -jnp.inf, F32)
    acc_ref[...] = jnp.zeros(acc_ref.shape, F32)
    ones = jnp.ones((tk, LANES), BF16)
    rc = ATT_RC
    chunks = [slice(r0, r0 + rc) for r0 in range(0, 2 * tq, rc)]

    def scores(j, rows):
        kj = k_ref[0, pl.ds(pl.multiple_of(j * tk, tk), tk), :]
        return lax.dot_general(qs_ref[rows, :], kj, (((1,), (1,)), ((), ())),
                               preferred_element_type=F32)

    def step(j, masked, prefetch):
        vj = jnp.concatenate([v_ref[0, pl.ds(pl.multiple_of(j * tk, tk), tk), :], ones], axis=1)
        for rows in chunks:
            r_lo = rows.start % tq
            kw = min(tk, -(-(r_lo + rc) // (2 * LANES)) * (2 * LANES)) if masked else tk
            s = s_ref[rows, 0:kw]
            if prefetch:
                s_ref[rows, :] = scores(j + 1, rows)
            if masked:
                r = lax.broadcasted_iota(jnp.int32, s.shape, 0) + r_lo
                c = lax.broadcasted_iota(jnp.int32, s.shape, 1)
                s = jnp.where(c <= r, s, -jnp.inf)
            m_old = m_ref[rows, :]
            m_new = jnp.maximum(m_old, jnp.max(s, axis=-1, keepdims=True))
            alpha = jnp.exp(m_old - m_new)
            p = jnp.exp(s - jnp.tile(m_new, (1, kw // LANES)))
            pv = jnp.dot(p.astype(BF16), vj[0:kw, :], preferred_element_type=F32)
            acc_ref[rows, :] = jnp.tile(alpha, (1, 2)) * acc_ref[rows, :] + pv
            m_ref[rows, :] = m_new

    for rows in chunks:
        s_ref[rows, :] = scores(0, rows)

    def body(j, carry):
        step(j, False, True)
        return carry

    lax.fori_loop(0, qi, body, 0)
    step(qi, True, False)

    lam = (jnp.exp(jnp.sum(lam_ref[0:1, :] * lam_ref[1:2, :], axis=-1, keepdims=True))
           - jnp.exp(jnp.sum(lam_ref[2:3, :] * lam_ref[3:4, :], axis=-1, keepdims=True))
           + LAM_INIT)
    o1 = acc_ref[0:tq, 0:LANES] / acc_ref[0:tq, LANES:2 * LANES]
    o2 = acc_ref[tq:2 * tq, 0:LANES] / acc_ref[tq:2 * tq, LANES:2 * LANES]
    o = o1 - lam * o2
    ms = jnp.mean(o * o, axis=-1, keepdims=True)
    on = o * lax.rsqrt(ms + EPS) * sg_ref[...]
    o_ref[0] = (on * (1.0 - LAM_INIT)).astype(BF16)


def _diff_attention(q3, k3, v3, lam_vecs, subln_g):
    B, S, _ = q3.shape
    tq, tk = ATT_TQ, ATT_TK
    return pl.pallas_call(
        functools.partial(_attn_kernel, tq=tq, tk=tk),
        grid=(B, DA_HEADS, S // tq),
        in_specs=[
            pl.BlockSpec((1, tq, DA_V_DIM), lambda b, h, i: (b, i, h)),
            pl.BlockSpec((1, S, DA_V_DIM), lambda b, h, i: (b, 0, h)),
            pl.BlockSpec((1, S, DA_V_DIM), lambda b, h, i: (b, 0, h)),
            _const_spec((4, DA_HEAD_DIM)),
            _const_spec((1, DA_V_DIM)),
        ],
        out_specs=pl.BlockSpec((1, tq, DA_V_DIM), lambda b, h, i: (b, i, h)),
        out_shape=jax.ShapeDtypeStruct((B, S, ATTN_W), BF16),
        scratch_shapes=[
            pltpu.VMEM((2 * tq, DA_V_DIM), BF16),
            pltpu.VMEM((2 * tq, LANES), F32),
            pltpu.VMEM((2 * tq, 2 * LANES), F32),
            pltpu.VMEM((2 * tq, tk), F32),
        ],
        compiler_params=pltpu.CompilerParams(
            dimension_semantics=("parallel", "parallel", "parallel"),
            vmem_limit_bytes=VMEM_LIMIT),
        name="diff_attention",
    )(q3, k3, v3, lam_vecs, subln_g)


def _merge_kernel(conv_ref, attn_ref, gate_ref, x_ref, wc_ref, wa_ref, wo_ref, fg_ref,
                  wr_ref, br_ref, x1_ref, h2_ref, logit_ref):
    c = jnp.dot(conv_ref[...], wc_ref[...], preferred_element_type=F32)
    a = jnp.dot(attn_ref[...], wa_ref[...], preferred_element_type=F32)
    g0 = gate_ref[:, 0:D_MODEL].astype(F32)
    g1 = gate_ref[:, D_MODEL:2 * D_MODEL].astype(F32)
    merged = (g0 * c + g1 * a).astype(BF16)
    x1 = x_ref[...] + jnp.dot(merged, wo_ref[...], preferred_element_type=F32)
    x1_ref[...] = x1
    ms = jnp.mean(x1 * x1, axis=-1, keepdims=True)
    h2 = x1 * lax.rsqrt(ms + EPS) * fg_ref[...]
    h2_ref[...] = h2.astype(BF16)
    h2_hi = h2.astype(BF16)
    h2_lo = (h2 - h2_hi.astype(F32)).astype(BF16)
    r = (jnp.dot(h2_hi, wr_ref[...], preferred_element_type=F32)
         + jnp.dot(h2_lo, wr_ref[...], preferred_element_type=F32))
    logit_ref[...] = r[:, 0:ROUTER_LANES] + r[:, ROUTER_LANES:2 * ROUTER_LANES] + br_ref[...]


def _merge(conv_act, attn_o, gates, x2, wc, wa, wo, fg, wr, br):
    T = x2.shape[0]
    tm = TM_PROJ
    row = lambda i: (i, 0)
    return pl.pallas_call(
        _merge_kernel,
        grid=(T // tm,),
        in_specs=[
            pl.BlockSpec((tm, CONV_CH), row),
            pl.BlockSpec((tm, ATTN_W), row),
            pl.BlockSpec((tm, COL_GATE), row),
            pl.BlockSpec((tm, D_MODEL), row),
            _const_spec((CONV_CH, D_MODEL)),
            _const_spec((ATTN_W, D_MODEL)),
            _const_spec((D_MODEL, D_MODEL)),
            _const_spec((1, D_MODEL)),
            _const_spec((D_MODEL, 2 * ROUTER_LANES)),
            _const_spec((1, ROUTER_LANES)),
        ],
        out_specs=(
            pl.BlockSpec((tm, D_MODEL), row),
            pl.BlockSpec((tm, D_MODEL), row),
            pl.BlockSpec((tm, ROUTER_LANES), row),
        ),
        out_shape=(
            jax.ShapeDtypeStruct((T, D_MODEL), F32),
            jax.ShapeDtypeStruct((T, D_MODEL), BF16),
            jax.ShapeDtypeStruct((T, ROUTER_LANES), F32),
        ),
        compiler_params=pltpu.CompilerParams(
            dimension_semantics=("parallel",), vmem_limit_bytes=VMEM_LIMIT),
        name="merge_out_router",
    )(conv_act, attn_o, gates, x2, wc, wa, wo, fg, wr, br)


def _route_kernel(logit_ref, tri_ref, low_ref, dest_ref, wts_ref, blk_ref, seg_ref,
                  cnt_ref, base_ref, *, n_blk_lanes):
    ph = pl.program_id(0)
    i = pl.program_id(1)
    tm = logit_ref.shape[0]
    lt = logit_ref[...].T
    row8 = lax.broadcasted_iota(jnp.int32, (SUBLANES, tm), 0)

    g = jnp.where(row8 < N_GROUPS, lt[0:SUBLANES, :], -jnp.inf)
    g_max = jnp.max(g, axis=0, keepdims=True)
    g_sum = jnp.sum(jnp.exp(g - g_max), axis=0, keepdims=True)
    g_sel = jnp.min(jnp.where(g == g_max, row8, SUBLANES), axis=0, keepdims=True)
    g_w = 1.0 / g_sum

    e_sel = jnp.zeros((EXPERTS_PER_GROUP, tm), F32)
    for gi in range(N_GROUPS):
        lo = ROUTER_EXPERT_COL + gi * EXPERTS_PER_GROUP
        e_sel = jnp.where(g_sel == gi, lt[lo:lo + EXPERTS_PER_GROUP, :], e_sel)
    e_max = jnp.max(e_sel, axis=0, keepdims=True)
    e_exp = jnp.exp(e_sel - e_max)
    prob = e_exp / jnp.sum(e_exp, axis=0, keepdims=True)
    p1 = jnp.max(prob, axis=0, keepdims=True)
    i1 = jnp.min(jnp.where(prob == p1, row8, SUBLANES), axis=0, keepdims=True)
    rest = jnp.where(row8 == i1, -1.0, prob)
    p2 = jnp.max(rest, axis=0, keepdims=True)
    i2 = jnp.min(jnp.where(rest == p2, row8, SUBLANES), axis=0, keepdims=True)
    denom = p1 + p2
    e0 = g_sel * EXPERTS_PER_GROUP + i1
    e1 = g_sel * EXPERTS_PER_GROUP + i2

    row32 = lax.broadcasted_iota(jnp.int32, (N_EXPERTS, tm), 0)
    oh0 = row32 == e0
    oh1 = row32 == e1
    ohs = jnp.where(oh0 | oh1, 1.0, 0.0)
    tile_cnt = jnp.sum(ohs, axis=1, keepdims=True)

    @pl.when((ph == 0) & (i == 0))
    def _():
        cnt_ref[...] = jnp.zeros(cnt_ref.shape, F32)

    @pl.when(ph == 0)
    def _():
        cnt_ref[...] += tile_cnt

    @pl.when((ph == 1) & (i == 0))
    def _():
        cnt = cnt_ref[...]
        padded = jnp.ceil(cnt * (1.0 / RB)) * RB
        pstart = jnp.dot(low_ref[...], padded, preferred_element_type=F32,
                         precision=lax.Precision.HIGHEST)
        start = jnp.dot(low_ref[...], cnt, preferred_element_type=F32,
                        precision=lax.Precision.HIGHEST)
        base_ref[...] = pstart
        pad_end = pstart + padded
        lane_row = lax.broadcasted_iota(jnp.int32, (N_EXPERTS, n_blk_lanes), 1).astype(F32) * RB
        before = jnp.where(pad_end[:, 0:1] <= lane_row, 1.0, 0.0)
        blk = jnp.minimum(jnp.sum(before, axis=0, keepdims=True), N_EXPERTS - 1.0)
        blk_ref[...] = blk.astype(jnp.int32)
        seg_ref[0] = cnt.astype(jnp.int32)
        seg_ref[1] = pstart.astype(jnp.int32)
        seg_ref[2] = start.astype(jnp.int32)
        seg_ref[3] = pad_end.astype(jnp.int32)

    @pl.when(ph == 1)
    def _():
        before = jnp.dot(ohs.astype(BF16), tri_ref[...], preferred_element_type=F32)
        pos = base_ref[:, 0:1] + before
        d0 = jnp.sum(jnp.where(oh0, pos, 0.0), axis=0, keepdims=True)
        d1 = jnp.sum(jnp.where(oh1, pos, 0.0), axis=0, keepdims=True)
        dest_ref[0:1, :] = d0.astype(jnp.int32)
        dest_ref[1:2, :] = d1.astype(jnp.int32)
        wts_ref[0:1, :] = p1 / denom * g_w
        wts_ref[1:2, :] = p2 / denom * g_w
        base_ref[...] += tile_cnt


def _route(logits):
    T = logits.shape[0]
    tm = TM_ROUTE
    n_rows = 2 * T + N_EXPERTS * RB
    n_blk = n_rows // RB
    n_blk_lanes = -(-n_blk // LANES) * LANES
    r = jnp.arange(tm, dtype=jnp.int32)
    tri = (r[:, None] < r[None, :]).astype(BF16)
    e = jnp.arange(N_EXPERTS, dtype=jnp.int32)
    low = (e[None, :] < e[:, None]).astype(F32)
    dest, wts, blk, seg = pl.pallas_call(
        functools.partial(_route_kernel, n_blk_lanes=n_blk_lanes),
        grid=(2, T // tm),
        in_specs=[
            pl.BlockSpec((tm, ROUTER_LANES), lambda ph, i: (i, 0)),
            _const_spec((tm, tm)),
            _const_spec((N_EXPERTS, N_EXPERTS)),
        ],
        out_specs=(
            pl.BlockSpec((2, tm), lambda ph, i: (0, i * ph)),
            pl.BlockSpec((2, tm), lambda ph, i: (0, i * ph)),
            pl.BlockSpec((1, n_blk_lanes), lambda ph, i: (0, 0)),
            pl.BlockSpec((4, N_EXPERTS, LANES), lambda ph, i: (0, 0, 0)),
        ),
        out_shape=(
            jax.ShapeDtypeStruct((2, T), jnp.int32),
            jax.ShapeDtypeStruct((2, T), F32),
            jax.ShapeDtypeStruct((1, n_blk_lanes), jnp.int32),
            jax.ShapeDtypeStruct((4, N_EXPERTS, LANES), jnp.int32),
        ),
        scratch_shapes=[pltpu.VMEM((N_EXPERTS, LANES), F32), pltpu.VMEM((N_EXPERTS, LANES), F32)],
        compiler_params=pltpu.CompilerParams(
            dimension_semantics=("arbitrary", "arbitrary"), vmem_limit_bytes=VMEM_LIMIT),
        name="route_plan",
    )(logits, tri, low)
    return dest, wts, blk, seg


def _expert_kernel(blk_exp_ref, n_used_ref, xs_ref, wg_ref, wu_ref, wd_ref, y_ref,
                   wg_bf, wu_bf, wd_bf):
    b = pl.program_id(0)
    used = b < n_used_ref[0]
    new_expert = (b == 0) | (blk_exp_ref[b] != blk_exp_ref[jnp.maximum(b - 1, 0)])

    @pl.when(used & new_expert)
    def _():
        wg_bf[...] = wg_ref[0].astype(BF16)
        wu_bf[...] = wu_ref[0].astype(BF16)
        wd_bf[...] = wd_ref[0].astype(BF16)

    @pl.when(used)
    def _():
        xb = xs_ref[...]
        g = jnp.dot(xb, wg_bf[...], preferred_element_type=F32)
        u = jnp.dot(xb, wu_bf[...], preferred_element_type=F32)
        hid = (g * _sigmoid(g) * u).astype(BF16)
        y_ref[...] = jnp.dot(hid, wd_bf[...], preferred_element_type=F32).astype(BF16)

    @pl.when(jnp.logical_not(used))
    def _():
        y_ref[...] = jnp.zeros(y_ref.shape, BF16)


def _experts(blk_exp, n_used, xs, wg, wu, wd):
    n_rows = xs.shape[0]
    n_blk = n_rows // RB
    grid_spec = pltpu.PrefetchScalarGridSpec(
        num_scalar_prefetch=2,
        grid=(n_blk,),
        in_specs=[
            pl.BlockSpec((RB, D_MODEL), lambda b, be, nu: (b, 0)),
            pl.BlockSpec((1, D_MODEL, D_EXPERT), lambda b, be, nu: (be[b], 0, 0)),
            pl.BlockSpec((1, D_MODEL, D_EXPERT), lambda b, be, nu: (be[b], 0, 0)),
            pl.BlockSpec((1, D_EXPERT, D_MODEL), lambda b, be, nu: (be[b], 0, 0)),
        ],
        out_specs=pl.BlockSpec((RB, D_MODEL), lambda b, be, nu: (b, 0)),
        scratch_shapes=[
            pltpu.VMEM((D_MODEL, D_EXPERT), BF16),
            pltpu.VMEM((D_MODEL, D_EXPERT), BF16),
            pltpu.VMEM((D_EXPERT, D_MODEL), BF16),
        ],
    )
    return pl.pallas_call(
        _expert_kernel,
        grid_spec=grid_spec,
        out_shape=jax.ShapeDtypeStruct((n_rows, D_MODEL), BF16),
        compiler_params=pltpu.CompilerParams(
            dimension_semantics=("arbitrary",), vmem_limit_bytes=VMEM_LIMIT),
        name="expert_mlp",
    )(blk_exp, n_used, xs, wg, wu, wd)


def _combine_kernel(x1_ref, y0_ref, y1_ref, w_ref, o_ref):
    o_ref[...] = x1_ref[...] + (y0_ref[...].astype(F32) * w_ref[:, 0:1]
                                + y1_ref[...].astype(F32) * w_ref[:, 1:2])


def _combine(x1, y, wts):
    T = x1.shape[0]
    tm = TM_COMB
    row = lambda i: (i, 0)
    return pl.pallas_call(
        _combine_kernel,
        grid=(T // tm,),
        in_specs=[
            pl.BlockSpec((tm, D_MODEL), row),
            pl.BlockSpec((tm, D_MODEL), row),
            pl.BlockSpec((tm, D_MODEL), lambda i: (i + T // tm, 0)),
            pl.BlockSpec((tm, 2), row),
        ],
        out_specs=pl.BlockSpec((tm, D_MODEL), row),
        out_shape=jax.ShapeDtypeStruct((T, D_MODEL), F32),
        compiler_params=pltpu.CompilerParams(
            dimension_semantics=("parallel",), vmem_limit_bytes=VMEM_LIMIT),
        name="moe_combine",
    )(x1, y, y, wts)


def _sorted_row_sources(dest, blk, seg, n_rows):
    T = dest.shape[1]
    a = jnp.arange(2 * T, dtype=jnp.int32)
    _, tok_sorted = lax.sort_key_val(dest.T.reshape(2 * T), a // 2)
    cnt, pstart, start = seg[0, :, 0], seg[1, :, 0], seg[2, :, 0]
    r = jnp.arange(n_rows, dtype=jnp.int32)
    e_r = blk[0, :n_rows // RB][r // RB]
    off = r - pstart[e_r]
    src = tok_sorted[jnp.clip(start[e_r] + off, 0, 2 * T - 1)]
    return jnp.where(off < cnt[e_r], src, 0)


def kernel(x, attn_norm_g, w_in, conv_dw_w, conv_dw_b, conv_ln_g, conv_ln_b, w_conv_out,
           q_norm_g, k_norm_g, lambda_q1, lambda_k1, lambda_q2, lambda_k2, subln_g,
           w_attn_out, w_out, ffn_norm_g, w_router_group, b_router_group,
           w_router_expert, b_router_expert, w_gate_e, w_up_e, w_down_e):
    B, S, D = x.shape
    T = B * S
    l = 0
    x2 = x.reshape(T, D)

    reps = COL_Q // DA_HEAD_DIM
    qg = jnp.tile(q_norm_g[l], reps).reshape(1, COL_Q)
    kg = jnp.tile(k_norm_g[l], reps).reshape(1, COL_K)
    grp = jnp.arange(NORM_BLOCK, dtype=jnp.int32) // DA_HEAD_DIM
    bd = jnp.where(grp[:, None] == grp[None, :], 1.0 / DA_HEAD_DIM, 0.0).astype(BF16)

    glu, q, k, v, gates = _in_projection(
        x2, attn_norm_g[l].reshape(1, D), w_in[l].astype(BF16), qg, kg, bd)

    conv_act = _conv_branch(glu.reshape(B, S, CONV_CH), conv_dw_w[l],
                            conv_dw_b[l].reshape(1, CONV_CH), conv_ln_g[l].reshape(1, CONV_CH),
                            conv_ln_b[l].reshape(1, CONV_CH))

    lam_vecs = jnp.stack([lambda_q1[l], lambda_k1[l], lambda_q2[l], lambda_k2[l]]).astype(F32)
    attn_o = _diff_attention(q.reshape(B, S, COL_Q), k.reshape(B, S, COL_K),
                             v.reshape(B, S, COL_V), lam_vecs, subln_g[l].reshape(1, DA_V_DIM))

    e_lo, e_hi = ROUTER_EXPERT_COL, ROUTER_EXPERT_COL + N_EXPERTS
    wr = jnp.zeros((D, ROUTER_LANES), F32)
    wr = wr.at[:, :N_GROUPS].set(w_router_group[l]).at[:, e_lo:e_hi].set(w_router_expert[l])
    br = jnp.zeros((1, ROUTER_LANES), F32)
    br = br.at[0, :N_GROUPS].set(b_router_group[l]).at[0, e_lo:e_hi].set(b_router_expert[l])

    wr_hi = wr.astype(BF16)
    wr_lo = (wr - wr_hi.astype(F32)).astype(BF16)
    x1, h2, logits = _merge(conv_act.reshape(T, CONV_CH), attn_o.reshape(T, ATTN_W), gates, x2,
                            w_conv_out[l].astype(BF16), w_attn_out[l].astype(BF16),
                            w_out[l].astype(BF16), ffn_norm_g[l].reshape(1, D),
                            jnp.concatenate([wr_hi, wr_lo], axis=1), br)

    dest, wts, blk, seg = _route(logits)
    n_rows = 2 * T + N_EXPERTS * RB
    n_blk = n_rows // RB
    xs = h2[_sorted_row_sources(dest, blk, seg, n_rows)]
    n_used = seg[3, N_EXPERTS - 1, 0:1] // RB

    yb = _experts(blk[0, :n_blk], n_used, xs, w_gate_e[l], w_up_e[l], w_down_e[l])

    out = _combine(x1, yb[dest.reshape(2 * T)], wts.T)
    return out.reshape(B, S, D)
```

```python
import functools
import math

import jax
import jax.numpy as jnp
from jax import lax
from jax.experimental import pallas as pl
from jax.experimental.pallas import tpu as pltpu

F32 = jnp.float32
BF16 = jnp.bfloat16

D_MODEL = 1024
CONV_CH = 512
CONV_WIDTH = 31
DA_HEADS = 4
DA_HEAD_DIM = 64
DA_V_DIM = 128
ATTN_W = 512
N_GROUPS = 4
EXPERTS_PER_GROUP = 8
N_EXPERTS = 32
D_EXPERT = 512
EPS = 1e-6
LAM_INIT = 0.8 - 0.6 * math.exp(-0.3 * 0)

COL_GLU = 2 * CONV_CH
COL_Q = 512
COL_K = 512
COL_V = 512
COL_GATE = 2 * D_MODEL
OFF_Q = COL_GLU
OFF_K = OFF_Q + COL_Q
OFF_V = OFF_K + COL_K
OFF_GATE = OFF_V + COL_V
IN_COLS = OFF_GATE + COL_GATE

ROUTER_LANES = 128
ROUTER_EXPERT_COL = 8
SUBLANES = 8
NORM_BLOCK = 256
HALO = 32

TM_PROJ = 512
TS_CONV = 512
CONV_CHUNK = 64
ATT_TQ = 512
ATT_TK = 512
ATT_RC = 256
LANES = 128
RB = 512
TM_COMB = 512
TM_ROUTE = 512

VMEM_LIMIT = 48 * 1024 * 1024


def _sigmoid(x):
    return 1.0 / (1.0 + jnp.exp(-x))


def _const_spec(shape):
    nd = len(shape)
    return pl.BlockSpec(shape, lambda *_: (0,) * nd, pipeline_mode=pl.Buffered(1))


def _group_mean_sq(t, bd_ref):
    sq = t * t
    hi = sq.astype(BF16)
    lo = (sq - hi.astype(F32)).astype(BF16)
    w = bd_ref.shape[0]
    parts = [jnp.dot(hi[:, c:c + w], bd_ref[...], preferred_element_type=F32)
             + jnp.dot(lo[:, c:c + w], bd_ref[...], preferred_element_type=F32)
             for c in range(0, t.shape[1], w)]
    return jnp.concatenate(parts, axis=1)


def _inproj_kernel(x_ref, g_ref, w_ref, qg_ref, kg_ref, bd_ref,
                   glu_ref, q_ref, k_ref, v_ref, gate_ref):
    x = x_ref[...]
    ms = jnp.mean(x * x, axis=-1, keepdims=True)
    h = (x * lax.rsqrt(ms + EPS) * g_ref[...]).astype(BF16)

    def proj(lo, width):
        return jnp.dot(h, w_ref[:, lo:lo + width], preferred_element_type=F32)

    a = proj(0, CONV_CH)
    gt = proj(CONV_CH, CONV_CH)
    glu_ref[...] = (a * _sigmoid(gt)).astype(BF16)

    q = proj(OFF_Q, COL_Q)
    qn = q * lax.rsqrt(_group_mean_sq(q, bd_ref) + EPS) * qg_ref[...]
    q_ref[...] = (qn * (DA_HEAD_DIM ** -0.5)).astype(BF16)

    k = proj(OFF_K, COL_K)
    kn = k * lax.rsqrt(_group_mean_sq(k, bd_ref) + EPS) * kg_ref[...]
    k_ref[...] = kn.astype(BF16)

    v_ref[...] = proj(OFF_V, COL_V).astype(BF16)

    for c in range(COL_GATE // 512):
        gate_ref[:, c * 512:(c + 1) * 512] = _sigmoid(proj(OFF_GATE + c * 512, 512)).astype(BF16)


def _in_projection(x2, g, w_bf, qg, kg, bd):
    T = x2.shape[0]
    tm = TM_PROJ
    row = lambda i: (i, 0)
    out_shape = (
        jax.ShapeDtypeStruct((T, CONV_CH), BF16),
        jax.ShapeDtypeStruct((T, COL_Q), BF16),
        jax.ShapeDtypeStruct((T, COL_K), BF16),
        jax.ShapeDtypeStruct((T, COL_V), BF16),
        jax.ShapeDtypeStruct((T, COL_GATE), BF16),
    )
    return pl.pallas_call(
        _inproj_kernel,
        grid=(T // tm,),
        in_specs=[
            pl.BlockSpec((tm, D_MODEL), row),
            _const_spec((1, D_MODEL)),
            _const_spec((D_MODEL, IN_COLS)),
            _const_spec((1, COL_Q)),
            _const_spec((1, COL_K)),
            _const_spec((NORM_BLOCK, NORM_BLOCK)),
        ],
        out_specs=(
            pl.BlockSpec((tm, CONV_CH), row),
            pl.BlockSpec((tm, COL_Q), row),
            pl.BlockSpec((tm, COL_K), row),
            pl.BlockSpec((tm, COL_V), row),
            pl.BlockSpec((tm, COL_GATE), row),
        ),
        out_shape=out_shape,
        compiler_params=pltpu.CompilerParams(
            dimension_semantics=("parallel",), vmem_limit_bytes=VMEM_LIMIT),
        name="in_projection",
    )(x2, g, w_bf, qg, kg, bd)


def _conv_kernel(cur_ref, halo_ref, w_ref, b_ref, lng_ref, lnb_ref, out_ref, buf_ref):
    i = pl.program_id(1)
    ts = cur_ref.shape[1]
    halo = halo_ref[0].astype(F32)
    buf_ref[0:HALO, :] = jnp.where(i > 0, halo, 0.0)
    buf_ref[HALO:HALO + ts, :] = cur_ref[0].astype(F32)

    first = HALO - (CONV_WIDTH - 1)
    for c in range(ts // CONV_CHUNK):
        r0 = c * CONV_CHUNK
        acc = jnp.broadcast_to(b_ref[...], (CONV_CHUNK, CONV_CH))
        for j in range(CONV_WIDTH):
            acc = acc + w_ref[j:j + 1, :] * buf_ref[r0 + first + j:r0 + first + j + CONV_CHUNK, :]
        mu = jnp.mean(acc, axis=-1, keepdims=True)
        d = acc - mu
        var = jnp.mean(d * d, axis=-1, keepdims=True)
        y = d * lax.rsqrt(var + EPS) * lng_ref[...] + lnb_ref[...]
        out_ref[0, r0:r0 + CONV_CHUNK, :] = (y * _sigmoid(y)).astype(BF16)


def _conv_branch(glu3, dw_w, dw_b, ln_g, ln_b):
    B, S, C = glu3.shape
    ts = TS_CONV
    per = ts // HALO
    return pl.pallas_call(
        _conv_kernel,
        grid=(B, S // ts),
        in_specs=[
            pl.BlockSpec((1, ts, C), lambda b, i: (b, i, 0)),
            pl.BlockSpec((1, HALO, C), lambda b, i: (b, jnp.maximum(i * per - 1, 0), 0)),
            _const_spec((CONV_WIDTH, C)),
            _const_spec((1, C)),
            _const_spec((1, C)),
            _const_spec((1, C)),
        ],
        out_specs=pl.BlockSpec((1, ts, C), lambda b, i: (b, i, 0)),
        out_shape=jax.ShapeDtypeStruct((B, S, C), BF16),
        scratch_shapes=[pltpu.VMEM((HALO + ts, C), F32)],
        compiler_params=pltpu.CompilerParams(
            dimension_semantics=("parallel", "parallel"), vmem_limit_bytes=VMEM_LIMIT),
        name="conv_branch",
    )(glu3, glu3, dw_w, dw_b, ln_g, ln_b)


def _attn_kernel(q_ref, k_ref, v_ref, lam_ref, sg_ref, o_ref, qs_ref, m_ref, acc_ref, s_ref, *, tq, tk):
    qi = pl.program_id(2)
    q = q_ref[0]
    lane = lax.broadcasted_iota(jnp.int32, q.shape, 1)
    zero = jnp.zeros_like(q)
    qs_ref[0:tq, :] = jnp.where(lane < DA_HEAD_DIM, q, zero)
    qs_ref[tq:2 * tq, :] = jnp.where(lane >= DA_HEAD_DIM, q, zero)
    m_ref[...] = jnp.full(m_ref.shape, -jnp.inf, F32)
    acc_ref[...] = jnp.zeros(acc_ref.shape, F32)
    ones = jnp.ones((tk, LANES), BF16)
    rc = ATT_RC
    chunks = [slice(r0, r0 + rc) for r0 in range(0, 2 * tq, rc)]

    def scores(j, rows):
        kj = k_ref[0, pl.ds(pl.multiple_of(j * tk, tk), tk), :]
        return lax.dot_general(qs_ref[rows, :], kj, (((1,), (1,)), ((), ())),
                               preferred_element_type=F32)

    def step(j, masked, prefetch):
        vj = jnp.concatenate([v_ref[0, pl.ds(pl.multiple_of(j * tk, tk), tk), :], ones], axis=1)
        for rows in chunks:
            r_lo = rows.start % tq
            kw = min(tk, -(-(r_lo + rc) // (2 * LANES)) * (2 * LANES)) if masked else tk
            s = s_ref[rows, 0:kw]
            if prefetch:
                s_ref[rows, :] = scores(j + 1, rows)
            if masked:
                r = lax.broadcasted_iota(jnp.int32, s.shape, 0) + r_lo
                c = lax.broadcasted_iota(jnp.int32, s.shape, 1)
                s = jnp.where(c <= r, s, -jnp.inf)
            m_old = m_ref[rows, :]
            m_new = jnp.maximum(m_old, jnp.max(s, axis=-1, keepdims=True))
            alpha = jnp.exp(m_old - m_new)
            p = jnp.exp(s - jnp.tile(m_new, (1, kw // LANES)))
            pv = jnp.dot(p.astype(BF16), vj[0:kw, :], preferred_element_type=F32)
            acc_ref[rows, :] = jnp.tile(alpha, (1, 2)) * acc_ref[rows, :] + pv
            m_ref[rows, :] = m_new

    for rows in chunks:
        s_ref[rows, :] = scores(0, rows)

    def body(j, carry):
        step(j, False, True)
        return carry

    lax.fori_loop(0, qi, body, 0)
    step(qi, True, False)

    lam = (jnp.exp(jnp.sum(lam_ref[0:1, :] * lam_ref[1:2, :], axis=-1, keepdims=True))
           - jnp.exp(jnp.sum(lam_ref[2:3, :] * lam_ref[3:4, :], axis=-1, keepdims=True))
           + LAM_INIT)
    o1 = acc_ref[0:tq, 0:LANES] / acc_ref[0:tq, LANES:2 * LANES]
    o2 = acc_ref[tq:2 * tq, 0:LANES] / acc_ref[tq:2 * tq, LANES:2 * LANES]
    o = o1 - lam * o2
    ms = jnp.mean(o * o, axis=-1, keepdims=True)
    on = o * lax.rsqrt(ms + EPS) * sg_ref[...]
    o_ref[0] = (on * (1.0 - LAM_INIT)).astype(BF16)


def _diff_attention(q3, k3, v3, lam_vecs, subln_g):
    B, S, _ = q3.shape
    tq, tk = ATT_TQ, ATT_TK
    return pl.pallas_call(
        functools.partial(_attn_kernel, tq=tq, tk=tk),
        grid=(B, DA_HEADS, S // tq),
        in_specs=[
            pl.BlockSpec((1, tq, DA_V_DIM), lambda b, h, i: (b, i, h)),
            pl.BlockSpec((1, S, DA_V_DIM), lambda b, h, i: (b, 0, h)),
            pl.BlockSpec((1, S, DA_V_DIM), lambda b, h, i: (b, 0, h)),
            _const_spec((4, DA_HEAD_DIM)),
            _const_spec((1, DA_V_DIM)),
        ],
        out_specs=pl.BlockSpec((1, tq, DA_V_DIM), lambda b, h, i: (b, i, h)),
        out_shape=jax.ShapeDtypeStruct((B, S, ATTN_W), BF16),
        scratch_shapes=[
            pltpu.VMEM((2 * tq, DA_V_DIM), BF16),
            pltpu.VMEM((2 * tq, LANES), F32),
            pltpu.VMEM((2 * tq, 2 * LANES), F32),
            pltpu.VMEM((2 * tq, tk), F32),
        ],
        compiler_params=pltpu.CompilerParams(
            dimension_semantics=("parallel", "parallel", "parallel"),
            vmem_limit_bytes=VMEM_LIMIT),
        name="diff_attention",
    )(q3, k3, v3, lam_vecs, subln_g)


def _merge_kernel(conv_ref, attn_ref, gate_ref, x_ref, wc_ref, wa_ref, wo_ref, fg_ref,
                  wr_ref, br_ref, x1_ref, h2_ref, logit_ref):
    c = jnp.dot(conv_ref[...], wc_ref[...], preferred_element_type=F32)
    a = jnp.dot(attn_ref[...], wa_ref[...], preferred_element_type=F32)
    g0 = gate_ref[:, 0:D_MODEL].astype(F32)
    g1 = gate_ref[:, D_MODEL:2 * D_MODEL].astype(F32)
    merged = (g0 * c + g1 * a).astype(BF16)
    x1 = x_ref[...] + jnp.dot(merged, wo_ref[...], preferred_element_type=F32)
    x1_ref[...] = x1
    ms = jnp.mean(x1 * x1, axis=-1, keepdims=True)
    h2 = x1 * lax.rsqrt(ms + EPS) * fg_ref[...]
    h2_ref[...] = h2
    h2_hi = h2.astype(BF16)
    h2_lo = (h2 - h2_hi.astype(F32)).astype(BF16)
    r = (jnp.dot(h2_hi, wr_ref[...], preferred_element_type=F32)
         + jnp.dot(h2_lo, wr_ref[...], preferred_element_type=F32))
    logit_ref[...] = r[:, 0:ROUTER_LANES] + r[:, ROUTER_LANES:2 * ROUTER_LANES] + br_ref[...]


def _merge(conv_act, attn_o, gates, x2, wc, wa, wo, fg, wr, br):
    T = x2.shape[0]
    tm = TM_PROJ
    row = lambda i: (i, 0)
    return pl.pallas_call(
        _merge_kernel,
        grid=(T // tm,),
        in_specs=[
            pl.BlockSpec((tm, CONV_CH), row),
            pl.BlockSpec((tm, ATTN_W), row),
            pl.BlockSpec((tm, COL_GATE), row),
            pl.BlockSpec((tm, D_MODEL), row),
            _const_spec((CONV_CH, D_MODEL)),
            _const_spec((ATTN_W, D_MODEL)),
            _const_spec((D_MODEL, D_MODEL)),
            _const_spec((1, D_MODEL)),
            _const_spec((D_MODEL, 2 * ROUTER_LANES)),
            _const_spec((1, ROUTER_LANES)),
        ],
        out_specs=(
            pl.BlockSpec((tm, D_MODEL), row),
            pl.BlockSpec((tm, D_MODEL), row),
            pl.BlockSpec((tm, ROUTER_LANES), row),
        ),
        out_shape=(
            jax.ShapeDtypeStruct((T, D_MODEL), F32),
            jax.ShapeDtypeStruct((T, D_MODEL), F32),
            jax.ShapeDtypeStruct((T, ROUTER_LANES), F32),
        ),
        compiler_params=pltpu.CompilerParams(
            dimension_semantics=("parallel",), vmem_limit_bytes=VMEM_LIMIT),
        name="merge_out_router",
    )(conv_act, attn_o, gates, x2, wc, wa, wo, fg, wr, br)


def _route_kernel(logit_ref, tri_ref, low_ref, dest_ref, wts_ref, blk_ref, seg_ref,
                  cnt_ref, base_ref, *, n_blk_lanes):
    ph = pl.program_id(0)
    i = pl.program_id(1)
    tm = logit_ref.shape[0]
    lt = logit_ref[...].T
    row8 = lax.broadcasted_iota(jnp.int32, (SUBLANES, tm), 0)

    g = jnp.where(row8 < N_GROUPS, lt[0:SUBLANES, :], -jnp.inf)
    g_max = jnp.max(g, axis=0, keepdims=True)
    g_sum = jnp.sum(jnp.exp(g - g_max), axis=0, keepdims=True)
    g_sel = jnp.min(jnp.where(g == g_max, row8, SUBLANES), axis=0, keepdims=True)
    g_w = 1.0 / g_sum

    e_sel = jnp.zeros((EXPERTS_PER_GROUP, tm), F32)
    for gi in range(N_GROUPS):
        lo = ROUTER_EXPERT_COL + gi * EXPERTS_PER_GROUP
        e_sel = jnp.where(g_sel == gi, lt[lo:lo + EXPERTS_PER_GROUP, :], e_sel)
    e_max = jnp.max(e_sel, axis=0, keepdims=True)
    e_exp = jnp.exp(e_sel - e_max)
    prob = e_exp / jnp.sum(e_exp, axis=0, keepdims=True)
    p1 = jnp.max(prob, axis=0, keepdims=True)
    i1 = jnp.min(jnp.where(prob == p1, row8, SUBLANES), axis=0, keepdims=True)
    rest = jnp.where(row8 == i1, -1.0, prob)
    p2 = jnp.max(rest, axis=0, keepdims=True)
    i2 = jnp.min(jnp.where(rest == p2, row8, SUBLANES), axis=0, keepdims=True)
    denom = p1 + p2
    e0 = g_sel * EXPERTS_PER_GROUP + i1
    e1 = g_sel * EXPERTS_PER_GROUP + i2

    row32 = lax.broadcasted_iota(jnp.int32, (N_EXPERTS, tm), 0)
    oh0 = row32 == e0
    oh1 = row32 == e1
    ohs = jnp.where(oh0 | oh1, 1.0, 0.0)
    tile_cnt = jnp.sum(ohs, axis=1, keepdims=True)

    @pl.when((ph == 0) & (i == 0))
    def _():
        cnt_ref[...] = jnp.zeros(cnt_ref.shape, F32)

    @pl.when(ph == 0)
    def _():
        cnt_ref[...] += tile_cnt

    @pl.when((ph == 1) & (i == 0))
    def _():
        cnt = cnt_ref[...]
        padded = jnp.ceil(cnt * (1.0 / RB)) * RB
        pstart = jnp.dot(low_ref[...], padded, preferred_element_type=F32,
                         precision=lax.Precision.HIGHEST)
        start = jnp.dot(low_ref[...], cnt, preferred_element_type=F32,
                        precision=lax.Precision.HIGHEST)
        base_ref[...] = pstart
        pad_end = pstart + padded
        lane_row = lax.broadcasted_iota(jnp.int32, (N_EXPERTS, n_blk_lanes), 1).astype(F32) * RB
        before = jnp.where(pad_end[:, 0:1] <= lane_row, 1.0, 0.0)
        blk = jnp.minimum(jnp.sum(before, axis=0, keepdims=True), N_EXPERTS - 1.0)
        blk_ref[...] = blk.astype(jnp.int32)
        seg_ref[0] = cnt.astype(jnp.int32)
        seg_ref[1] = pstart.astype(jnp.int32)
        seg_ref[2] = start.astype(jnp.int32)
        seg_ref[3] = pad_end.astype(jnp.int32)

    @pl.when(ph == 1)
    def _():
        before = jnp.dot(ohs.astype(BF16), tri_ref[...], preferred_element_type=F32)
        pos = base_ref[:, 0:1] + before
        d0 = jnp.sum(jnp.where(oh0, pos, 0.0), axis=0, keepdims=True)
        d1 = jnp.sum(jnp.where(oh1, pos, 0.0), axis=0, keepdims=True)
        dest_ref[0:1, :] = d0.astype(jnp.int32)
        dest_ref[1:2, :] = d1.astype(jnp.int32)
        wts_ref[0:1, :] = p1 / denom * g_w
        wts_ref[1:2, :] = p2 / denom * g_w
        base_ref[...] += tile_cnt


def _route(logits):
    T = logits.shape[0]
    tm = TM_ROUTE
    n_rows = 2 * T + N_EXPERTS * RB
    n_blk = n_rows // RB
    n_blk_lanes = -(-n_blk // LANES) * LANES
    r = jnp.arange(tm, dtype=jnp.int32)
    tri = (r[:, None] < r[None, :]).astype(BF16)
    e = jnp.arange(N_EXPERTS, dtype=jnp.int32)
    low = (e[None, :] < e[:, None]).astype(F32)
    dest, wts, blk, seg = pl.pallas_call(
        functools.partial(_route_kernel, n_blk_lanes=n_blk_lanes),
        grid=(2, T // tm),
        in_specs=[
            pl.BlockSpec((tm, ROUTER_LANES), lambda ph, i: (i, 0)),
            _const_spec((tm, tm)),
            _const_spec((N_EXPERTS, N_EXPERTS)),
        ],
        out_specs=(
            pl.BlockSpec((2, tm), lambda ph, i: (0, i * ph)),
            pl.BlockSpec((2, tm), lambda ph, i: (0, i * ph)),
            pl.BlockSpec((1, n_blk_lanes), lambda ph, i: (0, 0)),
            pl.BlockSpec((4, N_EXPERTS, LANES), lambda ph, i: (0, 0, 0)),
        ),
        out_shape=(
            jax.ShapeDtypeStruct((2, T), jnp.int32),
            jax.ShapeDtypeStruct((2, T), F32),
            jax.ShapeDtypeStruct((1, n_blk_lanes), jnp.int32),
            jax.ShapeDtypeStruct((4, N_EXPERTS, LANES), jnp.int32),
        ),
        scratch_shapes=[pltpu.VMEM((N_EXPERTS, LANES), F32), pltpu.VMEM((N_EXPERTS, LANES), F32)],
        compiler_params=pltpu.CompilerParams(
            dimension_semantics=("arbitrary", "arbitrary"), vmem_limit_bytes=VMEM_LIMIT),
        name="route_plan",
    )(logits, tri, low)
    return dest, wts, blk, seg


def _expert_kernel(blk_exp_ref, n_used_ref, src_ref, src_next_ref, dst_ref,
                   h_hbm, wg_ref, wu_ref, wd_ref, y_hbm,
                   wg_bf, wu_bf, wd_bf, xbuf, ybuf, gsem, ssem):
    b = pl.program_id(0)
    n_used = n_used_ref[0]
    slot = b % 2

    def gather_row(idx_ref, r, to_slot):
        return pltpu.make_async_copy(h_hbm.at[pl.ds(idx_ref[0, 0, r], 1), :],
                                     xbuf.at[to_slot, pl.ds(r, 1), :], gsem.at[to_slot])

    def scatter_row(r, from_slot):
        return pltpu.make_async_copy(ybuf.at[from_slot, pl.ds(r, 1), :],
                                     y_hbm.at[pl.ds(dst_ref[0, 0, r], 1), :], ssem.at[from_slot])

    def all_rows_gathered(of_slot):
        return pltpu.make_async_copy(h_hbm.at[pl.ds(0, RB), :], xbuf.at[of_slot], gsem.at[of_slot])

    def all_rows_scattered(of_slot):
        return pltpu.make_async_copy(ybuf.at[of_slot], y_hbm.at[pl.ds(0, RB), :], ssem.at[of_slot])

    @pl.when((b == 0) & (n_used > 0))
    def _():
        for r in range(RB):
            gather_row(src_ref, r, 0).start()

    new_expert = (b == 0) | (blk_exp_ref[b] != blk_exp_ref[jnp.maximum(b - 1, 0)])

    @pl.when((b < n_used) & new_expert)
    def _():
        wg_bf[...] = wg_ref[0].astype(BF16)
        wu_bf[...] = wu_ref[0].astype(BF16)
        wd_bf[...] = wd_ref[0].astype(BF16)

    @pl.when((b < n_used) & (b >= 2))
    def _():
        all_rows_scattered(slot).wait()

    @pl.when(b < n_used)
    def _():
        all_rows_gathered(slot).wait()
        for r in range(RB):
            gather_row(src_next_ref, r, 1 - slot).start()
        xb = xbuf[slot].astype(BF16)
        g = jnp.dot(xb, wg_bf[...], preferred_element_type=F32)
        u = jnp.dot(xb, wu_bf[...], preferred_element_type=F32)
        hid = (g * _sigmoid(g) * u).astype(BF16)
        ybuf[slot] = jnp.dot(hid, wd_bf[...], preferred_element_type=F32)
        for r in range(RB):
            scatter_row(r, slot).start()

    @pl.when((b == n_used - 1) & (b >= 1))
    def _():
        all_rows_scattered(1 - slot).wait()

    @pl.when(b == n_used - 1)
    def _():
        all_rows_scattered(slot).wait()
        all_rows_gathered(1 - slot).wait()
        ybuf[slot] = jnp.zeros((RB, D_MODEL), F32)
        surplus = pltpu.make_async_copy(ybuf.at[slot], y_hbm.at[pl.ds(y_hbm.shape[0] - RB, RB), :],
                                        ssem.at[slot])
        surplus.start()
        surplus.wait()


def _experts(blk_exp, n_used, row_src, row_dst, h, wg, wu, wd, n_out_rows):
    n_blk = row_src.shape[0]
    smem_blk = lambda f: pl.BlockSpec((1, 1, RB), f, memory_space=pltpu.SMEM)
    grid_spec = pltpu.PrefetchScalarGridSpec(
        num_scalar_prefetch=2,
        grid=(n_blk,),
        in_specs=[
            smem_blk(lambda b, be, nu: (b, 0, 0)),
            smem_blk(lambda b, be, nu: (jnp.minimum(b + 1, n_blk - 1), 0, 0)),
            smem_blk(lambda b, be, nu: (b, 0, 0)),
            pl.BlockSpec(memory_space=pl.ANY),
            pl.BlockSpec((1, D_MODEL, D_EXPERT), lambda b, be, nu: (be[b], 0, 0)),
            pl.BlockSpec((1, D_MODEL, D_EXPERT), lambda b, be, nu: (be[b], 0, 0)),
            pl.BlockSpec((1, D_EXPERT, D_MODEL), lambda b, be, nu: (be[b], 0, 0)),
        ],
        out_specs=pl.BlockSpec(memory_space=pl.ANY),
        scratch_shapes=[
            pltpu.VMEM((D_MODEL, D_EXPERT), BF16),
            pltpu.VMEM((D_MODEL, D_EXPERT), BF16),
            pltpu.VMEM((D_EXPERT, D_MODEL), BF16),
            pltpu.VMEM((2, RB, D_MODEL), F32),
            pltpu.VMEM((2, RB, D_MODEL), F32),
            pltpu.SemaphoreType.DMA((2,)),
            pltpu.SemaphoreType.DMA((2,)),
        ],
    )
    return pl.pallas_call(
        _expert_kernel,
        grid_spec=grid_spec,
        out_shape=jax.ShapeDtypeStruct((n_out_rows, D_MODEL), F32),
        compiler_params=pltpu.CompilerParams(
            dimension_semantics=("arbitrary",), vmem_limit_bytes=VMEM_LIMIT),
        name="expert_mlp",
    )(blk_exp, n_used, row_src, row_src, row_dst, h, wg, wu, wd)


def _combine_kernel(x1_ref, y0_ref, y1_ref, w_ref, o_ref):
    o_ref[...] = x1_ref[...] + (y0_ref[...] * w_ref[:, 0:1] + y1_ref[...] * w_ref[:, 1:2])


def _combine(x1, y, wts):
    T = x1.shape[0]
    tm = TM_COMB
    row = lambda i: (i, 0)
    return pl.pallas_call(
        _combine_kernel,
        grid=(T // tm,),
        in_specs=[
            pl.BlockSpec((tm, D_MODEL), row),
            pl.BlockSpec((tm, D_MODEL), row),
            pl.BlockSpec((tm, D_MODEL), lambda i: (i + T // tm, 0)),
            pl.BlockSpec((tm, 2), row),
        ],
        out_specs=pl.BlockSpec((tm, D_MODEL), row),
        out_shape=jax.ShapeDtypeStruct((T, D_MODEL), F32),
        compiler_params=pltpu.CompilerParams(
            dimension_semantics=("parallel",), vmem_limit_bytes=VMEM_LIMIT),
        name="moe_combine",
    )(x1, y, y, wts)


def _sorted_row_plan(dest, blk, seg, n_rows):
    T = dest.shape[1]
    a = jnp.arange(2 * T, dtype=jnp.int32)
    _, a_sorted = lax.sort_key_val(dest.reshape(2 * T), a)
    cnt, pstart, start = seg[0, :, 0], seg[1, :, 0], seg[2, :, 0]
    r = jnp.arange(n_rows, dtype=jnp.int32)
    e_r = blk[0, :n_rows // RB][r // RB]
    off = r - pstart[e_r]
    valid = off < cnt[e_r]
    a_r = a_sorted[jnp.clip(start[e_r] + off, 0, 2 * T - 1)]
    src = jnp.where(valid, a_r % T, 0)
    dst = jnp.where(valid, a_r, 2 * T + r % RB)
    return src.reshape(n_rows // RB, 1, RB), dst.reshape(n_rows // RB, 1, RB)


def kernel(x, attn_norm_g, w_in, conv_dw_w, conv_dw_b, conv_ln_g, conv_ln_b, w_conv_out,
           q_norm_g, k_norm_g, lambda_q1, lambda_k1, lambda_q2, lambda_k2, subln_g,
           w_attn_out, w_out, ffn_norm_g, w_router_group, b_router_group,
           w_router_expert, b_router_expert, w_gate_e, w_up_e, w_down_e):
    B, S, D = x.shape
    T = B * S
    l = 0
    x2 = x.reshape(T, D)

    reps = COL_Q // DA_HEAD_DIM
    qg = jnp.tile(q_norm_g[l], reps).reshape(1, COL_Q)
    kg = jnp.tile(k_norm_g[l], reps).reshape(1, COL_K)
    grp = jnp.arange(NORM_BLOCK, dtype=jnp.int32) // DA_HEAD_DIM
    bd = jnp.where(grp[:, None] == grp[None, :], 1.0 / DA_HEAD_DIM, 0.0).astype(BF16)

    glu, q, k, v, gates = _in_projection(
        x2, attn_norm_g[l].reshape(1, D), w_in[l].astype(BF16), qg, kg, bd)

    conv_act = _conv_branch(glu.reshape(B, S, CONV_CH), conv_dw_w[l],
                            conv_dw_b[l].reshape(1, CONV_CH), conv_ln_g[l].reshape(1, CONV_CH),
                            conv_ln_b[l].reshape(1, CONV_CH))

    lam_vecs = jnp.stack([lambda_q1[l], lambda_k1[l], lambda_q2[l], lambda_k2[l]]).astype(F32)
    attn_o = _diff_attention(q.reshape(B, S, COL_Q), k.reshape(B, S, COL_K),
                             v.reshape(B, S, COL_V), lam_vecs, subln_g[l].reshape(1, DA_V_DIM))

    e_lo, e_hi = ROUTER_EXPERT_COL, ROUTER_EXPERT_COL + N_EXPERTS
    wr = jnp.zeros((D, ROUTER_LANES), F32)
    wr = wr.at[:, :N_GROUPS].set(w_router_group[l]).at[:, e_lo:e_hi].set(w_router_expert[l])
    br = jnp.zeros((1, ROUTER_LANES), F32)
    br = br.at[0, :N_GROUPS].set(b_router_group[l]).at[0, e_lo:e_hi].set(b_router_expert[l])

    wr_hi = wr.astype(BF16)
    wr_lo = (wr - wr_hi.astype(F32)).astype(BF16)
    x1, h2, logits = _merge(conv_act.reshape(T, CONV_CH), attn_o.reshape(T, ATTN_W), gates, x2,
                            w_conv_out[l].astype(BF16), w_attn_out[l].astype(BF16),
                            w_out[l].astype(BF16), ffn_norm_g[l].reshape(1, D),
                            jnp.concatenate([wr_hi, wr_lo], axis=1), br)

    dest, wts, blk, seg = _route(logits)
    n_rows = 2 * T + N_EXPERTS * RB
    n_blk = n_rows // RB
    row_src, row_dst = _sorted_row_plan(dest, blk, seg, n_rows)
    n_used = seg[3, N_EXPERTS - 1, 0:1] // RB

    y = _experts(blk[0, :n_blk], n_used, row_src, row_dst, h2, w_gate_e[l], w_up_e[l], w_down_e[l],
                 2 * T + RB)

    out = _combine(x1, y, wts.T)
    return out.reshape(B, S, D)
```

```python
import functools
import math

import jax
import jax.numpy as jnp
from jax import lax
from jax.experimental import pallas as pl
from jax.experimental.pallas import tpu as pltpu

F32 = jnp.float32
BF16 = jnp.bfloat16

D_MODEL = 1024
CONV_CH = 512
CONV_WIDTH = 31
DA_HEADS = 4
DA_HEAD_DIM = 64
DA_V_DIM = 128
ATTN_W = 512
N_GROUPS = 4
EXPERTS_PER_GROUP = 8
N_EXPERTS = 32
D_EXPERT = 512
EPS = 1e-6
LAM_INIT = 0.8 - 0.6 * math.exp(-0.3 * 0)

COL_GLU = 2 * CONV_CH
COL_Q = 512
COL_K = 512
COL_V = 512
COL_GATE = 2 * D_MODEL
OFF_Q = COL_GLU
OFF_K = OFF_Q + COL_Q
OFF_V = OFF_K + COL_K
OFF_GATE = OFF_V + COL_V
IN_COLS = OFF_GATE + COL_GATE

ROUTER_LANES = 128
ROUTER_EXPERT_COL = 8
SUBLANES = 8
NORM_BLOCK = 256
HALO = 32

TM_PROJ = 512
TS_CONV = 512
CONV_CHUNK = 64
ATT_TQ = 512
ATT_TK = 512
ATT_RC = 256
LANES = 128
RB = 512
TM_COMB = 512
TM_ROUTE = 512

VMEM_LIMIT = 48 * 1024 * 1024


def _sigmoid(x):
    return 1.0 / (1.0 + jnp.exp(-x))


def _const_spec(shape):
    nd = len(shape)
    return pl.BlockSpec(shape, lambda *_: (0,) * nd, pipeline_mode=pl.Buffered(1))


def _group_mean_sq(t, bd_ref):
    sq = t * t
    hi = sq.astype(BF16)
    lo = (sq - hi.astype(F32)).astype(BF16)
    w = bd_ref.shape[0]
    parts = [jnp.dot(hi[:, c:c + w], bd_ref[...], preferred_element_type=F32)
             + jnp.dot(lo[:, c:c + w], bd_ref[...], preferred_element_type=F32)
             for c in range(0, t.shape[1], w)]
    return jnp.concatenate(parts, axis=1)


def _inproj_kernel(x_ref, g_ref, w_ref, qg_ref, kg_ref, bd_ref,
                   glu_ref, q_ref, k_ref, v_ref, gate_ref):
    x = x_ref[...]
    ms = jnp.mean(x * x, axis=-1, keepdims=True)
    h = (x * lax.rsqrt(ms + EPS) * g_ref[...]).astype(BF16)

    def proj(lo, width):
        return jnp.dot(h, w_ref[:, lo:lo + width], preferred_element_type=F32)

    a = proj(0, CONV_CH)
    gt = proj(CONV_CH, CONV_CH)
    glu_ref[...] = (a * _sigmoid(gt)).astype(BF16)

    q = proj(OFF_Q, COL_Q)
    qn = q * lax.rsqrt(_group_mean_sq(q, bd_ref) + EPS) * qg_ref[...]
    q_ref[...] = (qn * (DA_HEAD_DIM ** -0.5)).astype(BF16)

    k = proj(OFF_K, COL_K)
    kn = k * lax.rsqrt(_group_mean_sq(k, bd_ref) + EPS) * kg_ref[...]
    k_ref[...] = kn.astype(BF16)

    v_ref[...] = proj(OFF_V, COL_V).astype(BF16)

    for c in range(COL_GATE // 512):
        gate_ref[:, c * 512:(c + 1) * 512] = _sigmoid(proj(OFF_GATE + c * 512, 512)).astype(BF16)


def _in_projection(x2, g, w_bf, qg, kg, bd):
    T = x2.shape[0]
    tm = TM_PROJ
    row = lambda i: (i, 0)
    out_shape = (
        jax.ShapeDtypeStruct((T, CONV_CH), BF16),
        jax.ShapeDtypeStruct((T, COL_Q), BF16),
        jax.ShapeDtypeStruct((T, COL_K), BF16),
        jax.ShapeDtypeStruct((T, COL_V), BF16),
        jax.ShapeDtypeStruct((T, COL_GATE), BF16),
    )
    return pl.pallas_call(
        _inproj_kernel,
        grid=(T // tm,),
        in_specs=[
            pl.BlockSpec((tm, D_MODEL), row),
            _const_spec((1, D_MODEL)),
            _const_spec((D_MODEL, IN_COLS)),
            _const_spec((1, COL_Q)),
            _const_spec((1, COL_K)),
            _const_spec((NORM_BLOCK, NORM_BLOCK)),
        ],
        out_specs=(
            pl.BlockSpec((tm, CONV_CH), row),
            pl.BlockSpec((tm, COL_Q), row),
            pl.BlockSpec((tm, COL_K), row),
            pl.BlockSpec((tm, COL_V), row),
            pl.BlockSpec((tm, COL_GATE), row),
        ),
        out_shape=out_shape,
        compiler_params=pltpu.CompilerParams(
            dimension_semantics=("parallel",), vmem_limit_bytes=VMEM_LIMIT),
        name="in_projection",
    )(x2, g, w_bf, qg, kg, bd)


def _conv_kernel(cur_ref, halo_ref, w_ref, b_ref, lng_ref, lnb_ref, out_ref, buf_ref):
    i = pl.program_id(1)
    ts = cur_ref.shape[1]
    halo = halo_ref[0].astype(F32)
    buf_ref[0:HALO, :] = jnp.where(i > 0, halo, 0.0)
    buf_ref[HALO:HALO + ts, :] = cur_ref[0].astype(F32)

    first = HALO - (CONV_WIDTH - 1)
    for c in range(ts // CONV_CHUNK):
        r0 = c * CONV_CHUNK
        acc = jnp.broadcast_to(b_ref[...], (CONV_CHUNK, CONV_CH))
        for j in range(CONV_WIDTH):
            acc = acc + w_ref[j:j + 1, :] * buf_ref[r0 + first + j:r0 + first + j + CONV_CHUNK, :]
        mu = jnp.mean(acc, axis=-1, keepdims=True)
        d = acc - mu
        var = jnp.mean(d * d, axis=-1, keepdims=True)
        y = d * lax.rsqrt(var + EPS) * lng_ref[...] + lnb_ref[...]
        out_ref[0, r0:r0 + CONV_CHUNK, :] = (y * _sigmoid(y)).astype(BF16)


def _conv_branch(glu3, dw_w, dw_b, ln_g, ln_b):
    B, S, C = glu3.shape
    ts = TS_CONV
    per = ts // HALO
    return pl.pallas_call(
        _conv_kernel,
        grid=(B, S // ts),
        in_specs=[
            pl.BlockSpec((1, ts, C), lambda b, i: (b, i, 0)),
            pl.BlockSpec((1, HALO, C), lambda b, i: (b, jnp.maximum(i * per - 1, 0), 0)),
            _const_spec((CONV_WIDTH, C)),
            _const_spec((1, C)),
            _const_spec((1, C)),
            _const_spec((1, C)),
        ],
        out_specs=pl.BlockSpec((1, ts, C), lambda b, i: (b, i, 0)),
        out_shape=jax.ShapeDtypeStruct((B, S, C), BF16),
        scratch_shapes=[pltpu.VMEM((HALO + ts, C), F32)],
        compiler_params=pltpu.CompilerParams(
            dimension_semantics=("parallel", "parallel"), vmem_limit_bytes=VMEM_LIMIT),
        name="conv_branch",
    )(glu3, glu3, dw_w, dw_b, ln_g, ln_b)


def _attn_kernel(q_ref, k_ref, v_ref, lam_ref, sg_ref, o_ref, qs_ref, m_ref, acc_ref, s_ref, *, tq, tk):
    qi = pl.program_id(2)
    q = q_ref[0]
    lane = lax.broadcasted_iota(jnp.int32, q.shape, 1)
    zero = jnp.zeros_like(q)
    qs_ref[0:tq, :] = jnp.where(lane < DA_HEAD_DIM, q, zero)
    qs_ref[tq:2 * tq, :] = jnp.where(lane >= DA_HEAD_DIM, q, zero)
    m_ref[...] = jnp.full(m_ref.shape, -jnp.inf, F32)
    acc_ref[...] = jnp.zeros(acc_ref.shape, F32)
    ones = jnp.ones((tk, LANES), BF16)
    rc = ATT_RC
    chunks = [slice(r0, r0 + rc) for r0 in range(0, 2 * tq, rc)]

    def scores(j, rows):
        kj = k_ref[0, pl.ds(pl.multiple_of(j * tk, tk), tk), :]
        return lax.dot_general(qs_ref[rows, :], kj, (((1,), (1,)), ((), ())),
                               preferred_element_type=F32)

    def step(j, masked, prefetch):
        vj = jnp.concatenate([v_ref[0, pl.ds(pl.multiple_of(j * tk, tk), tk), :], ones], axis=1)
        for rows in chunks:
            r_lo = rows.start % tq
            kw = min(tk, -(-(r_lo + rc) // (2 * LANES)) * (2 * LANES)) if masked else tk
            s = s_ref[rows, 0:kw]
            if prefetch:
                s_ref[rows, :] = scores(j + 1, rows)
            if masked:
                r = lax.broadcasted_iota(jnp.int32, s.shape, 0) + r_lo
                c = lax.broadcasted_iota(jnp.int32, s.shape, 1)
                s = jnp.where(c <= r, s, -jnp.inf)
            m_old = m_ref[rows, :]
            m_new = jnp.maximum(m_old, jnp.max(s, axis=-1, keepdims=True))
            alpha = jnp.exp(m_old - m_new)
            p = jnp.exp(s - jnp.tile(m_new, (1, kw // LANES)))
            pv = jnp.dot(p.astype(BF16), vj[0:kw, :], preferred_element_type=F32)
            acc_ref[rows, :] = jnp.tile(alpha, (1, 2)) * acc_ref[rows, :] + pv
            m_ref[rows, :] = m_new

    for rows in chunks:
        s_ref[rows, :] = scores(0, rows)

    def body(j, carry):
        step(j, False, True)
        return carry

    lax.fori_loop(0, qi, body, 0)
    step(qi, True, False)

    lam = (jnp.exp(jnp.sum(lam_ref[0:1, :] * lam_ref[1:2, :], axis=-1, keepdims=True))
           - jnp.exp(jnp.sum(lam_ref[2:3, :] * lam_ref[3:4, :], axis=-1, keepdims=True))
           + LAM_INIT)
    o1 = acc_ref[0:tq, 0:LANES] / acc_ref[0:tq, LANES:2 * LANES]
    o2 = acc_ref[tq:2 * tq, 0:LANES] / acc_ref[tq:2 * tq, LANES:2 * LANES]
    o = o1 - lam * o2
    ms = jnp.mean(o * o, axis=-1, keepdims=True)
    on = o * lax.rsqrt(ms + EPS) * sg_ref[...]
    o_ref[0] = (on * (1.0 - LAM_INIT)).astype(BF16)


def _diff_attention(q3, k3, v3, lam_vecs, subln_g):
    B, S, _ = q3.shape
    tq, tk = ATT_TQ, ATT_TK
    return pl.pallas_call(
        functools.partial(_attn_kernel, tq=tq, tk=tk),
        grid=(B, DA_HEADS, S // tq),
        in_specs=[
            pl.BlockSpec((1, tq, DA_V_DIM), lambda b, h, i: (b, i, h)),
            pl.BlockSpec((1, S, DA_V_DIM), lambda b, h, i: (b, 0, h)),
            pl.BlockSpec((1, S, DA_V_DIM), lambda b, h, i: (b, 0, h)),
            _const_spec((4, DA_HEAD_DIM)),
            _const_spec((1, DA_V_DIM)),
        ],
        out_specs=pl.BlockSpec((1, tq, DA_V_DIM), lambda b, h, i: (b, i, h)),
        out_shape=jax.ShapeDtypeStruct((B, S, ATTN_W), BF16),
        scratch_shapes=[
            pltpu.VMEM((2 * tq, DA_V_DIM), BF16),
            pltpu.VMEM((2 * tq, LANES), F32),
            pltpu.VMEM((2 * tq, 2 * LANES), F32),
            pltpu.VMEM((2 * tq, tk), F32),
        ],
        compiler_params=pltpu.CompilerParams(
            dimension_semantics=("parallel", "parallel", "parallel"),
            vmem_limit_bytes=VMEM_LIMIT),
        name="diff_attention",
    )(q3, k3, v3, lam_vecs, subln_g)


def _merge_kernel(conv_ref, attn_ref, gate_ref, x_ref, wc_ref, wa_ref, wo_ref, fg_ref,
                  wr_ref, br_ref, x1_ref, h2_ref, logit_ref):
    c = jnp.dot(conv_ref[...], wc_ref[...], preferred_element_type=F32)
    a = jnp.dot(attn_ref[...], wa_ref[...], preferred_element_type=F32)
    g0 = gate_ref[:, 0:D_MODEL].astype(F32)
    g1 = gate_ref[:, D_MODEL:2 * D_MODEL].astype(F32)
    merged = (g0 * c + g1 * a).astype(BF16)
    x1 = x_ref[...] + jnp.dot(merged, wo_ref[...], preferred_element_type=F32)
    x1_ref[...] = x1
    ms = jnp.mean(x1 * x1, axis=-1, keepdims=True)
    h2 = x1 * lax.rsqrt(ms + EPS) * fg_ref[...]
    h2_ref[...] = h2
    h2_hi = h2.astype(BF16)
    h2_lo = (h2 - h2_hi.astype(F32)).astype(BF16)
    r = (jnp.dot(h2_hi, wr_ref[...], preferred_element_type=F32)
         + jnp.dot(h2_lo, wr_ref[...], preferred_element_type=F32))
    logit_ref[...] = r[:, 0:ROUTER_LANES] + r[:, ROUTER_LANES:2 * ROUTER_LANES] + br_ref[...]


def _merge(conv_act, attn_o, gates, x2, wc, wa, wo, fg, wr, br):
    T = x2.shape[0]
    tm = TM_PROJ
    row = lambda i: (i, 0)
    return pl.pallas_call(
        _merge_kernel,
        grid=(T // tm,),
        in_specs=[
            pl.BlockSpec((tm, CONV_CH), row),
            pl.BlockSpec((tm, ATTN_W), row),
            pl.BlockSpec((tm, COL_GATE), row),
            pl.BlockSpec((tm, D_MODEL), row),
            _const_spec((CONV_CH, D_MODEL)),
            _const_spec((ATTN_W, D_MODEL)),
            _const_spec((D_MODEL, D_MODEL)),
            _const_spec((1, D_MODEL)),
            _const_spec((D_MODEL, 2 * ROUTER_LANES)),
            _const_spec((1, ROUTER_LANES)),
        ],
        out_specs=(
            pl.BlockSpec((tm, D_MODEL), row),
            pl.BlockSpec((tm, D_MODEL), row),
            pl.BlockSpec((tm, ROUTER_LANES), row),
        ),
        out_shape=(
            jax.ShapeDtypeStruct((T, D_MODEL), F32),
            jax.ShapeDtypeStruct((T, D_MODEL), F32),
            jax.ShapeDtypeStruct((T, ROUTER_LANES), F32),
        ),
        compiler_params=pltpu.CompilerParams(
            dimension_semantics=("parallel",), vmem_limit_bytes=VMEM_LIMIT),
        name="merge_out_router",
    )(conv_act, attn_o, gates, x2, wc, wa, wo, fg, wr, br)


def _route_kernel(logit_ref, tri_ref, low_ref, dest_ref, wts_ref, blk_ref, seg_ref,
                  cnt_ref, base_ref, *, n_blk_lanes):
    ph = pl.program_id(0)
    i = pl.program_id(1)
    tm = logit_ref.shape[0]
    lt = logit_ref[...].T
    row8 = lax.broadcasted_iota(jnp.int32, (SUBLANES, tm), 0)

    g = jnp.where(row8 < N_GROUPS, lt[0:SUBLANES, :], -jnp.inf)
    g_max = jnp.max(g, axis=0, keepdims=True)
    g_sum = jnp.sum(jnp.exp(g - g_max), axis=0, keepdims=True)
    g_sel = jnp.min(jnp.where(g == g_max, row8, SUBLANES), axis=0, keepdims=True)
    g_w = 1.0 / g_sum

    e_sel = jnp.zeros((EXPERTS_PER_GROUP, tm), F32)
    for gi in range(N_GROUPS):
        lo = ROUTER_EXPERT_COL + gi * EXPERTS_PER_GROUP
        e_sel = jnp.where(g_sel == gi, lt[lo:lo + EXPERTS_PER_GROUP, :], e_sel)
    e_max = jnp.max(e_sel, axis=0, keepdims=True)
    e_exp = jnp.exp(e_sel - e_max)
    prob = e_exp / jnp.sum(e_exp, axis=0, keepdims=True)
    p1 = jnp.max(prob, axis=0, keepdims=True)
    i1 = jnp.min(jnp.where(prob == p1, row8, SUBLANES), axis=0, keepdims=True)
    rest = jnp.where(row8 == i1, -1.0, prob)
    p2 = jnp.max(rest, axis=0, keepdims=True)
    i2 = jnp.min(jnp.where(rest == p2, row8, SUBLANES), axis=0, keepdims=True)
    denom = p1 + p2
    e0 = g_sel * EXPERTS_PER_GROUP + i1
    e1 = g_sel * EXPERTS_PER_GROUP + i2

    row32 = lax.broadcasted_iota(jnp.int32, (N_EXPERTS, tm), 0)
    oh0 = row32 == e0
    oh1 = row32 == e1
    ohs = jnp.where(oh0 | oh1, 1.0, 0.0)
    tile_cnt = jnp.sum(ohs, axis=1, keepdims=True)

    @pl.when((ph == 0) & (i == 0))
    def _():
        cnt_ref[...] = jnp.zeros(cnt_ref.shape, F32)

    @pl.when(ph == 0)
    def _():
        cnt_ref[...] += tile_cnt

    @pl.when((ph == 1) & (i == 0))
    def _():
        cnt = cnt_ref[...]
        padded = jnp.ceil(cnt * (1.0 / RB)) * RB
        pstart = jnp.dot(low_ref[...], padded, preferred_element_type=F32,
                         precision=lax.Precision.HIGHEST)
        start = jnp.dot(low_ref[...], cnt, preferred_element_type=F32,
                        precision=lax.Precision.HIGHEST)
        base_ref[...] = pstart
        pad_end = pstart + padded
        lane_row = lax.broadcasted_iota(jnp.int32, (N_EXPERTS, n_blk_lanes), 1).astype(F32) * RB
        before = jnp.where(pad_end[:, 0:1] <= lane_row, 1.0, 0.0)
        blk = jnp.minimum(jnp.sum(before, axis=0, keepdims=True), N_EXPERTS - 1.0)
        blk_ref[...] = blk.astype(jnp.int32)
        seg_ref[0] = cnt.astype(jnp.int32)
        seg_ref[1] = pstart.astype(jnp.int32)
        seg_ref[2] = start.astype(jnp.int32)
        seg_ref[3] = pad_end.astype(jnp.int32)

    @pl.when(ph == 1)
    def _():
        before = jnp.dot(ohs.astype(BF16), tri_ref[...], preferred_element_type=F32)
        pos = base_ref[:, 0:1] + before
        d0 = jnp.sum(jnp.where(oh0, pos, 0.0), axis=0, keepdims=True)
        d1 = jnp.sum(jnp.where(oh1, pos, 0.0), axis=0, keepdims=True)
        dest_ref[0:1, :] = d0.astype(jnp.int32)
        dest_ref[1:2, :] = d1.astype(jnp.int32)
        wts_ref[0:1, :] = p1 / denom * g_w
        wts_ref[1:2, :] = p2 / denom * g_w
        base_ref[...] += tile_cnt


def _route(logits):
    T = logits.shape[0]
    tm = TM_ROUTE
    n_rows = 2 * T + N_EXPERTS * RB
    n_blk = n_rows // RB
    n_blk_lanes = -(-n_blk // LANES) * LANES
    r = jnp.arange(tm, dtype=jnp.int32)
    tri = (r[:, None] < r[None, :]).astype(BF16)
    e = jnp.arange(N_EXPERTS, dtype=jnp.int32)
    low = (e[None, :] < e[:, None]).astype(F32)
    dest, wts, blk, seg = pl.pallas_call(
        functools.partial(_route_kernel, n_blk_lanes=n_blk_lanes),
        grid=(2, T // tm),
        in_specs=[
            pl.BlockSpec((tm, ROUTER_LANES), lambda ph, i: (i, 0)),
            _const_spec((tm, tm)),
            _const_spec((N_EXPERTS, N_EXPERTS)),
        ],
        out_specs=(
            pl.BlockSpec((2, tm), lambda ph, i: (0, i * ph)),
            pl.BlockSpec((2, tm), lambda ph, i: (0, i * ph)),
            pl.BlockSpec((1, n_blk_lanes), lambda ph, i: (0, 0)),
            pl.BlockSpec((4, N_EXPERTS, LANES), lambda ph, i: (0, 0, 0)),
        ),
        out_shape=(
            jax.ShapeDtypeStruct((2, T), jnp.int32),
            jax.ShapeDtypeStruct((2, T), F32),
            jax.ShapeDtypeStruct((1, n_blk_lanes), jnp.int32),
            jax.ShapeDtypeStruct((4, N_EXPERTS, LANES), jnp.int32),
        ),
        scratch_shapes=[pltpu.VMEM((N_EXPERTS, LANES), F32), pltpu.VMEM((N_EXPERTS, LANES), F32)],
        compiler_params=pltpu.CompilerParams(
            dimension_semantics=("arbitrary", "arbitrary"), vmem_limit_bytes=VMEM_LIMIT),
        name="route_plan",
    )(logits, tri, low)
    return dest, wts, blk, seg


def _expert_kernel(blk_exp_ref, n_used_ref, src_ref, src_next_ref, dst_ref,
                   h_hbm, wg_ref, wu_ref, wd_ref, y_hbm,
                   wg_bf, wu_bf, wd_bf, xbuf, ybuf, gsem, ssem):
    b = pl.program_id(0)
    n_used = n_used_ref[0]
    slot = b % 2

    def gather_row(idx_ref, r, to_slot):
        return pltpu.make_async_copy(h_hbm.at[pl.ds(idx_ref[0, 0, r], 1), :],
                                     xbuf.at[to_slot, pl.ds(r, 1), :], gsem.at[to_slot])

    def scatter_row(r, from_slot):
        return pltpu.make_async_copy(ybuf.at[from_slot, pl.ds(r, 1), :],
                                     y_hbm.at[pl.ds(dst_ref[0, 0, r], 1), :], ssem.at[from_slot])

    def all_rows_gathered(of_slot):
        return pltpu.make_async_copy(h_hbm.at[pl.ds(0, RB), :], xbuf.at[of_slot], gsem.at[of_slot])

    def all_rows_scattered(of_slot):
        return pltpu.make_async_copy(ybuf.at[of_slot], y_hbm.at[pl.ds(0, RB), :], ssem.at[of_slot])

    @pl.when((b == 0) & (n_used > 0))
    def _():
        for r in range(RB):
            gather_row(src_ref, r, 0).start(priority=r % 2)

    new_expert = (b == 0) | (blk_exp_ref[b] != blk_exp_ref[jnp.maximum(b - 1, 0)])

    @pl.when((b < n_used) & new_expert)
    def _():
        wg_bf[...] = wg_ref[0].astype(BF16)
        wu_bf[...] = wu_ref[0].astype(BF16)
        wd_bf[...] = wd_ref[0].astype(BF16)

    @pl.when((b < n_used) & (b >= 2))
    def _():
        all_rows_scattered(slot).wait()

    @pl.when(b < n_used)
    def _():
        all_rows_gathered(slot).wait()
        for r in range(RB):
            gather_row(src_next_ref, r, 1 - slot).start(priority=r % 2)
        xb = xbuf[slot].astype(BF16)
        g = jnp.dot(xb, wg_bf[...], preferred_element_type=F32)
        u = jnp.dot(xb, wu_bf[...], preferred_element_type=F32)
        hid = (g * _sigmoid(g) * u).astype(BF16)
        ybuf[slot] = jnp.dot(hid, wd_bf[...], preferred_element_type=F32)
        for r in range(RB):
            scatter_row(r, slot).start(priority=r % 2)

    @pl.when((b == n_used - 1) & (b >= 1))
    def _():
        all_rows_scattered(1 - slot).wait()

    @pl.when(b == n_used - 1)
    def _():
        all_rows_scattered(slot).wait()
        all_rows_gathered(1 - slot).wait()
        ybuf[slot] = jnp.zeros((RB, D_MODEL), F32)
        surplus = pltpu.make_async_copy(ybuf.at[slot], y_hbm.at[pl.ds(y_hbm.shape[0] - RB, RB), :],
                                        ssem.at[slot])
        surplus.start()
        surplus.wait()


def _experts(blk_exp, n_used, row_src, row_dst, h, wg, wu, wd, n_out_rows):
    n_blk = row_src.shape[0]
    smem_blk = lambda f: pl.BlockSpec((1, 1, RB), f, memory_space=pltpu.SMEM)
    grid_spec = pltpu.PrefetchScalarGridSpec(
        num_scalar_prefetch=2,
        grid=(n_blk,),
        in_specs=[
            smem_blk(lambda b, be, nu: (b, 0, 0)),
            smem_blk(lambda b, be, nu: (jnp.minimum(b + 1, n_blk - 1), 0, 0)),
            smem_blk(lambda b, be, nu: (b, 0, 0)),
            pl.BlockSpec(memory_space=pl.ANY),
            pl.BlockSpec((1, D_MODEL, D_EXPERT), lambda b, be, nu: (be[b], 0, 0)),
            pl.BlockSpec((1, D_MODEL, D_EXPERT), lambda b, be, nu: (be[b], 0, 0)),
            pl.BlockSpec((1, D_EXPERT, D_MODEL), lambda b, be, nu: (be[b], 0, 0)),
        ],
        out_specs=pl.BlockSpec(memory_space=pl.ANY),
        scratch_shapes=[
            pltpu.VMEM((D_MODEL, D_EXPERT), BF16),
            pltpu.VMEM((D_MODEL, D_EXPERT), BF16),
            pltpu.VMEM((D_EXPERT, D_MODEL), BF16),
            pltpu.VMEM((2, RB, D_MODEL), F32),
            pltpu.VMEM((2, RB, D_MODEL), F32),
            pltpu.SemaphoreType.DMA((2,)),
            pltpu.SemaphoreType.DMA((2,)),
        ],
    )
    return pl.pallas_call(
        _expert_kernel,
        grid_spec=grid_spec,
        out_shape=jax.ShapeDtypeStruct((n_out_rows, D_MODEL), F32),
        compiler_params=pltpu.CompilerParams(
            dimension_semantics=("arbitrary",), vmem_limit_bytes=VMEM_LIMIT),
        name="expert_mlp",
    )(blk_exp, n_used, row_src, row_src, row_dst, h, wg, wu, wd)


def _combine_kernel(x1_ref, y0_ref, y1_ref, w_ref, o_ref):
    o_ref[...] = x1_ref[...] + (y0_ref[...] * w_ref[:, 0:1] + y1_ref[...] * w_ref[:, 1:2])


def _combine(x1, y, wts):
    T = x1.shape[0]
    tm = TM_COMB
    row = lambda i: (i, 0)
    return pl.pallas_call(
        _combine_kernel,
        grid=(T // tm,),
        in_specs=[
            pl.BlockSpec((tm, D_MODEL), row),
            pl.BlockSpec((tm, D_MODEL), row),
            pl.BlockSpec((tm, D_MODEL), lambda i: (i + T // tm, 0)),
            pl.BlockSpec((tm, 2), row),
        ],
        out_specs=pl.BlockSpec((tm, D_MODEL), row),
        out_shape=jax.ShapeDtypeStruct((T, D_MODEL), F32),
        compiler_params=pltpu.CompilerParams(
            dimension_semantics=("parallel",), vmem_limit_bytes=VMEM_LIMIT),
        name="moe_combine",
    )(x1, y, y, wts)


def _sorted_row_plan(dest, blk, seg, n_rows):
    T = dest.shape[1]
    n_blk = n_rows // RB
    cnt, pstart = seg[0, :, 0], seg[1, :, 0]
    e_b = blk[0, :n_blk]
    b0 = jnp.arange(n_blk, dtype=jnp.int32) * RB
    n_valid = jnp.clip(pstart[e_b] + cnt[e_b] - b0, 0, RB)
    w = jnp.arange(RB, dtype=jnp.int32)
    r = b0[:, None] + w[None, :]
    is_pad = w[None, :] >= n_valid[:, None]
    a = jnp.arange(2 * T, dtype=jnp.int32)
    keys = jnp.concatenate([dest.reshape(2 * T), jnp.where(is_pad, r, n_rows).reshape(n_rows)])
    vals = jnp.concatenate([a, jnp.broadcast_to(2 * T + w[None, :], (n_blk, RB)).reshape(n_rows)])
    _, plan = lax.sort_key_val(keys, vals)
    plan = plan[:n_rows]
    src = jnp.where(plan < 2 * T, plan % T, 0)
    return src.reshape(n_blk, 1, RB), plan.reshape(n_blk, 1, RB)


def kernel(x, attn_norm_g, w_in, conv_dw_w, conv_dw_b, conv_ln_g, conv_ln_b, w_conv_out,
           q_norm_g, k_norm_g, lambda_q1, lambda_k1, lambda_q2, lambda_k2, subln_g,
           w_attn_out, w_out, ffn_norm_g, w_router_group, b_router_group,
           w_router_expert, b_router_expert, w_gate_e, w_up_e, w_down_e):
    B, S, D = x.shape
    T = B * S
    l = 0
    x2 = x.reshape(T, D)

    reps = COL_Q // DA_HEAD_DIM
    qg = jnp.tile(q_norm_g[l], reps).reshape(1, COL_Q)
    kg = jnp.tile(k_norm_g[l], reps).reshape(1, COL_K)
    grp = jnp.arange(NORM_BLOCK, dtype=jnp.int32) // DA_HEAD_DIM
    bd = jnp.where(grp[:, None] == grp[None, :], 1.0 / DA_HEAD_DIM, 0.0).astype(BF16)

    glu, q, k, v, gates = _in_projection(
        x2, attn_norm_g[l].reshape(1, D), w_in[l].astype(BF16), qg, kg, bd)

    conv_act = _conv_branch(glu.reshape(B, S, CONV_CH), conv_dw_w[l],
                            conv_dw_b[l].reshape(1, CONV_CH), conv_ln_g[l].reshape(1, CONV_CH),
                            conv_ln_b[l].reshape(1, CONV_CH))

    lam_vecs = jnp.stack([lambda_q1[l], lambda_k1[l], lambda_q2[l], lambda_k2[l]]).astype(F32)
    attn_o = _diff_attention(q.reshape(B, S, COL_Q), k.reshape(B, S, COL_K),
                             v.reshape(B, S, COL_V), lam_vecs, subln_g[l].reshape(1, DA_V_DIM))

    e_lo, e_hi = ROUTER_EXPERT_COL, ROUTER_EXPERT_COL + N_EXPERTS
    wr = jnp.zeros((D, ROUTER_LANES), F32)
    wr = wr.at[:, :N_GROUPS].set(w_router_group[l]).at[:, e_lo:e_hi].set(w_router_expert[l])
    br = jnp.zeros((1, ROUTER_LANES), F32)
    br = br.at[0, :N_GROUPS].set(b_router_group[l]).at[0, e_lo:e_hi].set(b_router_expert[l])

    wr_hi = wr.astype(BF16)
    wr_lo = (wr - wr_hi.astype(F32)).astype(BF16)
    x1, h2, logits = _merge(conv_act.reshape(T, CONV_CH), attn_o.reshape(T, ATTN_W), gates, x2,
                            w_conv_out[l].astype(BF16), w_attn_out[l].astype(BF16),
                            w_out[l].astype(BF16), ffn_norm_g[l].reshape(1, D),
                            jnp.concatenate([wr_hi, wr_lo], axis=1), br)

    dest, wts, blk, seg = _route(logits)
    n_rows = 2 * T + N_EXPERTS * RB
    n_blk = n_rows // RB
    row_src, row_dst = _sorted_row_plan(dest, blk, seg, n_rows)
    n_used = seg[3, N_EXPERTS - 1, 0:1] // RB

    y = _experts(blk[0, :n_blk], n_used, row_src, row_dst, h2, w_gate_e[l], w_up_e[l], w_down_e[l],
                 2 * T + RB)

    out = _combine(x1, y, wts.T)
    return out.reshape(B, S, D)
```

```python
import functools
import math

import jax
import jax.numpy as jnp
from jax import lax
from jax.experimental import pallas as pl
from jax.experimental.pallas import tpu as pltpu

F32 = jnp.float32
BF16 = jnp.bfloat16

D_MODEL = 1024
CONV_CH = 512
CONV_WIDTH = 31
DA_HEADS = 4
DA_HEAD_DIM = 64
DA_V_DIM = 128
ATTN_W = 512
N_GROUPS = 4
EXPERTS_PER_GROUP = 8
N_EXPERTS = 32
D_EXPERT = 512
EPS = 1e-6
LAM_INIT = 0.8 - 0.6 * math.exp(-0.3 * 0)

COL_GLU = 2 * CONV_CH
COL_Q = 512
COL_K = 512
COL_V = 512
COL_GATE = 2 * D_MODEL
OFF_Q = COL_GLU
OFF_K = OFF_Q + COL_Q
OFF_V = OFF_K + COL_K
OFF_GATE = OFF_V + COL_V
IN_COLS = OFF_GATE + COL_GATE

ROUTER_LANES = 128
ROUTER_EXPERT_COL = 8
SUBLANES = 8
ROW_TILE = D_MODEL // 128
NORM_BLOCK = 256
HALO = 32

TM_PROJ = 512
TS_CONV = 512
CONV_CHUNK = 64
ATT_TQ = 512
ATT_TK = 512
ATT_RC = 256
LANES = 128
RB = 512
TM_COMB = 512
TM_ROUTE = 512

VMEM_LIMIT = 48 * 1024 * 1024


def _sigmoid(x):
    return 1.0 / (1.0 + jnp.exp(-x))


def _const_spec(shape):
    nd = len(shape)
    return pl.BlockSpec(shape, lambda *_: (0,) * nd, pipeline_mode=pl.Buffered(1))


def _store_row_tiles(ref, lead, val):
    rows = val.shape[0]
    for j in range(val.shape[1] // LANES):
        ref[(*lead, pl.ds(j, rows, stride=ROW_TILE), slice(None))] = val[:, j * LANES:(j + 1) * LANES]


def _load_row_tiles(ref, lead, rows, width):
    return jnp.concatenate(
        [ref[(*lead, pl.ds(j, rows, stride=ROW_TILE), slice(None))] for j in range(width // LANES)], axis=1)


def _group_mean_sq(t, bd_ref):
    sq = t * t
    hi = sq.astype(BF16)
    lo = (sq - hi.astype(F32)).astype(BF16)
    w = bd_ref.shape[0]
    parts = [jnp.dot(hi[:, c:c + w], bd_ref[...], preferred_element_type=F32)
             + jnp.dot(lo[:, c:c + w], bd_ref[...], preferred_element_type=F32)
             for c in range(0, t.shape[1], w)]
    return jnp.concatenate(parts, axis=1)


def _inproj_kernel(x_ref, g_ref, w_ref, qg_ref, kg_ref, bd_ref,
                   glu_ref, q_ref, k_ref, v_ref, gate_ref):
    x = x_ref[...]
    ms = jnp.mean(x * x, axis=-1, keepdims=True)
    h = (x * lax.rsqrt(ms + EPS) * g_ref[...]).astype(BF16)

    def proj(lo, width):
        return jnp.dot(h, w_ref[:, lo:lo + width], preferred_element_type=F32)

    a = proj(0, CONV_CH)
    gt = proj(CONV_CH, CONV_CH)
    glu_ref[...] = (a * _sigmoid(gt)).astype(BF16)

    q = proj(OFF_Q, COL_Q)
    qn = q * lax.rsqrt(_group_mean_sq(q, bd_ref) + EPS) * qg_ref[...]
    q_ref[...] = (qn * (DA_HEAD_DIM ** -0.5)).astype(BF16)

    k = proj(OFF_K, COL_K)
    kn = k * lax.rsqrt(_group_mean_sq(k, bd_ref) + EPS) * kg_ref[...]
    k_ref[...] = kn.astype(BF16)

    v_ref[...] = proj(OFF_V, COL_V).astype(BF16)

    for c in range(COL_GATE // 512):
        gate_ref[:, c * 512:(c + 1) * 512] = _sigmoid(proj(OFF_GATE + c * 512, 512)).astype(BF16)


def _in_projection(x2, g, w_bf, qg, kg, bd):
    T = x2.shape[0]
    tm = TM_PROJ
    row = lambda i: (i, 0)
    out_shape = (
        jax.ShapeDtypeStruct((T, CONV_CH), BF16),
        jax.ShapeDtypeStruct((T, COL_Q), BF16),
        jax.ShapeDtypeStruct((T, COL_K), BF16),
        jax.ShapeDtypeStruct((T, COL_V), BF16),
        jax.ShapeDtypeStruct((T, COL_GATE), BF16),
    )
    return pl.pallas_call(
        _inproj_kernel,
        grid=(T // tm,),
        in_specs=[
            pl.BlockSpec((tm, D_MODEL), row),
            _const_spec((1, D_MODEL)),
            _const_spec((D_MODEL, IN_COLS)),
            _const_spec((1, COL_Q)),
            _const_spec((1, COL_K)),
            _const_spec((NORM_BLOCK, NORM_BLOCK)),
        ],
        out_specs=(
            pl.BlockSpec((tm, CONV_CH), row),
            pl.BlockSpec((tm, COL_Q), row),
            pl.BlockSpec((tm, COL_K), row),
            pl.BlockSpec((tm, COL_V), row),
            pl.BlockSpec((tm, COL_GATE), row),
        ),
        out_shape=out_shape,
        compiler_params=pltpu.CompilerParams(
            dimension_semantics=("parallel",), vmem_limit_bytes=VMEM_LIMIT),
        name="in_projection",
    )(x2, g, w_bf, qg, kg, bd)


def _conv_kernel(cur_ref, halo_ref, w_ref, b_ref, lng_ref, lnb_ref, out_ref, buf_ref):
    i = pl.program_id(1)
    ts = cur_ref.shape[1]
    halo = halo_ref[0].astype(F32)
    buf_ref[0:HALO, :] = jnp.where(i > 0, halo, 0.0)
    buf_ref[HALO:HALO + ts, :] = cur_ref[0].astype(F32)

    first = HALO - (CONV_WIDTH - 1)
    for c in range(ts // CONV_CHUNK):
        r0 = c * CONV_CHUNK
        acc = jnp.broadcast_to(b_ref[...], (CONV_CHUNK, CONV_CH))
        for j in range(CONV_WIDTH):
            acc = acc + w_ref[j:j + 1, :] * buf_ref[r0 + first + j:r0 + first + j + CONV_CHUNK, :]
        mu = jnp.mean(acc, axis=-1, keepdims=True)
        d = acc - mu
        var = jnp.mean(d * d, axis=-1, keepdims=True)
        y = d * lax.rsqrt(var + EPS) * lng_ref[...] + lnb_ref[...]
        out_ref[0, r0:r0 + CONV_CHUNK, :] = (y * _sigmoid(y)).astype(BF16)


def _conv_branch(glu3, dw_w, dw_b, ln_g, ln_b):
    B, S, C = glu3.shape
    ts = TS_CONV
    per = ts // HALO
    return pl.pallas_call(
        _conv_kernel,
        grid=(B, S // ts),
        in_specs=[
            pl.BlockSpec((1, ts, C), lambda b, i: (b, i, 0)),
            pl.BlockSpec((1, HALO, C), lambda b, i: (b, jnp.maximum(i * per - 1, 0), 0)),
            _const_spec((CONV_WIDTH, C)),
            _const_spec((1, C)),
            _const_spec((1, C)),
            _const_spec((1, C)),
        ],
        out_specs=pl.BlockSpec((1, ts, C), lambda b, i: (b, i, 0)),
        out_shape=jax.ShapeDtypeStruct((B, S, C), BF16),
        scratch_shapes=[pltpu.VMEM((HALO + ts, C), F32)],
        compiler_params=pltpu.CompilerParams(
            dimension_semantics=("parallel", "parallel"), vmem_limit_bytes=VMEM_LIMIT),
        name="conv_branch",
    )(glu3, glu3, dw_w, dw_b, ln_g, ln_b)


def _attn_kernel(q_ref, k_ref, v_ref, lam_ref, sg_ref, o_ref, qs_ref, m_ref, acc_ref, s_ref, *, tq, tk):
    qi = pl.program_id(2)
    q = q_ref[0]
    lane = lax.broadcasted_iota(jnp.int32, q.shape, 1)
    zero = jnp.zeros_like(q)
    qs_ref[0:tq, :] = jnp.where(lane < DA_HEAD_DIM, q, zero)
    qs_ref[tq:2 * tq, :] = jnp.where(lane >= DA_HEAD_DIM, q, zero)
    m_ref[...] = jnp.full(m_ref.shape, -jnp.inf, F32)
    acc_ref[...] = jnp.zeros(acc_ref.shape, F32)
    ones = jnp.ones((tk, LANES), BF16)
    rc = ATT_RC
    chunks = [slice(r0, r0 + rc) for r0 in range(0, 2 * tq, rc)]

    def scores(j, rows):
        kj = k_ref[0, pl.ds(pl.multiple_of(j * tk, tk), tk), :]
        return lax.dot_general(qs_ref[rows, :], kj, (((1,), (1,)), ((), ())),
                               preferred_element_type=F32)

    def step(j, masked, prefetch):
        vj = jnp.concatenate([v_ref[0, pl.ds(pl.multiple_of(j * tk, tk), tk), :], ones], axis=1)
        for rows in chunks:
            r_lo = rows.start % tq
            kw = min(tk, -(-(r_lo + rc) // (2 * LANES)) * (2 * LANES)) if masked else tk
            s = s_ref[rows, 0:kw]
            if prefetch:
                s_ref[rows, :] = scores(j + 1, rows)
            if masked:
                r = lax.broadcasted_iota(jnp.int32, s.shape, 0) + r_lo
                c = lax.broadcasted_iota(jnp.int32, s.shape, 1)
                s = jnp.where(c <= r, s, -jnp.inf)
            m_old = m_ref[rows, :]
            m_new = jnp.maximum(m_old, jnp.max(s, axis=-1, keepdims=True))
            alpha = jnp.exp(m_old - m_new)
            p = jnp.exp(s - jnp.tile(m_new, (1, kw // LANES)))
            pv = jnp.dot(p.astype(BF16), vj[0:kw, :], preferred_element_type=F32)
            acc_ref[rows, :] = jnp.tile(alpha, (1, 2)) * acc_ref[rows, :] + pv
            m_ref[rows, :] = m_new

    for rows in chunks:
        s_ref[rows, :] = scores(0, rows)

    def body(j, carry):
        step(j, False, True)
        return carry

    lax.fori_loop(0, qi, body, 0)
    step(qi, True, False)

    lam = (jnp.exp(jnp.sum(lam_ref[0:1, :] * lam_ref[1:2, :], axis=-1, keepdims=True))
           - jnp.exp(jnp.sum(lam_ref[2:3, :] * lam_ref[3:4, :], axis=-1, keepdims=True))
           + LAM_INIT)
    o1 = acc_ref[0:tq, 0:LANES] / acc_ref[0:tq, LANES:2 * LANES]
    o2 = acc_ref[tq:2 * tq, 0:LANES] / acc_ref[tq:2 * tq, LANES:2 * LANES]
    o = o1 - lam * o2
    ms = jnp.mean(o * o, axis=-1, keepdims=True)
    on = o * lax.rsqrt(ms + EPS) * sg_ref[...]
    o_ref[0] = (on * (1.0 - LAM_INIT)).astype(BF16)


def _diff_attention(q3, k3, v3, lam_vecs, subln_g):
    B, S, _ = q3.shape
    tq, tk = ATT_TQ, ATT_TK
    return pl.pallas_call(
        functools.partial(_attn_kernel, tq=tq, tk=tk),
        grid=(B, DA_HEADS, S // tq),
        in_specs=[
            pl.BlockSpec((1, tq, DA_V_DIM), lambda b, h, i: (b, i, h)),
            pl.BlockSpec((1, S, DA_V_DIM), lambda b, h, i: (b, 0, h)),
            pl.BlockSpec((1, S, DA_V_DIM), lambda b, h, i: (b, 0, h)),
            _const_spec((4, DA_HEAD_DIM)),
            _const_spec((1, DA_V_DIM)),
        ],
        out_specs=pl.BlockSpec((1, tq, DA_V_DIM), lambda b, h, i: (b, i, h)),
        out_shape=jax.ShapeDtypeStruct((B, S, ATTN_W), BF16),
        scratch_shapes=[
            pltpu.VMEM((2 * tq, DA_V_DIM), BF16),
            pltpu.VMEM((2 * tq, LANES), F32),
            pltpu.VMEM((2 * tq, 2 * LANES), F32),
            pltpu.VMEM((2 * tq, tk), F32),
        ],
        compiler_params=pltpu.CompilerParams(
            dimension_semantics=("parallel", "parallel", "parallel"),
            vmem_limit_bytes=VMEM_LIMIT),
        name="diff_attention",
    )(q3, k3, v3, lam_vecs, subln_g)


def _merge_kernel(conv_ref, attn_ref, gate_ref, x_ref, wc_ref, wa_ref, wo_ref, fg_ref,
                  wr_ref, br_ref, x1_ref, h2_ref, logit_ref):
    c = jnp.dot(conv_ref[...], wc_ref[...], preferred_element_type=F32)
    a = jnp.dot(attn_ref[...], wa_ref[...], preferred_element_type=F32)
    g0 = gate_ref[:, 0:D_MODEL].astype(F32)
    g1 = gate_ref[:, D_MODEL:2 * D_MODEL].astype(F32)
    merged = (g0 * c + g1 * a).astype(BF16)
    x1 = x_ref[...] + jnp.dot(merged, wo_ref[...], preferred_element_type=F32)
    x1_ref[...] = x1
    ms = jnp.mean(x1 * x1, axis=-1, keepdims=True)
    h2 = x1 * lax.rsqrt(ms + EPS) * fg_ref[...]
    _store_row_tiles(h2_ref, (), h2)
    h2_hi = h2.astype(BF16)
    h2_lo = (h2 - h2_hi.astype(F32)).astype(BF16)
    r = (jnp.dot(h2_hi, wr_ref[...], preferred_element_type=F32)
         + jnp.dot(h2_lo, wr_ref[...], preferred_element_type=F32))
    logit_ref[...] = r[:, 0:ROUTER_LANES] + r[:, ROUTER_LANES:2 * ROUTER_LANES] + br_ref[...]


def _merge(conv_act, attn_o, gates, x2, wc, wa, wo, fg, wr, br):
    T = x2.shape[0]
    tm = TM_PROJ
    row = lambda i: (i, 0)
    return pl.pallas_call(
        _merge_kernel,
        grid=(T // tm,),
        in_specs=[
            pl.BlockSpec((tm, CONV_CH), row),
            pl.BlockSpec((tm, ATTN_W), row),
            pl.BlockSpec((tm, COL_GATE), row),
            pl.BlockSpec((tm, D_MODEL), row),
            _const_spec((CONV_CH, D_MODEL)),
            _const_spec((ATTN_W, D_MODEL)),
            _const_spec((D_MODEL, D_MODEL)),
            _const_spec((1, D_MODEL)),
            _const_spec((D_MODEL, 2 * ROUTER_LANES)),
            _const_spec((1, ROUTER_LANES)),
        ],
        out_specs=(
            pl.BlockSpec((tm, D_MODEL), row),
            pl.BlockSpec((tm * ROW_TILE, LANES), row),
            pl.BlockSpec((tm, ROUTER_LANES), row),
        ),
        out_shape=(
            jax.ShapeDtypeStruct((T, D_MODEL), F32),
            jax.ShapeDtypeStruct((T * ROW_TILE, LANES), F32),
            jax.ShapeDtypeStruct((T, ROUTER_LANES), F32),
        ),
        compiler_params=pltpu.CompilerParams(
            dimension_semantics=("parallel",), vmem_limit_bytes=VMEM_LIMIT),
        name="merge_out_router",
    )(conv_act, attn_o, gates, x2, wc, wa, wo, fg, wr, br)


def _route_kernel(logit_ref, tri_ref, low_ref, dest_ref, wts_ref, blk_ref, seg_ref,
                  cnt_ref, base_ref, *, n_blk_lanes):
    ph = pl.program_id(0)
    i = pl.program_id(1)
    tm = logit_ref.shape[0]
    lt = logit_ref[...].T
    row8 = lax.broadcasted_iota(jnp.int32, (SUBLANES, tm), 0)

    g = jnp.where(row8 < N_GROUPS, lt[0:SUBLANES, :], -jnp.inf)
    g_max = jnp.max(g, axis=0, keepdims=True)
    g_sum = jnp.sum(jnp.exp(g - g_max), axis=0, keepdims=True)
    g_sel = jnp.min(jnp.where(g == g_max, row8, SUBLANES), axis=0, keepdims=True)
    g_w = 1.0 / g_sum

    e_sel = jnp.zeros((EXPERTS_PER_GROUP, tm), F32)
    for gi in range(N_GROUPS):
        lo = ROUTER_EXPERT_COL + gi * EXPERTS_PER_GROUP
        e_sel = jnp.where(g_sel == gi, lt[lo:lo + EXPERTS_PER_GROUP, :], e_sel)
    e_max = jnp.max(e_sel, axis=0, keepdims=True)
    e_exp = jnp.exp(e_sel - e_max)
    prob = e_exp / jnp.sum(e_exp, axis=0, keepdims=True)
    p1 = jnp.max(prob, axis=0, keepdims=True)
    i1 = jnp.min(jnp.where(prob == p1, row8, SUBLANES), axis=0, keepdims=True)
    rest = jnp.where(row8 == i1, -1.0, prob)
    p2 = jnp.max(rest, axis=0, keepdims=True)
    i2 = jnp.min(jnp.where(rest == p2, row8, SUBLANES), axis=0, keepdims=True)
    denom = p1 + p2
    e0 = g_sel * EXPERTS_PER_GROUP + i1
    e1 = g_sel * EXPERTS_PER_GROUP + i2

    row32 = lax.broadcasted_iota(jnp.int32, (N_EXPERTS, tm), 0)
    oh0 = row32 == e0
    oh1 = row32 == e1
    ohs = jnp.where(oh0 | oh1, 1.0, 0.0)
    tile_cnt = jnp.sum(ohs, axis=1, keepdims=True)

    @pl.when((ph == 0) & (i == 0))
    def _():
        cnt_ref[...] = jnp.zeros(cnt_ref.shape, F32)

    @pl.when(ph == 0)
    def _():
        cnt_ref[...] += tile_cnt

    @pl.when((ph == 1) & (i == 0))
    def _():
        cnt = cnt_ref[...]
        padded = jnp.ceil(cnt * (1.0 / RB)) * RB
        pstart = jnp.dot(low_ref[...], padded, preferred_element_type=F32,
                         precision=lax.Precision.HIGHEST)
        start = jnp.dot(low_ref[...], cnt, preferred_element_type=F32,
                        precision=lax.Precision.HIGHEST)
        base_ref[...] = pstart
        pad_end = pstart + padded
        lane_row = lax.broadcasted_iota(jnp.int32, (N_EXPERTS, n_blk_lanes), 1).astype(F32) * RB
        before = jnp.where(pad_end[:, 0:1] <= lane_row, 1.0, 0.0)
        blk = jnp.minimum(jnp.sum(before, axis=0, keepdims=True), N_EXPERTS - 1.0)
        blk_ref[...] = blk.astype(jnp.int32)
        seg_ref[0] = cnt.astype(jnp.int32)
        seg_ref[1] = pstart.astype(jnp.int32)
        seg_ref[2] = start.astype(jnp.int32)
        seg_ref[3] = pad_end.astype(jnp.int32)

    @pl.when(ph == 1)
    def _():
        before = jnp.dot(ohs.astype(BF16), tri_ref[...], preferred_element_type=F32)
        pos = base_ref[:, 0:1] + before
        d0 = jnp.sum(jnp.where(oh0, pos, 0.0), axis=0, keepdims=True)
        d1 = jnp.sum(jnp.where(oh1, pos, 0.0), axis=0, keepdims=True)
        dest_ref[0:1, :] = d0.astype(jnp.int32)
        dest_ref[1:2, :] = d1.astype(jnp.int32)
        wts_ref[0:1, :] = p1 / denom * g_w
        wts_ref[1:2, :] = p2 / denom * g_w
        base_ref[...] += tile_cnt


def _route(logits):
    T = logits.shape[0]
    tm = TM_ROUTE
    n_rows = 2 * T + N_EXPERTS * RB
    n_blk = n_rows // RB
    n_blk_lanes = -(-n_blk // LANES) * LANES
    r = jnp.arange(tm, dtype=jnp.int32)
    tri = (r[:, None] < r[None, :]).astype(BF16)
    e = jnp.arange(N_EXPERTS, dtype=jnp.int32)
    low = (e[None, :] < e[:, None]).astype(F32)
    dest, wts, blk, seg = pl.pallas_call(
        functools.partial(_route_kernel, n_blk_lanes=n_blk_lanes),
        grid=(2, T // tm),
        in_specs=[
            pl.BlockSpec((tm, ROUTER_LANES), lambda ph, i: (i, 0)),
            _const_spec((tm, tm)),
            _const_spec((N_EXPERTS, N_EXPERTS)),
        ],
        out_specs=(
            pl.BlockSpec((2, tm), lambda ph, i: (0, i * ph)),
            pl.BlockSpec((2, tm), lambda ph, i: (0, i * ph)),
            pl.BlockSpec((1, n_blk_lanes), lambda ph, i: (0, 0)),
            pl.BlockSpec((4, N_EXPERTS, LANES), lambda ph, i: (0, 0, 0)),
        ),
        out_shape=(
            jax.ShapeDtypeStruct((2, T), jnp.int32),
            jax.ShapeDtypeStruct((2, T), F32),
            jax.ShapeDtypeStruct((1, n_blk_lanes), jnp.int32),
            jax.ShapeDtypeStruct((4, N_EXPERTS, LANES), jnp.int32),
        ),
        scratch_shapes=[pltpu.VMEM((N_EXPERTS, LANES), F32), pltpu.VMEM((N_EXPERTS, LANES), F32)],
        compiler_params=pltpu.CompilerParams(
            dimension_semantics=("arbitrary", "arbitrary"), vmem_limit_bytes=VMEM_LIMIT),
        name="route_plan",
    )(logits, tri, low)
    return dest, wts, blk, seg


def _expert_kernel(blk_exp_ref, n_used_ref, src_ref, src_next_ref, dst_ref,
                   h_hbm, wg_ref, wu_ref, wd_ref, y_hbm,
                   wg_bf, wu_bf, wd_bf, xbuf, ybuf, gsem, ssem):
    b = pl.program_id(0)
    n_used = n_used_ref[0]
    slot = b % 2

    def gather_row(idx_ref, r, to_slot):
        tile = pl.multiple_of(idx_ref[0, 0, r] * ROW_TILE, ROW_TILE)
        return pltpu.make_async_copy(h_hbm.at[pl.ds(tile, ROW_TILE), :],
                                     xbuf.at[to_slot, pl.ds(r * ROW_TILE, ROW_TILE), :], gsem.at[to_slot])

    def scatter_row(r, from_slot):
        tile = pl.multiple_of(dst_ref[0, 0, r] * ROW_TILE, ROW_TILE)
        return pltpu.make_async_copy(ybuf.at[from_slot, pl.ds(r * ROW_TILE, ROW_TILE), :],
                                     y_hbm.at[pl.ds(tile, ROW_TILE), :], ssem.at[from_slot])

    def all_rows_gathered(of_slot):
        return pltpu.make_async_copy(h_hbm.at[pl.ds(0, RB * ROW_TILE), :], xbuf.at[of_slot],
                                     gsem.at[of_slot])

    def all_rows_scattered(of_slot):
        return pltpu.make_async_copy(ybuf.at[of_slot], y_hbm.at[pl.ds(0, RB * ROW_TILE), :],
                                     ssem.at[of_slot])

    @pl.when((b == 0) & (n_used > 0))
    def _():
        for r in range(RB):
            gather_row(src_ref, r, 0).start(priority=r % 2)

    new_expert = (b == 0) | (blk_exp_ref[b] != blk_exp_ref[jnp.maximum(b - 1, 0)])

    @pl.when((b < n_used) & new_expert)
    def _():
        wg_bf[...] = wg_ref[0].astype(BF16)
        wu_bf[...] = wu_ref[0].astype(BF16)
        wd_bf[...] = wd_ref[0].astype(BF16)

    @pl.when((b < n_used) & (b >= 2))
    def _():
        all_rows_scattered(slot).wait()

    @pl.when(b < n_used)
    def _():
        all_rows_gathered(slot).wait()
        for r in range(RB):
            gather_row(src_next_ref, r, 1 - slot).start(priority=r % 2)
        xb = _load_row_tiles(xbuf, (slot,), RB, D_MODEL).astype(BF16)
        g = jnp.dot(xb, wg_bf[...], preferred_element_type=F32)
        u = jnp.dot(xb, wu_bf[...], preferred_element_type=F32)
        hid = (g * _sigmoid(g) * u).astype(BF16)
        _store_row_tiles(ybuf, (slot,), jnp.dot(hid, wd_bf[...], preferred_element_type=F32))
        for r in range(RB):
            scatter_row(r, slot).start(priority=r % 2)

    @pl.when((b == n_used - 1) & (b >= 1))
    def _():
        all_rows_scattered(1 - slot).wait()

    @pl.when(b == n_used - 1)
    def _():
        all_rows_scattered(slot).wait()
        all_rows_gathered(1 - slot).wait()
        ybuf[slot] = jnp.zeros(ybuf.shape[1:], F32)
        surplus = pltpu.make_async_copy(
            ybuf.at[slot], y_hbm.at[pl.ds(y_hbm.shape[0] - RB * ROW_TILE, RB * ROW_TILE), :], ssem.at[slot])
        surplus.start()
        surplus.wait()


def _experts(blk_exp, n_used, row_src, row_dst, h, wg, wu, wd, n_out_rows):
    n_blk = row_src.shape[0]
    smem_blk = lambda f: pl.BlockSpec((1, 1, RB), f, memory_space=pltpu.SMEM)
    grid_spec = pltpu.PrefetchScalarGridSpec(
        num_scalar_prefetch=2,
        grid=(n_blk,),
        in_specs=[
            smem_blk(lambda b, be, nu: (b, 0, 0)),
            smem_blk(lambda b, be, nu: (jnp.minimum(b + 1, n_blk - 1), 0, 0)),
            smem_blk(lambda b, be, nu: (b, 0, 0)),
            pl.BlockSpec(memory_space=pl.ANY),
            pl.BlockSpec((1, D_MODEL, D_EXPERT), lambda b, be, nu: (be[b], 0, 0)),
            pl.BlockSpec((1, D_MODEL, D_EXPERT), lambda b, be, nu: (be[b], 0, 0)),
            pl.BlockSpec((1, D_EXPERT, D_MODEL), lambda b, be, nu: (be[b], 0, 0)),
        ],
        out_specs=pl.BlockSpec(memory_space=pl.ANY),
        scratch_shapes=[
            pltpu.VMEM((D_MODEL, D_EXPERT), BF16),
            pltpu.VMEM((D_MODEL, D_EXPERT), BF16),
            pltpu.VMEM((D_EXPERT, D_MODEL), BF16),
            pltpu.VMEM((2, RB * ROW_TILE, LANES), F32),
            pltpu.VMEM((2, RB * ROW_TILE, LANES), F32),
            pltpu.SemaphoreType.DMA((2,)),
            pltpu.SemaphoreType.DMA((2,)),
        ],
    )
    return pl.pallas_call(
        _expert_kernel,
        grid_spec=grid_spec,
        out_shape=jax.ShapeDtypeStruct((n_out_rows * ROW_TILE, LANES), F32),
        compiler_params=pltpu.CompilerParams(
            dimension_semantics=("arbitrary",), vmem_limit_bytes=VMEM_LIMIT),
        name="expert_mlp",
    )(blk_exp, n_used, row_src, row_src, row_dst, h, wg, wu, wd)


def _combine_kernel(x1_ref, y0_ref, y1_ref, w_ref, o_ref):
    tm = x1_ref.shape[0]
    y0 = _load_row_tiles(y0_ref, (), tm, D_MODEL)
    y1 = _load_row_tiles(y1_ref, (), tm, D_MODEL)
    o_ref[...] = x1_ref[...] + (y0 * w_ref[:, 0:1] + y1 * w_ref[:, 1:2])


def _combine(x1, y, wts):
    T = x1.shape[0]
    tm = TM_COMB
    row = lambda i: (i, 0)
    return pl.pallas_call(
        _combine_kernel,
        grid=(T // tm,),
        in_specs=[
            pl.BlockSpec((tm, D_MODEL), row),
            pl.BlockSpec((tm * ROW_TILE, LANES), row),
            pl.BlockSpec((tm * ROW_TILE, LANES), lambda i: (i + T // tm, 0)),
            pl.BlockSpec((tm, 2), row),
        ],
        out_specs=pl.BlockSpec((tm, D_MODEL), row),
        out_shape=jax.ShapeDtypeStruct((T, D_MODEL), F32),
        compiler_params=pltpu.CompilerParams(
            dimension_semantics=("parallel",), vmem_limit_bytes=VMEM_LIMIT),
        name="moe_combine",
    )(x1, y, y, wts)


def _sorted_row_plan(dest, blk, seg, n_rows):
    T = dest.shape[1]
    n_blk = n_rows // RB
    cnt, pstart = seg[0, :, 0], seg[1, :, 0]
    e_b = blk[0, :n_blk]
    b0 = jnp.arange(n_blk, dtype=jnp.int32) * RB
    n_valid = jnp.clip(pstart[e_b] + cnt[e_b] - b0, 0, RB)
    w = jnp.arange(RB, dtype=jnp.int32)
    r = b0[:, None] + w[None, :]
    is_pad = w[None, :] >= n_valid[:, None]
    a = jnp.arange(2 * T, dtype=jnp.int32)
    keys = jnp.concatenate([dest.reshape(2 * T), jnp.where(is_pad, r, n_rows).reshape(n_rows)])
    vals = jnp.concatenate([a, jnp.broadcast_to(2 * T + w[None, :], (n_blk, RB)).reshape(n_rows)])
    _, plan = lax.sort_key_val(keys, vals)
    plan = plan[:n_rows]
    src = jnp.where(plan < 2 * T, plan % T, 0)
    return src.reshape(n_blk, 1, RB), plan.reshape(n_blk, 1, RB)


def kernel(x, attn_norm_g, w_in, conv_dw_w, conv_dw_b, conv_ln_g, conv_ln_b, w_conv_out,
           q_norm_g, k_norm_g, lambda_q1, lambda_k1, lambda_q2, lambda_k2, subln_g,
           w_attn_out, w_out, ffn_norm_g, w_router_group, b_router_group,
           w_router_expert, b_router_expert, w_gate_e, w_up_e, w_down_e):
    B, S, D = x.shape
    T = B * S
    l = 0
    x2 = x.reshape(T, D)

    reps = COL_Q // DA_HEAD_DIM
    qg = jnp.tile(q_norm_g[l], reps).reshape(1, COL_Q)
    kg = jnp.tile(k_norm_g[l], reps).reshape(1, COL_K)
    grp = jnp.arange(NORM_BLOCK, dtype=jnp.int32) // DA_HEAD_DIM
    bd = jnp.where(grp[:, None] == grp[None, :], 1.0 / DA_HEAD_DIM, 0.0).astype(BF16)

    glu, q, k, v, gates = _in_projection(
        x2, attn_norm_g[l].reshape(1, D), w_in[l].astype(BF16), qg, kg, bd)

    conv_act = _conv_branch(glu.reshape(B, S, CONV_CH), conv_dw_w[l],
                            conv_dw_b[l].reshape(1, CONV_CH), conv_ln_g[l].reshape(1, CONV_CH),
                            conv_ln_b[l].reshape(1, CONV_CH))

    lam_vecs = jnp.stack([lambda_q1[l], lambda_k1[l], lambda_q2[l], lambda_k2[l]]).astype(F32)
    attn_o = _diff_attention(q.reshape(B, S, COL_Q), k.reshape(B, S, COL_K),
                             v.reshape(B, S, COL_V), lam_vecs, subln_g[l].reshape(1, DA_V_DIM))

    e_lo, e_hi = ROUTER_EXPERT_COL, ROUTER_EXPERT_COL + N_EXPERTS
    wr = jnp.zeros((D, ROUTER_LANES), F32)
    wr = wr.at[:, :N_GROUPS].set(w_router_group[l]).at[:, e_lo:e_hi].set(w_router_expert[l])
    br = jnp.zeros((1, ROUTER_LANES), F32)
    br = br.at[0, :N_GROUPS].set(b_router_group[l]).at[0, e_lo:e_hi].set(b_router_expert[l])

    wr_hi = wr.astype(BF16)
    wr_lo = (wr - wr_hi.astype(F32)).astype(BF16)
    x1, h2, logits = _merge(conv_act.reshape(T, CONV_CH), attn_o.reshape(T, ATTN_W), gates, x2,
                            w_conv_out[l].astype(BF16), w_attn_out[l].astype(BF16),
                            w_out[l].astype(BF16), ffn_norm_g[l].reshape(1, D),
                            jnp.concatenate([wr_hi, wr_lo], axis=1), br)

    dest, wts, blk, seg = _route(logits)
    n_rows = 2 * T + N_EXPERTS * RB
    n_blk = n_rows // RB
    row_src, row_dst = _sorted_row_plan(dest, blk, seg, n_rows)
    n_used = seg[3, N_EXPERTS - 1, 0:1] // RB

    y = _experts(blk[0, :n_blk], n_used, row_src, row_dst, h2, w_gate_e[l], w_up_e[l], w_down_e[l],
                 2 * T + RB)

    out = _combine(x1, y, wts.T)
    return out.reshape(B, S, D)
```

```python
import functools
import math

import jax
import jax.numpy as jnp
from jax import lax
from jax.experimental import pallas as pl
from jax.experimental.pallas import tpu as pltpu
from jax.experimental.pallas import tpu_sc as plsc

F32 = jnp.float32
BF16 = jnp.bfloat16

D_MODEL = 1024
CONV_CH = 512
CONV_WIDTH = 31
DA_HEADS = 4
DA_HEAD_DIM = 64
DA_V_DIM = 128
ATTN_W = 512
N_GROUPS = 4
EXPERTS_PER_GROUP = 8
N_EXPERTS = 32
D_EXPERT = 512
EPS = 1e-6
LAM_INIT = 0.8 - 0.6 * math.exp(-0.3 * 0)

COL_GLU = 2 * CONV_CH
COL_Q = 512
COL_K = 512
COL_V = 512
COL_GATE = 2 * D_MODEL
OFF_Q = COL_GLU
OFF_K = OFF_Q + COL_Q
OFF_V = OFF_K + COL_K
OFF_GATE = OFF_V + COL_V
IN_COLS = OFF_GATE + COL_GATE

ROUTER_LANES = 128
ROUTER_EXPERT_COL = 8
SUBLANES = 8
ROW_TILE = D_MODEL // 128
NORM_BLOCK = 256
HALO = 32

TM_PROJ = 512
TS_CONV = 512
CONV_CHUNK = 64
ATT_TQ = 512
ATT_TK = 512
ATT_RC = 256
LANES = 128
RB = 512
TM_COMB = 512
TM_ROUTE = 512
SC_CORES = 2
SC_SUBCORES = 16
SC_ROWS = 32

VMEM_LIMIT = 48 * 1024 * 1024


def _sigmoid(x):
    return 1.0 / (1.0 + jnp.exp(-x))


def _const_spec(shape):
    nd = len(shape)
    return pl.BlockSpec(shape, lambda *_: (0,) * nd, pipeline_mode=pl.Buffered(1))


def _store_row_tiles(ref, lead, val):
    rows = val.shape[0]
    for j in range(val.shape[1] // LANES):
        ref[(*lead, pl.ds(j, rows, stride=ROW_TILE), slice(None))] = val[:, j * LANES:(j + 1) * LANES]


def _load_row_tiles(ref, lead, rows, width):
    return jnp.concatenate(
        [ref[(*lead, pl.ds(j, rows, stride=ROW_TILE), slice(None))] for j in range(width // LANES)], axis=1)


def _group_mean_sq(t, bd_ref):
    sq = t * t
    hi = sq.astype(BF16)
    lo = (sq - hi.astype(F32)).astype(BF16)
    w = bd_ref.shape[0]
    parts = [jnp.dot(hi[:, c:c + w], bd_ref[...], preferred_element_type=F32)
             + jnp.dot(lo[:, c:c + w], bd_ref[...], preferred_element_type=F32)
             for c in range(0, t.shape[1], w)]
    return jnp.concatenate(parts, axis=1)


def _inproj_kernel(x_ref, g_ref, w_ref, qg_ref, kg_ref, bd_ref,
                   glu_ref, q_ref, k_ref, v_ref, gate_ref):
    x = x_ref[...]
    ms = jnp.mean(x * x, axis=-1, keepdims=True)
    h = (x * lax.rsqrt(ms + EPS) * g_ref[...]).astype(BF16)

    def proj(lo, width):
        return jnp.dot(h, w_ref[:, lo:lo + width], preferred_element_type=F32)

    a = proj(0, CONV_CH)
    gt = proj(CONV_CH, CONV_CH)
    glu_ref[...] = (a * _sigmoid(gt)).astype(BF16)

    q = proj(OFF_Q, COL_Q)
    qn = q * lax.rsqrt(_group_mean_sq(q, bd_ref) + EPS) * qg_ref[...]
    q_ref[...] = (qn * (DA_HEAD_DIM ** -0.5)).astype(BF16)

    k = proj(OFF_K, COL_K)
    kn = k * lax.rsqrt(_group_mean_sq(k, bd_ref) + EPS) * kg_ref[...]
    k_ref[...] = kn.astype(BF16)

    v_ref[...] = proj(OFF_V, COL_V).astype(BF16)

    for c in range(COL_GATE // 512):
        gate_ref[:, c * 512:(c + 1) * 512] = _sigmoid(proj(OFF_GATE + c * 512, 512)).astype(BF16)


def _in_projection(x2, g, w_bf, qg, kg, bd):
    T = x2.shape[0]
    tm = TM_PROJ
    row = lambda i: (i, 0)
    out_shape = (
        jax.ShapeDtypeStruct((T, CONV_CH), BF16),
        jax.ShapeDtypeStruct((T, COL_Q), BF16),
        jax.ShapeDtypeStruct((T, COL_K), BF16),
        jax.ShapeDtypeStruct((T, COL_V), BF16),
        jax.ShapeDtypeStruct((T, COL_GATE), BF16),
    )
    return pl.pallas_call(
        _inproj_kernel,
        grid=(T // tm,),
        in_specs=[
            pl.BlockSpec((tm, D_MODEL), row),
            _const_spec((1, D_MODEL)),
            _const_spec((D_MODEL, IN_COLS)),
            _const_spec((1, COL_Q)),
            _const_spec((1, COL_K)),
            _const_spec((NORM_BLOCK, NORM_BLOCK)),
        ],
        out_specs=(
            pl.BlockSpec((tm, CONV_CH), row),
            pl.BlockSpec((tm, COL_Q), row),
            pl.BlockSpec((tm, COL_K), row),
            pl.BlockSpec((tm, COL_V), row),
            pl.BlockSpec((tm, COL_GATE), row),
        ),
        out_shape=out_shape,
        compiler_params=pltpu.CompilerParams(
            dimension_semantics=("parallel",), vmem_limit_bytes=VMEM_LIMIT),
        name="in_projection",
    )(x2, g, w_bf, qg, kg, bd)


def _conv_kernel(cur_ref, halo_ref, w_ref, b_ref, lng_ref, lnb_ref, out_ref, buf_ref):
    i = pl.program_id(1)
    ts = cur_ref.shape[1]
    halo = halo_ref[0].astype(F32)
    buf_ref[0:HALO, :] = jnp.where(i > 0, halo, 0.0)
    buf_ref[HALO:HALO + ts, :] = cur_ref[0].astype(F32)

    first = HALO - (CONV_WIDTH - 1)
    for c in range(ts // CONV_CHUNK):
        r0 = c * CONV_CHUNK
        acc = jnp.broadcast_to(b_ref[...], (CONV_CHUNK, CONV_CH))
        for j in range(CONV_WIDTH):
            acc = acc + w_ref[j:j + 1, :] * buf_ref[r0 + first + j:r0 + first + j + CONV_CHUNK, :]
        mu = jnp.mean(acc, axis=-1, keepdims=True)
        d = acc - mu
        var = jnp.mean(d * d, axis=-1, keepdims=True)
        y = d * lax.rsqrt(var + EPS) * lng_ref[...] + lnb_ref[...]
        out_ref[0, r0:r0 + CONV_CHUNK, :] = (y * _sigmoid(y)).astype(BF16)


def _conv_branch(glu3, dw_w, dw_b, ln_g, ln_b):
    B, S, C = glu3.shape
    ts = TS_CONV
    per = ts // HALO
    return pl.pallas_call(
        _conv_kernel,
        grid=(B, S // ts),
        in_specs=[
            pl.BlockSpec((1, ts, C), lambda b, i: (b, i, 0)),
            pl.BlockSpec((1, HALO, C), lambda b, i: (b, jnp.maximum(i * per - 1, 0), 0)),
            _const_spec((CONV_WIDTH, C)),
            _const_spec((1, C)),
            _const_spec((1, C)),
            _const_spec((1, C)),
        ],
        out_specs=pl.BlockSpec((1, ts, C), lambda b, i: (b, i, 0)),
        out_shape=jax.ShapeDtypeStruct((B, S, C), BF16),
        scratch_shapes=[pltpu.VMEM((HALO + ts, C), F32)],
        compiler_params=pltpu.CompilerParams(
            dimension_semantics=("parallel", "parallel"), vmem_limit_bytes=VMEM_LIMIT),
        name="conv_branch",
    )(glu3, glu3, dw_w, dw_b, ln_g, ln_b)


def _attn_kernel(q_ref, k_ref, v_ref, lam_ref, sg_ref, o_ref, qs_ref, m_ref, acc_ref, s_ref, *, tq, tk):
    qi = pl.program_id(2)
    q = q_ref[0]
    lane = lax.broadcasted_iota(jnp.int32, q.shape, 1)
    zero = jnp.zeros_like(q)
    qs_ref[0:tq, :] = jnp.where(lane < DA_HEAD_DIM, q, zero)
    qs_ref[tq:2 * tq, :] = jnp.where(lane >= DA_HEAD_DIM, q, zero)
    m_ref[...] = jnp.full(m_ref.shape, -jnp.inf, F32)
    acc_ref[...] = jnp.zeros(acc_ref.shape, F32)
    ones = jnp.ones((tk, LANES), BF16)
    rc = ATT_RC
    chunks = [slice(r0, r0 + rc) for r0 in range(0, 2 * tq, rc)]

    def scores(j, rows):
        kj = k_ref[0, pl.ds(pl.multiple_of(j * tk, tk), tk), :]
        return lax.dot_general(qs_ref[rows, :], kj, (((1,), (1,)), ((), ())),
                               preferred_element_type=F32)

    def step(j, masked, prefetch):
        vj = jnp.concatenate([v_ref[0, pl.ds(pl.multiple_of(j * tk, tk), tk), :], ones], axis=1)
        for rows in chunks:
            r_lo = rows.start % tq
            kw = min(tk, -(-(r_lo + rc) // (2 * LANES)) * (2 * LANES)) if masked else tk
            s = s_ref[rows, 0:kw]
            if prefetch:
                s_ref[rows, :] = scores(j + 1, rows)
            if masked:
                r = lax.broadcasted_iota(jnp.int32, s.shape, 0) + r_lo
                c = lax.broadcasted_iota(jnp.int32, s.shape, 1)
                s = jnp.where(c <= r, s, -jnp.inf)
            m_old = m_ref[rows, :]
            m_new = jnp.maximum(m_old, jnp.max(s, axis=-1, keepdims=True))
            alpha = jnp.exp(m_old - m_new)
            p = jnp.exp(s - jnp.tile(m_new, (1, kw // LANES)))
            pv = jnp.dot(p.astype(BF16), vj[0:kw, :], preferred_element_type=F32)
            acc_ref[rows, :] = jnp.tile(alpha, (1, 2)) * acc_ref[rows, :] + pv
            m_ref[rows, :] = m_new

    for rows in chunks:
        s_ref[rows, :] = scores(0, rows)

    def body(j, carry):
        step(j, False, True)
        return carry

    lax.fori_loop(0, qi, body, 0)
    step(qi, True, False)

    lam = (jnp.exp(jnp.sum(lam_ref[0:1, :] * lam_ref[1:2, :], axis=-1, keepdims=True))
           - jnp.exp(jnp.sum(lam_ref[2:3, :] * lam_ref[3:4, :], axis=-1, keepdims=True))
           + LAM_INIT)
    o1 = acc_ref[0:tq, 0:LANES] / acc_ref[0:tq, LANES:2 * LANES]
    o2 = acc_ref[tq:2 * tq, 0:LANES] / acc_ref[tq:2 * tq, LANES:2 * LANES]
    o = o1 - lam * o2
    ms = jnp.mean(o * o, axis=-1, keepdims=True)
    on = o * lax.rsqrt(ms + EPS) * sg_ref[...]
    o_ref[0] = (on * (1.0 - LAM_INIT)).astype(BF16)


def _diff_attention(q3, k3, v3, lam_vecs, subln_g):
    B, S, _ = q3.shape
    tq, tk = ATT_TQ, ATT_TK
    return pl.pallas_call(
        functools.partial(_attn_kernel, tq=tq, tk=tk),
        grid=(B, DA_HEADS, S // tq),
        in_specs=[
            pl.BlockSpec((1, tq, DA_V_DIM), lambda b, h, i: (b, i, h)),
            pl.BlockSpec((1, S, DA_V_DIM), lambda b, h, i: (b, 0, h)),
            pl.BlockSpec((1, S, DA_V_DIM), lambda b, h, i: (b, 0, h)),
            _const_spec((4, DA_HEAD_DIM)),
            _const_spec((1, DA_V_DIM)),
        ],
        out_specs=pl.BlockSpec((1, tq, DA_V_DIM), lambda b, h, i: (b, i, h)),
        out_shape=jax.ShapeDtypeStruct((B, S, ATTN_W), BF16),
        scratch_shapes=[
            pltpu.VMEM((2 * tq, DA_V_DIM), BF16),
            pltpu.VMEM((2 * tq, LANES), F32),
            pltpu.VMEM((2 * tq, 2 * LANES), F32),
            pltpu.VMEM((2 * tq, tk), F32),
        ],
        compiler_params=pltpu.CompilerParams(
            dimension_semantics=("parallel", "parallel", "parallel"),
            vmem_limit_bytes=VMEM_LIMIT),
        name="diff_attention",
    )(q3, k3, v3, lam_vecs, subln_g)


def _merge_kernel(conv_ref, attn_ref, gate_ref, x_ref, wc_ref, wa_ref, wo_ref, fg_ref,
                  wr_ref, br_ref, x1_ref, h2_ref, logit_ref):
    c = jnp.dot(conv_ref[...], wc_ref[...], preferred_element_type=F32)
    a = jnp.dot(attn_ref[...], wa_ref[...], preferred_element_type=F32)
    g0 = gate_ref[:, 0:D_MODEL].astype(F32)
    g1 = gate_ref[:, D_MODEL:2 * D_MODEL].astype(F32)
    merged = (g0 * c + g1 * a).astype(BF16)
    x1 = x_ref[...] + jnp.dot(merged, wo_ref[...], preferred_element_type=F32)
    x1_ref[...] = x1
    ms = jnp.mean(x1 * x1, axis=-1, keepdims=True)
    h2 = x1 * lax.rsqrt(ms + EPS) * fg_ref[...]
    _store_row_tiles(h2_ref, (), h2)
    h2_hi = h2.astype(BF16)
    h2_lo = (h2 - h2_hi.astype(F32)).astype(BF16)
    r = (jnp.dot(h2_hi, wr_ref[...], preferred_element_type=F32)
         + jnp.dot(h2_lo, wr_ref[...], preferred_element_type=F32))
    logit_ref[...] = r[:, 0:ROUTER_LANES] + r[:, ROUTER_LANES:2 * ROUTER_LANES] + br_ref[...]


def _merge(conv_act, attn_o, gates, x2, wc, wa, wo, fg, wr, br):
    T = x2.shape[0]
    tm = TM_PROJ
    row = lambda i: (i, 0)
    return pl.pallas_call(
        _merge_kernel,
        grid=(T // tm,),
        in_specs=[
            pl.BlockSpec((tm, CONV_CH), row),
            pl.BlockSpec((tm, ATTN_W), row),
            pl.BlockSpec((tm, COL_GATE), row),
            pl.BlockSpec((tm, D_MODEL), row),
            _const_spec((CONV_CH, D_MODEL)),
            _const_spec((ATTN_W, D_MODEL)),
            _const_spec((D_MODEL, D_MODEL)),
            _const_spec((1, D_MODEL)),
            _const_spec((D_MODEL, 2 * ROUTER_LANES)),
            _const_spec((1, ROUTER_LANES)),
        ],
        out_specs=(
            pl.BlockSpec((tm, D_MODEL), row),
            pl.BlockSpec((tm * ROW_TILE, LANES), row),
            pl.BlockSpec((tm, ROUTER_LANES), row),
        ),
        out_shape=(
            jax.ShapeDtypeStruct((T, D_MODEL), F32),
            jax.ShapeDtypeStruct((T * ROW_TILE, LANES), F32),
            jax.ShapeDtypeStruct((T, ROUTER_LANES), F32),
        ),
        compiler_params=pltpu.CompilerParams(
            dimension_semantics=("parallel",), vmem_limit_bytes=VMEM_LIMIT),
        name="merge_out_router",
    )(conv_act, attn_o, gates, x2, wc, wa, wo, fg, wr, br)


def _route_kernel(logit_ref, tri_ref, low_ref, dest_ref, wts_ref, blk_ref, seg_ref,
                  cnt_ref, base_ref, *, n_blk_lanes):
    ph = pl.program_id(0)
    i = pl.program_id(1)
    tm = logit_ref.shape[0]
    lt = logit_ref[...].T
    row8 = lax.broadcasted_iota(jnp.int32, (SUBLANES, tm), 0)

    g = jnp.where(row8 < N_GROUPS, lt[0:SUBLANES, :], -jnp.inf)
    g_max = jnp.max(g, axis=0, keepdims=True)
    g_sum = jnp.sum(jnp.exp(g - g_max), axis=0, keepdims=True)
    g_sel = jnp.min(jnp.where(g == g_max, row8, SUBLANES), axis=0, keepdims=True)
    g_w = 1.0 / g_sum

    e_sel = jnp.zeros((EXPERTS_PER_GROUP, tm), F32)
    for gi in range(N_GROUPS):
        lo = ROUTER_EXPERT_COL + gi * EXPERTS_PER_GROUP
        e_sel = jnp.where(g_sel == gi, lt[lo:lo + EXPERTS_PER_GROUP, :], e_sel)
    e_max = jnp.max(e_sel, axis=0, keepdims=True)
    e_exp = jnp.exp(e_sel - e_max)
    prob = e_exp / jnp.sum(e_exp, axis=0, keepdims=True)
    p1 = jnp.max(prob, axis=0, keepdims=True)
    i1 = jnp.min(jnp.where(prob == p1, row8, SUBLANES), axis=0, keepdims=True)
    rest = jnp.where(row8 == i1, -1.0, prob)
    p2 = jnp.max(rest, axis=0, keepdims=True)
    i2 = jnp.min(jnp.where(rest == p2, row8, SUBLANES), axis=0, keepdims=True)
    denom = p1 + p2
    e0 = g_sel * EXPERTS_PER_GROUP + i1
    e1 = g_sel * EXPERTS_PER_GROUP + i2

    row32 = lax.broadcasted_iota(jnp.int32, (N_EXPERTS, tm), 0)
    oh0 = row32 == e0
    oh1 = row32 == e1
    ohs = jnp.where(oh0 | oh1, 1.0, 0.0)
    tile_cnt = jnp.sum(ohs, axis=1, keepdims=True)

    @pl.when((ph == 0) & (i == 0))
    def _():
        cnt_ref[...] = jnp.zeros(cnt_ref.shape, F32)

    @pl.when(ph == 0)
    def _():
        cnt_ref[...] += tile_cnt

    @pl.when((ph == 1) & (i == 0))
    def _():
        cnt = cnt_ref[...]
        padded = jnp.ceil(cnt * (1.0 / RB)) * RB
        pstart = jnp.dot(low_ref[...], padded, preferred_element_type=F32,
                         precision=lax.Precision.HIGHEST)
        start = jnp.dot(low_ref[...], cnt, preferred_element_type=F32,
                        precision=lax.Precision.HIGHEST)
        base_ref[...] = pstart
        pad_end = pstart + padded
        lane_row = lax.broadcasted_iota(jnp.int32, (N_EXPERTS, n_blk_lanes), 1).astype(F32) * RB
        before = jnp.where(pad_end[:, 0:1] <= lane_row, 1.0, 0.0)
        blk = jnp.minimum(jnp.sum(before, axis=0, keepdims=True), N_EXPERTS - 1.0)
        blk_ref[...] = blk.astype(jnp.int32)
        seg_ref[0] = cnt.astype(jnp.int32)
        seg_ref[1] = pstart.astype(jnp.int32)
        seg_ref[2] = start.astype(jnp.int32)
        seg_ref[3] = pad_end.astype(jnp.int32)

    @pl.when(ph == 1)
    def _():
        before = jnp.dot(ohs.astype(BF16), tri_ref[...], preferred_element_type=F32)
        pos = base_ref[:, 0:1] + before
        d0 = jnp.sum(jnp.where(oh0, pos, 0.0), axis=0, keepdims=True)
        d1 = jnp.sum(jnp.where(oh1, pos, 0.0), axis=0, keepdims=True)
        dest_ref[0:1, :] = d0.astype(jnp.int32)
        dest_ref[1:2, :] = d1.astype(jnp.int32)
        wts_ref[0:1, :] = p1 / denom * g_w
        wts_ref[1:2, :] = p2 / denom * g_w
        base_ref[...] += tile_cnt


def _route(logits):
    T = logits.shape[0]
    tm = TM_ROUTE
    n_rows = 2 * T + N_EXPERTS * RB
    n_blk = n_rows // RB
    n_blk_lanes = -(-n_blk // LANES) * LANES
    r = jnp.arange(tm, dtype=jnp.int32)
    tri = (r[:, None] < r[None, :]).astype(BF16)
    e = jnp.arange(N_EXPERTS, dtype=jnp.int32)
    low = (e[None, :] < e[:, None]).astype(F32)
    dest, wts, blk, seg = pl.pallas_call(
        functools.partial(_route_kernel, n_blk_lanes=n_blk_lanes),
        grid=(2, T // tm),
        in_specs=[
            pl.BlockSpec((tm, ROUTER_LANES), lambda ph, i: (i, 0)),
            _const_spec((tm, tm)),
            _const_spec((N_EXPERTS, N_EXPERTS)),
        ],
        out_specs=(
            pl.BlockSpec((2, tm), lambda ph, i: (0, i * ph)),
            pl.BlockSpec((2, tm), lambda ph, i: (0, i * ph)),
            pl.BlockSpec((1, n_blk_lanes), lambda ph, i: (0, 0)),
            pl.BlockSpec((4, N_EXPERTS, LANES), lambda ph, i: (0, 0, 0)),
        ),
        out_shape=(
            jax.ShapeDtypeStruct((2, T), jnp.int32),
            jax.ShapeDtypeStruct((2, T), F32),
            jax.ShapeDtypeStruct((1, n_blk_lanes), jnp.int32),
            jax.ShapeDtypeStruct((4, N_EXPERTS, LANES), jnp.int32),
        ),
        scratch_shapes=[pltpu.VMEM((N_EXPERTS, LANES), F32), pltpu.VMEM((N_EXPERTS, LANES), F32)],
        compiler_params=pltpu.CompilerParams(
            dimension_semantics=("arbitrary", "arbitrary"), vmem_limit_bytes=VMEM_LIMIT),
        name="route_plan",
    )(logits, tri, low)
    return dest, wts, blk, seg


def _sc_worker():
    return lax.axis_index("s") * SC_CORES + lax.axis_index("c")


def _sc_scatter_rows(h, dest, n_rows):
    T = h.shape[0]
    n_workers = SC_CORES * SC_SUBCORES
    per_w = T // n_workers
    n_ch = per_w // SC_ROWS
    assert per_w * n_workers == T and n_ch * SC_ROWS == per_w and n_ch % 2 == 0
    mesh = plsc.VectorSubcoreMesh(core_axis_name="c", subcore_axis_name="s")

    @functools.partial(
        pl.kernel, mesh=mesh,
        out_type=jax.ShapeDtypeStruct((n_rows,) + h.shape[1:], h.dtype),
        scratch_types=[
            pltpu.VMEM((2, n_ch, SC_ROWS), jnp.int32),
            pltpu.VMEM((2, SC_ROWS) + h.shape[1:], h.dtype),
            pltpu.SemaphoreType.DMA((2,)),
            pltpu.SemaphoreType.DMA((2,)),
        ],
    )
    def k(h_hbm, idx_hbm, out_hbm, idx_v, rows_v, lsem, ssem):
        wid = _sc_worker()
        base = wid * per_w
        pltpu.sync_copy(idx_hbm.at[0, wid], idx_v.at[0])
        pltpu.sync_copy(idx_hbm.at[1, wid], idx_v.at[1])

        def load(c, slot):
            return pltpu.make_async_copy(h_hbm.at[pl.ds(base + c * SC_ROWS, SC_ROWS)], rows_v.at[slot],
                                         lsem.at[slot])

        def scatter(slot_k, c, slot):
            return pltpu.make_async_copy(rows_v.at[slot], out_hbm.at[idx_v.at[slot_k, c]], ssem.at[slot])

        load(0, 0).start()

        def body(g, carry):
            for slot in range(2):
                c = 2 * g + slot
                load(c, slot).wait()

                @pl.when(c >= 1)
                def _():
                    scatter(0, c - 1, 1 - slot).wait()
                    scatter(1, c - 1, 1 - slot).wait()

                @pl.when(c + 1 < n_ch)
                def _():
                    load(c + 1, 1 - slot).start()

                scatter(0, c, slot).start()
                scatter(1, c, slot).start()
            return carry

        lax.fori_loop(0, n_ch // 2, body, 0)
        scatter(0, n_ch - 1, 1).wait()
        scatter(1, n_ch - 1, 1).wait()

    return k(h, dest.reshape(2, n_workers, n_ch, SC_ROWS))


def _sc_gather_rows(table, idx):
    B = idx.shape[0]
    n_workers = SC_CORES * SC_SUBCORES
    per_w = B // n_workers
    n_ch = per_w // SC_ROWS
    assert per_w * n_workers == B and n_ch * SC_ROWS == per_w and n_ch % 2 == 0
    mesh = plsc.VectorSubcoreMesh(core_axis_name="c", subcore_axis_name="s")

    @functools.partial(
        pl.kernel, mesh=mesh,
        out_type=jax.ShapeDtypeStruct((B,) + table.shape[1:], table.dtype),
        scratch_types=[
            pltpu.VMEM((n_ch, SC_ROWS), jnp.int32),
            pltpu.VMEM((2, SC_ROWS) + table.shape[1:], table.dtype),
            pltpu.SemaphoreType.DMA((2,)),
            pltpu.SemaphoreType.DMA((2,)),
        ],
    )
    def k(table_hbm, idx_hbm, out_hbm, idx_v, rows_v, gsem, wsem):
        wid = _sc_worker()
        base = wid * per_w
        pltpu.sync_copy(idx_hbm.at[wid], idx_v)

        def gather(c, slot):
            return pltpu.make_async_copy(table_hbm.at[idx_v.at[c]], rows_v.at[slot], gsem.at[slot])

        def writeback(c, slot):
            return pltpu.make_async_copy(rows_v.at[slot], out_hbm.at[pl.ds(base + c * SC_ROWS, SC_ROWS)],
                                         wsem.at[slot])

        gather(0, 0).start()

        def body(g, carry):
            for slot in range(2):
                c = 2 * g + slot
                gather(c, slot).wait()

                @pl.when(c >= 1)
                def _():
                    writeback(c - 1, 1 - slot).wait()

                @pl.when(c + 1 < n_ch)
                def _():
                    gather(c + 1, 1 - slot).start()

                writeback(c, slot).start()
            return carry

        lax.fori_loop(0, n_ch // 2, body, 0)
        writeback(n_ch - 1, 1).wait()

    return k(table, idx.reshape(n_workers, n_ch, SC_ROWS))


def _expert_kernel(blk_exp_ref, n_used_ref, n_valid_ref, xs_ref, wg_ref, wu_ref, wd_ref, y_ref,
                   wg_bf, wu_bf, wd_bf):
    b = pl.program_id(0)
    used = b < n_used_ref[0]
    new_expert = (b == 0) | (blk_exp_ref[b] != blk_exp_ref[jnp.maximum(b - 1, 0)])

    @pl.when(used & new_expert)
    def _():
        wg_bf[...] = wg_ref[0].astype(BF16)
        wu_bf[...] = wu_ref[0].astype(BF16)
        wd_bf[...] = wd_ref[0].astype(BF16)

    @pl.when(used)
    def _():
        x = _load_row_tiles(xs_ref, (), RB, D_MODEL)
        row = lax.broadcasted_iota(jnp.int32, (RB, 1), 0)
        xb = jnp.where(row < n_valid_ref[b], x, 0.0).astype(BF16)
        g = jnp.dot(xb, wg_bf[...], preferred_element_type=F32)
        u = jnp.dot(xb, wu_bf[...], preferred_element_type=F32)
        hid = (g * _sigmoid(g) * u).astype(BF16)
        _store_row_tiles(y_ref, (), jnp.dot(hid, wd_bf[...], preferred_element_type=F32))

    @pl.when(jnp.logical_not(used))
    def _():
        y_ref[...] = jnp.zeros(y_ref.shape, F32)


def _experts(blk_exp, n_used, n_valid, xs, wg, wu, wd):
    n_blk = blk_exp.shape[0]
    grid_spec = pltpu.PrefetchScalarGridSpec(
        num_scalar_prefetch=3,
        grid=(n_blk,),
        in_specs=[
            pl.BlockSpec((RB * ROW_TILE, LANES), lambda b, be, nu, nv: (b, 0)),
            pl.BlockSpec((1, D_MODEL, D_EXPERT), lambda b, be, nu, nv: (be[b], 0, 0)),
            pl.BlockSpec((1, D_MODEL, D_EXPERT), lambda b, be, nu, nv: (be[b], 0, 0)),
            pl.BlockSpec((1, D_EXPERT, D_MODEL), lambda b, be, nu, nv: (be[b], 0, 0)),
        ],
        out_specs=pl.BlockSpec((RB * ROW_TILE, LANES), lambda b, be, nu, nv: (b, 0)),
        scratch_shapes=[
            pltpu.VMEM((D_MODEL, D_EXPERT), BF16),
            pltpu.VMEM((D_MODEL, D_EXPERT), BF16),
            pltpu.VMEM((D_EXPERT, D_MODEL), BF16),
        ],
    )
    return pl.pallas_call(
        _expert_kernel,
        grid_spec=grid_spec,
        out_shape=jax.ShapeDtypeStruct(xs.shape, F32),
        compiler_params=pltpu.CompilerParams(
            dimension_semantics=("arbitrary",), vmem_limit_bytes=VMEM_LIMIT),
        name="expert_mlp",
    )(blk_exp, n_used, n_valid, xs, wg, wu, wd)


def _combine_kernel(x1_ref, y0_ref, y1_ref, w_ref, o_ref):
    tm = x1_ref.shape[0]
    y0 = _load_row_tiles(y0_ref, (), tm, D_MODEL)
    y1 = _load_row_tiles(y1_ref, (), tm, D_MODEL)
    o_ref[...] = x1_ref[...] + (y0 * w_ref[:, 0:1] + y1 * w_ref[:, 1:2])


def _combine(x1, y, wts):
    T = x1.shape[0]
    tm = TM_COMB
    row = lambda i: (i, 0)
    return pl.pallas_call(
        _combine_kernel,
        grid=(T // tm,),
        in_specs=[
            pl.BlockSpec((tm, D_MODEL), row),
            pl.BlockSpec((tm * ROW_TILE, LANES), row),
            pl.BlockSpec((tm * ROW_TILE, LANES), lambda i: (i + T // tm, 0)),
            pl.BlockSpec((tm, 2), row),
        ],
        out_specs=pl.BlockSpec((tm, D_MODEL), row),
        out_shape=jax.ShapeDtypeStruct((T, D_MODEL), F32),
        compiler_params=pltpu.CompilerParams(
            dimension_semantics=("parallel",), vmem_limit_bytes=VMEM_LIMIT),
        name="moe_combine",
    )(x1, y, y, wts)


def _block_table(blk, seg, n_blk):
    cnt, pstart, pad_end = seg[0, :, 0], seg[1, :, 0], seg[3, :, 0]
    e_b = blk[0, :n_blk]
    b0 = jnp.arange(n_blk, dtype=jnp.int32) * RB
    n_valid = jnp.clip(pstart[e_b] + cnt[e_b] - b0, 0, RB)
    return e_b, pad_end[N_EXPERTS - 1:] // RB, n_valid


def kernel(x, attn_norm_g, w_in, conv_dw_w, conv_dw_b, conv_ln_g, conv_ln_b, w_conv_out,
           q_norm_g, k_norm_g, lambda_q1, lambda_k1, lambda_q2, lambda_k2, subln_g,
           w_attn_out, w_out, ffn_norm_g, w_router_group, b_router_group,
           w_router_expert, b_router_expert, w_gate_e, w_up_e, w_down_e):
    B, S, D = x.shape
    T = B * S
    l = 0
    x2 = x.reshape(T, D)

    reps = COL_Q // DA_HEAD_DIM
    qg = jnp.tile(q_norm_g[l], reps).reshape(1, COL_Q)
    kg = jnp.tile(k_norm_g[l], reps).reshape(1, COL_K)
    grp = jnp.arange(NORM_BLOCK, dtype=jnp.int32) // DA_HEAD_DIM
    bd = jnp.where(grp[:, None] == grp[None, :], 1.0 / DA_HEAD_DIM, 0.0).astype(BF16)

    glu, q, k, v, gates = _in_projection(
        x2, attn_norm_g[l].reshape(1, D), w_in[l].astype(BF16), qg, kg, bd)

    conv_act = _conv_branch(glu.reshape(B, S, CONV_CH), conv_dw_w[l],
                            conv_dw_b[l].reshape(1, CONV_CH), conv_ln_g[l].reshape(1, CONV_CH),
                            conv_ln_b[l].reshape(1, CONV_CH))

    lam_vecs = jnp.stack([lambda_q1[l], lambda_k1[l], lambda_q2[l], lambda_k2[l]]).astype(F32)
    attn_o = _diff_attention(q.reshape(B, S, COL_Q), k.reshape(B, S, COL_K),
                             v.reshape(B, S, COL_V), lam_vecs, subln_g[l].reshape(1, DA_V_DIM))

    e_lo, e_hi = ROUTER_EXPERT_COL, ROUTER_EXPERT_COL + N_EXPERTS
    wr = jnp.zeros((D, ROUTER_LANES), F32)
    wr = wr.at[:, :N_GROUPS].set(w_router_group[l]).at[:, e_lo:e_hi].set(w_router_expert[l])
    br = jnp.zeros((1, ROUTER_LANES), F32)
    br = br.at[0, :N_GROUPS].set(b_router_group[l]).at[0, e_lo:e_hi].set(b_router_expert[l])

    wr_hi = wr.astype(BF16)
    wr_lo = (wr - wr_hi.astype(F32)).astype(BF16)
    x1, h2, logits = _merge(conv_act.reshape(T, CONV_CH), attn_o.reshape(T, ATTN_W), gates, x2,
                            w_conv_out[l].astype(BF16), w_attn_out[l].astype(BF16),
                            w_out[l].astype(BF16), ffn_norm_g[l].reshape(1, D),
                            jnp.concatenate([wr_hi, wr_lo], axis=1), br)

    dest, wts, blk, seg = _route(logits)
    n_rows = 2 * T + N_EXPERTS * RB
    n_blk = n_rows // RB
    blk_exp, n_used, n_valid = _block_table(blk, seg, n_blk)

    xs = _sc_scatter_rows(h2.reshape(T, ROW_TILE, LANES), dest, n_rows)
    yb = _experts(blk_exp, n_used, n_valid, xs.reshape(n_rows * ROW_TILE, LANES),
                  w_gate_e[l], w_up_e[l], w_down_e[l])
    y = _sc_gather_rows(yb.reshape(n_rows, ROW_TILE, LANES), dest.reshape(2 * T))

    out = _combine(x1, y.reshape(2 * T * ROW_TILE, LANES), wts.T)
    return out.reshape(B, S, D)
```

```python
import functools
import math

import jax
import jax.numpy as jnp
from jax import lax
from jax.experimental import pallas as pl
from jax.experimental.pallas import tpu as pltpu
from jax.experimental.pallas import tpu_sc as plsc

F32 = jnp.float32
BF16 = jnp.bfloat16

D_MODEL = 1024
CONV_CH = 512
CONV_WIDTH = 31
DA_HEADS = 4
DA_HEAD_DIM = 64
DA_V_DIM = 128
ATTN_W = 512
N_GROUPS = 4
EXPERTS_PER_GROUP = 8
N_EXPERTS = 32
D_EXPERT = 512
EPS = 1e-6
LAM_INIT = 0.8 - 0.6 * math.exp(-0.3 * 0)

COL_GLU = 2 * CONV_CH
COL_Q = 512
COL_K = 512
COL_V = 512
COL_GATE = 2 * D_MODEL
OFF_Q = COL_GLU
OFF_K = OFF_Q + COL_Q
OFF_V = OFF_K + COL_K
OFF_GATE = OFF_V + COL_V
IN_COLS = OFF_GATE + COL_GATE

ROUTER_LANES = 128
ROUTER_EXPERT_COL = 8
SUBLANES = 8
ROW_TILE = D_MODEL // 128
NORM_BLOCK = 256
HALO = 32

TM_PROJ = 512
TS_CONV = 512
CONV_CHUNK = 64
ATT_TQ = 1024
ATT_TK = 512
ATT_RC = 256
LANES = 128
RB = 512
TM_COMB = 512
TM_ROUTE = 512
SC_CORES = 2
SC_SUBCORES = 16
SC_ROWS = 32

VMEM_LIMIT = 48 * 1024 * 1024


def _sigmoid(x):
    return 1.0 / (1.0 + jnp.exp(-x))


def _const_spec(shape):
    nd = len(shape)
    return pl.BlockSpec(shape, lambda *_: (0,) * nd, pipeline_mode=pl.Buffered(1))


def _store_row_tiles(ref, lead, val):
    rows = val.shape[0]
    for j in range(val.shape[1] // LANES):
        ref[(*lead, pl.ds(j, rows, stride=ROW_TILE), slice(None))] = val[:, j * LANES:(j + 1) * LANES]


def _load_row_tiles(ref, lead, rows, width):
    return jnp.concatenate(
        [ref[(*lead, pl.ds(j, rows, stride=ROW_TILE), slice(None))] for j in range(width // LANES)], axis=1)


def _group_mean_sq(t, bd_ref):
    sq = t * t
    hi = sq.astype(BF16)
    lo = (sq - hi.astype(F32)).astype(BF16)
    w = bd_ref.shape[0]
    parts = [jnp.dot(hi[:, c:c + w], bd_ref[...], preferred_element_type=F32)
             + jnp.dot(lo[:, c:c + w], bd_ref[...], preferred_element_type=F32)
             for c in range(0, t.shape[1], w)]
    return jnp.concatenate(parts, axis=1)


def _inproj_kernel(x_ref, g_ref, w_ref, qg_ref, kg_ref, bd_ref,
                   glu_ref, q_ref, k_ref, v_ref, gate_ref):
    x = x_ref[...]
    ms = jnp.mean(x * x, axis=-1, keepdims=True)
    h = (x * lax.rsqrt(ms + EPS) * g_ref[...]).astype(BF16)

    def proj(lo, width):
        return jnp.dot(h, w_ref[:, lo:lo + width], preferred_element_type=F32)

    a = proj(0, CONV_CH)
    gt = proj(CONV_CH, CONV_CH)
    glu_ref[...] = (a * _sigmoid(gt)).astype(BF16)

    q = proj(OFF_Q, COL_Q)
    qn = q * lax.rsqrt(_group_mean_sq(q, bd_ref) + EPS) * qg_ref[...]
    q_ref[...] = (qn * (DA_HEAD_DIM ** -0.5)).astype(BF16)

    k = proj(OFF_K, COL_K)
    kn = k * lax.rsqrt(_group_mean_sq(k, bd_ref) + EPS) * kg_ref[...]
    k_ref[...] = kn.astype(BF16)

    v_ref[...] = proj(OFF_V, COL_V).astype(BF16)

    for c in range(COL_GATE // 512):
        gate_ref[:, c * 512:(c + 1) * 512] = _sigmoid(proj(OFF_GATE + c * 512, 512)).astype(BF16)


def _in_projection(x2, g, w_bf, qg, kg, bd):
    T = x2.shape[0]
    tm = TM_PROJ
    row = lambda i: (i, 0)
    out_shape = (
        jax.ShapeDtypeStruct((T, CONV_CH), BF16),
        jax.ShapeDtypeStruct((T, COL_Q), BF16),
        jax.ShapeDtypeStruct((T, COL_K), BF16),
        jax.ShapeDtypeStruct((T, COL_V), BF16),
        jax.ShapeDtypeStruct((T, COL_GATE), BF16),
    )
    return pl.pallas_call(
        _inproj_kernel,
        grid=(T // tm,),
        in_specs=[
            pl.BlockSpec((tm, D_MODEL), row),
            _const_spec((1, D_MODEL)),
            _const_spec((D_MODEL, IN_COLS)),
            _const_spec((1, COL_Q)),
            _const_spec((1, COL_K)),
            _const_spec((NORM_BLOCK, NORM_BLOCK)),
        ],
        out_specs=(
            pl.BlockSpec((tm, CONV_CH), row),
            pl.BlockSpec((tm, COL_Q), row),
            pl.BlockSpec((tm, COL_K), row),
            pl.BlockSpec((tm, COL_V), row),
            pl.BlockSpec((tm, COL_GATE), row),
        ),
        out_shape=out_shape,
        compiler_params=pltpu.CompilerParams(
            dimension_semantics=("parallel",), vmem_limit_bytes=VMEM_LIMIT),
        name="in_projection",
    )(x2, g, w_bf, qg, kg, bd)


def _conv_kernel(cur_ref, halo_ref, w_ref, b_ref, lng_ref, lnb_ref, out_ref, buf_ref):
    i = pl.program_id(1)
    ts = cur_ref.shape[1]
    rows = HALO + ts
    halo = halo_ref[0].astype(F32)
    buf_ref[0, 0:HALO, :] = jnp.where(i > 0, halo, 0.0)
    buf_ref[0, HALO:rows, :] = cur_ref[0].astype(F32)
    for r in range(1, SUBLANES):
        buf_ref[r, 0:rows - SUBLANES, :] = buf_ref[0, r:r + rows - SUBLANES, :]

    first = HALO - (CONV_WIDTH - 1)
    for c in range(ts // CONV_CHUNK):
        r0 = c * CONV_CHUNK
        acc = jnp.broadcast_to(b_ref[...], (CONV_CHUNK, CONV_CH))
        for j in range(CONV_WIDTH):
            shift = (first + j) % SUBLANES
            lo = r0 + first + j - shift
            acc = acc + w_ref[j:j + 1, :] * buf_ref[shift, lo:lo + CONV_CHUNK, :]
        mu = jnp.mean(acc, axis=-1, keepdims=True)
        d = acc - mu
        var = jnp.mean(d * d, axis=-1, keepdims=True)
        y = d * lax.rsqrt(var + EPS) * lng_ref[...] + lnb_ref[...]
        out_ref[0, r0:r0 + CONV_CHUNK, :] = (y * _sigmoid(y)).astype(BF16)


def _conv_branch(glu3, dw_w, dw_b, ln_g, ln_b):
    B, S, C = glu3.shape
    ts = TS_CONV
    per = ts // HALO
    return pl.pallas_call(
        _conv_kernel,
        grid=(B, S // ts),
        in_specs=[
            pl.BlockSpec((1, ts, C), lambda b, i: (b, i, 0)),
            pl.BlockSpec((1, HALO, C), lambda b, i: (b, jnp.maximum(i * per - 1, 0), 0)),
            _const_spec((CONV_WIDTH, C)),
            _const_spec((1, C)),
            _const_spec((1, C)),
            _const_spec((1, C)),
        ],
        out_specs=pl.BlockSpec((1, ts, C), lambda b, i: (b, i, 0)),
        out_shape=jax.ShapeDtypeStruct((B, S, C), BF16),
        scratch_shapes=[pltpu.VMEM((SUBLANES, HALO + ts, C), F32)],
        compiler_params=pltpu.CompilerParams(
            dimension_semantics=("parallel", "parallel"), vmem_limit_bytes=VMEM_LIMIT),
        name="conv_branch",
    )(glu3, glu3, dw_w, dw_b, ln_g, ln_b)


def _attn_kernel(q_ref, k_ref, v_ref, lam_ref, sg_ref, o_ref, qs_ref, m_ref, acc_ref, s_ref, *, tq, tk):
    qi = pl.program_id(2)
    q = q_ref[0]
    lane = lax.broadcasted_iota(jnp.int32, q.shape, 1)
    zero = jnp.zeros_like(q)
    qs_ref[0:tq, :] = jnp.where(lane < DA_HEAD_DIM, q, zero)
    qs_ref[tq:2 * tq, :] = jnp.where(lane >= DA_HEAD_DIM, q, zero)
    m_ref[...] = jnp.full(m_ref.shape, -jnp.inf, F32)
    acc_ref[...] = jnp.zeros(acc_ref.shape, F32)
    ones = jnp.ones((tk, LANES), BF16)
    rc = ATT_RC
    chunks = [slice(r0, r0 + rc) for r0 in range(0, 2 * tq, rc)]
    n_diag = tq // tk

    def scores(j, rows):
        kj = k_ref[0, pl.ds(pl.multiple_of(j * tk, tk), tk), :]
        return lax.dot_general(qs_ref[rows, :], kj, (((1,), (1,)), ((), ())),
                               preferred_element_type=F32)

    def step(j, diag, prefetch):
        vj = jnp.concatenate([v_ref[0, pl.ds(pl.multiple_of(j * tk, tk), tk), :], ones], axis=1)
        for rows in chunks:
            r_lo = rows.start % tq
            if diag is None:
                kw, masked = tk, False
            else:
                k_lo = diag * tk
                if k_lo > r_lo + rc - 1:
                    if prefetch:
                        s_ref[rows, :] = scores(j + 1, rows)
                    continue
                kw = min(tk, -(-(r_lo + rc - k_lo) // (2 * LANES)) * (2 * LANES))
                masked = k_lo + kw - 1 > r_lo
            s = s_ref[rows, 0:kw]
            if prefetch:
                s_ref[rows, :] = scores(j + 1, rows)
            if masked:
                r = lax.broadcasted_iota(jnp.int32, s.shape, 0) + r_lo
                c = lax.broadcasted_iota(jnp.int32, s.shape, 1) + k_lo
                s = jnp.where(c <= r, s, -jnp.inf)
            m_old = m_ref[rows, :]
            m_new = jnp.maximum(m_old, jnp.max(s, axis=-1, keepdims=True))
            alpha = jnp.exp(m_old - m_new)
            p = jnp.exp(s - jnp.tile(m_new, (1, kw // LANES)))
            pv = jnp.dot(p.astype(BF16), vj[0:kw, :], preferred_element_type=F32)
            acc_ref[rows, :] = jnp.tile(alpha, (1, 2)) * acc_ref[rows, :] + pv
            m_ref[rows, :] = m_new

    for rows in chunks:
        s_ref[rows, :] = scores(0, rows)

    def body(j, carry):
        step(j, None, True)
        return carry

    n_full = qi * n_diag
    lax.fori_loop(0, n_full, body, 0)
    for t in range(n_diag):
        step(n_full + t, t, t + 1 < n_diag)

    lam = (jnp.exp(jnp.sum(lam_ref[0:1, :] * lam_ref[1:2, :], axis=-1, keepdims=True))
           - jnp.exp(jnp.sum(lam_ref[2:3, :] * lam_ref[3:4, :], axis=-1, keepdims=True))
           + LAM_INIT)
    o1 = acc_ref[0:tq, 0:LANES] / acc_ref[0:tq, LANES:2 * LANES]
    o2 = acc_ref[tq:2 * tq, 0:LANES] / acc_ref[tq:2 * tq, LANES:2 * LANES]
    o = o1 - lam * o2
    ms = jnp.mean(o * o, axis=-1, keepdims=True)
    on = o * lax.rsqrt(ms + EPS) * sg_ref[...]
    o_ref[0] = (on * (1.0 - LAM_INIT)).astype(BF16)


def _diff_attention(q3, k3, v3, lam_vecs, subln_g):
    B, S, _ = q3.shape
    tq, tk = ATT_TQ, ATT_TK
    return pl.pallas_call(
        functools.partial(_attn_kernel, tq=tq, tk=tk),
        grid=(B, DA_HEADS, S // tq),
        in_specs=[
            pl.BlockSpec((1, tq, DA_V_DIM), lambda b, h, i: (b, i, h)),
            pl.BlockSpec((1, S, DA_V_DIM), lambda b, h, i: (b, 0, h)),
            pl.BlockSpec((1, S, DA_V_DIM), lambda b, h, i: (b, 0, h)),
            _const_spec((4, DA_HEAD_DIM)),
            _const_spec((1, DA_V_DIM)),
        ],
        out_specs=pl.BlockSpec((1, tq, DA_V_DIM), lambda b, h, i: (b, i, h)),
        out_shape=jax.ShapeDtypeStruct((B, S, ATTN_W), BF16),
        scratch_shapes=[
            pltpu.VMEM((2 * tq, DA_V_DIM), BF16),
            pltpu.VMEM((2 * tq, LANES), F32),
            pltpu.VMEM((2 * tq, 2 * LANES), F32),
            pltpu.VMEM((2 * tq, tk), F32),
        ],
        compiler_params=pltpu.CompilerParams(
            dimension_semantics=("parallel", "parallel", "parallel"),
            vmem_limit_bytes=VMEM_LIMIT),
        name="diff_attention",
    )(q3, k3, v3, lam_vecs, subln_g)


def _merge_kernel(conv_ref, attn_ref, gate_ref, x_ref, wc_ref, wa_ref, wo_ref, fg_ref,
                  wr_ref, br_ref, x1_ref, h2_ref, logit_ref):
    c = jnp.dot(conv_ref[...], wc_ref[...], preferred_element_type=F32)
    a = jnp.dot(attn_ref[...], wa_ref[...], preferred_element_type=F32)
    g0 = gate_ref[:, 0:D_MODEL].astype(F32)
    g1 = gate_ref[:, D_MODEL:2 * D_MODEL].astype(F32)
    merged = (g0 * c + g1 * a).astype(BF16)
    x1 = x_ref[...] + jnp.dot(merged, wo_ref[...], preferred_element_type=F32)
    x1_ref[...] = x1
    ms = jnp.mean(x1 * x1, axis=-1, keepdims=True)
    h2 = x1 * lax.rsqrt(ms + EPS) * fg_ref[...]
    _store_row_tiles(h2_ref, (), h2)
    h2_hi = h2.astype(BF16)
    h2_lo = (h2 - h2_hi.astype(F32)).astype(BF16)
    r = (jnp.dot(h2_hi, wr_ref[...], preferred_element_type=F32)
         + jnp.dot(h2_lo, wr_ref[...], preferred_element_type=F32))
    logit_ref[...] = r[:, 0:ROUTER_LANES] + r[:, ROUTER_LANES:2 * ROUTER_LANES] + br_ref[...]


def _merge(conv_act, attn_o, gates, x2, wc, wa, wo, fg, wr, br):
    T = x2.shape[0]
    tm = TM_PROJ
    row = lambda i: (i, 0)
    return pl.pallas_call(
        _merge_kernel,
        grid=(T // tm,),
        in_specs=[
            pl.BlockSpec((tm, CONV_CH), row),
            pl.BlockSpec((tm, ATTN_W), row),
            pl.BlockSpec((tm, COL_GATE), row),
            pl.BlockSpec((tm, D_MODEL), row),
            _const_spec((CONV_CH, D_MODEL)),
            _const_spec((ATTN_W, D_MODEL)),
            _const_spec((D_MODEL, D_MODEL)),
            _const_spec((1, D_MODEL)),
            _const_spec((D_MODEL, 2 * ROUTER_LANES)),
            _const_spec((1, ROUTER_LANES)),
        ],
        out_specs=(
            pl.BlockSpec((tm, D_MODEL), row),
            pl.BlockSpec((tm * ROW_TILE, LANES), row),
            pl.BlockSpec((tm, ROUTER_LANES), row),
        ),
        out_shape=(
            jax.ShapeDtypeStruct((T, D_MODEL), F32),
            jax.ShapeDtypeStruct((T * ROW_TILE, LANES), F32),
            jax.ShapeDtypeStruct((T, ROUTER_LANES), F32),
        ),
        compiler_params=pltpu.CompilerParams(
            dimension_semantics=("parallel",), vmem_limit_bytes=VMEM_LIMIT),
        name="merge_out_router",
    )(conv_act, attn_o, gates, x2, wc, wa, wo, fg, wr, br)


def _route_kernel(logit_ref, tri_ref, low_ref, dest_ref, wts_ref, blk_ref, seg_ref,
                  cnt_ref, base_ref, *, n_blk_lanes):
    ph = pl.program_id(0)
    i = pl.program_id(1)
    tm = logit_ref.shape[0]
    lt = logit_ref[...].T
    row8 = lax.broadcasted_iota(jnp.int32, (SUBLANES, tm), 0)

    g = jnp.where(row8 < N_GROUPS, lt[0:SUBLANES, :], -jnp.inf)
    g_max = jnp.max(g, axis=0, keepdims=True)
    g_sum = jnp.sum(jnp.exp(g - g_max), axis=0, keepdims=True)
    g_sel = jnp.min(jnp.where(g == g_max, row8, SUBLANES), axis=0, keepdims=True)
    g_w = 1.0 / g_sum

    e_sel = jnp.zeros((EXPERTS_PER_GROUP, tm), F32)
    for gi in range(N_GROUPS):
        lo = ROUTER_EXPERT_COL + gi * EXPERTS_PER_GROUP
        e_sel = jnp.where(g_sel == gi, lt[lo:lo + EXPERTS_PER_GROUP, :], e_sel)
    e_max = jnp.max(e_sel, axis=0, keepdims=True)
    e_exp = jnp.exp(e_sel - e_max)
    prob = e_exp / jnp.sum(e_exp, axis=0, keepdims=True)
    p1 = jnp.max(prob, axis=0, keepdims=True)
    i1 = jnp.min(jnp.where(prob == p1, row8, SUBLANES), axis=0, keepdims=True)
    rest = jnp.where(row8 == i1, -1.0, prob)
    p2 = jnp.max(rest, axis=0, keepdims=True)
    i2 = jnp.min(jnp.where(rest == p2, row8, SUBLANES), axis=0, keepdims=True)
    denom = p1 + p2
    e0 = g_sel * EXPERTS_PER_GROUP + i1
    e1 = g_sel * EXPERTS_PER_GROUP + i2

    row32 = lax.broadcasted_iota(jnp.int32, (N_EXPERTS, tm), 0)
    oh0 = row32 == e0
    oh1 = row32 == e1
    ohs = jnp.where(oh0 | oh1, 1.0, 0.0)
    tile_cnt = jnp.sum(ohs, axis=1, keepdims=True)

    @pl.when((ph == 0) & (i == 0))
    def _():
        cnt_ref[...] = jnp.zeros(cnt_ref.shape, F32)

    @pl.when(ph == 0)
    def _():
        cnt_ref[...] += tile_cnt

    @pl.when((ph == 1) & (i == 0))
    def _():
        cnt = cnt_ref[...]
        padded = jnp.ceil(cnt * (1.0 / RB)) * RB
        pstart = jnp.dot(low_ref[...], padded, preferred_element_type=F32,
                         precision=lax.Precision.HIGHEST)
        start = jnp.dot(low_ref[...], cnt, preferred_element_type=F32,
                        precision=lax.Precision.HIGHEST)
        base_ref[...] = pstart
        pad_end = pstart + padded
        lane_row = lax.broadcasted_iota(jnp.int32, (N_EXPERTS, n_blk_lanes), 1).astype(F32) * RB
        before = jnp.where(pad_end[:, 0:1] <= lane_row, 1.0, 0.0)
        blk = jnp.minimum(jnp.sum(before, axis=0, keepdims=True), N_EXPERTS - 1.0)
        blk_ref[...] = blk.astype(jnp.int32)
        seg_ref[0] = cnt.astype(jnp.int32)
        seg_ref[1] = pstart.astype(jnp.int32)
        seg_ref[2] = start.astype(jnp.int32)
        seg_ref[3] = pad_end.astype(jnp.int32)

    @pl.when(ph == 1)
    def _():
        before = jnp.dot(ohs.astype(BF16), tri_ref[...], preferred_element_type=F32)
        pos = base_ref[:, 0:1] + before
        d0 = jnp.sum(jnp.where(oh0, pos, 0.0), axis=0, keepdims=True)
        d1 = jnp.sum(jnp.where(oh1, pos, 0.0), axis=0, keepdims=True)
        dest_ref[0:1, :] = d0.astype(jnp.int32)
        dest_ref[1:2, :] = d1.astype(jnp.int32)
        wts_ref[0:1, :] = p1 / denom * g_w
        wts_ref[1:2, :] = p2 / denom * g_w
        base_ref[...] += tile_cnt


def _route(logits):
    T = logits.shape[0]
    tm = TM_ROUTE
    n_rows = 2 * T + N_EXPERTS * RB
    n_blk = n_rows // RB
    n_blk_lanes = -(-n_blk // LANES) * LANES
    r = jnp.arange(tm, dtype=jnp.int32)
    tri = (r[:, None] < r[None, :]).astype(BF16)
    e = jnp.arange(N_EXPERTS, dtype=jnp.int32)
    low = (e[None, :] < e[:, None]).astype(F32)
    dest, wts, blk, seg = pl.pallas_call(
        functools.partial(_route_kernel, n_blk_lanes=n_blk_lanes),
        grid=(2, T // tm),
        in_specs=[
            pl.BlockSpec((tm, ROUTER_LANES), lambda ph, i: (i, 0)),
            _const_spec((tm, tm)),
            _const_spec((N_EXPERTS, N_EXPERTS)),
        ],
        out_specs=(
            pl.BlockSpec((2, tm), lambda ph, i: (0, i * ph)),
            pl.BlockSpec((2, tm), lambda ph, i: (0, i * ph)),
            pl.BlockSpec((1, n_blk_lanes), lambda ph, i: (0, 0)),
            pl.BlockSpec((4, N_EXPERTS, LANES), lambda ph, i: (0, 0, 0)),
        ),
        out_shape=(
            jax.ShapeDtypeStruct((2, T), jnp.int32),
            jax.ShapeDtypeStruct((2, T), F32),
            jax.ShapeDtypeStruct((1, n_blk_lanes), jnp.int32),
            jax.ShapeDtypeStruct((4, N_EXPERTS, LANES), jnp.int32),
        ),
        scratch_shapes=[pltpu.VMEM((N_EXPERTS, LANES), F32), pltpu.VMEM((N_EXPERTS, LANES), F32)],
        compiler_params=pltpu.CompilerParams(
            dimension_semantics=("arbitrary", "arbitrary"), vmem_limit_bytes=VMEM_LIMIT),
        name="route_plan",
    )(logits, tri, low)
    return dest, wts, blk, seg


def _sc_worker():
    return lax.axis_index("s") * SC_CORES + lax.axis_index("c")


def _sc_scatter_rows(h, dest, n_rows):
    T = h.shape[0]
    n_workers = SC_CORES * SC_SUBCORES
    per_w = T // n_workers
    n_ch = per_w // SC_ROWS
    assert per_w * n_workers == T and n_ch * SC_ROWS == per_w and n_ch % 2 == 0
    mesh = plsc.VectorSubcoreMesh(core_axis_name="c", subcore_axis_name="s")

    @functools.partial(
        pl.kernel, mesh=mesh,
        out_type=jax.ShapeDtypeStruct((n_rows,) + h.shape[1:], h.dtype),
        scratch_types=[
            pltpu.VMEM((2, n_ch, SC_ROWS), jnp.int32),
            pltpu.VMEM((2, SC_ROWS) + h.shape[1:], h.dtype),
            pltpu.SemaphoreType.DMA((2,)),
            pltpu.SemaphoreType.DMA((2,)),
        ],
    )
    def k(h_hbm, idx_hbm, out_hbm, idx_v, rows_v, lsem, ssem):
        wid = _sc_worker()
        base = wid * per_w
        pltpu.sync_copy(idx_hbm.at[0, wid], idx_v.at[0])
        pltpu.sync_copy(idx_hbm.at[1, wid], idx_v.at[1])

        def load(c, slot):
            return pltpu.make_async_copy(h_hbm.at[pl.ds(base + c * SC_ROWS, SC_ROWS)], rows_v.at[slot],
                                         lsem.at[slot])

        def scatter(slot_k, c, slot):
            return pltpu.make_async_copy(rows_v.at[slot], out_hbm.at[idx_v.at[slot_k, c]], ssem.at[slot])

        load(0, 0).start()

        def body(g, carry):
            for slot in range(2):
                c = 2 * g + slot
                load(c, slot).wait()

                @pl.when(c >= 1)
                def _():
                    scatter(0, c - 1, 1 - slot).wait()
                    scatter(1, c - 1, 1 - slot).wait()

                @pl.when(c + 1 < n_ch)
                def _():
                    load(c + 1, 1 - slot).start()

                scatter(0, c, slot).start()
                scatter(1, c, slot).start()
            return carry

        lax.fori_loop(0, n_ch // 2, body, 0)
        scatter(0, n_ch - 1, 1).wait()
        scatter(1, n_ch - 1, 1).wait()

    return k(h, dest.reshape(2, n_workers, n_ch, SC_ROWS))


def _sc_gather_rows(table, idx):
    B = idx.shape[0]
    n_workers = SC_CORES * SC_SUBCORES
    per_w = B // n_workers
    n_ch = per_w // SC_ROWS
    assert per_w * n_workers == B and n_ch * SC_ROWS == per_w and n_ch % 2 == 0
    mesh = plsc.VectorSubcoreMesh(core_axis_name="c", subcore_axis_name="s")

    @functools.partial(
        pl.kernel, mesh=mesh,
        out_type=jax.ShapeDtypeStruct((B,) + table.shape[1:], table.dtype),
        scratch_types=[
            pltpu.VMEM((n_ch, SC_ROWS), jnp.int32),
            pltpu.VMEM((2, SC_ROWS) + table.shape[1:], table.dtype),
            pltpu.SemaphoreType.DMA((2,)),
            pltpu.SemaphoreType.DMA((2,)),
        ],
    )
    def k(table_hbm, idx_hbm, out_hbm, idx_v, rows_v, gsem, wsem):
        wid = _sc_worker()
        base = wid * per_w
        pltpu.sync_copy(idx_hbm.at[wid], idx_v)

        def gather(c, slot):
            return pltpu.make_async_copy(table_hbm.at[idx_v.at[c]], rows_v.at[slot], gsem.at[slot])

        def writeback(c, slot):
            return pltpu.make_async_copy(rows_v.at[slot], out_hbm.at[pl.ds(base + c * SC_ROWS, SC_ROWS)],
                                         wsem.at[slot])

        gather(0, 0).start()

        def body(g, carry):
            for slot in range(2):
                c = 2 * g + slot
                gather(c, slot).wait()

                @pl.when(c >= 1)
                def _():
                    writeback(c - 1, 1 - slot).wait()

                @pl.when(c + 1 < n_ch)
                def _():
                    gather(c + 1, 1 - slot).start()

                writeback(c, slot).start()
            return carry

        lax.fori_loop(0, n_ch // 2, body, 0)
        writeback(n_ch - 1, 1).wait()

    return k(table, idx.reshape(n_workers, n_ch, SC_ROWS))


def _expert_kernel(blk_exp_ref, n_used_ref, n_valid_ref, xs_ref, wg_ref, wu_ref, wd_ref, y_ref,
                   wg_bf, wu_bf, wd_bf):
    b = pl.program_id(0)
    used = b < n_used_ref[0]
    new_expert = (b == 0) | (blk_exp_ref[b] != blk_exp_ref[jnp.maximum(b - 1, 0)])

    @pl.when(used & new_expert)
    def _():
        wg_bf[...] = wg_ref[0].astype(BF16)
        wu_bf[...] = wu_ref[0].astype(BF16)
        wd_bf[...] = wd_ref[0].astype(BF16)

    @pl.when(used)
    def _():
        x = _load_row_tiles(xs_ref, (), RB, D_MODEL)
        row = lax.broadcasted_iota(jnp.int32, (RB, 1), 0)
        xb = jnp.where(row < n_valid_ref[b], x, 0.0).astype(BF16)
        g = jnp.dot(xb, wg_bf[...], preferred_element_type=F32)
        u = jnp.dot(xb, wu_bf[...], preferred_element_type=F32)
        hid = (g * _sigmoid(g) * u).astype(BF16)
        _store_row_tiles(y_ref, (), jnp.dot(hid, wd_bf[...], preferred_element_type=F32))

    @pl.when(jnp.logical_not(used))
    def _():
        y_ref[...] = jnp.zeros(y_ref.shape, F32)


def _experts(blk_exp, n_used, n_valid, xs, wg, wu, wd):
    n_blk = blk_exp.shape[0]
    grid_spec = pltpu.PrefetchScalarGridSpec(
        num_scalar_prefetch=3,
        grid=(n_blk,),
        in_specs=[
            pl.BlockSpec((RB * ROW_TILE, LANES), lambda b, be, nu, nv: (b, 0)),
            pl.BlockSpec((1, D_MODEL, D_EXPERT), lambda b, be, nu, nv: (be[b], 0, 0)),
            pl.BlockSpec((1, D_MODEL, D_EXPERT), lambda b, be, nu, nv: (be[b], 0, 0)),
            pl.BlockSpec((1, D_EXPERT, D_MODEL), lambda b, be, nu, nv: (be[b], 0, 0)),
        ],
        out_specs=pl.BlockSpec((RB * ROW_TILE, LANES), lambda b, be, nu, nv: (b, 0)),
        scratch_shapes=[
            pltpu.VMEM((D_MODEL, D_EXPERT), BF16),
            pltpu.VMEM((D_MODEL, D_EXPERT), BF16),
            pltpu.VMEM((D_EXPERT, D_MODEL), BF16),
        ],
    )
    return pl.pallas_call(
        _expert_kernel,
        grid_spec=grid_spec,
        out_shape=jax.ShapeDtypeStruct(xs.shape, F32),
        compiler_params=pltpu.CompilerParams(
            dimension_semantics=("arbitrary",), vmem_limit_bytes=VMEM_LIMIT),
        name="expert_mlp",
    )(blk_exp, n_used, n_valid, xs, wg, wu, wd)


def _combine_kernel(x1_ref, y0_ref, y1_ref, w_ref, o_ref):
    tm = x1_ref.shape[0]
    y0 = _load_row_tiles(y0_ref, (), tm, D_MODEL)
    y1 = _load_row_tiles(y1_ref, (), tm, D_MODEL)
    o_ref[...] = x1_ref[...] + (y0 * w_ref[:, 0:1] + y1 * w_ref[:, 1:2])


def _combine(x1, y, wts):
    T = x1.shape[0]
    tm = TM_COMB
    row = lambda i: (i, 0)
    return pl.pallas_call(
        _combine_kernel,
        grid=(T // tm,),
        in_specs=[
            pl.BlockSpec((tm, D_MODEL), row),
            pl.BlockSpec((tm * ROW_TILE, LANES), row),
            pl.BlockSpec((tm * ROW_TILE, LANES), lambda i: (i + T // tm, 0)),
            pl.BlockSpec((tm, 2), row),
        ],
        out_specs=pl.BlockSpec((tm, D_MODEL), row),
        out_shape=jax.ShapeDtypeStruct((T, D_MODEL), F32),
        compiler_params=pltpu.CompilerParams(
            dimension_semantics=("parallel",), vmem_limit_bytes=VMEM_LIMIT),
        name="moe_combine",
    )(x1, y, y, wts)


def _block_table(blk, seg, n_blk):
    cnt, pstart, pad_end = seg[0, :, 0], seg[1, :, 0], seg[3, :, 0]
    e_b = blk[0, :n_blk]
    b0 = jnp.arange(n_blk, dtype=jnp.int32) * RB
    n_valid = jnp.clip(pstart[e_b] + cnt[e_b] - b0, 0, RB)
    return e_b, pad_end[N_EXPERTS - 1:] // RB, n_valid


def kernel(x, attn_norm_g, w_in, conv_dw_w, conv_dw_b, conv_ln_g, conv_ln_b, w_conv_out,
           q_norm_g, k_norm_g, lambda_q1, lambda_k1, lambda_q2, lambda_k2, subln_g,
           w_attn_out, w_out, ffn_norm_g, w_router_group, b_router_group,
           w_router_expert, b_router_expert, w_gate_e, w_up_e, w_down_e):
    B, S, D = x.shape
    T = B * S
    l = 0
    x2 = x.reshape(T, D)

    reps = COL_Q // DA_HEAD_DIM
    qg = jnp.tile(q_norm_g[l], reps).reshape(1, COL_Q)
    kg = jnp.tile(k_norm_g[l], reps).reshape(1, COL_K)
    grp = jnp.arange(NORM_BLOCK, dtype=jnp.int32) // DA_HEAD_DIM
    bd = jnp.where(grp[:, None] == grp[None, :], 1.0 / DA_HEAD_DIM, 0.0).astype(BF16)

    glu, q, k, v, gates = _in_projection(
        x2, attn_norm_g[l].reshape(1, D), w_in[l].astype(BF16), qg, kg, bd)

    conv_act = _conv_branch(glu.reshape(B, S, CONV_CH), conv_dw_w[l],
                            conv_dw_b[l].reshape(1, CONV_CH), conv_ln_g[l].reshape(1, CONV_CH),
                            conv_ln_b[l].reshape(1, CONV_CH))

    lam_vecs = jnp.stack([lambda_q1[l], lambda_k1[l], lambda_q2[l], lambda_k2[l]]).astype(F32)
    attn_o = _diff_attention(q.reshape(B, S, COL_Q), k.reshape(B, S, COL_K),
                             v.reshape(B, S, COL_V), lam_vecs, subln_g[l].reshape(1, DA_V_DIM))

    e_lo, e_hi = ROUTER_EXPERT_COL, ROUTER_EXPERT_COL + N_EXPERTS
    wr = jnp.zeros((D, ROUTER_LANES), F32)
    wr = wr.at[:, :N_GROUPS].set(w_router_group[l]).at[:, e_lo:e_hi].set(w_router_expert[l])
    br = jnp.zeros((1, ROUTER_LANES), F32)
    br = br.at[0, :N_GROUPS].set(b_router_group[l]).at[0, e_lo:e_hi].set(b_router_expert[l])

    wr_hi = wr.astype(BF16)
    wr_lo = (wr - wr_hi.astype(F32)).astype(BF16)
    x1, h2, logits = _merge(conv_act.reshape(T, CONV_CH), attn_o.reshape(T, ATTN_W), gates, x2,
                            w_conv_out[l].astype(BF16), w_attn_out[l].astype(BF16),
                            w_out[l].astype(BF16), ffn_norm_g[l].reshape(1, D),
                            jnp.concatenate([wr_hi, wr_lo], axis=1), br)

    dest, wts, blk, seg = _route(logits)
    n_rows = 2 * T + N_EXPERTS * RB
    n_blk = n_rows // RB
    blk_exp, n_used, n_valid = _block_table(blk, seg, n_blk)

    xs = _sc_scatter_rows(h2.reshape(T, ROW_TILE, LANES), dest, n_rows)
    yb = _experts(blk_exp, n_used, n_valid, xs.reshape(n_rows * ROW_TILE, LANES),
                  w_gate_e[l], w_up_e[l], w_down_e[l])
    y = _sc_gather_rows(yb.reshape(n_rows, ROW_TILE, LANES), dest.reshape(2 * T))

    out = _combine(x1, y.reshape(2 * T * ROW_TILE, LANES), wts.T)
    return out.reshape(B, S, D)
```

```python
import functools
import math

import jax
import jax.numpy as jnp
from jax import lax
from jax.experimental import pallas as pl
from jax.experimental.pallas import tpu as pltpu
from jax.experimental.pallas import tpu_sc as plsc

F32 = jnp.float32
BF16 = jnp.bfloat16

D_MODEL = 1024
CONV_CH = 512
CONV_WIDTH = 31
DA_HEADS = 4
DA_HEAD_DIM = 64
DA_V_DIM = 128
ATTN_W = 512
N_GROUPS = 4
EXPERTS_PER_GROUP = 8
N_EXPERTS = 32
D_EXPERT = 512
EPS = 1e-6
LAM_INIT = 0.8 - 0.6 * math.exp(-0.3 * 0)

COL_GLU = 2 * CONV_CH
COL_Q = 512
COL_K = 512
COL_V = 512
COL_GATE = 2 * D_MODEL
OFF_Q = COL_GLU
OFF_K = OFF_Q + COL_Q
OFF_V = OFF_K + COL_K
OFF_GATE = OFF_V + COL_V
IN_COLS = OFF_GATE + COL_GATE

ROUTER_LANES = 128
ROUTER_EXPERT_COL = 8
SUBLANES = 8
ROW_TILE = D_MODEL // (2 * 128)
NORM_BLOCK = 256
HALO = 32

TM_PROJ = 512
TS_CONV = 512
CONV_CHUNK = 64
ATT_TQ = 1024
ATT_TK = 512
ATT_RC = 256
LANES = 128
RB = 512
TM_COMB = 512
TM_ROUTE = 512
SC_CORES = 2
SC_SUBCORES = 16
SC_ROWS = 64

VMEM_LIMIT = 48 * 1024 * 1024


def _sigmoid(x):
    return 1.0 / (1.0 + jnp.exp(-x))


def _const_spec(shape):
    nd = len(shape)
    return pl.BlockSpec(shape, lambda *_: (0,) * nd, pipeline_mode=pl.Buffered(1))


def _store_rows(ref, lead, val):
    rows, half = val.shape[0], val.shape[1] // 2
    hi = pltpu.bitcast(val[:, :half].astype(BF16).astype(F32), jnp.uint32)
    lo = pltpu.bitcast(val[:, half:].astype(BF16).astype(F32), jnp.uint32)
    words = hi | (lo >> 16)
    for j in range(half // LANES):
        ref[(*lead, pl.ds(j, rows, stride=ROW_TILE), slice(None))] = words[:, j * LANES:(j + 1) * LANES]


def _load_rows(ref, lead, rows):
    words = jnp.concatenate(
        [ref[(*lead, pl.ds(j, rows, stride=ROW_TILE), slice(None))] for j in range(ROW_TILE)], axis=1)
    hi = pltpu.bitcast(words & jnp.uint32(0xFFFF0000), F32).astype(BF16)
    lo = pltpu.bitcast(words << 16, F32).astype(BF16)
    return jnp.concatenate([hi, lo], axis=1)


def _group_mean_sq(t, bd_ref):
    sq = t * t
    hi = sq.astype(BF16)
    lo = (sq - hi.astype(F32)).astype(BF16)
    w = bd_ref.shape[0]
    parts = [jnp.dot(hi[:, c:c + w], bd_ref[...], preferred_element_type=F32)
             + jnp.dot(lo[:, c:c + w], bd_ref[...], preferred_element_type=F32)
             for c in range(0, t.shape[1], w)]
    return jnp.concatenate(parts, axis=1)


def _inproj_kernel(x_ref, g_ref, w_ref, qg_ref, kg_ref, bd_ref,
                   glu_ref, q_ref, k_ref, v_ref, gate_ref):
    x = x_ref[...]
    ms = jnp.mean(x * x, axis=-1, keepdims=True)
    h = (x * lax.rsqrt(ms + EPS) * g_ref[...]).astype(BF16)

    def proj(lo, width):
        return jnp.dot(h, w_ref[:, lo:lo + width], preferred_element_type=F32)

    a = proj(0, CONV_CH)
    gt = proj(CONV_CH, CONV_CH)
    glu_ref[...] = (a * _sigmoid(gt)).astype(BF16)

    q = proj(OFF_Q, COL_Q)
    qn = q * lax.rsqrt(_group_mean_sq(q, bd_ref) + EPS) * qg_ref[...]
    q_ref[...] = (qn * (DA_HEAD_DIM ** -0.5)).astype(BF16)

    k = proj(OFF_K, COL_K)
    kn = k * lax.rsqrt(_group_mean_sq(k, bd_ref) + EPS) * kg_ref[...]
    k_ref[...] = kn.astype(BF16)

    v_ref[...] = proj(OFF_V, COL_V).astype(BF16)

    for c in range(COL_GATE // 512):
        gate_ref[:, c * 512:(c + 1) * 512] = _sigmoid(proj(OFF_GATE + c * 512, 512)).astype(BF16)


def _in_projection(x2, g, w_bf, qg, kg, bd):
    T = x2.shape[0]
    tm = TM_PROJ
    row = lambda i: (i, 0)
    out_shape = (
        jax.ShapeDtypeStruct((T, CONV_CH), BF16),
        jax.ShapeDtypeStruct((T, COL_Q), BF16),
        jax.ShapeDtypeStruct((T, COL_K), BF16),
        jax.ShapeDtypeStruct((T, COL_V), BF16),
        jax.ShapeDtypeStruct((T, COL_GATE), BF16),
    )
    return pl.pallas_call(
        _inproj_kernel,
        grid=(T // tm,),
        in_specs=[
            pl.BlockSpec((tm, D_MODEL), row),
            _const_spec((1, D_MODEL)),
            _const_spec((D_MODEL, IN_COLS)),
            _const_spec((1, COL_Q)),
            _const_spec((1, COL_K)),
            _const_spec((NORM_BLOCK, NORM_BLOCK)),
        ],
        out_specs=(
            pl.BlockSpec((tm, CONV_CH), row),
            pl.BlockSpec((tm, COL_Q), row),
            pl.BlockSpec((tm, COL_K), row),
            pl.BlockSpec((tm, COL_V), row),
            pl.BlockSpec((tm, COL_GATE), row),
        ),
        out_shape=out_shape,
        compiler_params=pltpu.CompilerParams(
            dimension_semantics=("parallel",), vmem_limit_bytes=VMEM_LIMIT),
        name="in_projection",
    )(x2, g, w_bf, qg, kg, bd)


def _conv_kernel(cur_ref, halo_ref, w_ref, b_ref, lng_ref, lnb_ref, out_ref, buf_ref):
    i = pl.program_id(1)
    ts = cur_ref.shape[1]
    rows = HALO + ts
    halo = halo_ref[0].astype(F32)
    buf_ref[0, 0:HALO, :] = jnp.where(i > 0, halo, 0.0)
    buf_ref[0, HALO:rows, :] = cur_ref[0].astype(F32)
    for r in range(1, SUBLANES):
        buf_ref[r, 0:rows - SUBLANES, :] = buf_ref[0, r:r + rows - SUBLANES, :]

    first = HALO - (CONV_WIDTH - 1)
    for c in range(ts // CONV_CHUNK):
        r0 = c * CONV_CHUNK
        acc = jnp.broadcast_to(b_ref[...], (CONV_CHUNK, CONV_CH))
        for j in range(CONV_WIDTH):
            shift = (first + j) % SUBLANES
            lo = r0 + first + j - shift
            acc = acc + w_ref[j:j + 1, :] * buf_ref[shift, lo:lo + CONV_CHUNK, :]
        mu = jnp.mean(acc, axis=-1, keepdims=True)
        d = acc - mu
        var = jnp.mean(d * d, axis=-1, keepdims=True)
        y = d * lax.rsqrt(var + EPS) * lng_ref[...] + lnb_ref[...]
        out_ref[0, r0:r0 + CONV_CHUNK, :] = (y * _sigmoid(y)).astype(BF16)


def _conv_branch(glu3, dw_w, dw_b, ln_g, ln_b):
    B, S, C = glu3.shape
    ts = TS_CONV
    per = ts // HALO
    return pl.pallas_call(
        _conv_kernel,
        grid=(B, S // ts),
        in_specs=[
            pl.BlockSpec((1, ts, C), lambda b, i: (b, i, 0)),
            pl.BlockSpec((1, HALO, C), lambda b, i: (b, jnp.maximum(i * per - 1, 0), 0)),
            _const_spec((CONV_WIDTH, C)),
            _const_spec((1, C)),
            _const_spec((1, C)),
            _const_spec((1, C)),
        ],
        out_specs=pl.BlockSpec((1, ts, C), lambda b, i: (b, i, 0)),
        out_shape=jax.ShapeDtypeStruct((B, S, C), BF16),
        scratch_shapes=[pltpu.VMEM((SUBLANES, HALO + ts, C), F32)],
        compiler_params=pltpu.CompilerParams(
            dimension_semantics=("parallel", "parallel"), vmem_limit_bytes=VMEM_LIMIT),
        name="conv_branch",
    )(glu3, glu3, dw_w, dw_b, ln_g, ln_b)


def _attn_kernel(q_ref, k_ref, v_ref, lam_ref, sg_ref, o_ref, qs_ref, m_ref, acc_ref, s_ref, *, tq, tk):
    qi = pl.program_id(2)
    q = q_ref[0]
    lane = lax.broadcasted_iota(jnp.int32, q.shape, 1)
    zero = jnp.zeros_like(q)
    qs_ref[0:tq, :] = jnp.where(lane < DA_HEAD_DIM, q, zero)
    qs_ref[tq:2 * tq, :] = jnp.where(lane >= DA_HEAD_DIM, q, zero)
    m_ref[...] = jnp.full(m_ref.shape, -jnp.inf, F32)
    acc_ref[...] = jnp.zeros(acc_ref.shape, F32)
    ones = jnp.ones((tk, LANES), BF16)
    rc = ATT_RC
    chunks = [slice(r0, r0 + rc) for r0 in range(0, 2 * tq, rc)]
    n_diag = tq // tk

    def scores(j, rows):
        kj = k_ref[0, pl.ds(pl.multiple_of(j * tk, tk), tk), :]
        return lax.dot_general(qs_ref[rows, :], kj, (((1,), (1,)), ((), ())),
                               preferred_element_type=F32)

    def step(j, diag, prefetch):
        vj = jnp.concatenate([v_ref[0, pl.ds(pl.multiple_of(j * tk, tk), tk), :], ones], axis=1)
        for rows in chunks:
            r_lo = rows.start % tq
            if diag is None:
                kw, masked = tk, False
            else:
                k_lo = diag * tk
                if k_lo > r_lo + rc - 1:
                    if prefetch:
                        s_ref[rows, :] = scores(j + 1, rows)
                    continue
                kw = min(tk, -(-(r_lo + rc - k_lo) // (2 * LANES)) * (2 * LANES))
                masked = k_lo + kw - 1 > r_lo
            s = s_ref[rows, 0:kw]
            if prefetch:
                s_ref[rows, :] = scores(j + 1, rows)
            if masked:
                r = lax.broadcasted_iota(jnp.int32, s.shape, 0) + r_lo
                c = lax.broadcasted_iota(jnp.int32, s.shape, 1) + k_lo
                s = jnp.where(c <= r, s, -jnp.inf)
            m_old = m_ref[rows, :]
            m_new = jnp.maximum(m_old, jnp.max(s, axis=-1, keepdims=True))
            alpha = jnp.exp(m_old - m_new)
            p = jnp.exp(s - jnp.tile(m_new, (1, kw // LANES)))
            pv = jnp.dot(p.astype(BF16), vj[0:kw, :], preferred_element_type=F32)
            acc_ref[rows, :] = jnp.tile(alpha, (1, 2)) * acc_ref[rows, :] + pv
            m_ref[rows, :] = m_new

    for rows in chunks:
        s_ref[rows, :] = scores(0, rows)

    def body(j, carry):
        step(j, None, True)
        return carry

    n_full = qi * n_diag
    lax.fori_loop(0, n_full, body, 0)
    for t in range(n_diag):
        step(n_full + t, t, t + 1 < n_diag)

    lam = (jnp.exp(jnp.sum(lam_ref[0:1, :] * lam_ref[1:2, :], axis=-1, keepdims=True))
           - jnp.exp(jnp.sum(lam_ref[2:3, :] * lam_ref[3:4, :], axis=-1, keepdims=True))
           + LAM_INIT)
    o1 = acc_ref[0:tq, 0:LANES] / acc_ref[0:tq, LANES:2 * LANES]
    o2 = acc_ref[tq:2 * tq, 0:LANES] / acc_ref[tq:2 * tq, LANES:2 * LANES]
    o = o1 - lam * o2
    ms = jnp.mean(o * o, axis=-1, keepdims=True)
    on = o * lax.rsqrt(ms + EPS) * sg_ref[...]
    o_ref[0] = (on * (1.0 - LAM_INIT)).astype(BF16)


def _diff_attention(q3, k3, v3, lam_vecs, subln_g):
    B, S, _ = q3.shape
    tq, tk = ATT_TQ, ATT_TK
    return pl.pallas_call(
        functools.partial(_attn_kernel, tq=tq, tk=tk),
        grid=(B, DA_HEADS, S // tq),
        in_specs=[
            pl.BlockSpec((1, tq, DA_V_DIM), lambda b, h, i: (b, i, h)),
            pl.BlockSpec((1, S, DA_V_DIM), lambda b, h, i: (b, 0, h)),
            pl.BlockSpec((1, S, DA_V_DIM), lambda b, h, i: (b, 0, h)),
            _const_spec((4, DA_HEAD_DIM)),
            _const_spec((1, DA_V_DIM)),
        ],
        out_specs=pl.BlockSpec((1, tq, DA_V_DIM), lambda b, h, i: (b, i, h)),
        out_shape=jax.ShapeDtypeStruct((B, S, ATTN_W), BF16),
        scratch_shapes=[
            pltpu.VMEM((2 * tq, DA_V_DIM), BF16),
            pltpu.VMEM((2 * tq, LANES), F32),
            pltpu.VMEM((2 * tq, 2 * LANES), F32),
            pltpu.VMEM((2 * tq, tk), F32),
        ],
        compiler_params=pltpu.CompilerParams(
            dimension_semantics=("parallel", "parallel", "parallel"),
            vmem_limit_bytes=VMEM_LIMIT),
        name="diff_attention",
    )(q3, k3, v3, lam_vecs, subln_g)


def _merge_kernel(conv_ref, attn_ref, gate_ref, x_ref, wc_ref, wa_ref, wo_ref, fg_ref,
                  wr_ref, br_ref, x1_ref, h2_ref, logit_ref):
    c = jnp.dot(conv_ref[...], wc_ref[...], preferred_element_type=F32)
    a = jnp.dot(attn_ref[...], wa_ref[...], preferred_element_type=F32)
    g0 = gate_ref[:, 0:D_MODEL].astype(F32)
    g1 = gate_ref[:, D_MODEL:2 * D_MODEL].astype(F32)
    merged = (g0 * c + g1 * a).astype(BF16)
    x1 = x_ref[...] + jnp.dot(merged, wo_ref[...], preferred_element_type=F32)
    x1_ref[...] = x1
    ms = jnp.mean(x1 * x1, axis=-1, keepdims=True)
    h2 = x1 * lax.rsqrt(ms + EPS) * fg_ref[...]
    _store_rows(h2_ref, (), h2)
    h2_hi = h2.astype(BF16)
    h2_lo = (h2 - h2_hi.astype(F32)).astype(BF16)
    r = (jnp.dot(h2_hi, wr_ref[...], preferred_element_type=F32)
         + jnp.dot(h2_lo, wr_ref[...], preferred_element_type=F32))
    logit_ref[...] = r[:, 0:ROUTER_LANES] + r[:, ROUTER_LANES:2 * ROUTER_LANES] + br_ref[...]


def _merge(conv_act, attn_o, gates, x2, wc, wa, wo, fg, wr, br):
    T = x2.shape[0]
    tm = TM_PROJ
    row = lambda i: (i, 0)
    return pl.pallas_call(
        _merge_kernel,
        grid=(T // tm,),
        in_specs=[
            pl.BlockSpec((tm, CONV_CH), row),
            pl.BlockSpec((tm, ATTN_W), row),
            pl.BlockSpec((tm, COL_GATE), row),
            pl.BlockSpec((tm, D_MODEL), row),
            _const_spec((CONV_CH, D_MODEL)),
            _const_spec((ATTN_W, D_MODEL)),
            _const_spec((D_MODEL, D_MODEL)),
            _const_spec((1, D_MODEL)),
            _const_spec((D_MODEL, 2 * ROUTER_LANES)),
            _const_spec((1, ROUTER_LANES)),
        ],
        out_specs=(
            pl.BlockSpec((tm, D_MODEL), row),
            pl.BlockSpec((tm * ROW_TILE, LANES), row),
            pl.BlockSpec((tm, ROUTER_LANES), row),
        ),
        out_shape=(
            jax.ShapeDtypeStruct((T, D_MODEL), F32),
            jax.ShapeDtypeStruct((T * ROW_TILE, LANES), jnp.uint32),
            jax.ShapeDtypeStruct((T, ROUTER_LANES), F32),
        ),
        compiler_params=pltpu.CompilerParams(
            dimension_semantics=("parallel",), vmem_limit_bytes=VMEM_LIMIT),
        name="merge_out_router",
    )(conv_act, attn_o, gates, x2, wc, wa, wo, fg, wr, br)


def _route_kernel(logit_ref, tri_ref, low_ref, dest_ref, wts_ref, blk_ref,
                  cnt_ref, base_ref, *, n_blk_lanes):
    ph = pl.program_id(0)
    i = pl.program_id(1)
    tm = logit_ref.shape[0]
    lt = logit_ref[...].T
    row8 = lax.broadcasted_iota(jnp.int32, (SUBLANES, tm), 0)

    g = jnp.where(row8 < N_GROUPS, lt[0:SUBLANES, :], -jnp.inf)
    g_max = jnp.max(g, axis=0, keepdims=True)
    g_sum = jnp.sum(jnp.exp(g - g_max), axis=0, keepdims=True)
    g_sel = jnp.min(jnp.where(g == g_max, row8, SUBLANES), axis=0, keepdims=True)
    g_w = 1.0 / g_sum

    e_sel = jnp.zeros((EXPERTS_PER_GROUP, tm), F32)
    for gi in range(N_GROUPS):
        lo = ROUTER_EXPERT_COL + gi * EXPERTS_PER_GROUP
        e_sel = jnp.where(g_sel == gi, lt[lo:lo + EXPERTS_PER_GROUP, :], e_sel)
    e_max = jnp.max(e_sel, axis=0, keepdims=True)
    e_exp = jnp.exp(e_sel - e_max)
    prob = e_exp / jnp.sum(e_exp, axis=0, keepdims=True)
    p1 = jnp.max(prob, axis=0, keepdims=True)
    i1 = jnp.min(jnp.where(prob == p1, row8, SUBLANES), axis=0, keepdims=True)
    rest = jnp.where(row8 == i1, -1.0, prob)
    p2 = jnp.max(rest, axis=0, keepdims=True)
    i2 = jnp.min(jnp.where(rest == p2, row8, SUBLANES), axis=0, keepdims=True)
    denom = p1 + p2
    e0 = g_sel * EXPERTS_PER_GROUP + i1
    e1 = g_sel * EXPERTS_PER_GROUP + i2

    row32 = lax.broadcasted_iota(jnp.int32, (N_EXPERTS, tm), 0)
    oh0 = row32 == e0
    oh1 = row32 == e1
    ohs = jnp.where(oh0 | oh1, 1.0, 0.0)
    tile_cnt = jnp.sum(ohs, axis=1, keepdims=True)

    @pl.when((ph == 0) & (i == 0))
    def _():
        cnt_ref[...] = jnp.zeros(cnt_ref.shape, F32)

    @pl.when(ph == 0)
    def _():
        cnt_ref[...] += tile_cnt

    @pl.when((ph == 1) & (i == 0))
    def _():
        cnt = cnt_ref[...]
        padded = jnp.ceil(cnt * (1.0 / RB)) * RB
        pstart = jnp.dot(low_ref[...], padded, preferred_element_type=F32,
                         precision=lax.Precision.HIGHEST)
        base_ref[...] = pstart
        pad_end = pstart + padded
        row0 = lax.broadcasted_iota(jnp.int32, (N_EXPERTS, n_blk_lanes), 1).astype(F32) * RB
        expert = lax.broadcasted_iota(jnp.int32, (N_EXPERTS, n_blk_lanes), 0).astype(F32)
        blk = jnp.minimum(jnp.sum(jnp.where(pad_end[:, 0:1] <= row0, 1.0, 0.0), axis=0, keepdims=True),
                          N_EXPERTS - 1.0)
        valid_end = jnp.sum(jnp.where(expert == blk, (pstart + cnt)[:, 0:1], 0.0), axis=0, keepdims=True)
        n_valid = jnp.clip(valid_end - row0[0:1, :], 0.0, float(RB))
        n_used = jnp.max(pad_end[:, 0:1], axis=0, keepdims=True) * (1.0 / RB) + jnp.zeros_like(blk)
        blk_ref[...] = jnp.zeros(blk_ref.shape, jnp.int32)
        blk_ref[0:1, :] = blk.astype(jnp.int32)
        blk_ref[1:2, :] = n_valid.astype(jnp.int32)
        blk_ref[2:3, :] = n_used.astype(jnp.int32)

    @pl.when(ph == 1)
    def _():
        before = jnp.dot(ohs.astype(BF16), tri_ref[...], preferred_element_type=F32)
        pos = base_ref[:, 0:1] + before
        d0 = jnp.sum(jnp.where(oh0, pos, 0.0), axis=0, keepdims=True)
        d1 = jnp.sum(jnp.where(oh1, pos, 0.0), axis=0, keepdims=True)
        dest_ref[0:1, :] = d0.astype(jnp.int32)
        dest_ref[1:2, :] = d1.astype(jnp.int32)
        wts_ref[...] = jnp.zeros(wts_ref.shape, F32)
        wts_ref[0:1, :] = p1 / denom * g_w
        wts_ref[1:2, :] = p2 / denom * g_w
        base_ref[...] += tile_cnt


def _route(logits):
    T = logits.shape[0]
    tm = TM_ROUTE
    n_rows = 2 * T + N_EXPERTS * RB
    n_blk = n_rows // RB
    n_blk_lanes = -(-n_blk // LANES) * LANES
    r = jnp.arange(tm, dtype=jnp.int32)
    tri = (r[:, None] < r[None, :]).astype(BF16)
    e = jnp.arange(N_EXPERTS, dtype=jnp.int32)
    low = (e[None, :] < e[:, None]).astype(F32)
    return pl.pallas_call(
        functools.partial(_route_kernel, n_blk_lanes=n_blk_lanes),
        grid=(2, T // tm),
        in_specs=[
            pl.BlockSpec((tm, ROUTER_LANES), lambda ph, i: (i, 0)),
            _const_spec((tm, tm)),
            _const_spec((N_EXPERTS, N_EXPERTS)),
        ],
        out_specs=(
            pl.BlockSpec((2, tm), lambda ph, i: (0, i * ph)),
            pl.BlockSpec((SUBLANES, tm), lambda ph, i: (0, i * ph)),
            pl.BlockSpec((SUBLANES, n_blk_lanes), lambda ph, i: (0, 0)),
        ),
        out_shape=(
            jax.ShapeDtypeStruct((2, T), jnp.int32),
            jax.ShapeDtypeStruct((SUBLANES, T), F32),
            jax.ShapeDtypeStruct((SUBLANES, n_blk_lanes), jnp.int32),
        ),
        scratch_shapes=[pltpu.VMEM((N_EXPERTS, LANES), F32), pltpu.VMEM((N_EXPERTS, LANES), F32)],
        compiler_params=pltpu.CompilerParams(
            dimension_semantics=("arbitrary", "arbitrary"), vmem_limit_bytes=VMEM_LIMIT),
        name="route_plan",
    )(logits, tri, low)


def _sc_worker():
    return lax.axis_index("s") * SC_CORES + lax.axis_index("c")


def _sc_scatter_rows(h, dest, n_rows):
    T = h.shape[0]
    n_workers = SC_CORES * SC_SUBCORES
    per_w = T // n_workers
    n_ch = per_w // SC_ROWS
    assert per_w * n_workers == T and n_ch * SC_ROWS == per_w and n_ch % 2 == 0
    mesh = plsc.VectorSubcoreMesh(core_axis_name="c", subcore_axis_name="s")

    @functools.partial(
        pl.kernel, mesh=mesh,
        out_type=jax.ShapeDtypeStruct((n_rows,) + h.shape[1:], h.dtype),
        scratch_types=[
            pltpu.VMEM((2, n_ch, SC_ROWS), jnp.int32),
            pltpu.VMEM((2, SC_ROWS) + h.shape[1:], h.dtype),
            pltpu.SemaphoreType.DMA((2,)),
            pltpu.SemaphoreType.DMA((2,)),
        ],
    )
    def k(h_hbm, idx_hbm, out_hbm, idx_v, rows_v, lsem, ssem):
        wid = _sc_worker()
        base = wid * per_w
        pltpu.sync_copy(idx_hbm.at[0, wid], idx_v.at[0])
        pltpu.sync_copy(idx_hbm.at[1, wid], idx_v.at[1])

        def load(c, slot):
            return pltpu.make_async_copy(h_hbm.at[pl.ds(base + c * SC_ROWS, SC_ROWS)], rows_v.at[slot],
                                         lsem.at[slot])

        def scatter(slot_k, c, slot):
            return pltpu.make_async_copy(rows_v.at[slot], out_hbm.at[idx_v.at[slot_k, c]], ssem.at[slot])

        load(0, 0).start()

        def body(g, carry):
            for slot in range(2):
                c = 2 * g + slot
                load(c, slot).wait()

                @pl.when(c >= 1)
                def _():
                    scatter(0, c - 1, 1 - slot).wait()
                    scatter(1, c - 1, 1 - slot).wait()

                @pl.when(c + 1 < n_ch)
                def _():
                    load(c + 1, 1 - slot).start()

                scatter(0, c, slot).start()
                scatter(1, c, slot).start()
            return carry

        lax.fori_loop(0, n_ch // 2, body, 0)
        scatter(0, n_ch - 1, 1).wait()
        scatter(1, n_ch - 1, 1).wait()

    return k(h, dest.reshape(2, n_workers, n_ch, SC_ROWS))


def _sc_gather_rows(table, idx):
    B = idx.shape[0]
    n_workers = SC_CORES * SC_SUBCORES
    per_w = B // n_workers
    n_ch = per_w // SC_ROWS
    assert per_w * n_workers == B and n_ch * SC_ROWS == per_w and n_ch % 2 == 0
    mesh = plsc.VectorSubcoreMesh(core_axis_name="c", subcore_axis_name="s")

    @functools.partial(
        pl.kernel, mesh=mesh,
        out_type=jax.ShapeDtypeStruct((B,) + table.shape[1:], table.dtype),
        scratch_types=[
            pltpu.VMEM((n_ch, SC_ROWS), jnp.int32),
            pltpu.VMEM((2, SC_ROWS) + table.shape[1:], table.dtype),
            pltpu.SemaphoreType.DMA((2,)),
            pltpu.SemaphoreType.DMA((2,)),
        ],
    )
    def k(table_hbm, idx_hbm, out_hbm, idx_v, rows_v, gsem, wsem):
        wid = _sc_worker()
        base = wid * per_w
        pltpu.sync_copy(idx_hbm.at[wid], idx_v)

        def gather(c, slot):
            return pltpu.make_async_copy(table_hbm.at[idx_v.at[c]], rows_v.at[slot], gsem.at[slot])

        def writeback(c, slot):
            return pltpu.make_async_copy(rows_v.at[slot], out_hbm.at[pl.ds(base + c * SC_ROWS, SC_ROWS)],
                                         wsem.at[slot])

        gather(0, 0).start()

        def body(g, carry):
            for slot in range(2):
                c = 2 * g + slot
                gather(c, slot).wait()

                @pl.when(c >= 1)
                def _():
                    writeback(c - 1, 1 - slot).wait()

                @pl.when(c + 1 < n_ch)
                def _():
                    gather(c + 1, 1 - slot).start()

                writeback(c, slot).start()
            return carry

        lax.fori_loop(0, n_ch // 2, body, 0)
        writeback(n_ch - 1, 1).wait()

    return k(table, idx.reshape(n_workers, n_ch, SC_ROWS))


def _expert_kernel(blk_exp_ref, n_used_ref, n_valid_ref, xs_ref, wg_ref, wu_ref, wd_ref, y_ref,
                   wg_bf, wu_bf, wd_bf):
    b = pl.program_id(0)
    used = b < n_used_ref[0]
    new_expert = (b == 0) | (blk_exp_ref[b] != blk_exp_ref[jnp.maximum(b - 1, 0)])

    @pl.when(used & new_expert)
    def _():
        wg_bf[...] = wg_ref[0].astype(BF16)
        wu_bf[...] = wu_ref[0].astype(BF16)
        wd_bf[...] = wd_ref[0].astype(BF16)

    @pl.when(used)
    def _():
        x = _load_rows(xs_ref, (), RB)
        row = lax.broadcasted_iota(jnp.int32, (RB, 1), 0)
        xb = jnp.where(row < n_valid_ref[b], x, jnp.zeros_like(x))
        g = jnp.dot(xb, wg_bf[...], preferred_element_type=F32)
        u = jnp.dot(xb, wu_bf[...], preferred_element_type=F32)
        hid = (g * _sigmoid(g) * u).astype(BF16)
        _store_rows(y_ref, (), jnp.dot(hid, wd_bf[...], preferred_element_type=F32))

    @pl.when(jnp.logical_not(used))
    def _():
        y_ref[...] = jnp.zeros(y_ref.shape, y_ref.dtype)


def _experts(blk_exp, n_used, n_valid, xs, wg, wu, wd):
    n_blk = blk_exp.shape[0]
    grid_spec = pltpu.PrefetchScalarGridSpec(
        num_scalar_prefetch=3,
        grid=(n_blk,),
        in_specs=[
            pl.BlockSpec((RB * ROW_TILE, LANES), lambda b, be, nu, nv: (b, 0)),
            pl.BlockSpec((1, D_MODEL, D_EXPERT), lambda b, be, nu, nv: (be[b], 0, 0)),
            pl.BlockSpec((1, D_MODEL, D_EXPERT), lambda b, be, nu, nv: (be[b], 0, 0)),
            pl.BlockSpec((1, D_EXPERT, D_MODEL), lambda b, be, nu, nv: (be[b], 0, 0)),
        ],
        out_specs=pl.BlockSpec((RB * ROW_TILE, LANES), lambda b, be, nu, nv: (b, 0)),
        scratch_shapes=[
            pltpu.VMEM((D_MODEL, D_EXPERT), BF16),
            pltpu.VMEM((D_MODEL, D_EXPERT), BF16),
            pltpu.VMEM((D_EXPERT, D_MODEL), BF16),
        ],
    )
    return pl.pallas_call(
        _expert_kernel,
        grid_spec=grid_spec,
        out_shape=jax.ShapeDtypeStruct(xs.shape, xs.dtype),
        compiler_params=pltpu.CompilerParams(
            dimension_semantics=("arbitrary",), vmem_limit_bytes=VMEM_LIMIT),
        name="expert_mlp",
    )(blk_exp, n_used, n_valid, xs, wg, wu, wd)


def _combine_kernel(x1_ref, y0_ref, y1_ref, w_ref, o_ref):
    tm = x1_ref.shape[0]
    y0 = _load_rows(y0_ref, (), tm).astype(F32)
    y1 = _load_rows(y1_ref, (), tm).astype(F32)
    w = w_ref[...].T
    o_ref[...] = x1_ref[...] + (y0 * w[:, 0:1] + y1 * w[:, 1:2])


def _combine(x1, y, wts):
    T = x1.shape[0]
    tm = TM_COMB
    row = lambda i: (i, 0)
    return pl.pallas_call(
        _combine_kernel,
        grid=(T // tm,),
        in_specs=[
            pl.BlockSpec((tm, D_MODEL), row),
            pl.BlockSpec((tm * ROW_TILE, LANES), row),
            pl.BlockSpec((tm * ROW_TILE, LANES), lambda i: (i + T // tm, 0)),
            pl.BlockSpec((SUBLANES, tm), lambda i: (0, i)),
        ],
        out_specs=pl.BlockSpec((tm, D_MODEL), row),
        out_shape=jax.ShapeDtypeStruct((T, D_MODEL), F32),
        compiler_params=pltpu.CompilerParams(
            dimension_semantics=("parallel",), vmem_limit_bytes=VMEM_LIMIT),
        name="moe_combine",
    )(x1, y, y, wts)


def kernel(x, attn_norm_g, w_in, conv_dw_w, conv_dw_b, conv_ln_g, conv_ln_b, w_conv_out,
           q_norm_g, k_norm_g, lambda_q1, lambda_k1, lambda_q2, lambda_k2, subln_g,
           w_attn_out, w_out, ffn_norm_g, w_router_group, b_router_group,
           w_router_expert, b_router_expert, w_gate_e, w_up_e, w_down_e):
    B, S, D = x.shape
    T = B * S
    l = 0
    x2 = x.reshape(T, D)

    reps = COL_Q // DA_HEAD_DIM
    qg = jnp.tile(q_norm_g[l], reps).reshape(1, COL_Q)
    kg = jnp.tile(k_norm_g[l], reps).reshape(1, COL_K)
    grp = jnp.arange(NORM_BLOCK, dtype=jnp.int32) // DA_HEAD_DIM
    bd = jnp.where(grp[:, None] == grp[None, :], 1.0 / DA_HEAD_DIM, 0.0).astype(BF16)

    glu, q, k, v, gates = _in_projection(
        x2, attn_norm_g[l].reshape(1, D), w_in[l].astype(BF16), qg, kg, bd)

    conv_act = _conv_branch(glu.reshape(B, S, CONV_CH), conv_dw_w[l],
                            conv_dw_b[l].reshape(1, CONV_CH), conv_ln_g[l].reshape(1, CONV_CH),
                            conv_ln_b[l].reshape(1, CONV_CH))

    lam_vecs = jnp.stack([lambda_q1[l], lambda_k1[l], lambda_q2[l], lambda_k2[l]]).astype(F32)
    attn_o = _diff_attention(q.reshape(B, S, COL_Q), k.reshape(B, S, COL_K),
                             v.reshape(B, S, COL_V), lam_vecs, subln_g[l].reshape(1, DA_V_DIM))

    e_lo, e_hi = ROUTER_EXPERT_COL, ROUTER_EXPERT_COL + N_EXPERTS
    wr = jnp.zeros((D, ROUTER_LANES), F32)
    wr = wr.at[:, :N_GROUPS].set(w_router_group[l]).at[:, e_lo:e_hi].set(w_router_expert[l])
    br = jnp.zeros((1, ROUTER_LANES), F32)
    br = br.at[0, :N_GROUPS].set(b_router_group[l]).at[0, e_lo:e_hi].set(b_router_expert[l])

    wr_hi = wr.astype(BF16)
    wr_lo = (wr - wr_hi.astype(F32)).astype(BF16)
    x1, h2, logits = _merge(conv_act.reshape(T, CONV_CH), attn_o.reshape(T, ATTN_W), gates, x2,
                            w_conv_out[l].astype(BF16), w_attn_out[l].astype(BF16),
                            w_out[l].astype(BF16), ffn_norm_g[l].reshape(1, D),
                            jnp.concatenate([wr_hi, wr_lo], axis=1), br)

    dest, wts, blk = _route(logits)
    n_rows = 2 * T + N_EXPERTS * RB
    n_blk = n_rows // RB
    blk_exp, n_valid, n_used = blk[0, :n_blk], blk[1, :n_blk], blk[2, 0:1]

    xs = _sc_scatter_rows(h2.reshape(T, ROW_TILE, LANES), dest, n_rows)
    yb = _experts(blk_exp, n_used, n_valid, xs.reshape(n_rows * ROW_TILE, LANES),
                  w_gate_e[l], w_up_e[l], w_down_e[l])
    y = _sc_gather_rows(yb.reshape(n_rows, ROW_TILE, LANES), dest.reshape(2 * T))

    out = _combine(x1, y.reshape(2 * T * ROW_TILE, LANES), wts)
    return out.reshape(B, S, D)
```

```python
import functools
import math

import jax
import jax.numpy as jnp
from jax import lax
from jax.experimental import pallas as pl
from jax.experimental.pallas import tpu as pltpu
from jax.experimental.pallas import tpu_sc as plsc

F32 = jnp.float32
BF16 = jnp.bfloat16

D_MODEL = 1024
CONV_CH = 512
CONV_WIDTH = 31
DA_HEADS = 4
DA_HEAD_DIM = 64
DA_V_DIM = 128
ATTN_W = 512
N_GROUPS = 4
EXPERTS_PER_GROUP = 8
N_EXPERTS = 32
D_EXPERT = 512
EPS = 1e-6
LAM_INIT = 0.8 - 0.6 * math.exp(-0.3 * 0)

COL_GLU = 2 * CONV_CH
COL_Q = 512
COL_K = 512
COL_V = 512
COL_GATE = 2 * D_MODEL
OFF_Q = COL_GLU
OFF_K = OFF_Q + COL_Q
OFF_V = OFF_K + COL_K
OFF_GATE = OFF_V + COL_V
IN_COLS = OFF_GATE + COL_GATE

ROUTER_LANES = 128
ROUTER_EXPERT_COL = 8
SUBLANES = 8
ROW_TILE = D_MODEL // (2 * 128)
NORM_BLOCK = 256
HALO = 32

TM_PROJ = 512
TM_MERGE = 1024
CONV_CHUNK = 64
ATT_TQ = 1024
ATT_TK = 512
ATT_RC = 256
ATT_HEADS = 2
LANES = 128
RB = 512
TM_COMB = 512
TM_ROUTE = 512
SC_CORES = 2
SC_SUBCORES = 16
SC_ROWS = 64

VMEM_LIMIT = 48 * 1024 * 1024


def _sigmoid(x):
    return 1.0 / (1.0 + jnp.exp(-x))


def _const_spec(shape):
    nd = len(shape)
    return pl.BlockSpec(shape, lambda *_: (0,) * nd, pipeline_mode=pl.Buffered(1))


def _store_rows(ref, lead, val):
    rows, half = val.shape[0], val.shape[1] // 2
    hi = pltpu.bitcast(val[:, :half].astype(BF16).astype(F32), jnp.uint32)
    lo = pltpu.bitcast(val[:, half:].astype(BF16).astype(F32), jnp.uint32)
    words = hi | (lo >> 16)
    for j in range(half // LANES):
        ref[(*lead, pl.ds(j, rows, stride=ROW_TILE), slice(None))] = words[:, j * LANES:(j + 1) * LANES]


def _load_rows(ref, lead, rows):
    words = jnp.concatenate(
        [ref[(*lead, pl.ds(j, rows, stride=ROW_TILE), slice(None))] for j in range(ROW_TILE)], axis=1)
    hi = pltpu.bitcast(words & jnp.uint32(0xFFFF0000), F32).astype(BF16)
    lo = pltpu.bitcast(words << 16, F32).astype(BF16)
    return jnp.concatenate([hi, lo], axis=1)


def _group_mean_sq(t, bd_ref):
    sq = t * t
    hi = sq.astype(BF16)
    lo = (sq - hi.astype(F32)).astype(BF16)
    w = bd_ref.shape[0]
    parts = [jnp.dot(hi[:, c:c + w], bd_ref[...], preferred_element_type=F32)
             + jnp.dot(lo[:, c:c + w], bd_ref[...], preferred_element_type=F32)
             for c in range(0, t.shape[1], w)]
    return jnp.concatenate(parts, axis=1)


def _inproj_kernel(x_ref, g_ref, w_ref, qg_ref, kg_ref, bd_ref, cw_ref, cb_ref, lng_ref, lnb_ref,
                   conv_ref, q_ref, k_ref, v_ref, gate_ref, buf_ref, halo_ref, *, tiles_per_seq):
    i = pl.program_id(0)
    tm = x_ref.shape[0]
    x = x_ref[...]
    ms = jnp.mean(x * x, axis=-1, keepdims=True)
    h = (x * lax.rsqrt(ms + EPS) * g_ref[...]).astype(BF16)

    def proj(lo, width):
        return jnp.dot(h, w_ref[:, lo:lo + width], preferred_element_type=F32)

    a = proj(0, CONV_CH)
    gt = proj(CONV_CH, CONV_CH)
    glu = a * _sigmoid(gt)

    rows = HALO + tm
    buf_ref[0, 0:HALO, :] = jnp.where(i % tiles_per_seq == 0, 0.0, halo_ref[...])
    buf_ref[0, HALO:rows, :] = glu
    halo_ref[...] = glu[tm - HALO:tm, :]
    for r in range(1, SUBLANES):
        buf_ref[r, 0:rows - SUBLANES, :] = buf_ref[0, r:r + rows - SUBLANES, :]
    first = HALO - (CONV_WIDTH - 1)
    for c in range(tm // CONV_CHUNK):
        r0 = c * CONV_CHUNK
        acc = jnp.broadcast_to(cb_ref[...], (CONV_CHUNK, CONV_CH))
        for j in range(CONV_WIDTH):
            shift = (first + j) % SUBLANES
            lo = r0 + first + j - shift
            acc = acc + cw_ref[j:j + 1, :] * buf_ref[shift, lo:lo + CONV_CHUNK, :]
        mu = jnp.mean(acc, axis=-1, keepdims=True)
        d = acc - mu
        var = jnp.mean(d * d, axis=-1, keepdims=True)
        y = d * lax.rsqrt(var + EPS) * lng_ref[...] + lnb_ref[...]
        conv_ref[r0:r0 + CONV_CHUNK, :] = (y * _sigmoid(y)).astype(BF16)

    q = proj(OFF_Q, COL_Q)
    qn = q * lax.rsqrt(_group_mean_sq(q, bd_ref) + EPS) * qg_ref[...]
    q_ref[...] = (qn * (DA_HEAD_DIM ** -0.5)).astype(BF16)

    k = proj(OFF_K, COL_K)
    kn = k * lax.rsqrt(_group_mean_sq(k, bd_ref) + EPS) * kg_ref[...]
    k_ref[...] = kn.astype(BF16)

    v_ref[...] = proj(OFF_V, COL_V).astype(BF16)

    for c in range(COL_GATE // 512):
        gate_ref[:, c * 512:(c + 1) * 512] = _sigmoid(proj(OFF_GATE + c * 512, 512)).astype(BF16)


def _in_projection(x2, g, w_bf, qg, kg, bd, dw_w, dw_b, ln_g, ln_b, seq_len):
    T = x2.shape[0]
    tm = TM_PROJ
    row = lambda i: (i, 0)
    out_shape = (
        jax.ShapeDtypeStruct((T, CONV_CH), BF16),
        jax.ShapeDtypeStruct((T, COL_Q), BF16),
        jax.ShapeDtypeStruct((T, COL_K), BF16),
        jax.ShapeDtypeStruct((T, COL_V), BF16),
        jax.ShapeDtypeStruct((T, COL_GATE), BF16),
    )
    return pl.pallas_call(
        functools.partial(_inproj_kernel, tiles_per_seq=seq_len // tm),
        grid=(T // tm,),
        in_specs=[
            pl.BlockSpec((tm, D_MODEL), row),
            _const_spec((1, D_MODEL)),
            _const_spec((D_MODEL, IN_COLS)),
            _const_spec((1, COL_Q)),
            _const_spec((1, COL_K)),
            _const_spec((NORM_BLOCK, NORM_BLOCK)),
            _const_spec((CONV_WIDTH, CONV_CH)),
            _const_spec((1, CONV_CH)),
            _const_spec((1, CONV_CH)),
            _const_spec((1, CONV_CH)),
        ],
        out_specs=(
            pl.BlockSpec((tm, CONV_CH), row),
            pl.BlockSpec((tm, COL_Q), row),
            pl.BlockSpec((tm, COL_K), row),
            pl.BlockSpec((tm, COL_V), row),
            pl.BlockSpec((tm, COL_GATE), row),
        ),
        out_shape=out_shape,
        scratch_shapes=[pltpu.VMEM((SUBLANES, HALO + tm, CONV_CH), F32), pltpu.VMEM((HALO, CONV_CH), F32)],
        compiler_params=pltpu.CompilerParams(
            dimension_semantics=("arbitrary",), vmem_limit_bytes=VMEM_LIMIT),
        name="in_projection",
    )(x2, g, w_bf, qg, kg, bd, dw_w, dw_b, ln_g, ln_b)


def _attn_kernel(q_ref, k_ref, v_ref, lam_ref, sg_ref, o_ref, qs_ref, m_ref, acc_ref, s_ref, *, tq, tk):
    qi = pl.program_id(2)
    lane = lax.broadcasted_iota(jnp.int32, (tq, DA_V_DIM), 1)
    for hh in range(ATT_HEADS):
        q = q_ref[0, :, hh * DA_V_DIM:(hh + 1) * DA_V_DIM]
        zero = jnp.zeros_like(q)
        qs_ref[hh, 0:tq, :] = jnp.where(lane < DA_HEAD_DIM, q, zero)
        qs_ref[hh, tq:2 * tq, :] = jnp.where(lane >= DA_HEAD_DIM, q, zero)
    m_ref[...] = jnp.full(m_ref.shape, -jnp.inf, F32)
    acc_ref[...] = jnp.zeros(acc_ref.shape, F32)
    ones = jnp.ones((tk, LANES), BF16)
    rc = ATT_RC
    chunks = [slice(r0, r0 + rc) for r0 in range(0, 2 * tq, rc)]
    n_diag = tq // tk

    def scores(hh, j, rows):
        kj = k_ref[0, pl.ds(pl.multiple_of(j * tk, tk), tk), hh * DA_V_DIM:(hh + 1) * DA_V_DIM]
        return lax.dot_general(qs_ref[hh, rows, :], kj, (((1,), (1,)), ((), ())),
                               preferred_element_type=F32)

    def step(j, diag, prefetch):
        for hh in range(ATT_HEADS):
            vj = jnp.concatenate(
                [v_ref[0, pl.ds(pl.multiple_of(j * tk, tk), tk), hh * DA_V_DIM:(hh + 1) * DA_V_DIM], ones], axis=1)
            for rows in chunks:
                r_lo = rows.start % tq
                if diag is None:
                    kw, masked = tk, False
                else:
                    k_lo = diag * tk
                    if k_lo > r_lo + rc - 1:
                        if prefetch:
                            s_ref[hh, rows, :] = scores(hh, j + 1, rows)
                        continue
                    kw = min(tk, -(-(r_lo + rc - k_lo) // (2 * LANES)) * (2 * LANES))
                    masked = k_lo + kw - 1 > r_lo
                s = s_ref[hh, rows, 0:kw]
                if prefetch:
                    s_ref[hh, rows, :] = scores(hh, j + 1, rows)
                if masked:
                    r = lax.broadcasted_iota(jnp.int32, s.shape, 0) + r_lo
                    c = lax.broadcasted_iota(jnp.int32, s.shape, 1) + k_lo
                    s = jnp.where(c <= r, s, -jnp.inf)
                m_old = m_ref[hh, rows, :]
                m_new = jnp.maximum(m_old, jnp.max(s, axis=-1, keepdims=True))
                alpha = jnp.exp(m_old - m_new)
                p = jnp.exp(s - jnp.tile(m_new, (1, kw // LANES)))
                pv = jnp.dot(p.astype(BF16), vj[0:kw, :], preferred_element_type=F32)
                acc_ref[hh, rows, :] = jnp.tile(alpha, (1, 2)) * acc_ref[hh, rows, :] + pv
                m_ref[hh, rows, :] = m_new

    for hh in range(ATT_HEADS):
        for rows in chunks:
            s_ref[hh, rows, :] = scores(hh, 0, rows)

    def body(j, carry):
        step(j, None, True)
        return carry

    n_full = qi * n_diag
    lax.fori_loop(0, n_full, body, 0)
    for t in range(n_diag):
        step(n_full + t, t, t + 1 < n_diag)

    lam = (jnp.exp(jnp.sum(lam_ref[0:1, :] * lam_ref[1:2, :], axis=-1, keepdims=True))
           - jnp.exp(jnp.sum(lam_ref[2:3, :] * lam_ref[3:4, :], axis=-1, keepdims=True))
           + LAM_INIT)
    for hh in range(ATT_HEADS):
        o1 = acc_ref[hh, 0:tq, 0:LANES] / acc_ref[hh, 0:tq, LANES:2 * LANES]
        o2 = acc_ref[hh, tq:2 * tq, 0:LANES] / acc_ref[hh, tq:2 * tq, LANES:2 * LANES]
        o = o1 - lam * o2
        ms = jnp.mean(o * o, axis=-1, keepdims=True)
        on = o * lax.rsqrt(ms + EPS) * sg_ref[...]
        o_ref[0, :, hh * DA_V_DIM:(hh + 1) * DA_V_DIM] = (on * (1.0 - LAM_INIT)).astype(BF16)


def _diff_attention(q3, k3, v3, lam_vecs, subln_g):
    B, S, _ = q3.shape
    tq, tk = ATT_TQ, ATT_TK
    w = ATT_HEADS * DA_V_DIM
    return pl.pallas_call(
        functools.partial(_attn_kernel, tq=tq, tk=tk),
        grid=(B, DA_HEADS // ATT_HEADS, S // tq),
        in_specs=[
            pl.BlockSpec((1, tq, w), lambda b, h, i: (b, i, h)),
            pl.BlockSpec((1, S, w), lambda b, h, i: (b, 0, h)),
            pl.BlockSpec((1, S, w), lambda b, h, i: (b, 0, h)),
            _const_spec((4, DA_HEAD_DIM)),
            _const_spec((1, DA_V_DIM)),
        ],
        out_specs=pl.BlockSpec((1, tq, w), lambda b, h, i: (b, i, h)),
        out_shape=jax.ShapeDtypeStruct((B, S, ATTN_W), BF16),
        scratch_shapes=[
            pltpu.VMEM((ATT_HEADS, 2 * tq, DA_V_DIM), BF16),
            pltpu.VMEM((ATT_HEADS, 2 * tq, LANES), F32),
            pltpu.VMEM((ATT_HEADS, 2 * tq, 2 * LANES), F32),
            pltpu.VMEM((ATT_HEADS, 2 * tq, tk), F32),
        ],
        compiler_params=pltpu.CompilerParams(
            dimension_semantics=("parallel", "parallel", "parallel"),
            vmem_limit_bytes=VMEM_LIMIT),
        name="diff_attention",
    )(q3, k3, v3, lam_vecs, subln_g)


def _merge_kernel(conv_ref, attn_ref, gate_ref, x_ref, wc_ref, wa_ref, wo_ref, fg_ref,
                  wr_ref, br_ref, x1_ref, h2_ref, logit_ref):
    c = jnp.dot(conv_ref[...], wc_ref[...], preferred_element_type=F32)
    a = jnp.dot(attn_ref[...], wa_ref[...], preferred_element_type=F32)
    g0 = gate_ref[:, 0:D_MODEL].astype(F32)
    g1 = gate_ref[:, D_MODEL:2 * D_MODEL].astype(F32)
    merged = (g0 * c + g1 * a).astype(BF16)
    x1 = x_ref[...] + jnp.dot(merged, wo_ref[...], preferred_element_type=F32)
    x1_ref[...] = x1
    ms = jnp.mean(x1 * x1, axis=-1, keepdims=True)
    h2 = x1 * lax.rsqrt(ms + EPS) * fg_ref[...]
    _store_rows(h2_ref, (), h2)
    h2_hi = h2.astype(BF16)
    h2_lo = (h2 - h2_hi.astype(F32)).astype(BF16)
    r = (jnp.dot(h2_hi, wr_ref[...], preferred_element_type=F32)
         + jnp.dot(h2_lo, wr_ref[...], preferred_element_type=F32))
    logit_ref[...] = r[:, 0:ROUTER_LANES] + r[:, ROUTER_LANES:2 * ROUTER_LANES] + br_ref[...]


def _merge(conv_act, attn_o, gates, x2, wc, wa, wo, fg, wr, br):
    T = x2.shape[0]
    tm = TM_MERGE
    row = lambda i: (i, 0)
    return pl.pallas_call(
        _merge_kernel,
        grid=(T // tm,),
        in_specs=[
            pl.BlockSpec((tm, CONV_CH), row),
            pl.BlockSpec((tm, ATTN_W), row),
            pl.BlockSpec((tm, COL_GATE), row),
            pl.BlockSpec((tm, D_MODEL), row),
            _const_spec((CONV_CH, D_MODEL)),
            _const_spec((ATTN_W, D_MODEL)),
            _const_spec((D_MODEL, D_MODEL)),
            _const_spec((1, D_MODEL)),
            _const_spec((D_MODEL, 2 * ROUTER_LANES)),
            _const_spec((1, ROUTER_LANES)),
        ],
        out_specs=(
            pl.BlockSpec((tm, D_MODEL), row),
            pl.BlockSpec((tm * ROW_TILE, LANES), row),
            pl.BlockSpec((tm, ROUTER_LANES), row),
        ),
        out_shape=(
            jax.ShapeDtypeStruct((T, D_MODEL), F32),
            jax.ShapeDtypeStruct((T * ROW_TILE, LANES), jnp.uint32),
            jax.ShapeDtypeStruct((T, ROUTER_LANES), F32),
        ),
        compiler_params=pltpu.CompilerParams(
            dimension_semantics=("parallel",), vmem_limit_bytes=VMEM_LIMIT),
        name="merge_out_router",
    )(conv_act, attn_o, gates, x2, wc, wa, wo, fg, wr, br)


def _route_kernel(logit_ref, tri_ref, low_ref, dest_ref, wts_ref, blk_ref,
                  cnt_ref, base_ref, *, n_blk_lanes):
    ph = pl.program_id(0)
    i = pl.program_id(1)
    tm = logit_ref.shape[0]
    lt = logit_ref[...].T
    row8 = lax.broadcasted_iota(jnp.int32, (SUBLANES, tm), 0)

    g = jnp.where(row8 < N_GROUPS, lt[0:SUBLANES, :], -jnp.inf)
    g_max = jnp.max(g, axis=0, keepdims=True)
    g_sum = jnp.sum(jnp.exp(g - g_max), axis=0, keepdims=True)
    g_sel = jnp.min(jnp.where(g == g_max, row8, SUBLANES), axis=0, keepdims=True)
    g_w = 1.0 / g_sum

    e_sel = jnp.zeros((EXPERTS_PER_GROUP, tm), F32)
    for gi in range(N_GROUPS):
        lo = ROUTER_EXPERT_COL + gi * EXPERTS_PER_GROUP
        e_sel = jnp.where(g_sel == gi, lt[lo:lo + EXPERTS_PER_GROUP, :], e_sel)
    e_max = jnp.max(e_sel, axis=0, keepdims=True)
    e_exp = jnp.exp(e_sel - e_max)
    prob = e_exp / jnp.sum(e_exp, axis=0, keepdims=True)
    p1 = jnp.max(prob, axis=0, keepdims=True)
    i1 = jnp.min(jnp.where(prob == p1, row8, SUBLANES), axis=0, keepdims=True)
    rest = jnp.where(row8 == i1, -1.0, prob)
    p2 = jnp.max(rest, axis=0, keepdims=True)
    i2 = jnp.min(jnp.where(rest == p2, row8, SUBLANES), axis=0, keepdims=True)
    denom = p1 + p2
    e0 = g_sel * EXPERTS_PER_GROUP + i1
    e1 = g_sel * EXPERTS_PER_GROUP + i2

    row32 = lax.broadcasted_iota(jnp.int32, (N_EXPERTS, tm), 0)
    oh0 = row32 == e0
    oh1 = row32 == e1
    ohs = jnp.where(oh0 | oh1, 1.0, 0.0)
    tile_cnt = jnp.sum(ohs, axis=1, keepdims=True)

    @pl.when((ph == 0) & (i == 0))
    def _():
        cnt_ref[...] = jnp.zeros(cnt_ref.shape, F32)

    @pl.when(ph == 0)
    def _():
        cnt_ref[...] += tile_cnt

    @pl.when((ph == 1) & (i == 0))
    def _():
        cnt = cnt_ref[...]
        padded = jnp.ceil(cnt * (1.0 / RB)) * RB
        pstart = jnp.dot(low_ref[...], padded, preferred_element_type=F32,
                         precision=lax.Precision.HIGHEST)
        base_ref[...] = pstart
        pad_end = pstart + padded
        row0 = lax.broadcasted_iota(jnp.int32, (N_EXPERTS, n_blk_lanes), 1).astype(F32) * RB
        expert = lax.broadcasted_iota(jnp.int32, (N_EXPERTS, n_blk_lanes), 0).astype(F32)
        blk = jnp.minimum(jnp.sum(jnp.where(pad_end[:, 0:1] <= row0, 1.0, 0.0), axis=0, keepdims=True),
                          N_EXPERTS - 1.0)
        valid_end = jnp.sum(jnp.where(expert == blk, (pstart + cnt)[:, 0:1], 0.0), axis=0, keepdims=True)
        n_valid = jnp.clip(valid_end - row0[0:1, :], 0.0, float(RB))
        n_used = jnp.max(pad_end[:, 0:1], axis=0, keepdims=True) * (1.0 / RB) + jnp.zeros_like(blk)
        blk_ref[...] = jnp.zeros(blk_ref.shape, jnp.int32)
        blk_ref[0:1, :] = blk.astype(jnp.int32)
        blk_ref[1:2, :] = n_valid.astype(jnp.int32)
        blk_ref[2:3, :] = n_used.astype(jnp.int32)

    @pl.when(ph == 1)
    def _():
        before = jnp.dot(ohs.astype(BF16), tri_ref[...], preferred_element_type=F32)
        pos = base_ref[:, 0:1] + before
        d0 = jnp.sum(jnp.where(oh0, pos, 0.0), axis=0, keepdims=True)
        d1 = jnp.sum(jnp.where(oh1, pos, 0.0), axis=0, keepdims=True)
        dest_ref[0:1, :] = d0.astype(jnp.int32)
        dest_ref[1:2, :] = d1.astype(jnp.int32)
        wts_ref[...] = jnp.zeros(wts_ref.shape, F32)
        wts_ref[0:1, :] = p1 / denom * g_w
        wts_ref[1:2, :] = p2 / denom * g_w
        base_ref[...] += tile_cnt


def _route(logits):
    T = logits.shape[0]
    tm = TM_ROUTE
    n_rows = 2 * T + N_EXPERTS * RB
    n_blk = n_rows // RB
    n_blk_lanes = -(-n_blk // LANES) * LANES
    r = jnp.arange(tm, dtype=jnp.int32)
    tri = (r[:, None] < r[None, :]).astype(BF16)
    e = jnp.arange(N_EXPERTS, dtype=jnp.int32)
    low = (e[None, :] < e[:, None]).astype(F32)
    return pl.pallas_call(
        functools.partial(_route_kernel, n_blk_lanes=n_blk_lanes),
        grid=(2, T // tm),
        in_specs=[
            pl.BlockSpec((tm, ROUTER_LANES), lambda ph, i: (i, 0)),
            _const_spec((tm, tm)),
            _const_spec((N_EXPERTS, N_EXPERTS)),
        ],
        out_specs=(
            pl.BlockSpec((2, tm), lambda ph, i: (0, i * ph)),
            pl.BlockSpec((SUBLANES, tm), lambda ph, i: (0, i * ph)),
            pl.BlockSpec((SUBLANES, n_blk_lanes), lambda ph, i: (0, 0)),
        ),
        out_shape=(
            jax.ShapeDtypeStruct((2, T), jnp.int32),
            jax.ShapeDtypeStruct((SUBLANES, T), F32),
            jax.ShapeDtypeStruct((SUBLANES, n_blk_lanes), jnp.int32),
        ),
        scratch_shapes=[pltpu.VMEM((N_EXPERTS, LANES), F32), pltpu.VMEM((N_EXPERTS, LANES), F32)],
        compiler_params=pltpu.CompilerParams(
            dimension_semantics=("arbitrary", "arbitrary"), vmem_limit_bytes=VMEM_LIMIT),
        name="route_plan",
    )(logits, tri, low)


def _sc_worker():
    return lax.axis_index("s") * SC_CORES + lax.axis_index("c")


def _sc_scatter_rows(h, dest, n_rows):
    T = h.shape[0]
    n_workers = SC_CORES * SC_SUBCORES
    per_w = T // n_workers
    n_ch = per_w // SC_ROWS
    assert per_w * n_workers == T and n_ch * SC_ROWS == per_w and n_ch % 2 == 0
    mesh = plsc.VectorSubcoreMesh(core_axis_name="c", subcore_axis_name="s")

    @functools.partial(
        pl.kernel, mesh=mesh,
        out_type=jax.ShapeDtypeStruct((n_rows,) + h.shape[1:], h.dtype),
        scratch_types=[
            pltpu.VMEM((2, n_ch, SC_ROWS), jnp.int32),
            pltpu.VMEM((2, SC_ROWS) + h.shape[1:], h.dtype),
            pltpu.SemaphoreType.DMA((2,)),
            pltpu.SemaphoreType.DMA((2,)),
        ],
    )
    def k(h_hbm, idx_hbm, out_hbm, idx_v, rows_v, lsem, ssem):
        wid = _sc_worker()
        base = wid * per_w
        pltpu.sync_copy(idx_hbm.at[0, wid], idx_v.at[0])
        pltpu.sync_copy(idx_hbm.at[1, wid], idx_v.at[1])

        def load(c, slot):
            return pltpu.make_async_copy(h_hbm.at[pl.ds(base + c * SC_ROWS, SC_ROWS)], rows_v.at[slot],
                                         lsem.at[slot])

        def scatter(slot_k, c, slot):
            return pltpu.make_async_copy(rows_v.at[slot], out_hbm.at[idx_v.at[slot_k, c]], ssem.at[slot])

        load(0, 0).start()

        def body(g, carry):
            for slot in range(2):
                c = 2 * g + slot
                load(c, slot).wait()

                @pl.when(c >= 1)
                def _():
                    scatter(0, c - 1, 1 - slot).wait()
                    scatter(1, c - 1, 1 - slot).wait()

                @pl.when(c + 1 < n_ch)
                def _():
                    load(c + 1, 1 - slot).start()

                scatter(0, c, slot).start()
                scatter(1, c, slot).start()
            return carry

        lax.fori_loop(0, n_ch // 2, body, 0)
        scatter(0, n_ch - 1, 1).wait()
        scatter(1, n_ch - 1, 1).wait()

    return k(h, dest.reshape(2, n_workers, n_ch, SC_ROWS))


def _sc_gather_rows(table, idx):
    B = idx.shape[0]
    n_workers = SC_CORES * SC_SUBCORES
    per_w = B // n_workers
    n_ch = per_w // SC_ROWS
    assert per_w * n_workers == B and n_ch * SC_ROWS == per_w and n_ch % 2 == 0
    mesh = plsc.VectorSubcoreMesh(core_axis_name="c", subcore_axis_name="s")

    @functools.partial(
        pl.kernel, mesh=mesh,
        out_type=jax.ShapeDtypeStruct((B,) + table.shape[1:], table.dtype),
        scratch_types=[
            pltpu.VMEM((n_ch, SC_ROWS), jnp.int32),
            pltpu.VMEM((2, SC_ROWS) + table.shape[1:], table.dtype),
            pltpu.SemaphoreType.DMA((2,)),
            pltpu.SemaphoreType.DMA((2,)),
        ],
    )
    def k(table_hbm, idx_hbm, out_hbm, idx_v, rows_v, gsem, wsem):
        wid = _sc_worker()
        base = wid * per_w
        pltpu.sync_copy(idx_hbm.at[wid], idx_v)

        def gather(c, slot):
            return pltpu.make_async_copy(table_hbm.at[idx_v.at[c]], rows_v.at[slot], gsem.at[slot])

        def writeback(c, slot):
            return pltpu.make_async_copy(rows_v.at[slot], out_hbm.at[pl.ds(base + c * SC_ROWS, SC_ROWS)],
                                         wsem.at[slot])

        gather(0, 0).start()

        def body(g, carry):
            for slot in range(2):
                c = 2 * g + slot
                gather(c, slot).wait()

                @pl.when(c >= 1)
                def _():
                    writeback(c - 1, 1 - slot).wait()

                @pl.when(c + 1 < n_ch)
                def _():
                    gather(c + 1, 1 - slot).start()

                writeback(c, slot).start()
            return carry

        lax.fori_loop(0, n_ch // 2, body, 0)
        writeback(n_ch - 1, 1).wait()

    return k(table, idx.reshape(n_workers, n_ch, SC_ROWS))


def _expert_kernel(blk_exp_ref, n_used_ref, n_valid_ref, xs_ref, wg_ref, wu_ref, wd_ref, y_ref,
                   wg_bf, wu_bf, wd_bf):
    b = pl.program_id(0)
    used = b < n_used_ref[0]
    new_expert = (b == 0) | (blk_exp_ref[b] != blk_exp_ref[jnp.maximum(b - 1, 0)])

    @pl.when(used & new_expert)
    def _():
        wg_bf[...] = wg_ref[0].astype(BF16)
        wu_bf[...] = wu_ref[0].astype(BF16)
        wd_bf[...] = wd_ref[0].astype(BF16)

    @pl.when(used)
    def _():
        x = _load_rows(xs_ref, (), RB)
        row = lax.broadcasted_iota(jnp.int32, (RB, 1), 0)
        xb = jnp.where(row < n_valid_ref[b], x, jnp.zeros_like(x))
        g = jnp.dot(xb, wg_bf[...], preferred_element_type=F32)
        u = jnp.dot(xb, wu_bf[...], preferred_element_type=F32)
        hid = (g * _sigmoid(g) * u).astype(BF16)
        _store_rows(y_ref, (), jnp.dot(hid, wd_bf[...], preferred_element_type=F32))

    @pl.when(jnp.logical_not(used))
    def _():
        y_ref[...] = jnp.zeros(y_ref.shape, y_ref.dtype)


def _experts(blk_exp, n_used, n_valid, xs, wg, wu, wd):
    n_blk = blk_exp.shape[0]
    grid_spec = pltpu.PrefetchScalarGridSpec(
        num_scalar_prefetch=3,
        grid=(n_blk,),
        in_specs=[
            pl.BlockSpec((RB * ROW_TILE, LANES), lambda b, be, nu, nv: (b, 0)),
            pl.BlockSpec((1, D_MODEL, D_EXPERT), lambda b, be, nu, nv: (be[b], 0, 0)),
            pl.BlockSpec((1, D_MODEL, D_EXPERT), lambda b, be, nu, nv: (be[b], 0, 0)),
            pl.BlockSpec((1, D_EXPERT, D_MODEL), lambda b, be, nu, nv: (be[b], 0, 0)),
        ],
        out_specs=pl.BlockSpec((RB * ROW_TILE, LANES), lambda b, be, nu, nv: (b, 0)),
        scratch_shapes=[
            pltpu.VMEM((D_MODEL, D_EXPERT), BF16),
            pltpu.VMEM((D_MODEL, D_EXPERT), BF16),
            pltpu.VMEM((D_EXPERT, D_MODEL), BF16),
        ],
    )
    return pl.pallas_call(
        _expert_kernel,
        grid_spec=grid_spec,
        out_shape=jax.ShapeDtypeStruct(xs.shape, xs.dtype),
        compiler_params=pltpu.CompilerParams(
            dimension_semantics=("arbitrary",), vmem_limit_bytes=VMEM_LIMIT),
        name="expert_mlp",
    )(blk_exp, n_used, n_valid, xs, wg, wu, wd)


def _combine_kernel(x1_ref, y0_ref, y1_ref, w_ref, o_ref):
    tm = x1_ref.shape[0]
    y0 = _load_rows(y0_ref, (), tm).astype(F32)
    y1 = _load_rows(y1_ref, (), tm).astype(F32)
    w = w_ref[...].T
    o_ref[...] = x1_ref[...] + (y0 * w[:, 0:1] + y1 * w[:, 1:2])


def _combine(x1, y, wts):
    T = x1.shape[0]
    tm = TM_COMB
    row = lambda i: (i, 0)
    return pl.pallas_call(
        _combine_kernel,
        grid=(T // tm,),
        in_specs=[
            pl.BlockSpec((tm, D_MODEL), row),
            pl.BlockSpec((tm * ROW_TILE, LANES), row),
            pl.BlockSpec((tm * ROW_TILE, LANES), lambda i: (i + T // tm, 0)),
            pl.BlockSpec((SUBLANES, tm), lambda i: (0, i)),
        ],
        out_specs=pl.BlockSpec((tm, D_MODEL), row),
        out_shape=jax.ShapeDtypeStruct((T, D_MODEL), F32),
        compiler_params=pltpu.CompilerParams(
            dimension_semantics=("parallel",), vmem_limit_bytes=VMEM_LIMIT),
        name="moe_combine",
    )(x1, y, y, wts)


def kernel(x, attn_norm_g, w_in, conv_dw_w, conv_dw_b, conv_ln_g, conv_ln_b, w_conv_out,
           q_norm_g, k_norm_g, lambda_q1, lambda_k1, lambda_q2, lambda_k2, subln_g,
           w_attn_out, w_out, ffn_norm_g, w_router_group, b_router_group,
           w_router_expert, b_router_expert, w_gate_e, w_up_e, w_down_e):
    B, S, D = x.shape
    T = B * S
    l = 0
    x2 = x.reshape(T, D)

    reps = COL_Q // DA_HEAD_DIM
    qg = jnp.tile(q_norm_g[l], reps).reshape(1, COL_Q)
    kg = jnp.tile(k_norm_g[l], reps).reshape(1, COL_K)
    grp = jnp.arange(NORM_BLOCK, dtype=jnp.int32) // DA_HEAD_DIM
    bd = jnp.where(grp[:, None] == grp[None, :], 1.0 / DA_HEAD_DIM, 0.0).astype(BF16)

    conv_act, q, k, v, gates = _in_projection(
        x2, attn_norm_g[l].reshape(1, D), w_in[l].astype(BF16), qg, kg, bd, conv_dw_w[l],
        conv_dw_b[l].reshape(1, CONV_CH), conv_ln_g[l].reshape(1, CONV_CH), conv_ln_b[l].reshape(1, CONV_CH), S)

    lam_vecs = jnp.stack([lambda_q1[l], lambda_k1[l], lambda_q2[l], lambda_k2[l]]).astype(F32)
    attn_o = _diff_attention(q.reshape(B, S, COL_Q), k.reshape(B, S, COL_K),
                             v.reshape(B, S, COL_V), lam_vecs, subln_g[l].reshape(1, DA_V_DIM))

    e_lo, e_hi = ROUTER_EXPERT_COL, ROUTER_EXPERT_COL + N_EXPERTS
    wr = jnp.zeros((D, ROUTER_LANES), F32)
    wr = wr.at[:, :N_GROUPS].set(w_router_group[l]).at[:, e_lo:e_hi].set(w_router_expert[l])
    br = jnp.zeros((1, ROUTER_LANES), F32)
    br = br.at[0, :N_GROUPS].set(b_router_group[l]).at[0, e_lo:e_hi].set(b_router_expert[l])

    wr_hi = wr.astype(BF16)
    wr_lo = (wr - wr_hi.astype(F32)).astype(BF16)
    x1, h2, logits = _merge(conv_act, attn_o.reshape(T, ATTN_W), gates, x2,
                            w_conv_out[l].astype(BF16), w_attn_out[l].astype(BF16),
                            w_out[l].astype(BF16), ffn_norm_g[l].reshape(1, D),
                            jnp.concatenate([wr_hi, wr_lo], axis=1), br)

    dest, wts, blk = _route(logits)
    n_rows = 2 * T + N_EXPERTS * RB
    n_blk = n_rows // RB
    blk_exp, n_valid, n_used = blk[0, :n_blk], blk[1, :n_blk], blk[2, 0:1]

    xs = _sc_scatter_rows(h2.reshape(T, ROW_TILE, LANES), dest, n_rows)
    yb = _experts(blk_exp, n_used, n_valid, xs.reshape(n_rows * ROW_TILE, LANES),
                  w_gate_e[l], w_up_e[l], w_down_e[l])
    y = _sc_gather_rows(yb.reshape(n_rows, ROW_TILE, LANES), dest.reshape(2 * T))

    out = _combine(x1, y.reshape(2 * T * ROW_TILE, LANES), wts)
    return out.reshape(B, S, D)
```

```python
import functools
import math

import jax
import jax.numpy as jnp
from jax import lax
from jax.experimental import pallas as pl
from jax.experimental.pallas import tpu as pltpu
from jax.experimental.pallas import tpu_sc as plsc

F32 = jnp.float32
BF16 = jnp.bfloat16

D_MODEL = 1024
CONV_CH = 512
CONV_WIDTH = 31
DA_HEADS = 4
DA_HEAD_DIM = 64
DA_V_DIM = 128
ATTN_W = 512
N_GROUPS = 4
EXPERTS_PER_GROUP = 8
N_EXPERTS = 32
D_EXPERT = 512
EPS = 1e-6
LAM_INIT = 0.8 - 0.6 * math.exp(-0.3 * 0)

COL_GLU = 2 * CONV_CH
COL_Q = 512
COL_K = 512
COL_V = 512
COL_GATE = 2 * D_MODEL
OFF_Q = COL_GLU
OFF_K = OFF_Q + COL_Q
OFF_V = OFF_K + COL_K
OFF_GATE = OFF_V + COL_V
IN_COLS = OFF_GATE + COL_GATE

ROUTER_LANES = 128
ROUTER_EXPERT_COL = 8
SUBLANES = 8
ROW_TILE = D_MODEL // (2 * 128)
NORM_BLOCK = 256
HALO = 32

TM_PROJ = 512
TM_MERGE = 1024
CONV_CHUNK = 64
ATT_TQ = 1024
ATT_TK = 512
ATT_RC = 256
ATT_HEADS = 2
LANES = 128
RB = 512
TM_COMB = 1024
TM_ROUTE = 1024
SC_CORES = 2
SC_SUBCORES = 16
SC_ROWS = 64

VMEM_LIMIT = 48 * 1024 * 1024


def _sigmoid(x):
    return 1.0 / (1.0 + jnp.exp(-x))


def _const_spec(shape):
    nd = len(shape)
    return pl.BlockSpec(shape, lambda *_: (0,) * nd, pipeline_mode=pl.Buffered(1))


def _store_rows(ref, lead, val):
    rows, half = val.shape[0], val.shape[1] // 2
    hi = pltpu.bitcast(val[:, :half].astype(BF16).astype(F32), jnp.uint32)
    lo = pltpu.bitcast(val[:, half:].astype(BF16).astype(F32), jnp.uint32)
    words = hi | (lo >> 16)
    for j in range(half // LANES):
        ref[(*lead, pl.ds(j, rows, stride=ROW_TILE), slice(None))] = words[:, j * LANES:(j + 1) * LANES]


def _load_rows(ref, lead, rows):
    words = jnp.concatenate(
        [ref[(*lead, pl.ds(j, rows, stride=ROW_TILE), slice(None))] for j in range(ROW_TILE)], axis=1)
    hi = pltpu.bitcast(words & jnp.uint32(0xFFFF0000), F32).astype(BF16)
    lo = pltpu.bitcast(words << 16, F32).astype(BF16)
    return jnp.concatenate([hi, lo], axis=1)


def _group_mean_sq(t, bd_ref):
    sq = t * t
    hi = sq.astype(BF16)
    lo = (sq - hi.astype(F32)).astype(BF16)
    w = bd_ref.shape[0]
    parts = [jnp.dot(hi[:, c:c + w], bd_ref[...], preferred_element_type=F32)
             + jnp.dot(lo[:, c:c + w], bd_ref[...], preferred_element_type=F32)
             for c in range(0, t.shape[1], w)]
    return jnp.concatenate(parts, axis=1)


def _inproj_kernel(x_ref, g_ref, w_ref, qg_ref, kg_ref, bd_ref, cw_ref, cb_ref, lng_ref, lnb_ref,
                   conv_ref, q_ref, k_ref, v_ref, gate_ref, buf_ref, halo_ref, *, tiles_per_seq):
    i = pl.program_id(0)
    tm = x_ref.shape[0]
    x = x_ref[...]
    ms = jnp.mean(x * x, axis=-1, keepdims=True)
    h = (x * lax.rsqrt(ms + EPS) * g_ref[...]).astype(BF16)

    def proj(lo, width):
        return jnp.dot(h, w_ref[:, lo:lo + width], preferred_element_type=F32)

    a = proj(0, CONV_CH)
    gt = proj(CONV_CH, CONV_CH)
    glu = a * _sigmoid(gt)

    rows = HALO + tm
    buf_ref[0, 0:HALO, :] = jnp.where(i % tiles_per_seq == 0, 0.0, halo_ref[...])
    buf_ref[0, HALO:rows, :] = glu
    halo_ref[...] = glu[tm - HALO:tm, :]
    for r in range(1, SUBLANES):
        buf_ref[r, 0:rows - SUBLANES, :] = buf_ref[0, r:r + rows - SUBLANES, :]
    first = HALO - (CONV_WIDTH - 1)
    for c in range(tm // CONV_CHUNK):
        r0 = c * CONV_CHUNK
        acc = jnp.broadcast_to(cb_ref[...], (CONV_CHUNK, CONV_CH))
        for j in range(CONV_WIDTH):
            shift = (first + j) % SUBLANES
            lo = r0 + first + j - shift
            acc = acc + cw_ref[j:j + 1, :] * buf_ref[shift, lo:lo + CONV_CHUNK, :]
        mu = jnp.mean(acc, axis=-1, keepdims=True)
        d = acc - mu
        var = jnp.mean(d * d, axis=-1, keepdims=True)
        y = d * lax.rsqrt(var + EPS) * lng_ref[...] + lnb_ref[...]
        conv_ref[r0:r0 + CONV_CHUNK, :] = (y * _sigmoid(y)).astype(BF16)

    q = proj(OFF_Q, COL_Q)
    qn = q * lax.rsqrt(_group_mean_sq(q, bd_ref) + EPS) * qg_ref[...]
    q_ref[...] = (qn * (DA_HEAD_DIM ** -0.5)).astype(BF16)

    k = proj(OFF_K, COL_K)
    kn = k * lax.rsqrt(_group_mean_sq(k, bd_ref) + EPS) * kg_ref[...]
    k_ref[...] = kn.astype(BF16)

    v_ref[...] = proj(OFF_V, COL_V).astype(BF16)

    for c in range(COL_GATE // 512):
        gate_ref[:, c * 512:(c + 1) * 512] = _sigmoid(proj(OFF_GATE + c * 512, 512)).astype(BF16)


def _in_projection(x2, g, w_bf, qg, kg, bd, dw_w, dw_b, ln_g, ln_b, seq_len):
    T = x2.shape[0]
    tm = TM_PROJ
    row = lambda i: (i, 0)
    out_shape = (
        jax.ShapeDtypeStruct((T, CONV_CH), BF16),
        jax.ShapeDtypeStruct((T, COL_Q), BF16),
        jax.ShapeDtypeStruct((T, COL_K), BF16),
        jax.ShapeDtypeStruct((T, COL_V), BF16),
        jax.ShapeDtypeStruct((T, COL_GATE), BF16),
    )
    return pl.pallas_call(
        functools.partial(_inproj_kernel, tiles_per_seq=seq_len // tm),
        grid=(T // tm,),
        in_specs=[
            pl.BlockSpec((tm, D_MODEL), row),
            _const_spec((1, D_MODEL)),
            _const_spec((D_MODEL, IN_COLS)),
            _const_spec((1, COL_Q)),
            _const_spec((1, COL_K)),
            _const_spec((NORM_BLOCK, NORM_BLOCK)),
            _const_spec((CONV_WIDTH, CONV_CH)),
            _const_spec((1, CONV_CH)),
            _const_spec((1, CONV_CH)),
            _const_spec((1, CONV_CH)),
        ],
        out_specs=(
            pl.BlockSpec((tm, CONV_CH), row),
            pl.BlockSpec((tm, COL_Q), row),
            pl.BlockSpec((tm, COL_K), row),
            pl.BlockSpec((tm, COL_V), row),
            pl.BlockSpec((tm, COL_GATE), row),
        ),
        out_shape=out_shape,
        scratch_shapes=[pltpu.VMEM((SUBLANES, HALO + tm, CONV_CH), F32), pltpu.VMEM((HALO, CONV_CH), F32)],
        compiler_params=pltpu.CompilerParams(
            dimension_semantics=("arbitrary",), vmem_limit_bytes=VMEM_LIMIT),
        name="in_projection",
    )(x2, g, w_bf, qg, kg, bd, dw_w, dw_b, ln_g, ln_b)


def _attn_kernel(q_ref, k_ref, v_ref, lam_ref, sg_ref, o_ref, qs_ref, m_ref, acc_ref, s_ref, *, tq, tk):
    qi = pl.program_id(2)
    lane = lax.broadcasted_iota(jnp.int32, (tq, DA_V_DIM), 1)
    for hh in range(ATT_HEADS):
        q = q_ref[0, :, hh * DA_V_DIM:(hh + 1) * DA_V_DIM]
        zero = jnp.zeros_like(q)
        qs_ref[hh, 0:tq, :] = jnp.where(lane < DA_HEAD_DIM, q, zero)
        qs_ref[hh, tq:2 * tq, :] = jnp.where(lane >= DA_HEAD_DIM, q, zero)
    m_ref[...] = jnp.full(m_ref.shape, -jnp.inf, F32)
    acc_ref[...] = jnp.zeros(acc_ref.shape, F32)
    ones = jnp.ones((tk, LANES), BF16)
    rc = ATT_RC
    chunks = [slice(r0, r0 + rc) for r0 in range(0, 2 * tq, rc)]
    n_diag = tq // tk

    def scores(hh, j, rows):
        kj = k_ref[0, pl.ds(pl.multiple_of(j * tk, tk), tk), hh * DA_V_DIM:(hh + 1) * DA_V_DIM]
        return lax.dot_general(qs_ref[hh, rows, :], kj, (((1,), (1,)), ((), ())),
                               preferred_element_type=F32)

    def step(j, diag, prefetch):
        for hh in range(ATT_HEADS):
            vj = jnp.concatenate(
                [v_ref[0, pl.ds(pl.multiple_of(j * tk, tk), tk), hh * DA_V_DIM:(hh + 1) * DA_V_DIM], ones], axis=1)
            for rows in chunks:
                r_lo = rows.start % tq
                if diag is None:
                    kw, masked = tk, False
                else:
                    k_lo = diag * tk
                    if k_lo > r_lo + rc - 1:
                        if prefetch:
                            s_ref[hh, rows, :] = scores(hh, j + 1, rows)
                        continue
                    kw = min(tk, -(-(r_lo + rc - k_lo) // (2 * LANES)) * (2 * LANES))
                    masked = k_lo + kw - 1 > r_lo
                s = s_ref[hh, rows, 0:kw]
                if prefetch:
                    s_ref[hh, rows, :] = scores(hh, j + 1, rows)
                if masked:
                    r = lax.broadcasted_iota(jnp.int32, s.shape, 0) + r_lo
                    c = lax.broadcasted_iota(jnp.int32, s.shape, 1) + k_lo
                    s = jnp.where(c <= r, s, -jnp.inf)
                m_old = m_ref[hh, rows, :]
                m_new = jnp.maximum(m_old, jnp.max(s, axis=-1, keepdims=True))
                alpha = jnp.exp(m_old - m_new)
                p = jnp.exp(s - jnp.tile(m_new, (1, kw // LANES)))
                pv = jnp.dot(p.astype(BF16), vj[0:kw, :], preferred_element_type=F32)
                acc_ref[hh, rows, :] = jnp.tile(alpha, (1, 2)) * acc_ref[hh, rows, :] + pv
                m_ref[hh, rows, :] = m_new

    for hh in range(ATT_HEADS):
        for rows in chunks:
            s_ref[hh, rows, :] = scores(hh, 0, rows)

    def body(j, carry):
        step(j, None, True)
        return carry

    n_full = qi * n_diag
    lax.fori_loop(0, n_full, body, 0)
    for t in range(n_diag):
        step(n_full + t, t, t + 1 < n_diag)

    lam = (jnp.exp(jnp.sum(lam_ref[0:1, :] * lam_ref[1:2, :], axis=-1, keepdims=True))
           - jnp.exp(jnp.sum(lam_ref[2:3, :] * lam_ref[3:4, :], axis=-1, keepdims=True))
           + LAM_INIT)
    for hh in range(ATT_HEADS):
        o1 = acc_ref[hh, 0:tq, 0:LANES] / acc_ref[hh, 0:tq, LANES:2 * LANES]
        o2 = acc_ref[hh, tq:2 * tq, 0:LANES] / acc_ref[hh, tq:2 * tq, LANES:2 * LANES]
        o = o1 - lam * o2
        ms = jnp.mean(o * o, axis=-1, keepdims=True)
        on = o * lax.rsqrt(ms + EPS) * sg_ref[...]
        o_ref[0, :, hh * DA_V_DIM:(hh + 1) * DA_V_DIM] = (on * (1.0 - LAM_INIT)).astype(BF16)


def _diff_attention(q3, k3, v3, lam_vecs, subln_g):
    B, S, _ = q3.shape
    tq, tk = ATT_TQ, ATT_TK
    w = ATT_HEADS * DA_V_DIM
    return pl.pallas_call(
        functools.partial(_attn_kernel, tq=tq, tk=tk),
        grid=(B, DA_HEADS // ATT_HEADS, S // tq),
        in_specs=[
            pl.BlockSpec((1, tq, w), lambda b, h, i: (b, i, h)),
            pl.BlockSpec((1, S, w), lambda b, h, i: (b, 0, h)),
            pl.BlockSpec((1, S, w), lambda b, h, i: (b, 0, h)),
            _const_spec((4, DA_HEAD_DIM)),
            _const_spec((1, DA_V_DIM)),
        ],
        out_specs=pl.BlockSpec((1, tq, w), lambda b, h, i: (b, i, h)),
        out_shape=jax.ShapeDtypeStruct((B, S, ATTN_W), BF16),
        scratch_shapes=[
            pltpu.VMEM((ATT_HEADS, 2 * tq, DA_V_DIM), BF16),
            pltpu.VMEM((ATT_HEADS, 2 * tq, LANES), F32),
            pltpu.VMEM((ATT_HEADS, 2 * tq, 2 * LANES), F32),
            pltpu.VMEM((ATT_HEADS, 2 * tq, tk), F32),
        ],
        compiler_params=pltpu.CompilerParams(
            dimension_semantics=("parallel", "parallel", "parallel"),
            vmem_limit_bytes=VMEM_LIMIT),
        name="diff_attention",
    )(q3, k3, v3, lam_vecs, subln_g)


def _merge_kernel(conv_ref, attn_ref, gate_ref, x_ref, wc_ref, wa_ref, wo_ref, fg_ref,
                  wr_ref, br_ref, x1_ref, h2_ref, logit_ref):
    c = jnp.dot(conv_ref[...], wc_ref[...], preferred_element_type=F32)
    a = jnp.dot(attn_ref[...], wa_ref[...], preferred_element_type=F32)
    g0 = gate_ref[:, 0:D_MODEL].astype(F32)
    g1 = gate_ref[:, D_MODEL:2 * D_MODEL].astype(F32)
    merged = (g0 * c + g1 * a).astype(BF16)
    x1 = x_ref[...] + jnp.dot(merged, wo_ref[...], preferred_element_type=F32)
    x1_ref[...] = x1
    ms = jnp.mean(x1 * x1, axis=-1, keepdims=True)
    h2 = x1 * lax.rsqrt(ms + EPS) * fg_ref[...]
    _store_rows(h2_ref, (), h2)
    h2_hi = h2.astype(BF16)
    h2_lo = (h2 - h2_hi.astype(F32)).astype(BF16)
    r = (jnp.dot(h2_hi, wr_ref[...], preferred_element_type=F32)
         + jnp.dot(h2_lo, wr_ref[...], preferred_element_type=F32))
    logit_ref[...] = r[:, 0:ROUTER_LANES] + r[:, ROUTER_LANES:2 * ROUTER_LANES] + br_ref[...]


def _merge(conv_act, attn_o, gates, x2, wc, wa, wo, fg, wr, br):
    T = x2.shape[0]
    tm = TM_MERGE
    row = lambda i: (i, 0)
    return pl.pallas_call(
        _merge_kernel,
        grid=(T // tm,),
        in_specs=[
            pl.BlockSpec((tm, CONV_CH), row),
            pl.BlockSpec((tm, ATTN_W), row),
            pl.BlockSpec((tm, COL_GATE), row),
            pl.BlockSpec((tm, D_MODEL), row),
            _const_spec((CONV_CH, D_MODEL)),
            _const_spec((ATTN_W, D_MODEL)),
            _const_spec((D_MODEL, D_MODEL)),
            _const_spec((1, D_MODEL)),
            _const_spec((D_MODEL, 2 * ROUTER_LANES)),
            _const_spec((1, ROUTER_LANES)),
        ],
        out_specs=(
            pl.BlockSpec((tm, D_MODEL), row),
            pl.BlockSpec((tm * ROW_TILE, LANES), row),
            pl.BlockSpec((tm, ROUTER_LANES), row),
        ),
        out_shape=(
            jax.ShapeDtypeStruct((T, D_MODEL), F32),
            jax.ShapeDtypeStruct((T * ROW_TILE, LANES), jnp.uint32),
            jax.ShapeDtypeStruct((T, ROUTER_LANES), F32),
        ),
        compiler_params=pltpu.CompilerParams(
            dimension_semantics=("parallel",), vmem_limit_bytes=VMEM_LIMIT),
        name="merge_out_router",
    )(conv_act, attn_o, gates, x2, wc, wa, wo, fg, wr, br)


def _route_kernel(logit_ref, tri_ref, low_ref, dest_ref, wts_ref, blk_ref,
                  cnt_ref, base_ref, *, n_blk_lanes):
    ph = pl.program_id(0)
    i = pl.program_id(1)
    tm = logit_ref.shape[0]
    lt = logit_ref[...].T
    row8 = lax.broadcasted_iota(jnp.int32, (SUBLANES, tm), 0)

    g = jnp.where(row8 < N_GROUPS, lt[0:SUBLANES, :], -jnp.inf)
    g_max = jnp.max(g, axis=0, keepdims=True)
    g_sum = jnp.sum(jnp.exp(g - g_max), axis=0, keepdims=True)
    g_sel = jnp.min(jnp.where(g == g_max, row8, SUBLANES), axis=0, keepdims=True)
    g_w = 1.0 / g_sum

    e_sel = jnp.zeros((EXPERTS_PER_GROUP, tm), F32)
    for gi in range(N_GROUPS):
        lo = ROUTER_EXPERT_COL + gi * EXPERTS_PER_GROUP
        e_sel = jnp.where(g_sel == gi, lt[lo:lo + EXPERTS_PER_GROUP, :], e_sel)
    e_max = jnp.max(e_sel, axis=0, keepdims=True)
    e_exp = jnp.exp(e_sel - e_max)
    prob = e_exp / jnp.sum(e_exp, axis=0, keepdims=True)
    p1 = jnp.max(prob, axis=0, keepdims=True)
    i1 = jnp.min(jnp.where(prob == p1, row8, SUBLANES), axis=0, keepdims=True)
    rest = jnp.where(row8 == i1, -1.0, prob)
    p2 = jnp.max(rest, axis=0, keepdims=True)
    i2 = jnp.min(jnp.where(rest == p2, row8, SUBLANES), axis=0, keepdims=True)
    denom = p1 + p2
    e0 = g_sel * EXPERTS_PER_GROUP + i1
    e1 = g_sel * EXPERTS_PER_GROUP + i2

    row32 = lax.broadcasted_iota(jnp.int32, (N_EXPERTS, tm), 0)
    oh0 = row32 == e0
    oh1 = row32 == e1
    ohs = jnp.where(oh0 | oh1, 1.0, 0.0)
    tile_cnt = jnp.sum(ohs, axis=1, keepdims=True)

    @pl.when((ph == 0) & (i == 0))
    def _():
        cnt_ref[...] = jnp.zeros(cnt_ref.shape, F32)

    @pl.when(ph == 0)
    def _():
        cnt_ref[...] += tile_cnt

    @pl.when((ph == 1) & (i == 0))
    def _():
        cnt = cnt_ref[...]
        padded = jnp.ceil(cnt * (1.0 / RB)) * RB
        pstart = jnp.dot(low_ref[...], padded, preferred_element_type=F32,
                         precision=lax.Precision.HIGHEST)
        base_ref[...] = pstart
        pad_end = pstart + padded
        row0 = lax.broadcasted_iota(jnp.int32, (N_EXPERTS, n_blk_lanes), 1).astype(F32) * RB
        expert = lax.broadcasted_iota(jnp.int32, (N_EXPERTS, n_blk_lanes), 0).astype(F32)
        blk = jnp.minimum(jnp.sum(jnp.where(pad_end[:, 0:1] <= row0, 1.0, 0.0), axis=0, keepdims=True),
                          N_EXPERTS - 1.0)
        valid_end = jnp.sum(jnp.where(expert == blk, (pstart + cnt)[:, 0:1], 0.0), axis=0, keepdims=True)
        n_valid = jnp.clip(valid_end - row0[0:1, :], 0.0, float(RB))
        n_used = jnp.max(pad_end[:, 0:1], axis=0, keepdims=True) * (1.0 / RB) + jnp.zeros_like(blk)
        blk_ref[...] = jnp.zeros(blk_ref.shape, jnp.int32)
        blk_ref[0:1, :] = blk.astype(jnp.int32)
        blk_ref[1:2, :] = n_valid.astype(jnp.int32)
        blk_ref[2:3, :] = n_used.astype(jnp.int32)

    @pl.when(ph == 1)
    def _():
        before = jnp.dot(ohs.astype(BF16), tri_ref[...], preferred_element_type=F32)
        pos = base_ref[:, 0:1] + before
        d0 = jnp.sum(jnp.where(oh0, pos, 0.0), axis=0, keepdims=True)
        d1 = jnp.sum(jnp.where(oh1, pos, 0.0), axis=0, keepdims=True)
        dest_ref[0:1, :] = d0.astype(jnp.int32)
        dest_ref[1:2, :] = d1.astype(jnp.int32)
        wts_ref[...] = jnp.zeros(wts_ref.shape, F32)
        wts_ref[0:1, :] = p1 / denom * g_w
        wts_ref[1:2, :] = p2 / denom * g_w
        base_ref[...] += tile_cnt


def _route(logits):
    T = logits.shape[0]
    tm = TM_ROUTE
    n_rows = 2 * T + N_EXPERTS * RB
    n_blk = n_rows // RB
    n_blk_lanes = -(-n_blk // LANES) * LANES
    r = jnp.arange(tm, dtype=jnp.int32)
    tri = (r[:, None] < r[None, :]).astype(BF16)
    e = jnp.arange(N_EXPERTS, dtype=jnp.int32)
    low = (e[None, :] < e[:, None]).astype(F32)
    return pl.pallas_call(
        functools.partial(_route_kernel, n_blk_lanes=n_blk_lanes),
        grid=(2, T // tm),
        in_specs=[
            pl.BlockSpec((tm, ROUTER_LANES), lambda ph, i: (i, 0)),
            _const_spec((tm, tm)),
            _const_spec((N_EXPERTS, N_EXPERTS)),
        ],
        out_specs=(
            pl.BlockSpec((2, tm), lambda ph, i: (0, i * ph)),
            pl.BlockSpec((SUBLANES, tm), lambda ph, i: (0, i * ph)),
            pl.BlockSpec((SUBLANES, n_blk_lanes), lambda ph, i: (0, 0)),
        ),
        out_shape=(
            jax.ShapeDtypeStruct((2, T), jnp.int32),
            jax.ShapeDtypeStruct((SUBLANES, T), F32),
            jax.ShapeDtypeStruct((SUBLANES, n_blk_lanes), jnp.int32),
        ),
        scratch_shapes=[pltpu.VMEM((N_EXPERTS, LANES), F32), pltpu.VMEM((N_EXPERTS, LANES), F32)],
        compiler_params=pltpu.CompilerParams(
            dimension_semantics=("arbitrary", "arbitrary"), vmem_limit_bytes=VMEM_LIMIT),
        name="route_plan",
    )(logits, tri, low)


def _sc_worker():
    return lax.axis_index("s") * SC_CORES + lax.axis_index("c")


def _sc_scatter_rows(h, dest, n_rows):
    T = h.shape[0]
    n_workers = SC_CORES * SC_SUBCORES
    per_w = T // n_workers
    n_ch = per_w // SC_ROWS
    assert per_w * n_workers == T and n_ch * SC_ROWS == per_w and n_ch % 2 == 0
    mesh = plsc.VectorSubcoreMesh(core_axis_name="c", subcore_axis_name="s")

    @functools.partial(
        pl.kernel, mesh=mesh,
        out_type=jax.ShapeDtypeStruct((n_rows,) + h.shape[1:], h.dtype),
        scratch_types=[
            pltpu.VMEM((2, n_ch, SC_ROWS), jnp.int32),
            pltpu.VMEM((2, SC_ROWS) + h.shape[1:], h.dtype),
            pltpu.SemaphoreType.DMA((2,)),
            pltpu.SemaphoreType.DMA((2,)),
        ],
    )
    def k(h_hbm, idx_hbm, out_hbm, idx_v, rows_v, lsem, ssem):
        wid = _sc_worker()
        base = wid * per_w
        pltpu.sync_copy(idx_hbm.at[0, wid], idx_v.at[0])
        pltpu.sync_copy(idx_hbm.at[1, wid], idx_v.at[1])

        def load(c, slot):
            return pltpu.make_async_copy(h_hbm.at[pl.ds(base + c * SC_ROWS, SC_ROWS)], rows_v.at[slot],
                                         lsem.at[slot])

        def scatter(slot_k, c, slot):
            return pltpu.make_async_copy(rows_v.at[slot], out_hbm.at[idx_v.at[slot_k, c]], ssem.at[slot])

        load(0, 0).start()

        def body(g, carry):
            for slot in range(2):
                c = 2 * g + slot
                load(c, slot).wait()

                @pl.when(c >= 1)
                def _():
                    scatter(0, c - 1, 1 - slot).wait()
                    scatter(1, c - 1, 1 - slot).wait()

                @pl.when(c + 1 < n_ch)
                def _():
                    load(c + 1, 1 - slot).start()

                scatter(0, c, slot).start()
                scatter(1, c, slot).start()
            return carry

        lax.fori_loop(0, n_ch // 2, body, 0)
        scatter(0, n_ch - 1, 1).wait()
        scatter(1, n_ch - 1, 1).wait()

    return k(h, dest.reshape(2, n_workers, n_ch, SC_ROWS))


def _sc_gather_rows(table, idx):
    B = idx.shape[0]
    n_workers = SC_CORES * SC_SUBCORES
    per_w = B // n_workers
    n_ch = per_w // SC_ROWS
    assert per_w * n_workers == B and n_ch * SC_ROWS == per_w and n_ch % 2 == 0
    mesh = plsc.VectorSubcoreMesh(core_axis_name="c", subcore_axis_name="s")

    @functools.partial(
        pl.kernel, mesh=mesh,
        out_type=jax.ShapeDtypeStruct((B,) + table.shape[1:], table.dtype),
        scratch_types=[
            pltpu.VMEM((n_ch, SC_ROWS), jnp.int32),
            pltpu.VMEM((2, SC_ROWS) + table.shape[1:], table.dtype),
            pltpu.SemaphoreType.DMA((2,)),
            pltpu.SemaphoreType.DMA((2,)),
        ],
    )
    def k(table_hbm, idx_hbm, out_hbm, idx_v, rows_v, gsem, wsem):
        wid = _sc_worker()
        base = wid * per_w
        pltpu.sync_copy(idx_hbm.at[wid], idx_v)

        def gather(c, slot):
            return pltpu.make_async_copy(table_hbm.at[idx_v.at[c]], rows_v.at[slot], gsem.at[slot])

        def writeback(c, slot):
            return pltpu.make_async_copy(rows_v.at[slot], out_hbm.at[pl.ds(base + c * SC_ROWS, SC_ROWS)],
                                         wsem.at[slot])

        gather(0, 0).start()

        def body(g, carry):
            for slot in range(2):
                c = 2 * g + slot
                gather(c, slot).wait()

                @pl.when(c >= 1)
                def _():
                    writeback(c - 1, 1 - slot).wait()

                @pl.when(c + 1 < n_ch)
                def _():
                    gather(c + 1, 1 - slot).start()

                writeback(c, slot).start()
            return carry

        lax.fori_loop(0, n_ch // 2, body, 0)
        writeback(n_ch - 1, 1).wait()

    return k(table, idx.reshape(n_workers, n_ch, SC_ROWS))


def _expert_kernel(blk_exp_ref, n_used_ref, n_valid_ref, xs_ref, wg_ref, wu_ref, wd_ref, y_ref,
                   wg_bf, wu_bf, wd_bf):
    b = pl.program_id(0)
    used = b < n_used_ref[0]
    new_expert = (b == 0) | (blk_exp_ref[b] != blk_exp_ref[jnp.maximum(b - 1, 0)])

    @pl.when(used & new_expert)
    def _():
        wg_bf[...] = wg_ref[0].astype(BF16)
        wu_bf[...] = wu_ref[0].astype(BF16)
        wd_bf[...] = wd_ref[0].astype(BF16)

    @pl.when(used)
    def _():
        x = _load_rows(xs_ref, (), RB)
        row = lax.broadcasted_iota(jnp.int32, (RB, 1), 0)
        xb = jnp.where(row < n_valid_ref[b], x, jnp.zeros_like(x))
        g = jnp.dot(xb, wg_bf[...], preferred_element_type=F32)
        u = jnp.dot(xb, wu_bf[...], preferred_element_type=F32)
        hid = (g * _sigmoid(g) * u).astype(BF16)
        _store_rows(y_ref, (), jnp.dot(hid, wd_bf[...], preferred_element_type=F32))

    @pl.when(jnp.logical_not(used))
    def _():
        y_ref[...] = jnp.zeros(y_ref.shape, y_ref.dtype)


def _experts(blk_exp, n_used, n_valid, xs, wg, wu, wd):
    n_blk = blk_exp.shape[0]
    grid_spec = pltpu.PrefetchScalarGridSpec(
        num_scalar_prefetch=3,
        grid=(n_blk,),
        in_specs=[
            pl.BlockSpec((RB * ROW_TILE, LANES), lambda b, be, nu, nv: (b, 0)),
            pl.BlockSpec((1, D_MODEL, D_EXPERT), lambda b, be, nu, nv: (be[b], 0, 0)),
            pl.BlockSpec((1, D_MODEL, D_EXPERT), lambda b, be, nu, nv: (be[b], 0, 0)),
            pl.BlockSpec((1, D_EXPERT, D_MODEL), lambda b, be, nu, nv: (be[b], 0, 0)),
        ],
        out_specs=pl.BlockSpec((RB * ROW_TILE, LANES), lambda b, be, nu, nv: (b, 0)),
        scratch_shapes=[
            pltpu.VMEM((D_MODEL, D_EXPERT), BF16),
            pltpu.VMEM((D_MODEL, D_EXPERT), BF16),
            pltpu.VMEM((D_EXPERT, D_MODEL), BF16),
        ],
    )
    return pl.pallas_call(
        _expert_kernel,
        grid_spec=grid_spec,
        out_shape=jax.ShapeDtypeStruct(xs.shape, xs.dtype),
        compiler_params=pltpu.CompilerParams(
            dimension_semantics=("arbitrary",), vmem_limit_bytes=VMEM_LIMIT),
        name="expert_mlp",
    )(blk_exp, n_used, n_valid, xs, wg, wu, wd)


def _combine_kernel(x1_ref, y0_ref, y1_ref, w_ref, o_ref):
    tm = x1_ref.shape[0]
    y0 = _load_rows(y0_ref, (), tm).astype(F32)
    y1 = _load_rows(y1_ref, (), tm).astype(F32)
    w = w_ref[...].T
    o_ref[...] = x1_ref[...] + (y0 * w[:, 0:1] + y1 * w[:, 1:2])


def _combine(x1, y, wts):
    T = x1.shape[0]
    tm = TM_COMB
    row = lambda i: (i, 0)
    return pl.pallas_call(
        _combine_kernel,
        grid=(T // tm,),
        in_specs=[
            pl.BlockSpec((tm, D_MODEL), row),
            pl.BlockSpec((tm * ROW_TILE, LANES), row),
            pl.BlockSpec((tm * ROW_TILE, LANES), lambda i: (i + T // tm, 0)),
            pl.BlockSpec((SUBLANES, tm), lambda i: (0, i)),
        ],
        out_specs=pl.BlockSpec((tm, D_MODEL), row),
        out_shape=jax.ShapeDtypeStruct((T, D_MODEL), F32),
        compiler_params=pltpu.CompilerParams(
            dimension_semantics=("parallel",), vmem_limit_bytes=VMEM_LIMIT),
        name="moe_combine",
    )(x1, y, y, wts)


def kernel(x, attn_norm_g, w_in, conv_dw_w, conv_dw_b, conv_ln_g, conv_ln_b, w_conv_out,
           q_norm_g, k_norm_g, lambda_q1, lambda_k1, lambda_q2, lambda_k2, subln_g,
           w_attn_out, w_out, ffn_norm_g, w_router_group, b_router_group,
           w_router_expert, b_router_expert, w_gate_e, w_up_e, w_down_e):
    B, S, D = x.shape
    T = B * S
    l = 0
    x2 = x.reshape(T, D)

    reps = COL_Q // DA_HEAD_DIM
    qg = jnp.tile(q_norm_g[l], reps).reshape(1, COL_Q)
    kg = jnp.tile(k_norm_g[l], reps).reshape(1, COL_K)
    grp = jnp.arange(NORM_BLOCK, dtype=jnp.int32) // DA_HEAD_DIM
    bd = jnp.where(grp[:, None] == grp[None, :], 1.0 / DA_HEAD_DIM, 0.0).astype(BF16)

    conv_act, q, k, v, gates = _in_projection(
        x2, attn_norm_g[l].reshape(1, D), w_in[l].astype(BF16), qg, kg, bd, conv_dw_w[l],
        conv_dw_b[l].reshape(1, CONV_CH), conv_ln_g[l].reshape(1, CONV_CH), conv_ln_b[l].reshape(1, CONV_CH), S)

    lam_vecs = jnp.stack([lambda_q1[l], lambda_k1[l], lambda_q2[l], lambda_k2[l]]).astype(F32)
    attn_o = _diff_attention(q.reshape(B, S, COL_Q), k.reshape(B, S, COL_K),
                             v.reshape(B, S, COL_V), lam_vecs, subln_g[l].reshape(1, DA_V_DIM))

    e_lo, e_hi = ROUTER_EXPERT_COL, ROUTER_EXPERT_COL + N_EXPERTS
    wr = jnp.zeros((D, ROUTER_LANES), F32)
    wr = wr.at[:, :N_GROUPS].set(w_router_group[l]).at[:, e_lo:e_hi].set(w_router_expert[l])
    br = jnp.zeros((1, ROUTER_LANES), F32)
    br = br.at[0, :N_GROUPS].set(b_router_group[l]).at[0, e_lo:e_hi].set(b_router_expert[l])

    wr_hi = wr.astype(BF16)
    wr_lo = (wr - wr_hi.astype(F32)).astype(BF16)
    x1, h2, logits = _merge(conv_act, attn_o.reshape(T, ATTN_W), gates, x2,
                            w_conv_out[l].astype(BF16), w_attn_out[l].astype(BF16),
                            w_out[l].astype(BF16), ffn_norm_g[l].reshape(1, D),
                            jnp.concatenate([wr_hi, wr_lo], axis=1), br)

    dest, wts, blk = _route(logits)
    n_rows = 2 * T + N_EXPERTS * RB
    n_blk = n_rows // RB
    blk_exp, n_valid, n_used = blk[0, :n_blk], blk[1, :n_blk], blk[2, 0:1]

    xs = _sc_scatter_rows(h2.reshape(T, ROW_TILE, LANES), dest, n_rows)
    yb = _experts(blk_exp, n_used, n_valid, xs.reshape(n_rows * ROW_TILE, LANES),
                  w_gate_e[l], w_up_e[l], w_down_e[l])
    y = _sc_gather_rows(yb.reshape(n_rows, ROW_TILE, LANES), dest.reshape(2 * T))

    out = _combine(x1, y.reshape(2 * T * ROW_TILE, LANES), wts)
    return out.reshape(B, S, D)
```

```python
import functools
import math

import jax
import jax.numpy as jnp
from jax import lax
from jax.experimental import pallas as pl
from jax.experimental.pallas import tpu as pltpu
from jax.experimental.pallas import tpu_sc as plsc

F32 = jnp.float32
BF16 = jnp.bfloat16

D_MODEL = 1024
CONV_CH = 512
CONV_WIDTH = 31
DA_HEADS = 4
DA_HEAD_DIM = 64
DA_V_DIM = 128
ATTN_W = 512
N_GROUPS = 4
EXPERTS_PER_GROUP = 8
N_EXPERTS = 32
D_EXPERT = 512
EPS = 1e-6
LAM_INIT = 0.8 - 0.6 * math.exp(-0.3 * 0)

COL_GLU = 2 * CONV_CH
COL_Q = 512
COL_K = 512
COL_V = 512
COL_GATE = 2 * D_MODEL
OFF_Q = COL_GLU
OFF_K = OFF_Q + COL_Q
OFF_V = OFF_K + COL_K
OFF_GATE = OFF_V + COL_V
IN_COLS = OFF_GATE + COL_GATE

ROUTER_LANES = 128
ROUTER_EXPERT_COL = 8
SUBLANES = 8
ROW_TILE = D_MODEL // (2 * 128)
NORM_BLOCK = 256
HALO = 32

TM_PROJ = 512
TM_MERGE = 1024
CONV_CHUNK = 64
ATT_TQ = 1024
ATT_TK = 512
ATT_RC = 256
ATT_HEADS = 2
LANES = 128
RB = 512
TM_COMB = 1024
COMBINE_PARTS = 2
TM_ROUTE = 1024
SC_CORES = 2
SC_SUBCORES = 16
SC_ROWS = 64

VMEM_LIMIT = 48 * 1024 * 1024


def _sigmoid(x):
    return 1.0 / (1.0 + jnp.exp(-x))


def _const_spec(shape):
    nd = len(shape)
    return pl.BlockSpec(shape, lambda *_: (0,) * nd, pipeline_mode=pl.Buffered(1))


def _store_rows(ref, lead, val):
    rows, half = val.shape[0], val.shape[1] // 2
    hi = pltpu.bitcast(val[:, :half].astype(BF16).astype(F32), jnp.uint32)
    lo = pltpu.bitcast(val[:, half:].astype(BF16).astype(F32), jnp.uint32)
    words = hi | (lo >> 16)
    for j in range(half // LANES):
        ref[(*lead, pl.ds(j, rows, stride=ROW_TILE), slice(None))] = words[:, j * LANES:(j + 1) * LANES]


def _load_rows(ref, lead, rows):
    words = jnp.concatenate(
        [ref[(*lead, pl.ds(j, rows, stride=ROW_TILE), slice(None))] for j in range(ROW_TILE)], axis=1)
    hi = pltpu.bitcast(words & jnp.uint32(0xFFFF0000), F32).astype(BF16)
    lo = pltpu.bitcast(words << 16, F32).astype(BF16)
    return jnp.concatenate([hi, lo], axis=1)


def _group_mean_sq(t, bd_ref):
    sq = t * t
    hi = sq.astype(BF16)
    lo = (sq - hi.astype(F32)).astype(BF16)
    w = bd_ref.shape[0]
    parts = [jnp.dot(hi[:, c:c + w], bd_ref[...], preferred_element_type=F32)
             + jnp.dot(lo[:, c:c + w], bd_ref[...], preferred_element_type=F32)
             for c in range(0, t.shape[1], w)]
    return jnp.concatenate(parts, axis=1)


def _inproj_kernel(x_ref, g_ref, w_ref, qg_ref, kg_ref, bd_ref, cw_ref, cb_ref, lng_ref, lnb_ref,
                   conv_ref, q_ref, k_ref, v_ref, gate_ref, buf_ref, halo_ref, *, tiles_per_seq):
    i = pl.program_id(0)
    tm = x_ref.shape[0]
    x = x_ref[...]
    ms = jnp.mean(x * x, axis=-1, keepdims=True)
    h = (x * lax.rsqrt(ms + EPS) * g_ref[...]).astype(BF16)

    def proj(lo, width):
        return jnp.dot(h, w_ref[:, lo:lo + width], preferred_element_type=F32)

    a = proj(0, CONV_CH)
    gt = proj(CONV_CH, CONV_CH)
    glu = a * _sigmoid(gt)

    rows = HALO + tm
    buf_ref[0, 0:HALO, :] = jnp.where(i % tiles_per_seq == 0, 0.0, halo_ref[...])
    buf_ref[0, HALO:rows, :] = glu
    halo_ref[...] = glu[tm - HALO:tm, :]
    for r in range(1, SUBLANES):
        buf_ref[r, 0:rows - SUBLANES, :] = buf_ref[0, r:r + rows - SUBLANES, :]
    first = HALO - (CONV_WIDTH - 1)
    for c in range(tm // CONV_CHUNK):
        r0 = c * CONV_CHUNK
        acc = jnp.broadcast_to(cb_ref[...], (CONV_CHUNK, CONV_CH))
        for j in range(CONV_WIDTH):
            shift = (first + j) % SUBLANES
            lo = r0 + first + j - shift
            acc = acc + cw_ref[j:j + 1, :] * buf_ref[shift, lo:lo + CONV_CHUNK, :]
        mu = jnp.mean(acc, axis=-1, keepdims=True)
        d = acc - mu
        var = jnp.mean(d * d, axis=-1, keepdims=True)
        y = d * lax.rsqrt(var + EPS) * lng_ref[...] + lnb_ref[...]
        conv_ref[r0:r0 + CONV_CHUNK, :] = (y * _sigmoid(y)).astype(BF16)

    q = proj(OFF_Q, COL_Q)
    qn = q * lax.rsqrt(_group_mean_sq(q, bd_ref) + EPS) * qg_ref[...]
    q_ref[...] = (qn * (DA_HEAD_DIM ** -0.5)).astype(BF16)

    k = proj(OFF_K, COL_K)
    kn = k * lax.rsqrt(_group_mean_sq(k, bd_ref) + EPS) * kg_ref[...]
    k_ref[...] = kn.astype(BF16)

    v_ref[...] = proj(OFF_V, COL_V).astype(BF16)

    for c in range(COL_GATE // 512):
        gate_ref[:, c * 512:(c + 1) * 512] = _sigmoid(proj(OFF_GATE + c * 512, 512)).astype(BF16)


def _in_projection(x2, g, w_bf, qg, kg, bd, dw_w, dw_b, ln_g, ln_b, seq_len):
    T = x2.shape[0]
    tm = TM_PROJ
    row = lambda i: (i, 0)
    out_shape = (
        jax.ShapeDtypeStruct((T, CONV_CH), BF16),
        jax.ShapeDtypeStruct((T, COL_Q), BF16),
        jax.ShapeDtypeStruct((T, COL_K), BF16),
        jax.ShapeDtypeStruct((T, COL_V), BF16),
        jax.ShapeDtypeStruct((T, COL_GATE), BF16),
    )
    return pl.pallas_call(
        functools.partial(_inproj_kernel, tiles_per_seq=seq_len // tm),
        grid=(T // tm,),
        in_specs=[
            pl.BlockSpec((tm, D_MODEL), row),
            _const_spec((1, D_MODEL)),
            _const_spec((D_MODEL, IN_COLS)),
            _const_spec((1, COL_Q)),
            _const_spec((1, COL_K)),
            _const_spec((NORM_BLOCK, NORM_BLOCK)),
            _const_spec((CONV_WIDTH, CONV_CH)),
            _const_spec((1, CONV_CH)),
            _const_spec((1, CONV_CH)),
            _const_spec((1, CONV_CH)),
        ],
        out_specs=(
            pl.BlockSpec((tm, CONV_CH), row),
            pl.BlockSpec((tm, COL_Q), row),
            pl.BlockSpec((tm, COL_K), row),
            pl.BlockSpec((tm, COL_V), row),
            pl.BlockSpec((tm, COL_GATE), row),
        ),
        out_shape=out_shape,
        scratch_shapes=[pltpu.VMEM((SUBLANES, HALO + tm, CONV_CH), F32), pltpu.VMEM((HALO, CONV_CH), F32)],
        compiler_params=pltpu.CompilerParams(
            dimension_semantics=("arbitrary",), vmem_limit_bytes=VMEM_LIMIT),
        name="in_projection",
    )(x2, g, w_bf, qg, kg, bd, dw_w, dw_b, ln_g, ln_b)


def _attn_kernel(q_ref, k_ref, v_ref, lam_ref, sg_ref, o_ref, qs_ref, m_ref, acc_ref, s_ref, *, tq, tk):
    qi = pl.program_id(2)
    lane = lax.broadcasted_iota(jnp.int32, (tq, DA_V_DIM), 1)
    for hh in range(ATT_HEADS):
        q = q_ref[0, :, hh * DA_V_DIM:(hh + 1) * DA_V_DIM]
        zero = jnp.zeros_like(q)
        qs_ref[hh, 0:tq, :] = jnp.where(lane < DA_HEAD_DIM, q, zero)
        qs_ref[hh, tq:2 * tq, :] = jnp.where(lane >= DA_HEAD_DIM, q, zero)
    m_ref[...] = jnp.full(m_ref.shape, -jnp.inf, F32)
    acc_ref[...] = jnp.zeros(acc_ref.shape, F32)
    ones = jnp.ones((tk, LANES), BF16)
    rc = ATT_RC
    chunks = [slice(r0, r0 + rc) for r0 in range(0, 2 * tq, rc)]
    n_diag = tq // tk

    def scores(hh, j, rows):
        kj = k_ref[0, pl.ds(pl.multiple_of(j * tk, tk), tk), hh * DA_V_DIM:(hh + 1) * DA_V_DIM]
        return lax.dot_general(qs_ref[hh, rows, :], kj, (((1,), (1,)), ((), ())),
                               preferred_element_type=F32)

    def step(j, diag, prefetch):
        for hh in range(ATT_HEADS):
            vj = jnp.concatenate(
                [v_ref[0, pl.ds(pl.multiple_of(j * tk, tk), tk), hh * DA_V_DIM:(hh + 1) * DA_V_DIM], ones], axis=1)
            for rows in chunks:
                r_lo = rows.start % tq
                if diag is None:
                    kw, masked = tk, False
                else:
                    k_lo = diag * tk
                    if k_lo > r_lo + rc - 1:
                        if prefetch:
                            s_ref[hh, rows, :] = scores(hh, j + 1, rows)
                        continue
                    kw = min(tk, -(-(r_lo + rc - k_lo) // (2 * LANES)) * (2 * LANES))
                    masked = k_lo + kw - 1 > r_lo
                s = s_ref[hh, rows, 0:kw]
                if prefetch:
                    s_ref[hh, rows, :] = scores(hh, j + 1, rows)
                if masked:
                    r = lax.broadcasted_iota(jnp.int32, s.shape, 0) + r_lo
                    c = lax.broadcasted_iota(jnp.int32, s.shape, 1) + k_lo
                    s = jnp.where(c <= r, s, -jnp.inf)
                m_old = m_ref[hh, rows, :]
                m_new = jnp.maximum(m_old, jnp.max(s, axis=-1, keepdims=True))
                alpha = jnp.exp(m_old - m_new)
                p = jnp.exp(s - jnp.tile(m_new, (1, kw // LANES)))
                pv = jnp.dot(p.astype(BF16), vj[0:kw, :], preferred_element_type=F32)
                acc_ref[hh, rows, :] = jnp.tile(alpha, (1, 2)) * acc_ref[hh, rows, :] + pv
                m_ref[hh, rows, :] = m_new

    for hh in range(ATT_HEADS):
        for rows in chunks:
            s_ref[hh, rows, :] = scores(hh, 0, rows)

    def body(j, carry):
        step(j, None, True)
        return carry

    n_full = qi * n_diag
    lax.fori_loop(0, n_full, body, 0)
    for t in range(n_diag):
        step(n_full + t, t, t + 1 < n_diag)

    lam = (jnp.exp(jnp.sum(lam_ref[0:1, :] * lam_ref[1:2, :], axis=-1, keepdims=True))
           - jnp.exp(jnp.sum(lam_ref[2:3, :] * lam_ref[3:4, :], axis=-1, keepdims=True))
           + LAM_INIT)
    for hh in range(ATT_HEADS):
        o1 = acc_ref[hh, 0:tq, 0:LANES] / acc_ref[hh, 0:tq, LANES:2 * LANES]
        o2 = acc_ref[hh, tq:2 * tq, 0:LANES] / acc_ref[hh, tq:2 * tq, LANES:2 * LANES]
        o = o1 - lam * o2
        ms = jnp.mean(o * o, axis=-1, keepdims=True)
        on = o * lax.rsqrt(ms + EPS) * sg_ref[...]
        o_ref[0, :, hh * DA_V_DIM:(hh + 1) * DA_V_DIM] = (on * (1.0 - LAM_INIT)).astype(BF16)


def _diff_attention(q3, k3, v3, lam_vecs, subln_g):
    B, S, _ = q3.shape
    tq, tk = ATT_TQ, ATT_TK
    w = ATT_HEADS * DA_V_DIM
    return pl.pallas_call(
        functools.partial(_attn_kernel, tq=tq, tk=tk),
        grid=(B, DA_HEADS // ATT_HEADS, S // tq),
        in_specs=[
            pl.BlockSpec((1, tq, w), lambda b, h, i: (b, i, h)),
            pl.BlockSpec((1, S, w), lambda b, h, i: (b, 0, h)),
            pl.BlockSpec((1, S, w), lambda b, h, i: (b, 0, h)),
            _const_spec((4, DA_HEAD_DIM)),
            _const_spec((1, DA_V_DIM)),
        ],
        out_specs=pl.BlockSpec((1, tq, w), lambda b, h, i: (b, i, h)),
        out_shape=jax.ShapeDtypeStruct((B, S, ATTN_W), BF16),
        scratch_shapes=[
            pltpu.VMEM((ATT_HEADS, 2 * tq, DA_V_DIM), BF16),
            pltpu.VMEM((ATT_HEADS, 2 * tq, LANES), F32),
            pltpu.VMEM((ATT_HEADS, 2 * tq, 2 * LANES), F32),
            pltpu.VMEM((ATT_HEADS, 2 * tq, tk), F32),
        ],
        compiler_params=pltpu.CompilerParams(
            dimension_semantics=("parallel", "parallel", "parallel"),
            vmem_limit_bytes=VMEM_LIMIT),
        name="diff_attention",
    )(q3, k3, v3, lam_vecs, subln_g)


def _merge_kernel(conv_ref, attn_ref, gate_ref, x_ref, wc_ref, wa_ref, wo_ref, fg_ref,
                  wr_ref, br_ref, x1_ref, h2_ref, logit_ref):
    c = jnp.dot(conv_ref[...], wc_ref[...], preferred_element_type=F32)
    a = jnp.dot(attn_ref[...], wa_ref[...], preferred_element_type=F32)
    g0 = gate_ref[:, 0:D_MODEL].astype(F32)
    g1 = gate_ref[:, D_MODEL:2 * D_MODEL].astype(F32)
    merged = (g0 * c + g1 * a).astype(BF16)
    x1 = x_ref[...] + jnp.dot(merged, wo_ref[...], preferred_element_type=F32)
    x1_ref[...] = x1
    ms = jnp.mean(x1 * x1, axis=-1, keepdims=True)
    h2 = x1 * lax.rsqrt(ms + EPS) * fg_ref[...]
    _store_rows(h2_ref, (), h2)
    h2_hi = h2.astype(BF16)
    h2_lo = (h2 - h2_hi.astype(F32)).astype(BF16)
    r = (jnp.dot(h2_hi, wr_ref[...], preferred_element_type=F32)
         + jnp.dot(h2_lo, wr_ref[...], preferred_element_type=F32))
    logit_ref[...] = r[:, 0:ROUTER_LANES] + r[:, ROUTER_LANES:2 * ROUTER_LANES] + br_ref[...]


def _merge(conv_act, attn_o, gates, x2, wc, wa, wo, fg, wr, br):
    T = x2.shape[0]
    tm = TM_MERGE
    row = lambda i: (i, 0)
    return pl.pallas_call(
        _merge_kernel,
        grid=(T // tm,),
        in_specs=[
            pl.BlockSpec((tm, CONV_CH), row),
            pl.BlockSpec((tm, ATTN_W), row),
            pl.BlockSpec((tm, COL_GATE), row),
            pl.BlockSpec((tm, D_MODEL), row),
            _const_spec((CONV_CH, D_MODEL)),
            _const_spec((ATTN_W, D_MODEL)),
            _const_spec((D_MODEL, D_MODEL)),
            _const_spec((1, D_MODEL)),
            _const_spec((D_MODEL, 2 * ROUTER_LANES)),
            _const_spec((1, ROUTER_LANES)),
        ],
        out_specs=(
            pl.BlockSpec((tm, D_MODEL), row),
            pl.BlockSpec((tm * ROW_TILE, LANES), row),
            pl.BlockSpec((tm, ROUTER_LANES), row),
        ),
        out_shape=(
            jax.ShapeDtypeStruct((T, D_MODEL), F32),
            jax.ShapeDtypeStruct((T * ROW_TILE, LANES), jnp.uint32),
            jax.ShapeDtypeStruct((T, ROUTER_LANES), F32),
        ),
        compiler_params=pltpu.CompilerParams(
            dimension_semantics=("parallel",), vmem_limit_bytes=VMEM_LIMIT),
        name="merge_out_router",
    )(conv_act, attn_o, gates, x2, wc, wa, wo, fg, wr, br)


def _route_kernel(logit_ref, tri_ref, low_ref, dest_ref, wts_ref, blk_ref,
                  cnt_ref, base_ref, *, n_blk_lanes):
    ph = pl.program_id(0)
    i = pl.program_id(1)
    tm = logit_ref.shape[0]
    lt = logit_ref[...].T
    row8 = lax.broadcasted_iota(jnp.int32, (SUBLANES, tm), 0)

    g = jnp.where(row8 < N_GROUPS, lt[0:SUBLANES, :], -jnp.inf)
    g_max = jnp.max(g, axis=0, keepdims=True)
    g_sum = jnp.sum(jnp.exp(g - g_max), axis=0, keepdims=True)
    g_sel = jnp.min(jnp.where(g == g_max, row8, SUBLANES), axis=0, keepdims=True)
    g_w = 1.0 / g_sum

    e_sel = jnp.zeros((EXPERTS_PER_GROUP, tm), F32)
    for gi in range(N_GROUPS):
        lo = ROUTER_EXPERT_COL + gi * EXPERTS_PER_GROUP
        e_sel = jnp.where(g_sel == gi, lt[lo:lo + EXPERTS_PER_GROUP, :], e_sel)
    e_max = jnp.max(e_sel, axis=0, keepdims=True)
    e_exp = jnp.exp(e_sel - e_max)
    prob = e_exp / jnp.sum(e_exp, axis=0, keepdims=True)
    p1 = jnp.max(prob, axis=0, keepdims=True)
    i1 = jnp.min(jnp.where(prob == p1, row8, SUBLANES), axis=0, keepdims=True)
    rest = jnp.where(row8 == i1, -1.0, prob)
    p2 = jnp.max(rest, axis=0, keepdims=True)
    i2 = jnp.min(jnp.where(rest == p2, row8, SUBLANES), axis=0, keepdims=True)
    denom = p1 + p2
    e0 = g_sel * EXPERTS_PER_GROUP + i1
    e1 = g_sel * EXPERTS_PER_GROUP + i2

    row32 = lax.broadcasted_iota(jnp.int32, (N_EXPERTS, tm), 0)
    oh0 = row32 == e0
    oh1 = row32 == e1
    ohs = jnp.where(oh0 | oh1, 1.0, 0.0)
    tile_cnt = jnp.sum(ohs, axis=1, keepdims=True)

    @pl.when((ph == 0) & (i == 0))
    def _():
        cnt_ref[...] = jnp.zeros(cnt_ref.shape, F32)

    @pl.when(ph == 0)
    def _():
        cnt_ref[...] += tile_cnt

    @pl.when((ph == 1) & (i == 0))
    def _():
        cnt = cnt_ref[...]
        padded = jnp.ceil(cnt * (1.0 / RB)) * RB
        pstart = jnp.dot(low_ref[...], padded, preferred_element_type=F32,
                         precision=lax.Precision.HIGHEST)
        base_ref[...] = pstart
        pad_end = pstart + padded
        row0 = lax.broadcasted_iota(jnp.int32, (N_EXPERTS, n_blk_lanes), 1).astype(F32) * RB
        expert = lax.broadcasted_iota(jnp.int32, (N_EXPERTS, n_blk_lanes), 0).astype(F32)
        blk = jnp.minimum(jnp.sum(jnp.where(pad_end[:, 0:1] <= row0, 1.0, 0.0), axis=0, keepdims=True),
                          N_EXPERTS - 1.0)
        valid_end = jnp.sum(jnp.where(expert == blk, (pstart + cnt)[:, 0:1], 0.0), axis=0, keepdims=True)
        n_valid = jnp.clip(valid_end - row0[0:1, :], 0.0, float(RB))
        n_used = jnp.max(pad_end[:, 0:1], axis=0, keepdims=True) * (1.0 / RB) + jnp.zeros_like(blk)
        blk_ref[...] = jnp.zeros(blk_ref.shape, jnp.int32)
        blk_ref[0:1, :] = blk.astype(jnp.int32)
        blk_ref[1:2, :] = n_valid.astype(jnp.int32)
        blk_ref[2:3, :] = n_used.astype(jnp.int32)

    @pl.when(ph == 1)
    def _():
        before = jnp.dot(ohs.astype(BF16), tri_ref[...], preferred_element_type=F32)
        pos = base_ref[:, 0:1] + before
        d0 = jnp.sum(jnp.where(oh0, pos, 0.0), axis=0, keepdims=True)
        d1 = jnp.sum(jnp.where(oh1, pos, 0.0), axis=0, keepdims=True)
        dest_ref[0:1, :] = d0.astype(jnp.int32)
        dest_ref[1:2, :] = d1.astype(jnp.int32)
        wts_ref[...] = jnp.zeros(wts_ref.shape, F32)
        wts_ref[0:1, :] = p1 / denom * g_w
        wts_ref[1:2, :] = p2 / denom * g_w
        base_ref[...] += tile_cnt


def _route(logits):
    T = logits.shape[0]
    tm = TM_ROUTE
    n_rows = 2 * T + N_EXPERTS * RB
    n_blk = n_rows // RB
    n_blk_lanes = -(-n_blk // LANES) * LANES
    r = jnp.arange(tm, dtype=jnp.int32)
    tri = (r[:, None] < r[None, :]).astype(BF16)
    e = jnp.arange(N_EXPERTS, dtype=jnp.int32)
    low = (e[None, :] < e[:, None]).astype(F32)
    return pl.pallas_call(
        functools.partial(_route_kernel, n_blk_lanes=n_blk_lanes),
        grid=(2, T // tm),
        in_specs=[
            pl.BlockSpec((tm, ROUTER_LANES), lambda ph, i: (i, 0)),
            _const_spec((tm, tm)),
            _const_spec((N_EXPERTS, N_EXPERTS)),
        ],
        out_specs=(
            pl.BlockSpec((2, tm), lambda ph, i: (0, i * ph)),
            pl.BlockSpec((SUBLANES, tm), lambda ph, i: (0, i * ph)),
            pl.BlockSpec((SUBLANES, n_blk_lanes), lambda ph, i: (0, 0)),
        ),
        out_shape=(
            jax.ShapeDtypeStruct((2, T), jnp.int32),
            jax.ShapeDtypeStruct((SUBLANES, T), F32),
            jax.ShapeDtypeStruct((SUBLANES, n_blk_lanes), jnp.int32),
        ),
        scratch_shapes=[pltpu.VMEM((N_EXPERTS, LANES), F32), pltpu.VMEM((N_EXPERTS, LANES), F32)],
        compiler_params=pltpu.CompilerParams(
            dimension_semantics=("arbitrary", "arbitrary"), vmem_limit_bytes=VMEM_LIMIT),
        name="route_plan",
    )(logits, tri, low)


def _sc_worker():
    return lax.axis_index("s") * SC_CORES + lax.axis_index("c")


def _sc_scatter_rows(h, dest, n_rows):
    T = h.shape[0]
    n_workers = SC_CORES * SC_SUBCORES
    per_w = T // n_workers
    n_ch = per_w // SC_ROWS
    assert per_w * n_workers == T and n_ch * SC_ROWS == per_w and n_ch % 2 == 0
    mesh = plsc.VectorSubcoreMesh(core_axis_name="c", subcore_axis_name="s")

    @functools.partial(
        pl.kernel, mesh=mesh,
        out_type=jax.ShapeDtypeStruct((n_rows,) + h.shape[1:], h.dtype),
        scratch_types=[
            pltpu.VMEM((2, n_ch, SC_ROWS), jnp.int32),
            pltpu.VMEM((2, SC_ROWS) + h.shape[1:], h.dtype),
            pltpu.SemaphoreType.DMA((2,)),
            pltpu.SemaphoreType.DMA((2,)),
        ],
    )
    def k(h_hbm, idx_hbm, out_hbm, idx_v, rows_v, lsem, ssem):
        wid = _sc_worker()
        base = wid * per_w
        pltpu.sync_copy(idx_hbm.at[0, wid], idx_v.at[0])
        pltpu.sync_copy(idx_hbm.at[1, wid], idx_v.at[1])

        def load(c, slot):
            return pltpu.make_async_copy(h_hbm.at[pl.ds(base + c * SC_ROWS, SC_ROWS)], rows_v.at[slot],
                                         lsem.at[slot])

        def scatter(slot_k, c, slot):
            return pltpu.make_async_copy(rows_v.at[slot], out_hbm.at[idx_v.at[slot_k, c]], ssem.at[slot])

        load(0, 0).start()

        def body(g, carry):
            for slot in range(2):
                c = 2 * g + slot
                load(c, slot).wait()

                @pl.when(c >= 1)
                def _():
                    scatter(0, c - 1, 1 - slot).wait()
                    scatter(1, c - 1, 1 - slot).wait()

                @pl.when(c + 1 < n_ch)
                def _():
                    load(c + 1, 1 - slot).start()

                scatter(0, c, slot).start()
                scatter(1, c, slot).start()
            return carry

        lax.fori_loop(0, n_ch // 2, body, 0)
        scatter(0, n_ch - 1, 1).wait()
        scatter(1, n_ch - 1, 1).wait()

    return k(h, dest.reshape(2, n_workers, n_ch, SC_ROWS))


def _sc_gather_rows(table, idx):
    B = idx.shape[0]
    n_workers = SC_CORES * SC_SUBCORES
    per_w = B // n_workers
    n_ch = per_w // SC_ROWS
    assert per_w * n_workers == B and n_ch * SC_ROWS == per_w and n_ch % 2 == 0
    mesh = plsc.VectorSubcoreMesh(core_axis_name="c", subcore_axis_name="s")

    @functools.partial(
        pl.kernel, mesh=mesh,
        out_type=jax.ShapeDtypeStruct((B,) + table.shape[1:], table.dtype),
        scratch_types=[
            pltpu.VMEM((n_ch, SC_ROWS), jnp.int32),
            pltpu.VMEM((2, SC_ROWS) + table.shape[1:], table.dtype),
            pltpu.SemaphoreType.DMA((2,)),
            pltpu.SemaphoreType.DMA((2,)),
        ],
    )
    def k(table_hbm, idx_hbm, out_hbm, idx_v, rows_v, gsem, wsem):
        wid = _sc_worker()
        base = wid * per_w
        pltpu.sync_copy(idx_hbm.at[wid], idx_v)

        def gather(c, slot):
            return pltpu.make_async_copy(table_hbm.at[idx_v.at[c]], rows_v.at[slot], gsem.at[slot])

        def writeback(c, slot):
            return pltpu.make_async_copy(rows_v.at[slot], out_hbm.at[pl.ds(base + c * SC_ROWS, SC_ROWS)],
                                         wsem.at[slot])

        gather(0, 0).start()

        def body(g, carry):
            for slot in range(2):
                c = 2 * g + slot
                gather(c, slot).wait()

                @pl.when(c >= 1)
                def _():
                    writeback(c - 1, 1 - slot).wait()

                @pl.when(c + 1 < n_ch)
                def _():
                    gather(c + 1, 1 - slot).start()

                writeback(c, slot).start()
            return carry

        lax.fori_loop(0, n_ch // 2, body, 0)
        writeback(n_ch - 1, 1).wait()

    return k(table, idx.reshape(n_workers, n_ch, SC_ROWS))


def _expert_kernel(blk_exp_ref, n_used_ref, n_valid_ref, xs_ref, wg_ref, wu_ref, wd_ref, y_ref,
                   wg_bf, wu_bf, wd_bf):
    b = pl.program_id(0)
    used = b < n_used_ref[0]
    new_expert = (b == 0) | (blk_exp_ref[b] != blk_exp_ref[jnp.maximum(b - 1, 0)])

    @pl.when(used & new_expert)
    def _():
        wg_bf[...] = wg_ref[0].astype(BF16)
        wu_bf[...] = wu_ref[0].astype(BF16)
        wd_bf[...] = wd_ref[0].astype(BF16)

    @pl.when(used)
    def _():
        x = _load_rows(xs_ref, (), RB)
        row = lax.broadcasted_iota(jnp.int32, (RB, 1), 0)
        xb = jnp.where(row < n_valid_ref[b], x, jnp.zeros_like(x))
        g = jnp.dot(xb, wg_bf[...], preferred_element_type=F32)
        u = jnp.dot(xb, wu_bf[...], preferred_element_type=F32)
        hid = (g * _sigmoid(g) * u).astype(BF16)
        _store_rows(y_ref, (), jnp.dot(hid, wd_bf[...], preferred_element_type=F32))

    @pl.when(jnp.logical_not(used))
    def _():
        y_ref[...] = jnp.zeros(y_ref.shape, y_ref.dtype)


def _experts(blk_exp, n_used, n_valid, xs, wg, wu, wd):
    n_blk = blk_exp.shape[0]
    grid_spec = pltpu.PrefetchScalarGridSpec(
        num_scalar_prefetch=3,
        grid=(n_blk,),
        in_specs=[
            pl.BlockSpec((RB * ROW_TILE, LANES), lambda b, be, nu, nv: (b, 0)),
            pl.BlockSpec((1, D_MODEL, D_EXPERT), lambda b, be, nu, nv: (be[b], 0, 0)),
            pl.BlockSpec((1, D_MODEL, D_EXPERT), lambda b, be, nu, nv: (be[b], 0, 0)),
            pl.BlockSpec((1, D_EXPERT, D_MODEL), lambda b, be, nu, nv: (be[b], 0, 0)),
        ],
        out_specs=pl.BlockSpec((RB * ROW_TILE, LANES), lambda b, be, nu, nv: (b, 0)),
        scratch_shapes=[
            pltpu.VMEM((D_MODEL, D_EXPERT), BF16),
            pltpu.VMEM((D_MODEL, D_EXPERT), BF16),
            pltpu.VMEM((D_EXPERT, D_MODEL), BF16),
        ],
    )
    return pl.pallas_call(
        _expert_kernel,
        grid_spec=grid_spec,
        out_shape=jax.ShapeDtypeStruct(xs.shape, xs.dtype),
        compiler_params=pltpu.CompilerParams(
            dimension_semantics=("arbitrary",), vmem_limit_bytes=VMEM_LIMIT),
        name="expert_mlp",
    )(blk_exp, n_used, n_valid, xs, wg, wu, wd)


def _combine_kernel(x1_ref, y0_ref, y1_ref, w_ref, o_ref):
    tm = x1_ref.shape[0]
    y0 = _load_rows(y0_ref, (), tm).astype(F32)
    y1 = _load_rows(y1_ref, (), tm).astype(F32)
    w = w_ref[...].T
    o_ref[...] = x1_ref[...] + (y0 * w[:, 0:1] + y1 * w[:, 1:2])


def _combine(x1, y, wts, part, prev):
    T = x1.shape[0]
    tm = TM_COMB
    steps = T // COMBINE_PARTS // tm
    tok = lambda i: (i + part * steps, 0)
    in_specs = [
        pl.BlockSpec((tm, D_MODEL), tok),
        pl.BlockSpec((tm * ROW_TILE, LANES), lambda i: (i, 0)),
        pl.BlockSpec((tm * ROW_TILE, LANES), lambda i: (i + steps, 0)),
        pl.BlockSpec((SUBLANES, tm), lambda i: (0, i + part * steps)),
    ]
    args = [x1, y, y, wts]
    kern = _combine_kernel
    aliases = {}
    if prev is not None:
        in_specs.append(pl.BlockSpec(memory_space=pl.ANY))
        args.append(prev)
        kern = lambda x1_ref, y0_ref, y1_ref, w_ref, prev_ref, o_ref: _combine_kernel(
            x1_ref, y0_ref, y1_ref, w_ref, o_ref)
        aliases = {4: 0}
    return pl.pallas_call(
        kern,
        grid=(steps,),
        in_specs=in_specs,
        out_specs=pl.BlockSpec((tm, D_MODEL), tok),
        out_shape=jax.ShapeDtypeStruct((T, D_MODEL), F32),
        input_output_aliases=aliases,
        compiler_params=pltpu.CompilerParams(
            dimension_semantics=("parallel",), vmem_limit_bytes=VMEM_LIMIT),
        name="moe_combine",
    )(*args)


def kernel(x, attn_norm_g, w_in, conv_dw_w, conv_dw_b, conv_ln_g, conv_ln_b, w_conv_out,
           q_norm_g, k_norm_g, lambda_q1, lambda_k1, lambda_q2, lambda_k2, subln_g,
           w_attn_out, w_out, ffn_norm_g, w_router_group, b_router_group,
           w_router_expert, b_router_expert, w_gate_e, w_up_e, w_down_e):
    B, S, D = x.shape
    T = B * S
    l = 0
    x2 = x.reshape(T, D)

    reps = COL_Q // DA_HEAD_DIM
    qg = jnp.tile(q_norm_g[l], reps).reshape(1, COL_Q)
    kg = jnp.tile(k_norm_g[l], reps).reshape(1, COL_K)
    grp = jnp.arange(NORM_BLOCK, dtype=jnp.int32) // DA_HEAD_DIM
    bd = jnp.where(grp[:, None] == grp[None, :], 1.0 / DA_HEAD_DIM, 0.0).astype(BF16)

    conv_act, q, k, v, gates = _in_projection(
        x2, attn_norm_g[l].reshape(1, D), w_in[l].astype(BF16), qg, kg, bd, conv_dw_w[l],
        conv_dw_b[l].reshape(1, CONV_CH), conv_ln_g[l].reshape(1, CONV_CH), conv_ln_b[l].reshape(1, CONV_CH), S)

    lam_vecs = jnp.stack([lambda_q1[l], lambda_k1[l], lambda_q2[l], lambda_k2[l]]).astype(F32)
    attn_o = _diff_attention(q.reshape(B, S, COL_Q), k.reshape(B, S, COL_K),
                             v.reshape(B, S, COL_V), lam_vecs, subln_g[l].reshape(1, DA_V_DIM))

    e_lo, e_hi = ROUTER_EXPERT_COL, ROUTER_EXPERT_COL + N_EXPERTS
    wr = jnp.zeros((D, ROUTER_LANES), F32)
    wr = wr.at[:, :N_GROUPS].set(w_router_group[l]).at[:, e_lo:e_hi].set(w_router_expert[l])
    br = jnp.zeros((1, ROUTER_LANES), F32)
    br = br.at[0, :N_GROUPS].set(b_router_group[l]).at[0, e_lo:e_hi].set(b_router_expert[l])

    wr_hi = wr.astype(BF16)
    wr_lo = (wr - wr_hi.astype(F32)).astype(BF16)
    x1, h2, logits = _merge(conv_act, attn_o.reshape(T, ATTN_W), gates, x2,
                            w_conv_out[l].astype(BF16), w_attn_out[l].astype(BF16),
                            w_out[l].astype(BF16), ffn_norm_g[l].reshape(1, D),
                            jnp.concatenate([wr_hi, wr_lo], axis=1), br)

    dest, wts, blk = _route(logits)
    n_rows = 2 * T + N_EXPERTS * RB
    n_blk = n_rows // RB
    blk_exp, n_valid, n_used = blk[0, :n_blk], blk[1, :n_blk], blk[2, 0:1]

    xs = _sc_scatter_rows(h2.reshape(T, ROW_TILE, LANES), dest, n_rows)
    yb = _experts(blk_exp, n_used, n_valid, xs.reshape(n_rows * ROW_TILE, LANES),
                  w_gate_e[l], w_up_e[l], w_down_e[l])
    yb3 = yb.reshape(n_rows, ROW_TILE, LANES)
    part = T // COMBINE_PARTS
    out = None
    for p in range(COMBINE_PARTS):
        y = _sc_gather_rows(yb3, dest[:, p * part:(p + 1) * part].reshape(2 * part))
        out = _combine(x1, y.reshape(2 * part * ROW_TILE, LANES), wts, p, out)
    return out.reshape(B, S, D)
```

```python
import functools
import math

import jax
import jax.numpy as jnp
from jax import lax
from jax.experimental import pallas as pl
from jax.experimental.pallas import tpu as pltpu
from jax.experimental.pallas import tpu_sc as plsc

F32 = jnp.float32
BF16 = jnp.bfloat16

D_MODEL = 1024
CONV_CH = 512
CONV_WIDTH = 31
DA_HEADS = 4
DA_HEAD_DIM = 64
DA_V_DIM = 128
ATTN_W = 512
N_GROUPS = 4
EXPERTS_PER_GROUP = 8
N_EXPERTS = 32
D_EXPERT = 512
EPS = 1e-6
LAM_INIT = 0.8 - 0.6 * math.exp(-0.3 * 0)

COL_GLU = 2 * CONV_CH
COL_Q = 512
COL_K = 512
COL_V = 512
COL_GATE = 2 * D_MODEL
OFF_Q = COL_GLU
OFF_K = OFF_Q + COL_Q
OFF_V = OFF_K + COL_K
OFF_GATE = OFF_V + COL_V
IN_COLS = OFF_GATE + COL_GATE

ROUTER_LANES = 128
ROUTER_EXPERT_COL = 8
SUBLANES = 8
ROW_TILE = D_MODEL // (2 * 128)
NORM_BLOCK = 256
HALO = 32

TM_PROJ = 512
TM_MERGE = 1024
CONV_CHUNK = 64
ATT_TQ = 1024
ATT_TK = 1024
ATT_RC = 256
ATT_HEADS = 2
LANES = 128
RB = 512
TM_COMB = 1024
TM_ROUTE = 1024
SC_CORES = 2
SC_SUBCORES = 16
SC_ROWS = 64

VMEM_LIMIT = 48 * 1024 * 1024


def _sigmoid(x):
    return 1.0 / (1.0 + jnp.exp(-x))


def _const_spec(shape):
    nd = len(shape)
    return pl.BlockSpec(shape, lambda *_: (0,) * nd, pipeline_mode=pl.Buffered(1))


def _store_rows(ref, lead, val):
    rows, half = val.shape[0], val.shape[1] // 2
    hi = pltpu.bitcast(val[:, :half].astype(BF16).astype(F32), jnp.uint32)
    lo = pltpu.bitcast(val[:, half:].astype(BF16).astype(F32), jnp.uint32)
    words = hi | (lo >> 16)
    for j in range(half // LANES):
        ref[(*lead, pl.ds(j, rows, stride=ROW_TILE), slice(None))] = words[:, j * LANES:(j + 1) * LANES]


def _load_rows(ref, lead, rows):
    words = jnp.concatenate(
        [ref[(*lead, pl.ds(j, rows, stride=ROW_TILE), slice(None))] for j in range(ROW_TILE)], axis=1)
    hi = pltpu.bitcast(words & jnp.uint32(0xFFFF0000), F32).astype(BF16)
    lo = pltpu.bitcast(words << 16, F32).astype(BF16)
    return jnp.concatenate([hi, lo], axis=1)


def _group_mean_sq(t, bd_ref):
    sq = t * t
    hi = sq.astype(BF16)
    lo = (sq - hi.astype(F32)).astype(BF16)
    w = bd_ref.shape[0]
    parts = [jnp.dot(hi[:, c:c + w], bd_ref[...], preferred_element_type=F32)
             + jnp.dot(lo[:, c:c + w], bd_ref[...], preferred_element_type=F32)
             for c in range(0, t.shape[1], w)]
    return jnp.concatenate(parts, axis=1)


def _inproj_kernel(x_ref, g_ref, w_ref, qg_ref, kg_ref, bd_ref, cw_ref, cb_ref, lng_ref, lnb_ref,
                   conv_ref, q_ref, k_ref, v_ref, gate_ref, buf_ref, halo_ref, *, tiles_per_seq):
    i = pl.program_id(0)
    tm = x_ref.shape[0]
    x = x_ref[...]
    ms = jnp.mean(x * x, axis=-1, keepdims=True)
    h = (x * lax.rsqrt(ms + EPS) * g_ref[...]).astype(BF16)

    def proj(lo, width):
        return jnp.dot(h, w_ref[:, lo:lo + width], preferred_element_type=F32)

    a = proj(0, CONV_CH)
    gt = proj(CONV_CH, CONV_CH)
    glu = a * _sigmoid(gt)

    rows = HALO + tm
    buf_ref[0, 0:HALO, :] = jnp.where(i % tiles_per_seq == 0, 0.0, halo_ref[...])
    buf_ref[0, HALO:rows, :] = glu
    halo_ref[...] = glu[tm - HALO:tm, :]
    for r in range(1, SUBLANES):
        buf_ref[r, 0:rows - SUBLANES, :] = buf_ref[0, r:r + rows - SUBLANES, :]
    first = HALO - (CONV_WIDTH - 1)
    for c in range(tm // CONV_CHUNK):
        r0 = c * CONV_CHUNK
        acc = jnp.broadcast_to(cb_ref[...], (CONV_CHUNK, CONV_CH))
        for j in range(CONV_WIDTH):
            shift = (first + j) % SUBLANES
            lo = r0 + first + j - shift
            acc = acc + cw_ref[j:j + 1, :] * buf_ref[shift, lo:lo + CONV_CHUNK, :]
        mu = jnp.mean(acc, axis=-1, keepdims=True)
        d = acc - mu
        var = jnp.mean(d * d, axis=-1, keepdims=True)
        y = d * lax.rsqrt(var + EPS) * lng_ref[...] + lnb_ref[...]
        conv_ref[r0:r0 + CONV_CHUNK, :] = (y * _sigmoid(y)).astype(BF16)

    q = proj(OFF_Q, COL_Q)
    qn = q * lax.rsqrt(_group_mean_sq(q, bd_ref) + EPS) * qg_ref[...]
    q_ref[...] = (qn * (DA_HEAD_DIM ** -0.5)).astype(BF16)

    k = proj(OFF_K, COL_K)
    kn = k * lax.rsqrt(_group_mean_sq(k, bd_ref) + EPS) * kg_ref[...]
    k_ref[...] = kn.astype(BF16)

    v_ref[...] = proj(OFF_V, COL_V).astype(BF16)

    for c in range(COL_GATE // 512):
        gate_ref[:, c * 512:(c + 1) * 512] = _sigmoid(proj(OFF_GATE + c * 512, 512)).astype(BF16)


def _in_projection(x2, g, w_bf, qg, kg, bd, dw_w, dw_b, ln_g, ln_b, seq_len):
    T = x2.shape[0]
    tm = TM_PROJ
    row = lambda i: (i, 0)
    out_shape = (
        jax.ShapeDtypeStruct((T, CONV_CH), BF16),
        jax.ShapeDtypeStruct((T, COL_Q), BF16),
        jax.ShapeDtypeStruct((T, COL_K), BF16),
        jax.ShapeDtypeStruct((T, COL_V), BF16),
        jax.ShapeDtypeStruct((T, COL_GATE), BF16),
    )
    return pl.pallas_call(
        functools.partial(_inproj_kernel, tiles_per_seq=seq_len // tm),
        grid=(T // tm,),
        in_specs=[
            pl.BlockSpec((tm, D_MODEL), row),
            _const_spec((1, D_MODEL)),
            _const_spec((D_MODEL, IN_COLS)),
            _const_spec((1, COL_Q)),
            _const_spec((1, COL_K)),
            _const_spec((NORM_BLOCK, NORM_BLOCK)),
            _const_spec((CONV_WIDTH, CONV_CH)),
            _const_spec((1, CONV_CH)),
            _const_spec((1, CONV_CH)),
            _const_spec((1, CONV_CH)),
        ],
        out_specs=(
            pl.BlockSpec((tm, CONV_CH), row),
            pl.BlockSpec((tm, COL_Q), row),
            pl.BlockSpec((tm, COL_K), row),
            pl.BlockSpec((tm, COL_V), row),
            pl.BlockSpec((tm, COL_GATE), row),
        ),
        out_shape=out_shape,
        scratch_shapes=[pltpu.VMEM((SUBLANES, HALO + tm, CONV_CH), F32), pltpu.VMEM((HALO, CONV_CH), F32)],
        compiler_params=pltpu.CompilerParams(
            dimension_semantics=("arbitrary",), vmem_limit_bytes=VMEM_LIMIT),
        name="in_projection",
    )(x2, g, w_bf, qg, kg, bd, dw_w, dw_b, ln_g, ln_b)


def _attn_kernel(q_ref, k_ref, v_ref, lam_ref, sg_ref, o_ref, qs_ref, m_ref, acc_ref, s_ref, *, tq, tk):
    qi = pl.program_id(2)
    lane = lax.broadcasted_iota(jnp.int32, (tq, DA_V_DIM), 1)
    for hh in range(ATT_HEADS):
        q = q_ref[0, :, hh * DA_V_DIM:(hh + 1) * DA_V_DIM]
        zero = jnp.zeros_like(q)
        qs_ref[hh, 0:tq, :] = jnp.where(lane < DA_HEAD_DIM, q, zero)
        qs_ref[hh, tq:2 * tq, :] = jnp.where(lane >= DA_HEAD_DIM, q, zero)
    m_ref[...] = jnp.full(m_ref.shape, -jnp.inf, F32)
    acc_ref[...] = jnp.zeros(acc_ref.shape, F32)
    ones = jnp.ones((tk, LANES), BF16)
    rc = ATT_RC
    chunks = [slice(r0, r0 + rc) for r0 in range(0, 2 * tq, rc)]
    n_diag = tq // tk

    def scores(hh, j, rows):
        kj = k_ref[0, pl.ds(pl.multiple_of(j * tk, tk), tk), hh * DA_V_DIM:(hh + 1) * DA_V_DIM]
        return lax.dot_general(qs_ref[hh, rows, :], kj, (((1,), (1,)), ((), ())),
                               preferred_element_type=F32)

    def step(j, diag, prefetch):
        for hh in range(ATT_HEADS):
            vj = jnp.concatenate(
                [v_ref[0, pl.ds(pl.multiple_of(j * tk, tk), tk), hh * DA_V_DIM:(hh + 1) * DA_V_DIM], ones], axis=1)
            for rows in chunks:
                r_lo = rows.start % tq
                if diag is None:
                    kw, masked = tk, False
                else:
                    k_lo = diag * tk
                    if k_lo > r_lo + rc - 1:
                        if prefetch:
                            s_ref[hh, rows, :] = scores(hh, j + 1, rows)
                        continue
                    kw = min(tk, -(-(r_lo + rc - k_lo) // (2 * LANES)) * (2 * LANES))
                    masked = k_lo + kw - 1 > r_lo
                s = s_ref[hh, rows, 0:kw]
                if prefetch:
                    s_ref[hh, rows, :] = scores(hh, j + 1, rows)
                if masked:
                    r = lax.broadcasted_iota(jnp.int32, s.shape, 0) + r_lo
                    c = lax.broadcasted_iota(jnp.int32, s.shape, 1) + k_lo
                    s = jnp.where(c <= r, s, -jnp.inf)
                m_old = m_ref[hh, rows, :]
                m_new = jnp.maximum(m_old, jnp.max(s, axis=-1, keepdims=True))
                alpha = jnp.exp(m_old - m_new)
                p = jnp.exp(s - jnp.tile(m_new, (1, kw // LANES)))
                pv = jnp.dot(p.astype(BF16), vj[0:kw, :], preferred_element_type=F32)
                acc_ref[hh, rows, :] = jnp.tile(alpha, (1, 2)) * acc_ref[hh, rows, :] + pv
                m_ref[hh, rows, :] = m_new

    for hh in range(ATT_HEADS):
        for rows in chunks:
            s_ref[hh, rows, :] = scores(hh, 0, rows)

    def body(j, carry):
        step(j, None, True)
        return carry

    n_full = qi * n_diag
    lax.fori_loop(0, n_full, body, 0)
    for t in range(n_diag):
        step(n_full + t, t, t + 1 < n_diag)

    lam = (jnp.exp(jnp.sum(lam_ref[0:1, :] * lam_ref[1:2, :], axis=-1, keepdims=True))
           - jnp.exp(jnp.sum(lam_ref[2:3, :] * lam_ref[3:4, :], axis=-1, keepdims=True))
           + LAM_INIT)
    for hh in range(ATT_HEADS):
        o1 = acc_ref[hh, 0:tq, 0:LANES] / acc_ref[hh, 0:tq, LANES:2 * LANES]
        o2 = acc_ref[hh, tq:2 * tq, 0:LANES] / acc_ref[hh, tq:2 * tq, LANES:2 * LANES]
        o = o1 - lam * o2
        ms = jnp.mean(o * o, axis=-1, keepdims=True)
        on = o * lax.rsqrt(ms + EPS) * sg_ref[...]
        o_ref[0, :, hh * DA_V_DIM:(hh + 1) * DA_V_DIM] = (on * (1.0 - LAM_INIT)).astype(BF16)


def _diff_attention(q3, k3, v3, lam_vecs, subln_g):
    B, S, _ = q3.shape
    tq, tk = ATT_TQ, ATT_TK
    w = ATT_HEADS * DA_V_DIM
    return pl.pallas_call(
        functools.partial(_attn_kernel, tq=tq, tk=tk),
        grid=(B, DA_HEADS // ATT_HEADS, S // tq),
        in_specs=[
            pl.BlockSpec((1, tq, w), lambda b, h, i: (b, i, h)),
            pl.BlockSpec((1, S, w), lambda b, h, i: (b, 0, h)),
            pl.BlockSpec((1, S, w), lambda b, h, i: (b, 0, h)),
            _const_spec((4, DA_HEAD_DIM)),
            _const_spec((1, DA_V_DIM)),
        ],
        out_specs=pl.BlockSpec((1, tq, w), lambda b, h, i: (b, i, h)),
        out_shape=jax.ShapeDtypeStruct((B, S, ATTN_W), BF16),
        scratch_shapes=[
            pltpu.VMEM((ATT_HEADS, 2 * tq, DA_V_DIM), BF16),
            pltpu.VMEM((ATT_HEADS, 2 * tq, LANES), F32),
            pltpu.VMEM((ATT_HEADS, 2 * tq, 2 * LANES), F32),
            pltpu.VMEM((ATT_HEADS, 2 * tq, tk), F32),
        ],
        compiler_params=pltpu.CompilerParams(
            dimension_semantics=("parallel", "parallel", "parallel"),
            vmem_limit_bytes=VMEM_LIMIT),
        name="diff_attention",
    )(q3, k3, v3, lam_vecs, subln_g)


def _merge_kernel(conv_ref, attn_ref, gate_ref, x_ref, wc_ref, wa_ref, wo_ref, fg_ref,
                  wr_ref, br_ref, x1_ref, h2_ref, logit_ref):
    c = jnp.dot(conv_ref[...], wc_ref[...], preferred_element_type=F32)
    a = jnp.dot(attn_ref[...], wa_ref[...], preferred_element_type=F32)
    g0 = gate_ref[:, 0:D_MODEL].astype(F32)
    g1 = gate_ref[:, D_MODEL:2 * D_MODEL].astype(F32)
    merged = (g0 * c + g1 * a).astype(BF16)
    x1 = x_ref[...] + jnp.dot(merged, wo_ref[...], preferred_element_type=F32)
    x1_ref[...] = x1
    ms = jnp.mean(x1 * x1, axis=-1, keepdims=True)
    h2 = x1 * lax.rsqrt(ms + EPS) * fg_ref[...]
    _store_rows(h2_ref, (), h2)
    h2_hi = h2.astype(BF16)
    h2_lo = (h2 - h2_hi.astype(F32)).astype(BF16)
    r = (jnp.dot(h2_hi, wr_ref[...], preferred_element_type=F32)
         + jnp.dot(h2_lo, wr_ref[...], preferred_element_type=F32))
    logit_ref[...] = r[:, 0:ROUTER_LANES] + r[:, ROUTER_LANES:2 * ROUTER_LANES] + br_ref[...]


def _merge(conv_act, attn_o, gates, x2, wc, wa, wo, fg, wr, br):
    T = x2.shape[0]
    tm = TM_MERGE
    row = lambda i: (i, 0)
    return pl.pallas_call(
        _merge_kernel,
        grid=(T // tm,),
        in_specs=[
            pl.BlockSpec((tm, CONV_CH), row),
            pl.BlockSpec((tm, ATTN_W), row),
            pl.BlockSpec((tm, COL_GATE), row),
            pl.BlockSpec((tm, D_MODEL), row),
            _const_spec((CONV_CH, D_MODEL)),
            _const_spec((ATTN_W, D_MODEL)),
            _const_spec((D_MODEL, D_MODEL)),
            _const_spec((1, D_MODEL)),
            _const_spec((D_MODEL, 2 * ROUTER_LANES)),
            _const_spec((1, ROUTER_LANES)),
        ],
        out_specs=(
            pl.BlockSpec((tm, D_MODEL), row),
            pl.BlockSpec((tm * ROW_TILE, LANES), row),
            pl.BlockSpec((tm, ROUTER_LANES), row),
        ),
        out_shape=(
            jax.ShapeDtypeStruct((T, D_MODEL), F32),
            jax.ShapeDtypeStruct((T * ROW_TILE, LANES), jnp.uint32),
            jax.ShapeDtypeStruct((T, ROUTER_LANES), F32),
        ),
        compiler_params=pltpu.CompilerParams(
            dimension_semantics=("parallel",), vmem_limit_bytes=VMEM_LIMIT),
        name="merge_out_router",
    )(conv_act, attn_o, gates, x2, wc, wa, wo, fg, wr, br)


def _route_kernel(logit_ref, tri_ref, low_ref, dest_ref, wts_ref, blk_ref,
                  cnt_ref, base_ref, *, n_blk_lanes):
    ph = pl.program_id(0)
    i = pl.program_id(1)
    tm = logit_ref.shape[0]
    lt = logit_ref[...].T
    row8 = lax.broadcasted_iota(jnp.int32, (SUBLANES, tm), 0)

    g = jnp.where(row8 < N_GROUPS, lt[0:SUBLANES, :], -jnp.inf)
    g_max = jnp.max(g, axis=0, keepdims=True)
    g_sum = jnp.sum(jnp.exp(g - g_max), axis=0, keepdims=True)
    g_sel = jnp.min(jnp.where(g == g_max, row8, SUBLANES), axis=0, keepdims=True)
    g_w = 1.0 / g_sum

    e_sel = jnp.zeros((EXPERTS_PER_GROUP, tm), F32)
    for gi in range(N_GROUPS):
        lo = ROUTER_EXPERT_COL + gi * EXPERTS_PER_GROUP
        e_sel = jnp.where(g_sel == gi, lt[lo:lo + EXPERTS_PER_GROUP, :], e_sel)
    e_max = jnp.max(e_sel, axis=0, keepdims=True)
    e_exp = jnp.exp(e_sel - e_max)
    prob = e_exp / jnp.sum(e_exp, axis=0, keepdims=True)
    p1 = jnp.max(prob, axis=0, keepdims=True)
    i1 = jnp.min(jnp.where(prob == p1, row8, SUBLANES), axis=0, keepdims=True)
    rest = jnp.where(row8 == i1, -1.0, prob)
    p2 = jnp.max(rest, axis=0, keepdims=True)
    i2 = jnp.min(jnp.where(rest == p2, row8, SUBLANES), axis=0, keepdims=True)
    denom = p1 + p2
    e0 = g_sel * EXPERTS_PER_GROUP + i1
    e1 = g_sel * EXPERTS_PER_GROUP + i2

    row32 = lax.broadcasted_iota(jnp.int32, (N_EXPERTS, tm), 0)
    oh0 = row32 == e0
    oh1 = row32 == e1
    ohs = jnp.where(oh0 | oh1, 1.0, 0.0)
    tile_cnt = jnp.sum(ohs, axis=1, keepdims=True)

    @pl.when((ph == 0) & (i == 0))
    def _():
        cnt_ref[...] = jnp.zeros(cnt_ref.shape, F32)

    @pl.when(ph == 0)
    def _():
        cnt_ref[...] += tile_cnt

    @pl.when((ph == 1) & (i == 0))
    def _():
        cnt = cnt_ref[...]
        padded = jnp.ceil(cnt * (1.0 / RB)) * RB
        pstart = jnp.dot(low_ref[...], padded, preferred_element_type=F32,
                         precision=lax.Precision.HIGHEST)
        base_ref[...] = pstart
        pad_end = pstart + padded
        row0 = lax.broadcasted_iota(jnp.int32, (N_EXPERTS, n_blk_lanes), 1).astype(F32) * RB
        expert = lax.broadcasted_iota(jnp.int32, (N_EXPERTS, n_blk_lanes), 0).astype(F32)
        blk = jnp.minimum(jnp.sum(jnp.where(pad_end[:, 0:1] <= row0, 1.0, 0.0), axis=0, keepdims=True),
                          N_EXPERTS - 1.0)
        valid_end = jnp.sum(jnp.where(expert == blk, (pstart + cnt)[:, 0:1], 0.0), axis=0, keepdims=True)
        n_valid = jnp.clip(valid_end - row0[0:1, :], 0.0, float(RB))
        n_used = jnp.max(pad_end[:, 0:1], axis=0, keepdims=True) * (1.0 / RB) + jnp.zeros_like(blk)
        blk_ref[...] = jnp.zeros(blk_ref.shape, jnp.int32)
        blk_ref[0:1, :] = blk.astype(jnp.int32)
        blk_ref[1:2, :] = n_valid.astype(jnp.int32)
        blk_ref[2:3, :] = n_used.astype(jnp.int32)

    @pl.when(ph == 1)
    def _():
        before = jnp.dot(ohs.astype(BF16), tri_ref[...], preferred_element_type=F32)
        pos = base_ref[:, 0:1] + before
        d0 = jnp.sum(jnp.where(oh0, pos, 0.0), axis=0, keepdims=True)
        d1 = jnp.sum(jnp.where(oh1, pos, 0.0), axis=0, keepdims=True)
        dest_ref[0:1, :] = d0.astype(jnp.int32)
        dest_ref[1:2, :] = d1.astype(jnp.int32)
        wts_ref[...] = jnp.zeros(wts_ref.shape, F32)
        wts_ref[0:1, :] = p1 / denom * g_w
        wts_ref[1:2, :] = p2 / denom * g_w
        base_ref[...] += tile_cnt


def _route(logits):
    T = logits.shape[0]
    tm = TM_ROUTE
    n_rows = 2 * T + N_EXPERTS * RB
    n_blk = n_rows // RB
    n_blk_lanes = -(-n_blk // LANES) * LANES
    r = jnp.arange(tm, dtype=jnp.int32)
    tri = (r[:, None] < r[None, :]).astype(BF16)
    e = jnp.arange(N_EXPERTS, dtype=jnp.int32)
    low = (e[None, :] < e[:, None]).astype(F32)
    return pl.pallas_call(
        functools.partial(_route_kernel, n_blk_lanes=n_blk_lanes),
        grid=(2, T // tm),
        in_specs=[
            pl.BlockSpec((tm, ROUTER_LANES), lambda ph, i: (i, 0)),
            _const_spec((tm, tm)),
            _const_spec((N_EXPERTS, N_EXPERTS)),
        ],
        out_specs=(
            pl.BlockSpec((2, tm), lambda ph, i: (0, i * ph)),
            pl.BlockSpec((SUBLANES, tm), lambda ph, i: (0, i * ph)),
            pl.BlockSpec((SUBLANES, n_blk_lanes), lambda ph, i: (0, 0)),
        ),
        out_shape=(
            jax.ShapeDtypeStruct((2, T), jnp.int32),
            jax.ShapeDtypeStruct((SUBLANES, T), F32),
            jax.ShapeDtypeStruct((SUBLANES, n_blk_lanes), jnp.int32),
        ),
        scratch_shapes=[pltpu.VMEM((N_EXPERTS, LANES), F32), pltpu.VMEM((N_EXPERTS, LANES), F32)],
        compiler_params=pltpu.CompilerParams(
            dimension_semantics=("arbitrary", "arbitrary"), vmem_limit_bytes=VMEM_LIMIT),
        name="route_plan",
    )(logits, tri, low)


def _sc_worker():
    return lax.axis_index("s") * SC_CORES + lax.axis_index("c")


def _sc_scatter_rows(h, dest, n_rows):
    T = h.shape[0]
    n_workers = SC_CORES * SC_SUBCORES
    per_w = T // n_workers
    n_ch = per_w // SC_ROWS
    assert per_w * n_workers == T and n_ch * SC_ROWS == per_w and n_ch % 2 == 0
    mesh = plsc.VectorSubcoreMesh(core_axis_name="c", subcore_axis_name="s")

    @functools.partial(
        pl.kernel, mesh=mesh,
        out_type=jax.ShapeDtypeStruct((n_rows,) + h.shape[1:], h.dtype),
        scratch_types=[
            pltpu.VMEM((2, n_ch, SC_ROWS), jnp.int32),
            pltpu.VMEM((2, SC_ROWS) + h.shape[1:], h.dtype),
            pltpu.SemaphoreType.DMA((2,)),
            pltpu.SemaphoreType.DMA((2,)),
        ],
    )
    def k(h_hbm, idx_hbm, out_hbm, idx_v, rows_v, lsem, ssem):
        wid = _sc_worker()
        base = wid * per_w
        pltpu.sync_copy(idx_hbm.at[0, wid], idx_v.at[0])
        pltpu.sync_copy(idx_hbm.at[1, wid], idx_v.at[1])

        def load(c, slot):
            return pltpu.make_async_copy(h_hbm.at[pl.ds(base + c * SC_ROWS, SC_ROWS)], rows_v.at[slot],
                                         lsem.at[slot])

        def scatter(slot_k, c, slot):
            return pltpu.make_async_copy(rows_v.at[slot], out_hbm.at[idx_v.at[slot_k, c]], ssem.at[slot])

        load(0, 0).start()

        def body(g, carry):
            for slot in range(2):
                c = 2 * g + slot
                load(c, slot).wait()

                @pl.when(c >= 1)
                def _():
                    scatter(0, c - 1, 1 - slot).wait()
                    scatter(1, c - 1, 1 - slot).wait()

                @pl.when(c + 1 < n_ch)
                def _():
                    load(c + 1, 1 - slot).start()

                scatter(0, c, slot).start()
                scatter(1, c, slot).start()
            return carry

        lax.fori_loop(0, n_ch // 2, body, 0)
        scatter(0, n_ch - 1, 1).wait()
        scatter(1, n_ch - 1, 1).wait()

    return k(h, dest.reshape(2, n_workers, n_ch, SC_ROWS))


def _sc_gather_rows(table, idx):
    B = idx.shape[0]
    n_workers = SC_CORES * SC_SUBCORES
    per_w = B // n_workers
    n_ch = per_w // SC_ROWS
    assert per_w * n_workers == B and n_ch * SC_ROWS == per_w and n_ch % 2 == 0
    mesh = plsc.VectorSubcoreMesh(core_axis_name="c", subcore_axis_name="s")

    @functools.partial(
        pl.kernel, mesh=mesh,
        out_type=jax.ShapeDtypeStruct((B,) + table.shape[1:], table.dtype),
        scratch_types=[
            pltpu.VMEM((n_ch, SC_ROWS), jnp.int32),
            pltpu.VMEM((2, SC_ROWS) + table.shape[1:], table.dtype),
            pltpu.SemaphoreType.DMA((2,)),
            pltpu.SemaphoreType.DMA((2,)),
        ],
    )
    def k(table_hbm, idx_hbm, out_hbm, idx_v, rows_v, gsem, wsem):
        wid = _sc_worker()
        base = wid * per_w
        pltpu.sync_copy(idx_hbm.at[wid], idx_v)

        def gather(c, slot):
            return pltpu.make_async_copy(table_hbm.at[idx_v.at[c]], rows_v.at[slot], gsem.at[slot])

        def writeback(c, slot):
            return pltpu.make_async_copy(rows_v.at[slot], out_hbm.at[pl.ds(base + c * SC_ROWS, SC_ROWS)],
                                         wsem.at[slot])

        gather(0, 0).start()

        def body(g, carry):
            for slot in range(2):
                c = 2 * g + slot
                gather(c, slot).wait()

                @pl.when(c >= 1)
                def _():
                    writeback(c - 1, 1 - slot).wait()

                @pl.when(c + 1 < n_ch)
                def _():
                    gather(c + 1, 1 - slot).start()

                writeback(c, slot).start()
            return carry

        lax.fori_loop(0, n_ch // 2, body, 0)
        writeback(n_ch - 1, 1).wait()

    return k(table, idx.reshape(n_workers, n_ch, SC_ROWS))


def _expert_kernel(blk_exp_ref, n_used_ref, n_valid_ref, xs_ref, wg_ref, wu_ref, wd_ref, y_ref,
                   wg_bf, wu_bf, wd_bf):
    b = pl.program_id(0)
    used = b < n_used_ref[0]
    new_expert = (b == 0) | (blk_exp_ref[b] != blk_exp_ref[jnp.maximum(b - 1, 0)])

    @pl.when(used & new_expert)
    def _():
        wg_bf[...] = wg_ref[0].astype(BF16)
        wu_bf[...] = wu_ref[0].astype(BF16)
        wd_bf[...] = wd_ref[0].astype(BF16)

    @pl.when(used)
    def _():
        x = _load_rows(xs_ref, (), RB)
        row = lax.broadcasted_iota(jnp.int32, (RB, 1), 0)
        xb = jnp.where(row < n_valid_ref[b], x, jnp.zeros_like(x))
        g = jnp.dot(xb, wg_bf[...], preferred_element_type=F32)
        u = jnp.dot(xb, wu_bf[...], preferred_element_type=F32)
        hid = (g * _sigmoid(g) * u).astype(BF16)
        _store_rows(y_ref, (), jnp.dot(hid, wd_bf[...], preferred_element_type=F32))

    @pl.when(jnp.logical_not(used))
    def _():
        y_ref[...] = jnp.zeros(y_ref.shape, y_ref.dtype)


def _experts(blk_exp, n_used, n_valid, xs, wg, wu, wd):
    n_blk = blk_exp.shape[0]
    grid_spec = pltpu.PrefetchScalarGridSpec(
        num_scalar_prefetch=3,
        grid=(n_blk,),
        in_specs=[
            pl.BlockSpec((RB * ROW_TILE, LANES), lambda b, be, nu, nv: (b, 0)),
            pl.BlockSpec((1, D_MODEL, D_EXPERT), lambda b, be, nu, nv: (be[b], 0, 0)),
            pl.BlockSpec((1, D_MODEL, D_EXPERT), lambda b, be, nu, nv: (be[b], 0, 0)),
            pl.BlockSpec((1, D_EXPERT, D_MODEL), lambda b, be, nu, nv: (be[b], 0, 0)),
        ],
        out_specs=pl.BlockSpec((RB * ROW_TILE, LANES), lambda b, be, nu, nv: (b, 0)),
        scratch_shapes=[
            pltpu.VMEM((D_MODEL, D_EXPERT), BF16),
            pltpu.VMEM((D_MODEL, D_EXPERT), BF16),
            pltpu.VMEM((D_EXPERT, D_MODEL), BF16),
        ],
    )
    return pl.pallas_call(
        _expert_kernel,
        grid_spec=grid_spec,
        out_shape=jax.ShapeDtypeStruct(xs.shape, xs.dtype),
        compiler_params=pltpu.CompilerParams(
            dimension_semantics=("arbitrary",), vmem_limit_bytes=VMEM_LIMIT),
        name="expert_mlp",
    )(blk_exp, n_used, n_valid, xs, wg, wu, wd)


def _combine_kernel(x1_ref, y0_ref, y1_ref, w_ref, o_ref):
    tm = x1_ref.shape[0]
    y0 = _load_rows(y0_ref, (), tm).astype(F32)
    y1 = _load_rows(y1_ref, (), tm).astype(F32)
    w = w_ref[...].T
    o_ref[...] = x1_ref[...] + (y0 * w[:, 0:1] + y1 * w[:, 1:2])


def _combine(x1, y, wts):
    T = x1.shape[0]
    tm = TM_COMB
    row = lambda i: (i, 0)
    return pl.pallas_call(
        _combine_kernel,
        grid=(T // tm,),
        in_specs=[
            pl.BlockSpec((tm, D_MODEL), row),
            pl.BlockSpec((tm * ROW_TILE, LANES), row),
            pl.BlockSpec((tm * ROW_TILE, LANES), lambda i: (i + T // tm, 0)),
            pl.BlockSpec((SUBLANES, tm), lambda i: (0, i)),
        ],
        out_specs=pl.BlockSpec((tm, D_MODEL), row),
        out_shape=jax.ShapeDtypeStruct((T, D_MODEL), F32),
        compiler_params=pltpu.CompilerParams(
            dimension_semantics=("parallel",), vmem_limit_bytes=VMEM_LIMIT),
        name="moe_combine",
    )(x1, y, y, wts)


def kernel(x, attn_norm_g, w_in, conv_dw_w, conv_dw_b, conv_ln_g, conv_ln_b, w_conv_out,
           q_norm_g, k_norm_g, lambda_q1, lambda_k1, lambda_q2, lambda_k2, subln_g,
           w_attn_out, w_out, ffn_norm_g, w_router_group, b_router_group,
           w_router_expert, b_router_expert, w_gate_e, w_up_e, w_down_e):
    B, S, D = x.shape
    T = B * S
    l = 0
    x2 = x.reshape(T, D)

    reps = COL_Q // DA_HEAD_DIM
    qg = jnp.tile(q_norm_g[l], reps).reshape(1, COL_Q)
    kg = jnp.tile(k_norm_g[l], reps).reshape(1, COL_K)
    grp = jnp.arange(NORM_BLOCK, dtype=jnp.int32) // DA_HEAD_DIM
    bd = jnp.where(grp[:, None] == grp[None, :], 1.0 / DA_HEAD_DIM, 0.0).astype(BF16)

    conv_act, q, k, v, gates = _in_projection(
        x2, attn_norm_g[l].reshape(1, D), w_in[l].astype(BF16), qg, kg, bd, conv_dw_w[l],
        conv_dw_b[l].reshape(1, CONV_CH), conv_ln_g[l].reshape(1, CONV_CH), conv_ln_b[l].reshape(1, CONV_CH), S)

    lam_vecs = jnp.stack([lambda_q1[l], lambda_k1[l], lambda_q2[l], lambda_k2[l]]).astype(F32)
    attn_o = _diff_attention(q.reshape(B, S, COL_Q), k.reshape(B, S, COL_K),
                             v.reshape(B, S, COL_V), lam_vecs, subln_g[l].reshape(1, DA_V_DIM))

    e_lo, e_hi = ROUTER_EXPERT_COL, ROUTER_EXPERT_COL + N_EXPERTS
    wr = jnp.zeros((D, ROUTER_LANES), F32)
    wr = wr.at[:, :N_GROUPS].set(w_router_group[l]).at[:, e_lo:e_hi].set(w_router_expert[l])
    br = jnp.zeros((1, ROUTER_LANES), F32)
    br = br.at[0, :N_GROUPS].set(b_router_group[l]).at[0, e_lo:e_hi].set(b_router_expert[l])

    wr_hi = wr.astype(BF16)
    wr_lo = (wr - wr_hi.astype(F32)).astype(BF16)
    x1, h2, logits = _merge(conv_act, attn_o.reshape(T, ATTN_W), gates, x2,
                            w_conv_out[l].astype(BF16), w_attn_out[l].astype(BF16),
                            w_out[l].astype(BF16), ffn_norm_g[l].reshape(1, D),
                            jnp.concatenate([wr_hi, wr_lo], axis=1), br)

    dest, wts, blk = _route(logits)
    n_rows = 2 * T + N_EXPERTS * RB
    n_blk = n_rows // RB
    blk_exp, n_valid, n_used = blk[0, :n_blk], blk[1, :n_blk], blk[2, 0:1]

    xs = _sc_scatter_rows(h2.reshape(T, ROW_TILE, LANES), dest, n_rows)
    yb = _experts(blk_exp, n_used, n_valid, xs.reshape(n_rows * ROW_TILE, LANES),
                  w_gate_e[l], w_up_e[l], w_down_e[l])
    y = _sc_gather_rows(yb.reshape(n_rows, ROW_TILE, LANES), dest.reshape(2 * T))

    out = _combine(x1, y.reshape(2 * T * ROW_TILE, LANES), wts)
    return out.reshape(B, S, D)
```

```python
import functools
import math

import jax
import jax.numpy as jnp
from jax import lax
from jax.experimental import pallas as pl
from jax.experimental.pallas import tpu as pltpu
from jax.experimental.pallas import tpu_sc as plsc

F32 = jnp.float32
BF16 = jnp.bfloat16

D_MODEL = 1024
CONV_CH = 512
CONV_WIDTH = 31
DA_HEADS = 4
DA_HEAD_DIM = 64
DA_V_DIM = 128
ATTN_W = 512
N_GROUPS = 4
EXPERTS_PER_GROUP = 8
N_EXPERTS = 32
D_EXPERT = 512
EPS = 1e-6
LAM_INIT = 0.8 - 0.6 * math.exp(-0.3 * 0)

COL_GLU = 2 * CONV_CH
COL_Q = 512
COL_K = 512
COL_V = 512
COL_GATE = 2 * D_MODEL
OFF_Q = COL_GLU
OFF_K = OFF_Q + COL_Q
OFF_V = OFF_K + COL_K
OFF_GATE = OFF_V + COL_V
IN_COLS = OFF_GATE + COL_GATE

ROUTER_LANES = 128
ROUTER_EXPERT_COL = 8
SUBLANES = 8
ROW_TILE = D_MODEL // (2 * 128)
NORM_BLOCK = 256
HALO = 32

TM_PROJ = 512
TM_MERGE = 1024
CONV_CHUNK = 64
ATT_TQ = 1024
ATT_TK = 1024
ATT_RC = 256
ATT_HEADS = 2
LANES = 128
RB = 512
TM_COMB = 1024
TM_ROUTE = 1024
SC_CORES = 2
SC_SUBCORES = 16
SC_ROWS = 64

VMEM_LIMIT = 48 * 1024 * 1024


def _sigmoid(x):
    return 1.0 / (1.0 + jnp.exp(-x))


def _const_spec(shape):
    nd = len(shape)
    return pl.BlockSpec(shape, lambda *_: (0,) * nd, pipeline_mode=pl.Buffered(1))


def _store_rows(ref, lead, val):
    rows, half = val.shape[0], val.shape[1] // 2
    hi = pltpu.bitcast(val[:, :half].astype(BF16).astype(F32), jnp.uint32)
    lo = pltpu.bitcast(val[:, half:].astype(BF16).astype(F32), jnp.uint32)
    words = hi | (lo >> 16)
    for j in range(half // LANES):
        ref[(*lead, pl.ds(j, rows, stride=ROW_TILE), slice(None))] = words[:, j * LANES:(j + 1) * LANES]


def _load_rows(ref, lead, rows):
    words = jnp.concatenate(
        [ref[(*lead, pl.ds(j, rows, stride=ROW_TILE), slice(None))] for j in range(ROW_TILE)], axis=1)
    hi = pltpu.bitcast(words & jnp.uint32(0xFFFF0000), F32).astype(BF16)
    lo = pltpu.bitcast(words << 16, F32).astype(BF16)
    return jnp.concatenate([hi, lo], axis=1)


def _group_mean_sq(t, bd_ref):
    sq = t * t
    hi = sq.astype(BF16)
    lo = (sq - hi.astype(F32)).astype(BF16)
    w = bd_ref.shape[0]
    parts = [jnp.dot(hi[:, c:c + w], bd_ref[...], preferred_element_type=F32)
             + jnp.dot(lo[:, c:c + w], bd_ref[...], preferred_element_type=F32)
             for c in range(0, t.shape[1], w)]
    return jnp.concatenate(parts, axis=1)


def _inproj_kernel(x_ref, g_ref, w_ref, qg_ref, kg_ref, bd_ref, cw_ref, cb_ref, lng_ref, lnb_ref,
                   conv_ref, q_ref, k_ref, v_ref, gate_ref, buf_ref, halo_ref, *, tiles_per_seq):
    i = pl.program_id(0)
    tm = x_ref.shape[0]
    x = x_ref[...]
    ms = jnp.mean(x * x, axis=-1, keepdims=True)
    h = (x * lax.rsqrt(ms + EPS) * g_ref[...]).astype(BF16)

    def proj(lo, width):
        return jnp.dot(h, w_ref[:, lo:lo + width], preferred_element_type=F32)

    a = proj(0, CONV_CH)
    gt = proj(CONV_CH, CONV_CH)
    glu = a * _sigmoid(gt)

    rows = HALO + tm
    buf_ref[0, 0:HALO, :] = jnp.where(i % tiles_per_seq == 0, 0.0, halo_ref[...])
    buf_ref[0, HALO:rows, :] = glu
    halo_ref[...] = glu[tm - HALO:tm, :]
    for r in range(1, SUBLANES):
        buf_ref[r, 0:rows - SUBLANES, :] = buf_ref[0, r:r + rows - SUBLANES, :]
    first = HALO - (CONV_WIDTH - 1)
    for c in range(tm // CONV_CHUNK):
        r0 = c * CONV_CHUNK
        acc = jnp.broadcast_to(cb_ref[...], (CONV_CHUNK, CONV_CH))
        for j in range(CONV_WIDTH):
            shift = (first + j) % SUBLANES
            lo = r0 + first + j - shift
            acc = acc + cw_ref[j:j + 1, :] * buf_ref[shift, lo:lo + CONV_CHUNK, :]
        mu = jnp.mean(acc, axis=-1, keepdims=True)
        d = acc - mu
        var = jnp.mean(d * d, axis=-1, keepdims=True)
        y = d * lax.rsqrt(var + EPS) * lng_ref[...] + lnb_ref[...]
        conv_ref[r0:r0 + CONV_CHUNK, :] = (y * _sigmoid(y)).astype(BF16)

    q = proj(OFF_Q, COL_Q)
    qn = q * lax.rsqrt(_group_mean_sq(q, bd_ref) + EPS) * qg_ref[...]
    q_ref[...] = (qn * (DA_HEAD_DIM ** -0.5)).astype(BF16)

    k = proj(OFF_K, COL_K)
    kn = k * lax.rsqrt(_group_mean_sq(k, bd_ref) + EPS) * kg_ref[...]
    k_ref[...] = kn.astype(BF16)

    v_ref[...] = proj(OFF_V, COL_V).astype(BF16)

    for c in range(COL_GATE // 512):
        gate_ref[:, c * 512:(c + 1) * 512] = _sigmoid(proj(OFF_GATE + c * 512, 512)).astype(BF16)


def _in_projection(x2, g, w_bf, qg, kg, bd, dw_w, dw_b, ln_g, ln_b, seq_len):
    T = x2.shape[0]
    tm = TM_PROJ
    row = lambda i: (i, 0)
    out_shape = (
        jax.ShapeDtypeStruct((T, CONV_CH), BF16),
        jax.ShapeDtypeStruct((T, COL_Q), BF16),
        jax.ShapeDtypeStruct((T, COL_K), BF16),
        jax.ShapeDtypeStruct((T, COL_V), BF16),
        jax.ShapeDtypeStruct((T, COL_GATE), BF16),
    )
    return pl.pallas_call(
        functools.partial(_inproj_kernel, tiles_per_seq=seq_len // tm),
        grid=(T // tm,),
        in_specs=[
            pl.BlockSpec((tm, D_MODEL), row),
            _const_spec((1, D_MODEL)),
            _const_spec((D_MODEL, IN_COLS)),
            _const_spec((1, COL_Q)),
            _const_spec((1, COL_K)),
            _const_spec((NORM_BLOCK, NORM_BLOCK)),
            _const_spec((CONV_WIDTH, CONV_CH)),
            _const_spec((1, CONV_CH)),
            _const_spec((1, CONV_CH)),
            _const_spec((1, CONV_CH)),
        ],
        out_specs=(
            pl.BlockSpec((tm, CONV_CH), row),
            pl.BlockSpec((tm, COL_Q), row),
            pl.BlockSpec((tm, COL_K), row),
            pl.BlockSpec((tm, COL_V), row),
            pl.BlockSpec((tm, COL_GATE), row),
        ),
        out_shape=out_shape,
        scratch_shapes=[pltpu.VMEM((SUBLANES, HALO + tm, CONV_CH), F32), pltpu.VMEM((HALO, CONV_CH), F32)],
        compiler_params=pltpu.CompilerParams(
            dimension_semantics=("arbitrary",), vmem_limit_bytes=VMEM_LIMIT),
        name="in_projection",
    )(x2, g, w_bf, qg, kg, bd, dw_w, dw_b, ln_g, ln_b)


def _attn_kernel(q_ref, k_ref, v_ref, lam_ref, sg_ref, o_ref, qs_ref, m_ref, acc_ref, s_ref, *, tq, tk):
    qi = pl.program_id(2)
    lane = lax.broadcasted_iota(jnp.int32, (tq, DA_V_DIM), 1)
    for hh in range(ATT_HEADS):
        q = q_ref[0, :, hh * DA_V_DIM:(hh + 1) * DA_V_DIM]
        zero = jnp.zeros_like(q)
        qs_ref[hh, 0:tq, :] = jnp.where(lane < DA_HEAD_DIM, q, zero)
        qs_ref[hh, tq:2 * tq, :] = jnp.where(lane >= DA_HEAD_DIM, q, zero)
    m_ref[...] = jnp.full(m_ref.shape, -jnp.inf, F32)
    acc_ref[...] = jnp.zeros(acc_ref.shape, F32)
    ones = jnp.ones((tk, LANES), BF16)
    rc = ATT_RC
    chunks = [slice(r0, r0 + rc) for r0 in range(0, 2 * tq, rc)]
    n_diag = tq // tk

    def scores(hh, j, rows):
        kj = k_ref[0, pl.ds(pl.multiple_of(j * tk, tk), tk), hh * DA_V_DIM:(hh + 1) * DA_V_DIM]
        return lax.dot_general(qs_ref[hh, rows, :], kj, (((1,), (1,)), ((), ())),
                               preferred_element_type=F32)

    def step(j, diag, prefetch):
        for hh in range(ATT_HEADS):
            vj = jnp.concatenate(
                [v_ref[0, pl.ds(pl.multiple_of(j * tk, tk), tk), hh * DA_V_DIM:(hh + 1) * DA_V_DIM], ones], axis=1)
            for rows in chunks:
                r_lo = rows.start % tq
                if diag is None:
                    kw, masked = tk, False
                else:
                    k_lo = diag * tk
                    if k_lo > r_lo + rc - 1:
                        if prefetch:
                            s_ref[hh, rows, :] = scores(hh, j + 1, rows)
                        continue
                    kw = min(tk, -(-(r_lo + rc - k_lo) // (2 * LANES)) * (2 * LANES))
                    masked = k_lo + kw - 1 > r_lo
                s = s_ref[hh, rows, 0:kw]
                if prefetch:
                    s_ref[hh, rows, :] = scores(hh, j + 1, rows)
                if masked:
                    r = lax.broadcasted_iota(jnp.int32, s.shape, 0) + r_lo
                    c = lax.broadcasted_iota(jnp.int32, s.shape, 1) + k_lo
                    s = jnp.where(c <= r, s, -jnp.inf)
                m_old = m_ref[hh, rows, :]
                m_new = jnp.maximum(m_old, jnp.max(s, axis=-1, keepdims=True))
                alpha = jnp.exp(m_old - m_new)
                p = jnp.exp(s - jnp.tile(m_new, (1, kw // LANES)))
                pv = jnp.dot(p.astype(BF16), vj[0:kw, :], preferred_element_type=F32)
                acc_ref[hh, rows, :] = jnp.tile(alpha, (1, 2)) * acc_ref[hh, rows, :] + pv
                m_ref[hh, rows, :] = m_new

    for hh in range(ATT_HEADS):
        for rows in chunks:
            s_ref[hh, rows, :] = scores(hh, 0, rows)

    def body(j, carry):
        step(j, None, True)
        return carry

    n_full = qi * n_diag
    lax.fori_loop(0, n_full, body, 0)
    for t in range(n_diag):
        step(n_full + t, t, t + 1 < n_diag)

    lam = (jnp.exp(jnp.sum(lam_ref[0:1, :] * lam_ref[1:2, :], axis=-1, keepdims=True))
           - jnp.exp(jnp.sum(lam_ref[2:3, :] * lam_ref[3:4, :], axis=-1, keepdims=True))
           + LAM_INIT)
    for hh in range(ATT_HEADS):
        o1 = acc_ref[hh, 0:tq, 0:LANES] / acc_ref[hh, 0:tq, LANES:2 * LANES]
        o2 = acc_ref[hh, tq:2 * tq, 0:LANES] / acc_ref[hh, tq:2 * tq, LANES:2 * LANES]
        o = o1 - lam * o2
        ms = jnp.mean(o * o, axis=-1, keepdims=True)
        on = o * lax.rsqrt(ms + EPS) * sg_ref[...]
        o_ref[0, :, hh * DA_V_DIM:(hh + 1) * DA_V_DIM] = (on * (1.0 - LAM_INIT)).astype(BF16)


def _diff_attention(q3, k3, v3, lam_vecs, subln_g):
    B, S, _ = q3.shape
    tq, tk = ATT_TQ, ATT_TK
    w = ATT_HEADS * DA_V_DIM
    return pl.pallas_call(
        functools.partial(_attn_kernel, tq=tq, tk=tk),
        grid=(B, DA_HEADS // ATT_HEADS, S // tq),
        in_specs=[
            pl.BlockSpec((1, tq, w), lambda b, h, i: (b, i, h)),
            pl.BlockSpec((1, S, w), lambda b, h, i: (b, 0, h)),
            pl.BlockSpec((1, S, w), lambda b, h, i: (b, 0, h)),
            _const_spec((4, DA_HEAD_DIM)),
            _const_spec((1, DA_V_DIM)),
        ],
        out_specs=pl.BlockSpec((1, tq, w), lambda b, h, i: (b, i, h)),
        out_shape=jax.ShapeDtypeStruct((B, S, ATTN_W), BF16),
        scratch_shapes=[
            pltpu.VMEM((ATT_HEADS, 2 * tq, DA_V_DIM), BF16),
            pltpu.VMEM((ATT_HEADS, 2 * tq, LANES), F32),
            pltpu.VMEM((ATT_HEADS, 2 * tq, 2 * LANES), F32),
            pltpu.VMEM((ATT_HEADS, 2 * tq, tk), F32),
        ],
        compiler_params=pltpu.CompilerParams(
            dimension_semantics=("parallel", "parallel", "parallel"),
            vmem_limit_bytes=VMEM_LIMIT),
        name="diff_attention",
    )(q3, k3, v3, lam_vecs, subln_g)


def _merge_kernel(conv_ref, attn_ref, gate_ref, x_ref, wc_ref, wa_ref, wo_ref, fg_ref,
                  wr_ref, br_ref, x1_ref, h2_ref, logit_ref):
    c = jnp.dot(conv_ref[...], wc_ref[...], preferred_element_type=F32)
    a = jnp.dot(attn_ref[...], wa_ref[...], preferred_element_type=F32)
    g0 = gate_ref[:, 0:D_MODEL].astype(F32)
    g1 = gate_ref[:, D_MODEL:2 * D_MODEL].astype(F32)
    merged = (g0 * c + g1 * a).astype(BF16)
    x1 = x_ref[...] + jnp.dot(merged, wo_ref[...], preferred_element_type=F32)
    x1_ref[...] = x1
    ms = jnp.mean(x1 * x1, axis=-1, keepdims=True)
    h2 = x1 * lax.rsqrt(ms + EPS) * fg_ref[...]
    _store_rows(h2_ref, (), h2)
    h2_hi = h2.astype(BF16)
    h2_lo = (h2 - h2_hi.astype(F32)).astype(BF16)
    half = h2.shape[0] // 2
    for rows in (slice(0, half), slice(half, 2 * half)):
        r = (jnp.dot(h2_hi[rows], wr_ref[...], preferred_element_type=F32)
             + jnp.dot(h2_lo[rows], wr_ref[...], preferred_element_type=F32))
        logit_ref[rows, :] = r[:, 0:ROUTER_LANES] + r[:, ROUTER_LANES:2 * ROUTER_LANES] + br_ref[...]


def _merge(conv_act, attn_o, gates, x2, wc, wa, wo, fg, wr, br):
    T = x2.shape[0]
    tm = TM_MERGE
    row = lambda i: (i, 0)
    return pl.pallas_call(
        _merge_kernel,
        grid=(T // tm,),
        in_specs=[
            pl.BlockSpec((tm, CONV_CH), row),
            pl.BlockSpec((tm, ATTN_W), row),
            pl.BlockSpec((tm, COL_GATE), row),
            pl.BlockSpec((tm, D_MODEL), row),
            _const_spec((CONV_CH, D_MODEL)),
            _const_spec((ATTN_W, D_MODEL)),
            _const_spec((D_MODEL, D_MODEL)),
            _const_spec((1, D_MODEL)),
            _const_spec((D_MODEL, 2 * ROUTER_LANES)),
            _const_spec((1, ROUTER_LANES)),
        ],
        out_specs=(
            pl.BlockSpec((tm, D_MODEL), row),
            pl.BlockSpec((tm * ROW_TILE, LANES), row),
            pl.BlockSpec((tm, ROUTER_LANES), row),
        ),
        out_shape=(
            jax.ShapeDtypeStruct((T, D_MODEL), F32),
            jax.ShapeDtypeStruct((T * ROW_TILE, LANES), jnp.uint32),
            jax.ShapeDtypeStruct((T, ROUTER_LANES), F32),
        ),
        compiler_params=pltpu.CompilerParams(
            dimension_semantics=("parallel",), vmem_limit_bytes=VMEM_LIMIT),
        name="merge_out_router",
    )(conv_act, attn_o, gates, x2, wc, wa, wo, fg, wr, br)


def _route_kernel(logit_ref, tri_ref, low_ref, dest_ref, wts_ref, blk_ref,
                  cnt_ref, base_ref, *, n_blk_lanes):
    ph = pl.program_id(0)
    i = pl.program_id(1)
    tm = logit_ref.shape[0]
    lt = logit_ref[...].T
    row8 = lax.broadcasted_iota(jnp.int32, (SUBLANES, tm), 0)

    g = jnp.where(row8 < N_GROUPS, lt[0:SUBLANES, :], -jnp.inf)
    g_max = jnp.max(g, axis=0, keepdims=True)
    g_sum = jnp.sum(jnp.exp(g - g_max), axis=0, keepdims=True)
    g_sel = jnp.min(jnp.where(g == g_max, row8, SUBLANES), axis=0, keepdims=True)
    g_w = 1.0 / g_sum

    e_sel = jnp.zeros((EXPERTS_PER_GROUP, tm), F32)
    for gi in range(N_GROUPS):
        lo = ROUTER_EXPERT_COL + gi * EXPERTS_PER_GROUP
        e_sel = jnp.where(g_sel == gi, lt[lo:lo + EXPERTS_PER_GROUP, :], e_sel)
    e_max = jnp.max(e_sel, axis=0, keepdims=True)
    e_exp = jnp.exp(e_sel - e_max)
    prob = e_exp / jnp.sum(e_exp, axis=0, keepdims=True)
    p1 = jnp.max(prob, axis=0, keepdims=True)
    i1 = jnp.min(jnp.where(prob == p1, row8, SUBLANES), axis=0, keepdims=True)
    rest = jnp.where(row8 == i1, -1.0, prob)
    p2 = jnp.max(rest, axis=0, keepdims=True)
    i2 = jnp.min(jnp.where(rest == p2, row8, SUBLANES), axis=0, keepdims=True)
    denom = p1 + p2
    e0 = g_sel * EXPERTS_PER_GROUP + i1
    e1 = g_sel * EXPERTS_PER_GROUP + i2

    row32 = lax.broadcasted_iota(jnp.int32, (N_EXPERTS, tm), 0)
    oh0 = row32 == e0
    oh1 = row32 == e1
    ohs = jnp.where(oh0 | oh1, 1.0, 0.0)
    tile_cnt = jnp.sum(ohs, axis=1, keepdims=True)

    @pl.when((ph == 0) & (i == 0))
    def _():
        cnt_ref[...] = jnp.zeros(cnt_ref.shape, F32)

    @pl.when(ph == 0)
    def _():
        cnt_ref[...] += tile_cnt

    @pl.when((ph == 1) & (i == 0))
    def _():
        cnt = cnt_ref[...]
        padded = jnp.ceil(cnt * (1.0 / RB)) * RB
        pstart = jnp.dot(low_ref[...], padded, preferred_element_type=F32,
                         precision=lax.Precision.HIGHEST)
        base_ref[...] = pstart
        pad_end = pstart + padded
        row0 = lax.broadcasted_iota(jnp.int32, (N_EXPERTS, n_blk_lanes), 1).astype(F32) * RB
        expert = lax.broadcasted_iota(jnp.int32, (N_EXPERTS, n_blk_lanes), 0).astype(F32)
        blk = jnp.minimum(jnp.sum(jnp.where(pad_end[:, 0:1] <= row0, 1.0, 0.0), axis=0, keepdims=True),
                          N_EXPERTS - 1.0)
        valid_end = jnp.sum(jnp.where(expert == blk, (pstart + cnt)[:, 0:1], 0.0), axis=0, keepdims=True)
        n_valid = jnp.clip(valid_end - row0[0:1, :], 0.0, float(RB))
        n_used = jnp.max(pad_end[:, 0:1], axis=0, keepdims=True) * (1.0 / RB) + jnp.zeros_like(blk)
        blk_ref[...] = jnp.zeros(blk_ref.shape, jnp.int32)
        blk_ref[0:1, :] = blk.astype(jnp.int32)
        blk_ref[1:2, :] = n_valid.astype(jnp.int32)
        blk_ref[2:3, :] = n_used.astype(jnp.int32)

    @pl.when(ph == 1)
    def _():
        before = jnp.dot(ohs.astype(BF16), tri_ref[...], preferred_element_type=F32)
        pos = base_ref[:, 0:1] + before
        d0 = jnp.sum(jnp.where(oh0, pos, 0.0), axis=0, keepdims=True)
        d1 = jnp.sum(jnp.where(oh1, pos, 0.0), axis=0, keepdims=True)
        dest_ref[0:1, :] = d0.astype(jnp.int32)
        dest_ref[1:2, :] = d1.astype(jnp.int32)
        wts_ref[...] = jnp.zeros(wts_ref.shape, F32)
        wts_ref[0:1, :] = p1 / denom * g_w
        wts_ref[1:2, :] = p2 / denom * g_w
        base_ref[...] += tile_cnt


def _route(logits):
    T = logits.shape[0]
    tm = TM_ROUTE
    n_rows = 2 * T + N_EXPERTS * RB
    n_blk = n_rows // RB
    n_blk_lanes = -(-n_blk // LANES) * LANES
    r = jnp.arange(tm, dtype=jnp.int32)
    tri = (r[:, None] < r[None, :]).astype(BF16)
    e = jnp.arange(N_EXPERTS, dtype=jnp.int32)
    low = (e[None, :] < e[:, None]).astype(F32)
    return pl.pallas_call(
        functools.partial(_route_kernel, n_blk_lanes=n_blk_lanes),
        grid=(2, T // tm),
        in_specs=[
            pl.BlockSpec((tm, ROUTER_LANES), lambda ph, i: (i, 0)),
            _const_spec((tm, tm)),
            _const_spec((N_EXPERTS, N_EXPERTS)),
        ],
        out_specs=(
            pl.BlockSpec((2, tm), lambda ph, i: (0, i * ph)),
            pl.BlockSpec((SUBLANES, tm), lambda ph, i: (0, i * ph)),
            pl.BlockSpec((SUBLANES, n_blk_lanes), lambda ph, i: (0, 0)),
        ),
        out_shape=(
            jax.ShapeDtypeStruct((2, T), jnp.int32),
            jax.ShapeDtypeStruct((SUBLANES, T), F32),
            jax.ShapeDtypeStruct((SUBLANES, n_blk_lanes), jnp.int32),
        ),
        scratch_shapes=[pltpu.VMEM((N_EXPERTS, LANES), F32), pltpu.VMEM((N_EXPERTS, LANES), F32)],
        compiler_params=pltpu.CompilerParams(
            dimension_semantics=("arbitrary", "arbitrary"), vmem_limit_bytes=VMEM_LIMIT),
        name="route_plan",
    )(logits, tri, low)


def _sc_worker():
    return lax.axis_index("s") * SC_CORES + lax.axis_index("c")


def _sc_scatter_rows(h, dest, n_rows):
    T = h.shape[0]
    n_workers = SC_CORES * SC_SUBCORES
    per_w = T // n_workers
    n_ch = per_w // SC_ROWS
    assert per_w * n_workers == T and n_ch * SC_ROWS == per_w and n_ch % 2 == 0
    mesh = plsc.VectorSubcoreMesh(core_axis_name="c", subcore_axis_name="s")

    @functools.partial(
        pl.kernel, mesh=mesh,
        out_type=jax.ShapeDtypeStruct((n_rows,) + h.shape[1:], h.dtype),
        scratch_types=[
            pltpu.VMEM((2, n_ch, SC_ROWS), jnp.int32),
            pltpu.VMEM((2, SC_ROWS) + h.shape[1:], h.dtype),
            pltpu.SemaphoreType.DMA((2,)),
            pltpu.SemaphoreType.DMA((2,)),
        ],
    )
    def k(h_hbm, idx_hbm, out_hbm, idx_v, rows_v, lsem, ssem):
        wid = _sc_worker()
        base = wid * per_w
        pltpu.sync_copy(idx_hbm.at[0, wid], idx_v.at[0])
        pltpu.sync_copy(idx_hbm.at[1, wid], idx_v.at[1])

        def load(c, slot):
            return pltpu.make_async_copy(h_hbm.at[pl.ds(base + c * SC_ROWS, SC_ROWS)], rows_v.at[slot],
                                         lsem.at[slot])

        def scatter(slot_k, c, slot):
            return pltpu.make_async_copy(rows_v.at[slot], out_hbm.at[idx_v.at[slot_k, c]], ssem.at[slot])

        load(0, 0).start()

        def body(g, carry):
            for slot in range(2):
                c = 2 * g + slot
                load(c, slot).wait()

                @pl.when(c >= 1)
                def _():
                    scatter(0, c - 1, 1 - slot).wait()
                    scatter(1, c - 1, 1 - slot).wait()

                @pl.when(c + 1 < n_ch)
                def _():
                    load(c + 1, 1 - slot).start()

                scatter(0, c, slot).start()
                scatter(1, c, slot).start()
            return carry

        lax.fori_loop(0, n_ch // 2, body, 0)
        scatter(0, n_ch - 1, 1).wait()
        scatter(1, n_ch - 1, 1).wait()

    return k(h, dest.reshape(2, n_workers, n_ch, SC_ROWS))


def _sc_gather_rows(table, idx):
    B = idx.shape[0]
    n_workers = SC_CORES * SC_SUBCORES
    per_w = B // n_workers
    n_ch = per_w // SC_ROWS
    assert per_w * n_workers == B and n_ch * SC_ROWS == per_w and n_ch % 2 == 0
    mesh = plsc.VectorSubcoreMesh(core_axis_name="c", subcore_axis_name="s")

    @functools.partial(
        pl.kernel, mesh=mesh,
        out_type=jax.ShapeDtypeStruct((B,) + table.shape[1:], table.dtype),
        scratch_types=[
            pltpu.VMEM((n_ch, SC_ROWS), jnp.int32),
            pltpu.VMEM((2, SC_ROWS) + table.shape[1:], table.dtype),
            pltpu.SemaphoreType.DMA((2,)),
            pltpu.SemaphoreType.DMA((2,)),
        ],
    )
    def k(table_hbm, idx_hbm, out_hbm, idx_v, rows_v, gsem, wsem):
        wid = _sc_worker()
        base = wid * per_w
        pltpu.sync_copy(idx_hbm.at[wid], idx_v)

        def gather(c, slot):
            return pltpu.make_async_copy(table_hbm.at[idx_v.at[c]], rows_v.at[slot], gsem.at[slot])

        def writeback(c, slot):
            return pltpu.make_async_copy(rows_v.at[slot], out_hbm.at[pl.ds(base + c * SC_ROWS, SC_ROWS)],
                                         wsem.at[slot])

        gather(0, 0).start()

        def body(g, carry):
            for slot in range(2):
                c = 2 * g + slot
                gather(c, slot).wait()

                @pl.when(c >= 1)
                def _():
                    writeback(c - 1, 1 - slot).wait()

                @pl.when(c + 1 < n_ch)
                def _():
                    gather(c + 1, 1 - slot).start()

                writeback(c, slot).start()
            return carry

        lax.fori_loop(0, n_ch // 2, body, 0)
        writeback(n_ch - 1, 1).wait()

    return k(table, idx.reshape(n_workers, n_ch, SC_ROWS))


def _expert_kernel(blk_exp_ref, n_used_ref, n_valid_ref, xs_ref, wg_ref, wu_ref, wd_ref, y_ref,
                   wg_bf, wu_bf, wd_bf):
    b = pl.program_id(0)
    used = b < n_used_ref[0]
    new_expert = (b == 0) | (blk_exp_ref[b] != blk_exp_ref[jnp.maximum(b - 1, 0)])

    @pl.when(used & new_expert)
    def _():
        wg_bf[...] = wg_ref[0].astype(BF16)
        wu_bf[...] = wu_ref[0].astype(BF16)
        wd_bf[...] = wd_ref[0].astype(BF16)

    @pl.when(used)
    def _():
        x = _load_rows(xs_ref, (), RB)
        row = lax.broadcasted_iota(jnp.int32, (RB, 1), 0)
        xb = jnp.where(row < n_valid_ref[b], x, jnp.zeros_like(x))
        g = jnp.dot(xb, wg_bf[...], preferred_element_type=F32)
        u = jnp.dot(xb, wu_bf[...], preferred_element_type=F32)
        hid = (g * _sigmoid(g) * u).astype(BF16)
        _store_rows(y_ref, (), jnp.dot(hid, wd_bf[...], preferred_element_type=F32))

    @pl.when(jnp.logical_not(used))
    def _():
        y_ref[...] = jnp.zeros(y_ref.shape, y_ref.dtype)


def _experts(blk_exp, n_used, n_valid, xs, wg, wu, wd):
    n_blk = blk_exp.shape[0]
    grid_spec = pltpu.PrefetchScalarGridSpec(
        num_scalar_prefetch=3,
        grid=(n_blk,),
        in_specs=[
            pl.BlockSpec((RB * ROW_TILE, LANES), lambda b, be, nu, nv: (b, 0)),
            pl.BlockSpec((1, D_MODEL, D_EXPERT), lambda b, be, nu, nv: (be[b], 0, 0)),
            pl.BlockSpec((1, D_MODEL, D_EXPERT), lambda b, be, nu, nv: (be[b], 0, 0)),
            pl.BlockSpec((1, D_EXPERT, D_MODEL), lambda b, be, nu, nv: (be[b], 0, 0)),
        ],
        out_specs=pl.BlockSpec((RB * ROW_TILE, LANES), lambda b, be, nu, nv: (b, 0)),
        scratch_shapes=[
            pltpu.VMEM((D_MODEL, D_EXPERT), BF16),
            pltpu.VMEM((D_MODEL, D_EXPERT), BF16),
            pltpu.VMEM((D_EXPERT, D_MODEL), BF16),
        ],
    )
    return pl.pallas_call(
        _expert_kernel,
        grid_spec=grid_spec,
        out_shape=jax.ShapeDtypeStruct(xs.shape, xs.dtype),
        compiler_params=pltpu.CompilerParams(
            dimension_semantics=("arbitrary",), vmem_limit_bytes=VMEM_LIMIT),
        name="expert_mlp",
    )(blk_exp, n_used, n_valid, xs, wg, wu, wd)


def _combine_kernel(x1_ref, y0_ref, y1_ref, w_ref, o_ref):
    tm = x1_ref.shape[0]
    y0 = _load_rows(y0_ref, (), tm).astype(F32)
    y1 = _load_rows(y1_ref, (), tm).astype(F32)
    w = w_ref[...].T
    o_ref[...] = x1_ref[...] + (y0 * w[:, 0:1] + y1 * w[:, 1:2])


def _combine(x1, y, wts):
    T = x1.shape[0]
    tm = TM_COMB
    row = lambda i: (i, 0)
    return pl.pallas_call(
        _combine_kernel,
        grid=(T // tm,),
        in_specs=[
            pl.BlockSpec((tm, D_MODEL), row),
            pl.BlockSpec((tm * ROW_TILE, LANES), row),
            pl.BlockSpec((tm * ROW_TILE, LANES), lambda i: (i + T // tm, 0)),
            pl.BlockSpec((SUBLANES, tm), lambda i: (0, i)),
        ],
        out_specs=pl.BlockSpec((tm, D_MODEL), row),
        out_shape=jax.ShapeDtypeStruct((T, D_MODEL), F32),
        compiler_params=pltpu.CompilerParams(
            dimension_semantics=("parallel",), vmem_limit_bytes=VMEM_LIMIT),
        name="moe_combine",
    )(x1, y, y, wts)


def kernel(x, attn_norm_g, w_in, conv_dw_w, conv_dw_b, conv_ln_g, conv_ln_b, w_conv_out,
           q_norm_g, k_norm_g, lambda_q1, lambda_k1, lambda_q2, lambda_k2, subln_g,
           w_attn_out, w_out, ffn_norm_g, w_router_group, b_router_group,
           w_router_expert, b_router_expert, w_gate_e, w_up_e, w_down_e):
    B, S, D = x.shape
    T = B * S
    l = 0
    x2 = x.reshape(T, D)

    reps = COL_Q // DA_HEAD_DIM
    qg = jnp.tile(q_norm_g[l], reps).reshape(1, COL_Q)
    kg = jnp.tile(k_norm_g[l], reps).reshape(1, COL_K)
    grp = jnp.arange(NORM_BLOCK, dtype=jnp.int32) // DA_HEAD_DIM
    bd = jnp.where(grp[:, None] == grp[None, :], 1.0 / DA_HEAD_DIM, 0.0).astype(BF16)

    conv_act, q, k, v, gates = _in_projection(
        x2, attn_norm_g[l].reshape(1, D), w_in[l].astype(BF16), qg, kg, bd, conv_dw_w[l],
        conv_dw_b[l].reshape(1, CONV_CH), conv_ln_g[l].reshape(1, CONV_CH), conv_ln_b[l].reshape(1, CONV_CH), S)

    lam_vecs = jnp.stack([lambda_q1[l], lambda_k1[l], lambda_q2[l], lambda_k2[l]]).astype(F32)
    attn_o = _diff_attention(q.reshape(B, S, COL_Q), k.reshape(B, S, COL_K),
                             v.reshape(B, S, COL_V), lam_vecs, subln_g[l].reshape(1, DA_V_DIM))

    e_lo, e_hi = ROUTER_EXPERT_COL, ROUTER_EXPERT_COL + N_EXPERTS
    wr = jnp.zeros((D, ROUTER_LANES), F32)
    wr = wr.at[:, :N_GROUPS].set(w_router_group[l]).at[:, e_lo:e_hi].set(w_router_expert[l])
    br = jnp.zeros((1, ROUTER_LANES), F32)
    br = br.at[0, :N_GROUPS].set(b_router_group[l]).at[0, e_lo:e_hi].set(b_router_expert[l])

    wr_hi = wr.astype(BF16)
    wr_lo = (wr - wr_hi.astype(F32)).astype(BF16)
    x1, h2, logits = _merge(conv_act, attn_o.reshape(T, ATTN_W), gates, x2,
                            w_conv_out[l].astype(BF16), w_attn_out[l].astype(BF16),
                            w_out[l].astype(BF16), ffn_norm_g[l].reshape(1, D),
                            jnp.concatenate([wr_hi, wr_lo], axis=1), br)

    dest, wts, blk = _route(logits)
    n_rows = 2 * T + N_EXPERTS * RB
    n_blk = n_rows // RB
    blk_exp, n_valid, n_used = blk[0, :n_blk], blk[1, :n_blk], blk[2, 0:1]

    xs = _sc_scatter_rows(h2.reshape(T, ROW_TILE, LANES), dest, n_rows)
    yb = _experts(blk_exp, n_used, n_valid, xs.reshape(n_rows * ROW_TILE, LANES),
                  w_gate_e[l], w_up_e[l], w_down_e[l])
    y = _sc_gather_rows(yb.reshape(n_rows, ROW_TILE, LANES), dest.reshape(2 * T))

    out = _combine(x1, y.reshape(2 * T * ROW_TILE, LANES), wts)
    return out.reshape(B, S, D)
```

```python
import functools
import math

import jax
import jax.numpy as jnp
from jax import lax
from jax.experimental import pallas as pl
from jax.experimental.pallas import tpu as pltpu
from jax.experimental.pallas import tpu_sc as plsc

F32 = jnp.float32
BF16 = jnp.bfloat16

D_MODEL = 1024
CONV_CH = 512
CONV_WIDTH = 31
DA_HEADS = 4
DA_HEAD_DIM = 64
DA_V_DIM = 128
ATTN_W = 512
N_GROUPS = 4
EXPERTS_PER_GROUP = 8
N_EXPERTS = 32
D_EXPERT = 512
EPS = 1e-6
LAM_INIT = 0.8 - 0.6 * math.exp(-0.3 * 0)

COL_GLU = 2 * CONV_CH
COL_Q = 512
COL_K = 512
COL_V = 512
COL_GATE = 2 * D_MODEL
OFF_Q = COL_GLU
OFF_K = OFF_Q + COL_Q
OFF_V = OFF_K + COL_K
OFF_GATE = OFF_V + COL_V
IN_COLS = OFF_GATE + COL_GATE

ROUTER_LANES = 128
ROUTER_EXPERT_COL = 8
SUBLANES = 8
ROW_TILE = D_MODEL // (2 * 128)
NORM_BLOCK = 256
HALO = 32

TM_PROJ = 512
TM_MERGE = 1024
CONV_CHUNK = 512
ATT_TQ = 1024
ATT_TK = 1024
ATT_RC = 256
ATT_HEADS = 2
LANES = 128
RB = 512
TM_COMB = 1024
TM_ROUTE = 1024
SC_CORES = 2
SC_SUBCORES = 16
SC_ROWS = 64

VMEM_LIMIT = 48 * 1024 * 1024


def _sigmoid(x):
    return 1.0 / (1.0 + jnp.exp(-x))


def _const_spec(shape):
    nd = len(shape)
    return pl.BlockSpec(shape, lambda *_: (0,) * nd, pipeline_mode=pl.Buffered(1))


def _store_rows(ref, lead, val):
    rows, half = val.shape[0], val.shape[1] // 2
    hi = pltpu.bitcast(val[:, :half].astype(BF16).astype(F32), jnp.uint32)
    lo = pltpu.bitcast(val[:, half:].astype(BF16).astype(F32), jnp.uint32)
    words = hi | (lo >> 16)
    for j in range(half // LANES):
        ref[(*lead, pl.ds(j, rows, stride=ROW_TILE), slice(None))] = words[:, j * LANES:(j + 1) * LANES]


def _load_rows(ref, lead, rows):
    words = jnp.concatenate(
        [ref[(*lead, pl.ds(j, rows, stride=ROW_TILE), slice(None))] for j in range(ROW_TILE)], axis=1)
    hi = pltpu.bitcast(words & jnp.uint32(0xFFFF0000), F32).astype(BF16)
    lo = pltpu.bitcast(words << 16, F32).astype(BF16)
    return jnp.concatenate([hi, lo], axis=1)


def _group_mean_sq(t, bd_ref):
    sq = t * t
    hi = sq.astype(BF16)
    lo = (sq - hi.astype(F32)).astype(BF16)
    w = bd_ref.shape[0]
    parts = [jnp.dot(hi[:, c:c + w], bd_ref[...], preferred_element_type=F32)
             + jnp.dot(lo[:, c:c + w], bd_ref[...], preferred_element_type=F32)
             for c in range(0, t.shape[1], w)]
    return jnp.concatenate(parts, axis=1)


def _inproj_kernel(x_ref, g_ref, w_ref, qg_ref, kg_ref, bd_ref, cw_ref, cb_ref, lng_ref, lnb_ref,
                   conv_ref, q_ref, k_ref, v_ref, gate_ref, buf_ref, halo_ref, *, tiles_per_seq):
    i = pl.program_id(0)
    tm = x_ref.shape[0]
    x = x_ref[...]
    ms = jnp.mean(x * x, axis=-1, keepdims=True)
    h = (x * lax.rsqrt(ms + EPS) * g_ref[...]).astype(BF16)

    def proj(lo, width):
        return jnp.dot(h, w_ref[:, lo:lo + width], preferred_element_type=F32)

    a = proj(0, CONV_CH)
    gt = proj(CONV_CH, CONV_CH)
    glu = a * _sigmoid(gt)

    rows = HALO + tm
    buf_ref[0, 0:HALO, :] = jnp.where(i % tiles_per_seq == 0, 0.0, halo_ref[...])
    buf_ref[0, HALO:rows, :] = glu
    halo_ref[...] = glu[tm - HALO:tm, :]
    for r in range(1, SUBLANES):
        buf_ref[r, 0:rows - SUBLANES, :] = buf_ref[0, r:r + rows - SUBLANES, :]
    first = HALO - (CONV_WIDTH - 1)
    for c in range(tm // CONV_CHUNK):
        r0 = c * CONV_CHUNK
        acc = jnp.broadcast_to(cb_ref[...], (CONV_CHUNK, CONV_CH))
        for j in range(CONV_WIDTH):
            shift = (first + j) % SUBLANES
            lo = r0 + first + j - shift
            acc = acc + cw_ref[j:j + 1, :] * buf_ref[shift, lo:lo + CONV_CHUNK, :]
        mu = jnp.mean(acc, axis=-1, keepdims=True)
        d = acc - mu
        var = jnp.mean(d * d, axis=-1, keepdims=True)
        y = d * lax.rsqrt(var + EPS) * lng_ref[...] + lnb_ref[...]
        conv_ref[r0:r0 + CONV_CHUNK, :] = (y * _sigmoid(y)).astype(BF16)

    q = proj(OFF_Q, COL_Q)
    qn = q * lax.rsqrt(_group_mean_sq(q, bd_ref) + EPS) * qg_ref[...]
    q_ref[...] = (qn * (DA_HEAD_DIM ** -0.5)).astype(BF16)

    k = proj(OFF_K, COL_K)
    kn = k * lax.rsqrt(_group_mean_sq(k, bd_ref) + EPS) * kg_ref[...]
    k_ref[...] = kn.astype(BF16)

    v_ref[...] = proj(OFF_V, COL_V).astype(BF16)

    for c in range(COL_GATE // 512):
        gate_ref[:, c * 512:(c + 1) * 512] = _sigmoid(proj(OFF_GATE + c * 512, 512)).astype(BF16)


def _in_projection(x2, g, w_bf, qg, kg, bd, dw_w, dw_b, ln_g, ln_b, seq_len):
    T = x2.shape[0]
    tm = TM_PROJ
    row = lambda i: (i, 0)
    out_shape = (
        jax.ShapeDtypeStruct((T, CONV_CH), BF16),
        jax.ShapeDtypeStruct((T, COL_Q), BF16),
        jax.ShapeDtypeStruct((T, COL_K), BF16),
        jax.ShapeDtypeStruct((T, COL_V), BF16),
        jax.ShapeDtypeStruct((T, COL_GATE), BF16),
    )
    return pl.pallas_call(
        functools.partial(_inproj_kernel, tiles_per_seq=seq_len // tm),
        grid=(T // tm,),
        in_specs=[
            pl.BlockSpec((tm, D_MODEL), row),
            _const_spec((1, D_MODEL)),
            _const_spec((D_MODEL, IN_COLS)),
            _const_spec((1, COL_Q)),
            _const_spec((1, COL_K)),
            _const_spec((NORM_BLOCK, NORM_BLOCK)),
            _const_spec((CONV_WIDTH, CONV_CH)),
            _const_spec((1, CONV_CH)),
            _const_spec((1, CONV_CH)),
            _const_spec((1, CONV_CH)),
        ],
        out_specs=(
            pl.BlockSpec((tm, CONV_CH), row),
            pl.BlockSpec((tm, COL_Q), row),
            pl.BlockSpec((tm, COL_K), row),
            pl.BlockSpec((tm, COL_V), row),
            pl.BlockSpec((tm, COL_GATE), row),
        ),
        out_shape=out_shape,
        scratch_shapes=[pltpu.VMEM((SUBLANES, HALO + tm, CONV_CH), F32), pltpu.VMEM((HALO, CONV_CH), F32)],
        compiler_params=pltpu.CompilerParams(
            dimension_semantics=("arbitrary",), vmem_limit_bytes=VMEM_LIMIT),
        name="in_projection",
    )(x2, g, w_bf, qg, kg, bd, dw_w, dw_b, ln_g, ln_b)


def _attn_kernel(q_ref, k_ref, v_ref, lam_ref, sg_ref, o_ref, qs_ref, m_ref, acc_ref, s_ref, *, tq, tk):
    qi = pl.program_id(2)
    lane = lax.broadcasted_iota(jnp.int32, (tq, DA_V_DIM), 1)
    for hh in range(ATT_HEADS):
        q = q_ref[0, :, hh * DA_V_DIM:(hh + 1) * DA_V_DIM]
        zero = jnp.zeros_like(q)
        qs_ref[hh, 0:tq, :] = jnp.where(lane < DA_HEAD_DIM, q, zero)
        qs_ref[hh, tq:2 * tq, :] = jnp.where(lane >= DA_HEAD_DIM, q, zero)
    m_ref[...] = jnp.full(m_ref.shape, -jnp.inf, F32)
    acc_ref[...] = jnp.zeros(acc_ref.shape, F32)
    ones = jnp.ones((tk, LANES), BF16)
    rc = ATT_RC
    chunks = [slice(r0, r0 + rc) for r0 in range(0, 2 * tq, rc)]
    n_diag = tq // tk

    def scores(hh, j, rows):
        kj = k_ref[0, pl.ds(pl.multiple_of(j * tk, tk), tk), hh * DA_V_DIM:(hh + 1) * DA_V_DIM]
        return lax.dot_general(qs_ref[hh, rows, :], kj, (((1,), (1,)), ((), ())),
                               preferred_element_type=F32)

    def step(j, diag, prefetch):
        for hh in range(ATT_HEADS):
            vj = jnp.concatenate(
                [v_ref[0, pl.ds(pl.multiple_of(j * tk, tk), tk), hh * DA_V_DIM:(hh + 1) * DA_V_DIM], ones], axis=1)
            for rows in chunks:
                r_lo = rows.start % tq
                if diag is None:
                    kw, masked = tk, False
                else:
                    k_lo = diag * tk
                    if k_lo > r_lo + rc - 1:
                        if prefetch:
                            s_ref[hh, rows, :] = scores(hh, j + 1, rows)
                        continue
                    kw = min(tk, -(-(r_lo + rc - k_lo) // (2 * LANES)) * (2 * LANES))
                    masked = k_lo + kw - 1 > r_lo
                s = s_ref[hh, rows, 0:kw]
                if prefetch:
                    s_ref[hh, rows, :] = scores(hh, j + 1, rows)
                if masked:
                    r = lax.broadcasted_iota(jnp.int32, s.shape, 0) + r_lo
                    c = lax.broadcasted_iota(jnp.int32, s.shape, 1) + k_lo
                    s = jnp.where(c <= r, s, -jnp.inf)
                m_old = m_ref[hh, rows, :]
                m_new = jnp.maximum(m_old, jnp.max(s, axis=-1, keepdims=True))
                alpha = jnp.exp(m_old - m_new)
                p = jnp.exp(s - jnp.tile(m_new, (1, kw // LANES)))
                pv = jnp.dot(p.astype(BF16), vj[0:kw, :], preferred_element_type=F32)
                acc_ref[hh, rows, :] = jnp.tile(alpha, (1, 2)) * acc_ref[hh, rows, :] + pv
                m_ref[hh, rows, :] = m_new

    for hh in range(ATT_HEADS):
        for rows in chunks:
            s_ref[hh, rows, :] = scores(hh, 0, rows)

    def body(j, carry):
        step(j, None, True)
        return carry

    n_full = qi * n_diag
    lax.fori_loop(0, n_full, body, 0)
    for t in range(n_diag):
        step(n_full + t, t, t + 1 < n_diag)

    lam = (jnp.exp(jnp.sum(lam_ref[0:1, :] * lam_ref[1:2, :], axis=-1, keepdims=True))
           - jnp.exp(jnp.sum(lam_ref[2:3, :] * lam_ref[3:4, :], axis=-1, keepdims=True))
           + LAM_INIT)
    for hh in range(ATT_HEADS):
        o1 = acc_ref[hh, 0:tq, 0:LANES] / acc_ref[hh, 0:tq, LANES:2 * LANES]
        o2 = acc_ref[hh, tq:2 * tq, 0:LANES] / acc_ref[hh, tq:2 * tq, LANES:2 * LANES]
        o = o1 - lam * o2
        ms = jnp.mean(o * o, axis=-1, keepdims=True)
        on = o * lax.rsqrt(ms + EPS) * sg_ref[...]
        o_ref[0, :, hh * DA_V_DIM:(hh + 1) * DA_V_DIM] = (on * (1.0 - LAM_INIT)).astype(BF16)


def _diff_attention(q3, k3, v3, lam_vecs, subln_g):
    B, S, _ = q3.shape
    tq, tk = ATT_TQ, ATT_TK
    w = ATT_HEADS * DA_V_DIM
    return pl.pallas_call(
        functools.partial(_attn_kernel, tq=tq, tk=tk),
        grid=(B, DA_HEADS // ATT_HEADS, S // tq),
        in_specs=[
            pl.BlockSpec((1, tq, w), lambda b, h, i: (b, i, h)),
            pl.BlockSpec((1, S, w), lambda b, h, i: (b, 0, h)),
            pl.BlockSpec((1, S, w), lambda b, h, i: (b, 0, h)),
            _const_spec((4, DA_HEAD_DIM)),
            _const_spec((1, DA_V_DIM)),
        ],
        out_specs=pl.BlockSpec((1, tq, w), lambda b, h, i: (b, i, h)),
        out_shape=jax.ShapeDtypeStruct((B, S, ATTN_W), BF16),
        scratch_shapes=[
            pltpu.VMEM((ATT_HEADS, 2 * tq, DA_V_DIM), BF16),
            pltpu.VMEM((ATT_HEADS, 2 * tq, LANES), F32),
            pltpu.VMEM((ATT_HEADS, 2 * tq, 2 * LANES), F32),
            pltpu.VMEM((ATT_HEADS, 2 * tq, tk), F32),
        ],
        compiler_params=pltpu.CompilerParams(
            dimension_semantics=("parallel", "parallel", "parallel"),
            vmem_limit_bytes=VMEM_LIMIT),
        name="diff_attention",
    )(q3, k3, v3, lam_vecs, subln_g)


def _merge_kernel(conv_ref, attn_ref, gate_ref, x_ref, wc_ref, wa_ref, wo_ref, fg_ref,
                  wr_ref, br_ref, x1_ref, h2_ref, logit_ref):
    c = jnp.dot(conv_ref[...], wc_ref[...], preferred_element_type=F32)
    a = jnp.dot(attn_ref[...], wa_ref[...], preferred_element_type=F32)
    g0 = gate_ref[:, 0:D_MODEL].astype(F32)
    g1 = gate_ref[:, D_MODEL:2 * D_MODEL].astype(F32)
    merged = (g0 * c + g1 * a).astype(BF16)
    x1 = x_ref[...] + jnp.dot(merged, wo_ref[...], preferred_element_type=F32)
    x1_ref[...] = x1
    ms = jnp.mean(x1 * x1, axis=-1, keepdims=True)
    h2 = x1 * lax.rsqrt(ms + EPS) * fg_ref[...]
    _store_rows(h2_ref, (), h2)
    h2_hi = h2.astype(BF16)
    h2_lo = (h2 - h2_hi.astype(F32)).astype(BF16)
    half = h2.shape[0] // 2
    for rows in (slice(0, half), slice(half, 2 * half)):
        r = (jnp.dot(h2_hi[rows], wr_ref[...], preferred_element_type=F32)
             + jnp.dot(h2_lo[rows], wr_ref[...], preferred_element_type=F32))
        logit_ref[rows, :] = r[:, 0:ROUTER_LANES] + r[:, ROUTER_LANES:2 * ROUTER_LANES] + br_ref[...]


def _merge(conv_act, attn_o, gates, x2, wc, wa, wo, fg, wr, br):
    T = x2.shape[0]
    tm = TM_MERGE
    row = lambda i: (i, 0)
    return pl.pallas_call(
        _merge_kernel,
        grid=(T // tm,),
        in_specs=[
            pl.BlockSpec((tm, CONV_CH), row),
            pl.BlockSpec((tm, ATTN_W), row),
            pl.BlockSpec((tm, COL_GATE), row),
            pl.BlockSpec((tm, D_MODEL), row),
            _const_spec((CONV_CH, D_MODEL)),
            _const_spec((ATTN_W, D_MODEL)),
            _const_spec((D_MODEL, D_MODEL)),
            _const_spec((1, D_MODEL)),
            _const_spec((D_MODEL, 2 * ROUTER_LANES)),
            _const_spec((1, ROUTER_LANES)),
        ],
        out_specs=(
            pl.BlockSpec((tm, D_MODEL), row),
            pl.BlockSpec((tm * ROW_TILE, LANES), row),
            pl.BlockSpec((tm, ROUTER_LANES), row),
        ),
        out_shape=(
            jax.ShapeDtypeStruct((T, D_MODEL), F32),
            jax.ShapeDtypeStruct((T * ROW_TILE, LANES), jnp.uint32),
            jax.ShapeDtypeStruct((T, ROUTER_LANES), F32),
        ),
        compiler_params=pltpu.CompilerParams(
            dimension_semantics=("parallel",), vmem_limit_bytes=VMEM_LIMIT),
        name="merge_out_router",
    )(conv_act, attn_o, gates, x2, wc, wa, wo, fg, wr, br)


def _route_kernel(logit_ref, tri_ref, low_ref, dest_ref, wts_ref, blk_ref,
                  cnt_ref, base_ref, *, n_blk_lanes):
    ph = pl.program_id(0)
    i = pl.program_id(1)
    tm = logit_ref.shape[0]
    lt = logit_ref[...].T
    row8 = lax.broadcasted_iota(jnp.int32, (SUBLANES, tm), 0)

    g = jnp.where(row8 < N_GROUPS, lt[0:SUBLANES, :], -jnp.inf)
    g_max = jnp.max(g, axis=0, keepdims=True)
    g_sum = jnp.sum(jnp.exp(g - g_max), axis=0, keepdims=True)
    g_sel = jnp.min(jnp.where(g == g_max, row8, SUBLANES), axis=0, keepdims=True)
    g_w = 1.0 / g_sum

    e_sel = jnp.zeros((EXPERTS_PER_GROUP, tm), F32)
    for gi in range(N_GROUPS):
        lo = ROUTER_EXPERT_COL + gi * EXPERTS_PER_GROUP
        e_sel = jnp.where(g_sel == gi, lt[lo:lo + EXPERTS_PER_GROUP, :], e_sel)
    e_max = jnp.max(e_sel, axis=0, keepdims=True)
    e_exp = jnp.exp(e_sel - e_max)
    prob = e_exp / jnp.sum(e_exp, axis=0, keepdims=True)
    p1 = jnp.max(prob, axis=0, keepdims=True)
    i1 = jnp.min(jnp.where(prob == p1, row8, SUBLANES), axis=0, keepdims=True)
    rest = jnp.where(row8 == i1, -1.0, prob)
    p2 = jnp.max(rest, axis=0, keepdims=True)
    i2 = jnp.min(jnp.where(rest == p2, row8, SUBLANES), axis=0, keepdims=True)
    denom = p1 + p2
    e0 = g_sel * EXPERTS_PER_GROUP + i1
    e1 = g_sel * EXPERTS_PER_GROUP + i2

    row32 = lax.broadcasted_iota(jnp.int32, (N_EXPERTS, tm), 0)
    oh0 = row32 == e0
    oh1 = row32 == e1
    ohs = jnp.where(oh0 | oh1, 1.0, 0.0)
    tile_cnt = jnp.sum(ohs, axis=1, keepdims=True)

    @pl.when((ph == 0) & (i == 0))
    def _():
        cnt_ref[...] = jnp.zeros(cnt_ref.shape, F32)

    @pl.when(ph == 0)
    def _():
        cnt_ref[...] += tile_cnt

    @pl.when((ph == 1) & (i == 0))
    def _():
        cnt = cnt_ref[...]
        padded = jnp.ceil(cnt * (1.0 / RB)) * RB
        pstart = jnp.dot(low_ref[...], padded, preferred_element_type=F32,
                         precision=lax.Precision.HIGHEST)
        base_ref[...] = pstart
        pad_end = pstart + padded
        row0 = lax.broadcasted_iota(jnp.int32, (N_EXPERTS, n_blk_lanes), 1).astype(F32) * RB
        expert = lax.broadcasted_iota(jnp.int32, (N_EXPERTS, n_blk_lanes), 0).astype(F32)
        blk = jnp.minimum(jnp.sum(jnp.where(pad_end[:, 0:1] <= row0, 1.0, 0.0), axis=0, keepdims=True),
                          N_EXPERTS - 1.0)
        valid_end = jnp.sum(jnp.where(expert == blk, (pstart + cnt)[:, 0:1], 0.0), axis=0, keepdims=True)
        n_valid = jnp.clip(valid_end - row0[0:1, :], 0.0, float(RB))
        n_used = jnp.max(pad_end[:, 0:1], axis=0, keepdims=True) * (1.0 / RB) + jnp.zeros_like(blk)
        blk_ref[...] = jnp.zeros(blk_ref.shape, jnp.int32)
        blk_ref[0:1, :] = blk.astype(jnp.int32)
        blk_ref[1:2, :] = n_valid.astype(jnp.int32)
        blk_ref[2:3, :] = n_used.astype(jnp.int32)

    @pl.when(ph == 1)
    def _():
        before = jnp.dot(ohs.astype(BF16), tri_ref[...], preferred_element_type=F32)
        pos = base_ref[:, 0:1] + before
        d0 = jnp.sum(jnp.where(oh0, pos, 0.0), axis=0, keepdims=True)
        d1 = jnp.sum(jnp.where(oh1, pos, 0.0), axis=0, keepdims=True)
        dest_ref[0:1, :] = d0.astype(jnp.int32)
        dest_ref[1:2, :] = d1.astype(jnp.int32)
        wts_ref[...] = jnp.zeros(wts_ref.shape, F32)
        wts_ref[0:1, :] = p1 / denom * g_w
        wts_ref[1:2, :] = p2 / denom * g_w
        base_ref[...] += tile_cnt


def _route(logits):
    T = logits.shape[0]
    tm = TM_ROUTE
    n_rows = 2 * T + N_EXPERTS * RB
    n_blk = n_rows // RB
    n_blk_lanes = -(-n_blk // LANES) * LANES
    r = jnp.arange(tm, dtype=jnp.int32)
    tri = (r[:, None] < r[None, :]).astype(BF16)
    e = jnp.arange(N_EXPERTS, dtype=jnp.int32)
    low = (e[None, :] < e[:, None]).astype(F32)
    return pl.pallas_call(
        functools.partial(_route_kernel, n_blk_lanes=n_blk_lanes),
        grid=(2, T // tm),
        in_specs=[
            pl.BlockSpec((tm, ROUTER_LANES), lambda ph, i: (i, 0)),
            _const_spec((tm, tm)),
            _const_spec((N_EXPERTS, N_EXPERTS)),
        ],
        out_specs=(
            pl.BlockSpec((2, tm), lambda ph, i: (0, i * ph)),
            pl.BlockSpec((SUBLANES, tm), lambda ph, i: (0, i * ph)),
            pl.BlockSpec((SUBLANES, n_blk_lanes), lambda ph, i: (0, 0)),
        ),
        out_shape=(
            jax.ShapeDtypeStruct((2, T), jnp.int32),
            jax.ShapeDtypeStruct((SUBLANES, T), F32),
            jax.ShapeDtypeStruct((SUBLANES, n_blk_lanes), jnp.int32),
        ),
        scratch_shapes=[pltpu.VMEM((N_EXPERTS, LANES), F32), pltpu.VMEM((N_EXPERTS, LANES), F32)],
        compiler_params=pltpu.CompilerParams(
            dimension_semantics=("arbitrary", "arbitrary"), vmem_limit_bytes=VMEM_LIMIT),
        name="route_plan",
    )(logits, tri, low)


def _sc_worker():
    return lax.axis_index("s") * SC_CORES + lax.axis_index("c")


def _sc_scatter_rows(h, dest, n_rows):
    T = h.shape[0]
    n_workers = SC_CORES * SC_SUBCORES
    per_w = T // n_workers
    n_ch = per_w // SC_ROWS
    assert per_w * n_workers == T and n_ch * SC_ROWS == per_w and n_ch % 2 == 0
    mesh = plsc.VectorSubcoreMesh(core_axis_name="c", subcore_axis_name="s")

    @functools.partial(
        pl.kernel, mesh=mesh,
        out_type=jax.ShapeDtypeStruct((n_rows,) + h.shape[1:], h.dtype),
        scratch_types=[
            pltpu.VMEM((2, n_ch, SC_ROWS), jnp.int32),
            pltpu.VMEM((2, SC_ROWS) + h.shape[1:], h.dtype),
            pltpu.SemaphoreType.DMA((2,)),
            pltpu.SemaphoreType.DMA((2,)),
        ],
    )
    def k(h_hbm, idx_hbm, out_hbm, idx_v, rows_v, lsem, ssem):
        wid = _sc_worker()
        base = wid * per_w
        pltpu.sync_copy(idx_hbm.at[0, wid], idx_v.at[0])
        pltpu.sync_copy(idx_hbm.at[1, wid], idx_v.at[1])

        def load(c, slot):
            return pltpu.make_async_copy(h_hbm.at[pl.ds(base + c * SC_ROWS, SC_ROWS)], rows_v.at[slot],
                                         lsem.at[slot])

        def scatter(slot_k, c, slot):
            return pltpu.make_async_copy(rows_v.at[slot], out_hbm.at[idx_v.at[slot_k, c]], ssem.at[slot])

        load(0, 0).start()

        def body(g, carry):
            for slot in range(2):
                c = 2 * g + slot
                load(c, slot).wait()

                @pl.when(c >= 1)
                def _():
                    scatter(0, c - 1, 1 - slot).wait()
                    scatter(1, c - 1, 1 - slot).wait()

                @pl.when(c + 1 < n_ch)
                def _():
                    load(c + 1, 1 - slot).start()

                scatter(0, c, slot).start()
                scatter(1, c, slot).start()
            return carry

        lax.fori_loop(0, n_ch // 2, body, 0)
        scatter(0, n_ch - 1, 1).wait()
        scatter(1, n_ch - 1, 1).wait()

    return k(h, dest.reshape(2, n_workers, n_ch, SC_ROWS))


def _sc_gather_rows(table, idx):
    B = idx.shape[0]
    n_workers = SC_CORES * SC_SUBCORES
    per_w = B // n_workers
    n_ch = per_w // SC_ROWS
    assert per_w * n_workers == B and n_ch * SC_ROWS == per_w and n_ch % 2 == 0
    mesh = plsc.VectorSubcoreMesh(core_axis_name="c", subcore_axis_name="s")

    @functools.partial(
        pl.kernel, mesh=mesh,
        out_type=jax.ShapeDtypeStruct((B,) + table.shape[1:], table.dtype),
        scratch_types=[
            pltpu.VMEM((n_ch, SC_ROWS), jnp.int32),
            pltpu.VMEM((2, SC_ROWS) + table.shape[1:], table.dtype),
            pltpu.SemaphoreType.DMA((2,)),
            pltpu.SemaphoreType.DMA((2,)),
        ],
    )
    def k(table_hbm, idx_hbm, out_hbm, idx_v, rows_v, gsem, wsem):
        wid = _sc_worker()
        base = wid * per_w
        pltpu.sync_copy(idx_hbm.at[wid], idx_v)

        def gather(c, slot):
            return pltpu.make_async_copy(table_hbm.at[idx_v.at[c]], rows_v.at[slot], gsem.at[slot])

        def writeback(c, slot):
            return pltpu.make_async_copy(rows_v.at[slot], out_hbm.at[pl.ds(base + c * SC_ROWS, SC_ROWS)],
                                         wsem.at[slot])

        gather(0, 0).start()

        def body(g, carry):
            for slot in range(2):
                c = 2 * g + slot
                gather(c, slot).wait()

                @pl.when(c >= 1)
                def _():
                    writeback(c - 1, 1 - slot).wait()

                @pl.when(c + 1 < n_ch)
                def _():
                    gather(c + 1, 1 - slot).start()

                writeback(c, slot).start()
            return carry

        lax.fori_loop(0, n_ch // 2, body, 0)
        writeback(n_ch - 1, 1).wait()

    return k(table, idx.reshape(n_workers, n_ch, SC_ROWS))


def _expert_kernel(blk_exp_ref, n_used_ref, n_valid_ref, xs_ref, wg_ref, wu_ref, wd_ref, y_ref,
                   wg_bf, wu_bf, wd_bf):
    b = pl.program_id(0)
    used = b < n_used_ref[0]
    new_expert = (b == 0) | (blk_exp_ref[b] != blk_exp_ref[jnp.maximum(b - 1, 0)])

    @pl.when(used & new_expert)
    def _():
        wg_bf[...] = wg_ref[0].astype(BF16)
        wu_bf[...] = wu_ref[0].astype(BF16)
        wd_bf[...] = wd_ref[0].astype(BF16)

    @pl.when(used)
    def _():
        x = _load_rows(xs_ref, (), RB)
        row = lax.broadcasted_iota(jnp.int32, (RB, 1), 0)
        xb = jnp.where(row < n_valid_ref[b], x, jnp.zeros_like(x))
        g = jnp.dot(xb, wg_bf[...], preferred_element_type=F32)
        u = jnp.dot(xb, wu_bf[...], preferred_element_type=F32)
        hid = (g * _sigmoid(g) * u).astype(BF16)
        _store_rows(y_ref, (), jnp.dot(hid, wd_bf[...], preferred_element_type=F32))

    @pl.when(jnp.logical_not(used))
    def _():
        y_ref[...] = jnp.zeros(y_ref.shape, y_ref.dtype)


def _experts(blk_exp, n_used, n_valid, xs, wg, wu, wd):
    n_blk = blk_exp.shape[0]
    grid_spec = pltpu.PrefetchScalarGridSpec(
        num_scalar_prefetch=3,
        grid=(n_blk,),
        in_specs=[
            pl.BlockSpec((RB * ROW_TILE, LANES), lambda b, be, nu, nv: (b, 0)),
            pl.BlockSpec((1, D_MODEL, D_EXPERT), lambda b, be, nu, nv: (be[b], 0, 0)),
            pl.BlockSpec((1, D_MODEL, D_EXPERT), lambda b, be, nu, nv: (be[b], 0, 0)),
            pl.BlockSpec((1, D_EXPERT, D_MODEL), lambda b, be, nu, nv: (be[b], 0, 0)),
        ],
        out_specs=pl.BlockSpec((RB * ROW_TILE, LANES), lambda b, be, nu, nv: (b, 0)),
        scratch_shapes=[
            pltpu.VMEM((D_MODEL, D_EXPERT), BF16),
            pltpu.VMEM((D_MODEL, D_EXPERT), BF16),
            pltpu.VMEM((D_EXPERT, D_MODEL), BF16),
        ],
    )
    return pl.pallas_call(
        _expert_kernel,
        grid_spec=grid_spec,
        out_shape=jax.ShapeDtypeStruct(xs.shape, xs.dtype),
        compiler_params=pltpu.CompilerParams(
            dimension_semantics=("arbitrary",), vmem_limit_bytes=VMEM_LIMIT),
        name="expert_mlp",
    )(blk_exp, n_used, n_valid, xs, wg, wu, wd)


def _combine_kernel(x1_ref, y0_ref, y1_ref, w_ref, o_ref):
    tm = x1_ref.shape[0]
    y0 = _load_rows(y0_ref, (), tm).astype(F32)
    y1 = _load_rows(y1_ref, (), tm).astype(F32)
    w = w_ref[...].T
    o_ref[...] = x1_ref[...] + (y0 * w[:, 0:1] + y1 * w[:, 1:2])


def _combine(x1, y, wts):
    T = x1.shape[0]
    tm = TM_COMB
    row = lambda i: (i, 0)
    return pl.pallas_call(
        _combine_kernel,
        grid=(T // tm,),
        in_specs=[
            pl.BlockSpec((tm, D_MODEL), row),
            pl.BlockSpec((tm * ROW_TILE, LANES), row),
            pl.BlockSpec((tm * ROW_TILE, LANES), lambda i: (i + T // tm, 0)),
            pl.BlockSpec((SUBLANES, tm), lambda i: (0, i)),
        ],
        out_specs=pl.BlockSpec((tm, D_MODEL), row),
        out_shape=jax.ShapeDtypeStruct((T, D_MODEL), F32),
        compiler_params=pltpu.CompilerParams(
            dimension_semantics=("parallel",), vmem_limit_bytes=VMEM_LIMIT),
        name="moe_combine",
    )(x1, y, y, wts)


def kernel(x, attn_norm_g, w_in, conv_dw_w, conv_dw_b, conv_ln_g, conv_ln_b, w_conv_out,
           q_norm_g, k_norm_g, lambda_q1, lambda_k1, lambda_q2, lambda_k2, subln_g,
           w_attn_out, w_out, ffn_norm_g, w_router_group, b_router_group,
           w_router_expert, b_router_expert, w_gate_e, w_up_e, w_down_e):
    B, S, D = x.shape
    T = B * S
    l = 0
    x2 = x.reshape(T, D)

    reps = COL_Q // DA_HEAD_DIM
    qg = jnp.tile(q_norm_g[l], reps).reshape(1, COL_Q)
    kg = jnp.tile(k_norm_g[l], reps).reshape(1, COL_K)
    grp = jnp.arange(NORM_BLOCK, dtype=jnp.int32) // DA_HEAD_DIM
    bd = jnp.where(grp[:, None] == grp[None, :], 1.0 / DA_HEAD_DIM, 0.0).astype(BF16)

    conv_act, q, k, v, gates = _in_projection(
        x2, attn_norm_g[l].reshape(1, D), w_in[l].astype(BF16), qg, kg, bd, conv_dw_w[l],
        conv_dw_b[l].reshape(1, CONV_CH), conv_ln_g[l].reshape(1, CONV_CH), conv_ln_b[l].reshape(1, CONV_CH), S)

    lam_vecs = jnp.stack([lambda_q1[l], lambda_k1[l], lambda_q2[l], lambda_k2[l]]).astype(F32)
    attn_o = _diff_attention(q.reshape(B, S, COL_Q), k.reshape(B, S, COL_K),
                             v.reshape(B, S, COL_V), lam_vecs, subln_g[l].reshape(1, DA_V_DIM))

    e_lo, e_hi = ROUTER_EXPERT_COL, ROUTER_EXPERT_COL + N_EXPERTS
    wr = jnp.zeros((D, ROUTER_LANES), F32)
    wr = wr.at[:, :N_GROUPS].set(w_router_group[l]).at[:, e_lo:e_hi].set(w_router_expert[l])
    br = jnp.zeros((1, ROUTER_LANES), F32)
    br = br.at[0, :N_GROUPS].set(b_router_group[l]).at[0, e_lo:e_hi].set(b_router_expert[l])

    wr_hi = wr.astype(BF16)
    wr_lo = (wr - wr_hi.astype(F32)).astype(BF16)
    x1, h2, logits = _merge(conv_act, attn_o.reshape(T, ATTN_W), gates, x2,
                            w_conv_out[l].astype(BF16), w_attn_out[l].astype(BF16),
                            w_out[l].astype(BF16), ffn_norm_g[l].reshape(1, D),
                            jnp.concatenate([wr_hi, wr_lo], axis=1), br)

    dest, wts, blk = _route(logits)
    n_rows = 2 * T + N_EXPERTS * RB
    n_blk = n_rows // RB
    blk_exp, n_valid, n_used = blk[0, :n_blk], blk[1, :n_blk], blk[2, 0:1]

    xs = _sc_scatter_rows(h2.reshape(T, ROW_TILE, LANES), dest, n_rows)
    yb = _experts(blk_exp, n_used, n_valid, xs.reshape(n_rows * ROW_TILE, LANES),
                  w_gate_e[l], w_up_e[l], w_down_e[l])
    y = _sc_gather_rows(yb.reshape(n_rows, ROW_TILE, LANES), dest.reshape(2 * T))

    out = _combine(x1, y.reshape(2 * T * ROW_TILE, LANES), wts)
    return out.reshape(B, S, D)
```

```python
import functools
import math

import jax
import jax.numpy as jnp
from jax import lax
from jax.experimental import pallas as pl
from jax.experimental.pallas import tpu as pltpu
from jax.experimental.pallas import tpu_sc as plsc

F32 = jnp.float32
BF16 = jnp.bfloat16

D_MODEL = 1024
CONV_CH = 512
CONV_WIDTH = 31
DA_HEADS = 4
DA_HEAD_DIM = 64
DA_V_DIM = 128
ATTN_W = 512
N_GROUPS = 4
EXPERTS_PER_GROUP = 8
N_EXPERTS = 32
D_EXPERT = 512
EPS = 1e-6
LAM_INIT = 0.8 - 0.6 * math.exp(-0.3 * 0)

COL_GLU = 2 * CONV_CH
COL_Q = 512
COL_K = 512
COL_V = 512
COL_GATE = 2 * D_MODEL
OFF_Q = COL_GLU
OFF_K = OFF_Q + COL_Q
OFF_V = OFF_K + COL_K
OFF_GATE = OFF_V + COL_V
IN_COLS = OFF_GATE + COL_GATE

ROUTER_LANES = 128
ROUTER_EXPERT_COL = 8
SUBLANES = 8
ROW_TILE = D_MODEL // (2 * 128)
NORM_BLOCK = 256
HALO = 32

TM_PROJ = 512
TM_MERGE = 1024
CONV_CHUNK = 512
ATT_TQ = 1024
ATT_TK = 1024
ATT_RC = 256
ATT_HEADS = 2
LANES = 128
RB = 512
TM_COMB = 1024
TM_ROUTE = 1024
SC_CORES = 2
SC_SUBCORES = 16
SC_ROWS = 64

VMEM_LIMIT = 48 * 1024 * 1024


def _sigmoid(x):
    return 1.0 / (1.0 + jnp.exp(-x))


def _const_spec(shape):
    nd = len(shape)
    return pl.BlockSpec(shape, lambda *_: (0,) * nd, pipeline_mode=pl.Buffered(1))


def _store_rows(ref, lead, val):
    rows, half = val.shape[0], val.shape[1] // 2
    hi = pltpu.bitcast(val[:, :half].astype(BF16).astype(F32), jnp.uint32)
    lo = pltpu.bitcast(val[:, half:].astype(BF16).astype(F32), jnp.uint32)
    words = hi | (lo >> 16)
    for j in range(half // LANES):
        ref[(*lead, pl.ds(j, rows, stride=ROW_TILE), slice(None))] = words[:, j * LANES:(j + 1) * LANES]


def _load_rows(ref, lead, rows):
    words = jnp.concatenate(
        [ref[(*lead, pl.ds(j, rows, stride=ROW_TILE), slice(None))] for j in range(ROW_TILE)], axis=1)
    hi = pltpu.bitcast(words & jnp.uint32(0xFFFF0000), F32).astype(BF16)
    lo = pltpu.bitcast(words << 16, F32).astype(BF16)
    return jnp.concatenate([hi, lo], axis=1)


def _group_mean_sq(t, bd_ref):
    sq = t * t
    hi = sq.astype(BF16)
    lo = (sq - hi.astype(F32)).astype(BF16)
    w = bd_ref.shape[0]
    parts = [jnp.dot(hi[:, c:c + w], bd_ref[...], preferred_element_type=F32)
             + jnp.dot(lo[:, c:c + w], bd_ref[...], preferred_element_type=F32)
             for c in range(0, t.shape[1], w)]
    return jnp.concatenate(parts, axis=1)


def _inproj_kernel(x_ref, g_ref, w_ref, qg_ref, kg_ref, bd_ref, cw_ref, cb_ref, lng_ref, lnb_ref,
                   conv_ref, q_ref, k_ref, v_ref, gate_ref, buf_ref, halo_ref, *, tiles_per_seq):
    i = pl.program_id(0)
    tm = x_ref.shape[0]
    x = x_ref[...]
    ms = jnp.mean(x * x, axis=-1, keepdims=True)
    h = (x * lax.rsqrt(ms + EPS) * g_ref[...]).astype(BF16)

    def proj(lo, width):
        return jnp.dot(h, w_ref[:, lo:lo + width], preferred_element_type=F32)

    a = proj(0, CONV_CH)
    gt = proj(CONV_CH, CONV_CH)
    glu = a * _sigmoid(gt)

    rows = HALO + tm
    buf_ref[0, 0:HALO, :] = jnp.where(i % tiles_per_seq == 0, 0.0, halo_ref[...])
    buf_ref[0, HALO:rows, :] = glu
    halo_ref[...] = glu[tm - HALO:tm, :]
    for r in range(1, SUBLANES):
        buf_ref[r, 0:rows - SUBLANES, :] = buf_ref[0, r:r + rows - SUBLANES, :]
    first = HALO - (CONV_WIDTH - 1)
    for c in range(tm // CONV_CHUNK):
        r0 = c * CONV_CHUNK
        acc = jnp.broadcast_to(cb_ref[...], (CONV_CHUNK, CONV_CH))
        for j in range(CONV_WIDTH):
            shift = (first + j) % SUBLANES
            lo = r0 + first + j - shift
            acc = acc + cw_ref[j:j + 1, :] * buf_ref[shift, lo:lo + CONV_CHUNK, :]
        mu = jnp.mean(acc, axis=-1, keepdims=True)
        d = acc - mu
        var = jnp.mean(d * d, axis=-1, keepdims=True)
        y = d * lax.rsqrt(var + EPS) * lng_ref[...] + lnb_ref[...]
        conv_ref[r0:r0 + CONV_CHUNK, :] = (y * _sigmoid(y)).astype(BF16)

    q = proj(OFF_Q, COL_Q)
    qn = q * lax.rsqrt(_group_mean_sq(q, bd_ref) + EPS) * qg_ref[...]
    q_ref[...] = (qn * (DA_HEAD_DIM ** -0.5)).astype(BF16)

    k = proj(OFF_K, COL_K)
    kn = k * lax.rsqrt(_group_mean_sq(k, bd_ref) + EPS) * kg_ref[...]
    k_ref[...] = kn.astype(BF16)

    v_ref[...] = proj(OFF_V, COL_V).astype(BF16)

    for c in range(COL_GATE // 512):
        gate_ref[:, c * 512:(c + 1) * 512] = _sigmoid(proj(OFF_GATE + c * 512, 512)).astype(BF16)


def _in_projection(x2, g, w_bf, qg, kg, bd, dw_w, dw_b, ln_g, ln_b, seq_len):
    T = x2.shape[0]
    tm = TM_PROJ
    row = lambda i: (i, 0)
    out_shape = (
        jax.ShapeDtypeStruct((T, CONV_CH), BF16),
        jax.ShapeDtypeStruct((T, COL_Q), BF16),
        jax.ShapeDtypeStruct((T, COL_K), BF16),
        jax.ShapeDtypeStruct((T, COL_V), BF16),
        jax.ShapeDtypeStruct((T, COL_GATE), BF16),
    )
    return pl.pallas_call(
        functools.partial(_inproj_kernel, tiles_per_seq=seq_len // tm),
        grid=(T // tm,),
        in_specs=[
            pl.BlockSpec((tm, D_MODEL), row),
            _const_spec((1, D_MODEL)),
            _const_spec((D_MODEL, IN_COLS)),
            _const_spec((1, COL_Q)),
            _const_spec((1, COL_K)),
            _const_spec((NORM_BLOCK, NORM_BLOCK)),
            _const_spec((CONV_WIDTH, CONV_CH)),
            _const_spec((1, CONV_CH)),
            _const_spec((1, CONV_CH)),
            _const_spec((1, CONV_CH)),
        ],
        out_specs=(
            pl.BlockSpec((tm, CONV_CH), row),
            pl.BlockSpec((tm, COL_Q), row),
            pl.BlockSpec((tm, COL_K), row),
            pl.BlockSpec((tm, COL_V), row),
            pl.BlockSpec((tm, COL_GATE), row),
        ),
        out_shape=out_shape,
        scratch_shapes=[pltpu.VMEM((SUBLANES, HALO + tm, CONV_CH), F32), pltpu.VMEM((HALO, CONV_CH), F32)],
        compiler_params=pltpu.CompilerParams(
            dimension_semantics=("arbitrary",), vmem_limit_bytes=VMEM_LIMIT),
        name="in_projection",
    )(x2, g, w_bf, qg, kg, bd, dw_w, dw_b, ln_g, ln_b)


def _attn_kernel(q_ref, k_ref, v_ref, lam_ref, sg_ref, o_ref, qs_ref, m_ref, acc_ref, s_ref, *, tq, tk):
    qi = pl.program_id(2)
    lane = lax.broadcasted_iota(jnp.int32, (tq, DA_V_DIM), 1)
    for hh in range(ATT_HEADS):
        q = q_ref[0, :, hh * DA_V_DIM:(hh + 1) * DA_V_DIM]
        zero = jnp.zeros_like(q)
        qs_ref[hh, 0:tq, :] = jnp.where(lane < DA_HEAD_DIM, q, zero)
        qs_ref[hh, tq:2 * tq, :] = jnp.where(lane >= DA_HEAD_DIM, q, zero)
    ones = jnp.ones((tk, LANES), BF16)
    rc = ATT_RC
    chunks = [slice(r0, r0 + rc) for r0 in range(0, 2 * tq, rc)]
    n_diag = tq // tk

    def scores(hh, j, rows):
        kj = k_ref[0, pl.ds(pl.multiple_of(j * tk, tk), tk), hh * DA_V_DIM:(hh + 1) * DA_V_DIM]
        return lax.dot_general(qs_ref[hh, rows, :], kj, (((1,), (1,)), ((), ())),
                               preferred_element_type=F32)

    def step(j, diag, prefetch, first):
        assert not first or diag in (None, 0)
        for hh in range(ATT_HEADS):
            vj = jnp.concatenate(
                [v_ref[0, pl.ds(pl.multiple_of(j * tk, tk), tk), hh * DA_V_DIM:(hh + 1) * DA_V_DIM], ones], axis=1)
            for rows in chunks:
                r_lo = rows.start % tq
                if diag is None:
                    kw, masked = tk, False
                else:
                    k_lo = diag * tk
                    if k_lo > r_lo + rc - 1:
                        if prefetch:
                            s_ref[hh, rows, :] = scores(hh, j + 1, rows)
                        continue
                    kw = min(tk, -(-(r_lo + rc - k_lo) // (2 * LANES)) * (2 * LANES))
                    masked = k_lo + kw - 1 > r_lo
                s = s_ref[hh, rows, 0:kw]
                if prefetch:
                    s_ref[hh, rows, :] = scores(hh, j + 1, rows)
                if masked:
                    r = lax.broadcasted_iota(jnp.int32, s.shape, 0) + r_lo
                    c = lax.broadcasted_iota(jnp.int32, s.shape, 1) + k_lo
                    s = jnp.where(c <= r, s, -jnp.inf)
                m_cur = jnp.max(s, axis=-1, keepdims=True)
                if first:
                    m_new = jnp.broadcast_to(m_cur, (rc, LANES))
                else:
                    m_old = m_ref[hh, rows, :]
                    m_new = jnp.maximum(m_old, m_cur)
                p = jnp.exp(s - jnp.tile(m_new, (1, kw // LANES)))
                pv = jnp.dot(p.astype(BF16), vj[0:kw, :], preferred_element_type=F32)
                if first:
                    acc_ref[hh, rows, :] = pv
                else:
                    alpha = jnp.exp(m_old - m_new)
                    acc_ref[hh, rows, :] = jnp.tile(alpha, (1, 2)) * acc_ref[hh, rows, :] + pv
                m_ref[hh, rows, :] = m_new

    for hh in range(ATT_HEADS):
        for rows in chunks:
            s_ref[hh, rows, :] = scores(hh, 0, rows)

    def body(j, carry):
        step(j, None, True, False)
        return carry

    n_full = qi * n_diag

    def diagonal_tiles(first):
        for t in range(n_diag):
            step(n_full + t, t, t + 1 < n_diag, first and t == 0)

    @pl.when(qi == 0)
    def _():
        diagonal_tiles(True)

    @pl.when(qi > 0)
    def _():
        step(0, None, True, True)
        lax.fori_loop(1, n_full, body, 0)
        diagonal_tiles(False)

    lam = (jnp.exp(jnp.sum(lam_ref[0:1, :] * lam_ref[1:2, :], axis=-1, keepdims=True))
           - jnp.exp(jnp.sum(lam_ref[2:3, :] * lam_ref[3:4, :], axis=-1, keepdims=True))
           + LAM_INIT)
    for hh in range(ATT_HEADS):
        o1 = acc_ref[hh, 0:tq, 0:LANES] / acc_ref[hh, 0:tq, LANES:2 * LANES]
        o2 = acc_ref[hh, tq:2 * tq, 0:LANES] / acc_ref[hh, tq:2 * tq, LANES:2 * LANES]
        o = o1 - lam * o2
        ms = jnp.mean(o * o, axis=-1, keepdims=True)
        on = o * lax.rsqrt(ms + EPS) * sg_ref[...]
        o_ref[0, :, hh * DA_V_DIM:(hh + 1) * DA_V_DIM] = (on * (1.0 - LAM_INIT)).astype(BF16)


def _diff_attention(q3, k3, v3, lam_vecs, subln_g):
    B, S, _ = q3.shape
    tq, tk = ATT_TQ, ATT_TK
    w = ATT_HEADS * DA_V_DIM
    return pl.pallas_call(
        functools.partial(_attn_kernel, tq=tq, tk=tk),
        grid=(B, DA_HEADS // ATT_HEADS, S // tq),
        in_specs=[
            pl.BlockSpec((1, tq, w), lambda b, h, i: (b, i, h)),
            pl.BlockSpec((1, S, w), lambda b, h, i: (b, 0, h)),
            pl.BlockSpec((1, S, w), lambda b, h, i: (b, 0, h)),
            _const_spec((4, DA_HEAD_DIM)),
            _const_spec((1, DA_V_DIM)),
        ],
        out_specs=pl.BlockSpec((1, tq, w), lambda b, h, i: (b, i, h)),
        out_shape=jax.ShapeDtypeStruct((B, S, ATTN_W), BF16),
        scratch_shapes=[
            pltpu.VMEM((ATT_HEADS, 2 * tq, DA_V_DIM), BF16),
            pltpu.VMEM((ATT_HEADS, 2 * tq, LANES), F32),
            pltpu.VMEM((ATT_HEADS, 2 * tq, 2 * LANES), F32),
            pltpu.VMEM((ATT_HEADS, 2 * tq, tk), F32),
        ],
        compiler_params=pltpu.CompilerParams(
            dimension_semantics=("parallel", "parallel", "parallel"),
            vmem_limit_bytes=VMEM_LIMIT),
        name="diff_attention",
    )(q3, k3, v3, lam_vecs, subln_g)


def _merge_kernel(conv_ref, attn_ref, gate_ref, x_ref, wc_ref, wa_ref, wo_ref, fg_ref,
                  wr_ref, br_ref, x1_ref, h2_ref, logit_ref):
    c = jnp.dot(conv_ref[...], wc_ref[...], preferred_element_type=F32)
    a = jnp.dot(attn_ref[...], wa_ref[...], preferred_element_type=F32)
    g0 = gate_ref[:, 0:D_MODEL].astype(F32)
    g1 = gate_ref[:, D_MODEL:2 * D_MODEL].astype(F32)
    merged = (g0 * c + g1 * a).astype(BF16)
    x1 = x_ref[...] + jnp.dot(merged, wo_ref[...], preferred_element_type=F32)
    x1_ref[...] = x1
    ms = jnp.mean(x1 * x1, axis=-1, keepdims=True)
    h2 = x1 * lax.rsqrt(ms + EPS) * fg_ref[...]
    _store_rows(h2_ref, (), h2)
    h2_hi = h2.astype(BF16)
    h2_lo = (h2 - h2_hi.astype(F32)).astype(BF16)
    half = h2.shape[0] // 2
    for rows in (slice(0, half), slice(half, 2 * half)):
        r = (jnp.dot(h2_hi[rows], wr_ref[...], preferred_element_type=F32)
             + jnp.dot(h2_lo[rows], wr_ref[...], preferred_element_type=F32))
        logit_ref[rows, :] = r[:, 0:ROUTER_LANES] + r[:, ROUTER_LANES:2 * ROUTER_LANES] + br_ref[...]


def _merge(conv_act, attn_o, gates, x2, wc, wa, wo, fg, wr, br):
    T = x2.shape[0]
    tm = TM_MERGE
    row = lambda i: (i, 0)
    return pl.pallas_call(
        _merge_kernel,
        grid=(T // tm,),
        in_specs=[
            pl.BlockSpec((tm, CONV_CH), row),
            pl.BlockSpec((tm, ATTN_W), row),
            pl.BlockSpec((tm, COL_GATE), row),
            pl.BlockSpec((tm, D_MODEL), row),
            _const_spec((CONV_CH, D_MODEL)),
            _const_spec((ATTN_W, D_MODEL)),
            _const_spec((D_MODEL, D_MODEL)),
            _const_spec((1, D_MODEL)),
            _const_spec((D_MODEL, 2 * ROUTER_LANES)),
            _const_spec((1, ROUTER_LANES)),
        ],
        out_specs=(
            pl.BlockSpec((tm, D_MODEL), row),
            pl.BlockSpec((tm * ROW_TILE, LANES), row),
            pl.BlockSpec((tm, ROUTER_LANES), row),
        ),
        out_shape=(
            jax.ShapeDtypeStruct((T, D_MODEL), F32),
            jax.ShapeDtypeStruct((T * ROW_TILE, LANES), jnp.uint32),
            jax.ShapeDtypeStruct((T, ROUTER_LANES), F32),
        ),
        compiler_params=pltpu.CompilerParams(
            dimension_semantics=("parallel",), vmem_limit_bytes=VMEM_LIMIT),
        name="merge_out_router",
    )(conv_act, attn_o, gates, x2, wc, wa, wo, fg, wr, br)


def _route_kernel(logit_ref, tri_ref, low_ref, dest_ref, wts_ref, blk_ref,
                  cnt_ref, base_ref, *, n_blk_lanes):
    ph = pl.program_id(0)
    i = pl.program_id(1)
    tm = logit_ref.shape[0]
    lt = logit_ref[...].T
    row8 = lax.broadcasted_iota(jnp.int32, (SUBLANES, tm), 0)

    g = jnp.where(row8 < N_GROUPS, lt[0:SUBLANES, :], -jnp.inf)
    g_max = jnp.max(g, axis=0, keepdims=True)
    g_sum = jnp.sum(jnp.exp(g - g_max), axis=0, keepdims=True)
    g_sel = jnp.min(jnp.where(g == g_max, row8, SUBLANES), axis=0, keepdims=True)
    g_w = 1.0 / g_sum

    e_sel = jnp.zeros((EXPERTS_PER_GROUP, tm), F32)
    for gi in range(N_GROUPS):
        lo = ROUTER_EXPERT_COL + gi * EXPERTS_PER_GROUP
        e_sel = jnp.where(g_sel == gi, lt[lo:lo + EXPERTS_PER_GROUP, :], e_sel)
    e_max = jnp.max(e_sel, axis=0, keepdims=True)
    e_exp = jnp.exp(e_sel - e_max)
    prob = e_exp / jnp.sum(e_exp, axis=0, keepdims=True)
    p1 = jnp.max(prob, axis=0, keepdims=True)
    i1 = jnp.min(jnp.where(prob == p1, row8, SUBLANES), axis=0, keepdims=True)
    rest = jnp.where(row8 == i1, -1.0, prob)
    p2 = jnp.max(rest, axis=0, keepdims=True)
    i2 = jnp.min(jnp.where(rest == p2, row8, SUBLANES), axis=0, keepdims=True)
    denom = p1 + p2
    e0 = g_sel * EXPERTS_PER_GROUP + i1
    e1 = g_sel * EXPERTS_PER_GROUP + i2

    row32 = lax.broadcasted_iota(jnp.int32, (N_EXPERTS, tm), 0)
    oh0 = row32 == e0
    oh1 = row32 == e1
    ohs = jnp.where(oh0 | oh1, 1.0, 0.0)
    tile_cnt = jnp.sum(ohs, axis=1, keepdims=True)

    @pl.when((ph == 0) & (i == 0))
    def _():
        cnt_ref[...] = jnp.zeros(cnt_ref.shape, F32)

    @pl.when(ph == 0)
    def _():
        cnt_ref[...] += tile_cnt

    @pl.when((ph == 1) & (i == 0))
    def _():
        cnt = cnt_ref[...]
        padded = jnp.ceil(cnt * (1.0 / RB)) * RB
        pstart = jnp.dot(low_ref[...], padded, preferred_element_type=F32,
                         precision=lax.Precision.HIGHEST)
        base_ref[...] = pstart
        pad_end = pstart + padded
        row0 = lax.broadcasted_iota(jnp.int32, (N_EXPERTS, n_blk_lanes), 1).astype(F32) * RB
        expert = lax.broadcasted_iota(jnp.int32, (N_EXPERTS, n_blk_lanes), 0).astype(F32)
        blk = jnp.minimum(jnp.sum(jnp.where(pad_end[:, 0:1] <= row0, 1.0, 0.0), axis=0, keepdims=True),
                          N_EXPERTS - 1.0)
        valid_end = jnp.sum(jnp.where(expert == blk, (pstart + cnt)[:, 0:1], 0.0), axis=0, keepdims=True)
        n_valid = jnp.clip(valid_end - row0[0:1, :], 0.0, float(RB))
        n_used = jnp.max(pad_end[:, 0:1], axis=0, keepdims=True) * (1.0 / RB) + jnp.zeros_like(blk)
        blk_ref[...] = jnp.zeros(blk_ref.shape, jnp.int32)
        blk_ref[0:1, :] = blk.astype(jnp.int32)
        blk_ref[1:2, :] = n_valid.astype(jnp.int32)
        blk_ref[2:3, :] = n_used.astype(jnp.int32)

    @pl.when(ph == 1)
    def _():
        before = jnp.dot(ohs.astype(BF16), tri_ref[...], preferred_element_type=F32)
        pos = base_ref[:, 0:1] + before
        d0 = jnp.sum(jnp.where(oh0, pos, 0.0), axis=0, keepdims=True)
        d1 = jnp.sum(jnp.where(oh1, pos, 0.0), axis=0, keepdims=True)
        dest_ref[0:1, :] = d0.astype(jnp.int32)
        dest_ref[1:2, :] = d1.astype(jnp.int32)
        wts_ref[...] = jnp.zeros(wts_ref.shape, F32)
        wts_ref[0:1, :] = p1 / denom * g_w
        wts_ref[1:2, :] = p2 / denom * g_w
        base_ref[...] += tile_cnt


def _route(logits):
    T = logits.shape[0]
    tm = TM_ROUTE
    n_rows = 2 * T + N_EXPERTS * RB
    n_blk = n_rows // RB
    n_blk_lanes = -(-n_blk // LANES) * LANES
    r = jnp.arange(tm, dtype=jnp.int32)
    tri = (r[:, None] < r[None, :]).astype(BF16)
    e = jnp.arange(N_EXPERTS, dtype=jnp.int32)
    low = (e[None, :] < e[:, None]).astype(F32)
    return pl.pallas_call(
        functools.partial(_route_kernel, n_blk_lanes=n_blk_lanes),
        grid=(2, T // tm),
        in_specs=[
            pl.BlockSpec((tm, ROUTER_LANES), lambda ph, i: (i, 0)),
            _const_spec((tm, tm)),
            _const_spec((N_EXPERTS, N_EXPERTS)),
        ],
        out_specs=(
            pl.BlockSpec((2, tm), lambda ph, i: (0, i * ph)),
            pl.BlockSpec((SUBLANES, tm), lambda ph, i: (0, i * ph)),
            pl.BlockSpec((SUBLANES, n_blk_lanes), lambda ph, i: (0, 0)),
        ),
        out_shape=(
            jax.ShapeDtypeStruct((2, T), jnp.int32),
            jax.ShapeDtypeStruct((SUBLANES, T), F32),
            jax.ShapeDtypeStruct((SUBLANES, n_blk_lanes), jnp.int32),
        ),
        scratch_shapes=[pltpu.VMEM((N_EXPERTS, LANES), F32), pltpu.VMEM((N_EXPERTS, LANES), F32)],
        compiler_params=pltpu.CompilerParams(
            dimension_semantics=("arbitrary", "arbitrary"), vmem_limit_bytes=VMEM_LIMIT),
        name="route_plan",
    )(logits, tri, low)


def _sc_worker():
    return lax.axis_index("s") * SC_CORES + lax.axis_index("c")


def _sc_scatter_rows(h, dest, n_rows):
    T = h.shape[0]
    n_workers = SC_CORES * SC_SUBCORES
    per_w = T // n_workers
    n_ch = per_w // SC_ROWS
    assert per_w * n_workers == T and n_ch * SC_ROWS == per_w and n_ch % 2 == 0
    mesh = plsc.VectorSubcoreMesh(core_axis_name="c", subcore_axis_name="s")

    @functools.partial(
        pl.kernel, mesh=mesh,
        out_type=jax.ShapeDtypeStruct((n_rows,) + h.shape[1:], h.dtype),
        scratch_types=[
            pltpu.VMEM((2, n_ch, SC_ROWS), jnp.int32),
            pltpu.VMEM((2, SC_ROWS) + h.shape[1:], h.dtype),
            pltpu.SemaphoreType.DMA((2,)),
            pltpu.SemaphoreType.DMA((2,)),
        ],
    )
    def k(h_hbm, idx_hbm, out_hbm, idx_v, rows_v, lsem, ssem):
        wid = _sc_worker()
        base = wid * per_w
        pltpu.sync_copy(idx_hbm.at[0, wid], idx_v.at[0])
        pltpu.sync_copy(idx_hbm.at[1, wid], idx_v.at[1])

        def load(c, slot):
            return pltpu.make_async_copy(h_hbm.at[pl.ds(base + c * SC_ROWS, SC_ROWS)], rows_v.at[slot],
                                         lsem.at[slot])

        def scatter(slot_k, c, slot):
            return pltpu.make_async_copy(rows_v.at[slot], out_hbm.at[idx_v.at[slot_k, c]], ssem.at[slot])

        load(0, 0).start()

        def body(g, carry):
            for slot in range(2):
                c = 2 * g + slot
                load(c, slot).wait()

                @pl.when(c >= 1)
                def _():
                    scatter(0, c - 1, 1 - slot).wait()
                    scatter(1, c - 1, 1 - slot).wait()

                @pl.when(c + 1 < n_ch)
                def _():
                    load(c + 1, 1 - slot).start()

                scatter(0, c, slot).start()
                scatter(1, c, slot).start()
            return carry

        lax.fori_loop(0, n_ch // 2, body, 0)
        scatter(0, n_ch - 1, 1).wait()
        scatter(1, n_ch - 1, 1).wait()

    return k(h, dest.reshape(2, n_workers, n_ch, SC_ROWS))


def _sc_gather_rows(table, idx):
    B = idx.shape[0]
    n_workers = SC_CORES * SC_SUBCORES
    per_w = B // n_workers
    n_ch = per_w // SC_ROWS
    assert per_w * n_workers == B and n_ch * SC_ROWS == per_w and n_ch % 2 == 0
    mesh = plsc.VectorSubcoreMesh(core_axis_name="c", subcore_axis_name="s")

    @functools.partial(
        pl.kernel, mesh=mesh,
        out_type=jax.ShapeDtypeStruct((B,) + table.shape[1:], table.dtype),
        scratch_types=[
            pltpu.VMEM((n_ch, SC_ROWS), jnp.int32),
            pltpu.VMEM((2, SC_ROWS) + table.shape[1:], table.dtype),
            pltpu.SemaphoreType.DMA((2,)),
            pltpu.SemaphoreType.DMA((2,)),
        ],
    )
    def k(table_hbm, idx_hbm, out_hbm, idx_v, rows_v, gsem, wsem):
        wid = _sc_worker()
        base = wid * per_w
        pltpu.sync_copy(idx_hbm.at[wid], idx_v)

        def gather(c, slot):
            return pltpu.make_async_copy(table_hbm.at[idx_v.at[c]], rows_v.at[slot], gsem.at[slot])

        def writeback(c, slot):
            return pltpu.make_async_copy(rows_v.at[slot], out_hbm.at[pl.ds(base + c * SC_ROWS, SC_ROWS)],
                                         wsem.at[slot])

        gather(0, 0).start()

        def body(g, carry):
            for slot in range(2):
                c = 2 * g + slot
                gather(c, slot).wait()

                @pl.when(c >= 1)
                def _():
                    writeback(c - 1, 1 - slot).wait()

                @pl.when(c + 1 < n_ch)
                def _():
                    gather(c + 1, 1 - slot).start()

                writeback(c, slot).start()
            return carry

        lax.fori_loop(0, n_ch // 2, body, 0)
        writeback(n_ch - 1, 1).wait()

    return k(table, idx.reshape(n_workers, n_ch, SC_ROWS))


def _expert_kernel(blk_exp_ref, n_used_ref, n_valid_ref, xs_ref, wg_ref, wu_ref, wd_ref, y_ref,
                   wg_bf, wu_bf, wd_bf):
    b = pl.program_id(0)
    used = b < n_used_ref[0]
    new_expert = (b == 0) | (blk_exp_ref[b] != blk_exp_ref[jnp.maximum(b - 1, 0)])

    @pl.when(used & new_expert)
    def _():
        wg_bf[...] = wg_ref[0].astype(BF16)
        wu_bf[...] = wu_ref[0].astype(BF16)
        wd_bf[...] = wd_ref[0].astype(BF16)

    @pl.when(used)
    def _():
        x = _load_rows(xs_ref, (), RB)
        row = lax.broadcasted_iota(jnp.int32, (RB, 1), 0)
        xb = jnp.where(row < n_valid_ref[b], x, jnp.zeros_like(x))
        g = jnp.dot(xb, wg_bf[...], preferred_element_type=F32)
        u = jnp.dot(xb, wu_bf[...], preferred_element_type=F32)
        hid = (g * _sigmoid(g) * u).astype(BF16)
        _store_rows(y_ref, (), jnp.dot(hid, wd_bf[...], preferred_element_type=F32))

    @pl.when(jnp.logical_not(used))
    def _():
        y_ref[...] = jnp.zeros(y_ref.shape, y_ref.dtype)


def _experts(blk_exp, n_used, n_valid, xs, wg, wu, wd):
    n_blk = blk_exp.shape[0]
    grid_spec = pltpu.PrefetchScalarGridSpec(
        num_scalar_prefetch=3,
        grid=(n_blk,),
        in_specs=[
            pl.BlockSpec((RB * ROW_TILE, LANES), lambda b, be, nu, nv: (b, 0)),
            pl.BlockSpec((1, D_MODEL, D_EXPERT), lambda b, be, nu, nv: (be[b], 0, 0)),
            pl.BlockSpec((1, D_MODEL, D_EXPERT), lambda b, be, nu, nv: (be[b], 0, 0)),
            pl.BlockSpec((1, D_EXPERT, D_MODEL), lambda b, be, nu, nv: (be[b], 0, 0)),
        ],
        out_specs=pl.BlockSpec((RB * ROW_TILE, LANES), lambda b, be, nu, nv: (b, 0)),
        scratch_shapes=[
            pltpu.VMEM((D_MODEL, D_EXPERT), BF16),
            pltpu.VMEM((D_MODEL, D_EXPERT), BF16),
            pltpu.VMEM((D_EXPERT, D_MODEL), BF16),
        ],
    )
    return pl.pallas_call(
        _expert_kernel,
        grid_spec=grid_spec,
        out_shape=jax.ShapeDtypeStruct(xs.shape, xs.dtype),
        compiler_params=pltpu.CompilerParams(
            dimension_semantics=("arbitrary",), vmem_limit_bytes=VMEM_LIMIT),
        name="expert_mlp",
    )(blk_exp, n_used, n_valid, xs, wg, wu, wd)


def _combine_kernel(x1_ref, y0_ref, y1_ref, w_ref, o_ref):
    tm = x1_ref.shape[0]
    y0 = _load_rows(y0_ref, (), tm).astype(F32)
    y1 = _load_rows(y1_ref, (), tm).astype(F32)
    w = w_ref[...].T
    o_ref[...] = x1_ref[...] + (y0 * w[:, 0:1] + y1 * w[:, 1:2])


def _combine(x1, y, wts):
    T = x1.shape[0]
    tm = TM_COMB
    row = lambda i: (i, 0)
    return pl.pallas_call(
        _combine_kernel,
        grid=(T // tm,),
        in_specs=[
            pl.BlockSpec((tm, D_MODEL), row),
            pl.BlockSpec((tm * ROW_TILE, LANES), row),
            pl.BlockSpec((tm * ROW_TILE, LANES), lambda i: (i + T // tm, 0)),
            pl.BlockSpec((SUBLANES, tm), lambda i: (0, i)),
        ],
        out_specs=pl.BlockSpec((tm, D_MODEL), row),
        out_shape=jax.ShapeDtypeStruct((T, D_MODEL), F32),
        compiler_params=pltpu.CompilerParams(
            dimension_semantics=("parallel",), vmem_limit_bytes=VMEM_LIMIT),
        name="moe_combine",
    )(x1, y, y, wts)


def kernel(x, attn_norm_g, w_in, conv_dw_w, conv_dw_b, conv_ln_g, conv_ln_b, w_conv_out,
           q_norm_g, k_norm_g, lambda_q1, lambda_k1, lambda_q2, lambda_k2, subln_g,
           w_attn_out, w_out, ffn_norm_g, w_router_group, b_router_group,
           w_router_expert, b_router_expert, w_gate_e, w_up_e, w_down_e):
    B, S, D = x.shape
    T = B * S
    l = 0
    x2 = x.reshape(T, D)

    reps = COL_Q // DA_HEAD_DIM
    qg = jnp.tile(q_norm_g[l], reps).reshape(1, COL_Q)
    kg = jnp.tile(k_norm_g[l], reps).reshape(1, COL_K)
    grp = jnp.arange(NORM_BLOCK, dtype=jnp.int32) // DA_HEAD_DIM
    bd = jnp.where(grp[:, None] == grp[None, :], 1.0 / DA_HEAD_DIM, 0.0).astype(BF16)

    conv_act, q, k, v, gates = _in_projection(
        x2, attn_norm_g[l].reshape(1, D), w_in[l].astype(BF16), qg, kg, bd, conv_dw_w[l],
        conv_dw_b[l].reshape(1, CONV_CH), conv_ln_g[l].reshape(1, CONV_CH), conv_ln_b[l].reshape(1, CONV_CH), S)

    lam_vecs = jnp.stack([lambda_q1[l], lambda_k1[l], lambda_q2[l], lambda_k2[l]]).astype(F32)
    attn_o = _diff_attention(q.reshape(B, S, COL_Q), k.reshape(B, S, COL_K),
                             v.reshape(B, S, COL_V), lam_vecs, subln_g[l].reshape(1, DA_V_DIM))

    e_lo, e_hi = ROUTER_EXPERT_COL, ROUTER_EXPERT_COL + N_EXPERTS
    wr = jnp.zeros((D, ROUTER_LANES), F32)
    wr = wr.at[:, :N_GROUPS].set(w_router_group[l]).at[:, e_lo:e_hi].set(w_router_expert[l])
    br = jnp.zeros((1, ROUTER_LANES), F32)
    br = br.at[0, :N_GROUPS].set(b_router_group[l]).at[0, e_lo:e_hi].set(b_router_expert[l])

    wr_hi = wr.astype(BF16)
    wr_lo = (wr - wr_hi.astype(F32)).astype(BF16)
    x1, h2, logits = _merge(conv_act, attn_o.reshape(T, ATTN_W), gates, x2,
                            w_conv_out[l].astype(BF16), w_attn_out[l].astype(BF16),
                            w_out[l].astype(BF16), ffn_norm_g[l].reshape(1, D),
                            jnp.concatenate([wr_hi, wr_lo], axis=1), br)

    dest, wts, blk = _route(logits)
    n_rows = 2 * T + N_EXPERTS * RB
    n_blk = n_rows // RB
    blk_exp, n_valid, n_used = blk[0, :n_blk], blk[1, :n_blk], blk[2, 0:1]

    xs = _sc_scatter_rows(h2.reshape(T, ROW_TILE, LANES), dest, n_rows)
    yb = _experts(blk_exp, n_used, n_valid, xs.reshape(n_rows * ROW_TILE, LANES),
                  w_gate_e[l], w_up_e[l], w_down_e[l])
    y = _sc_gather_rows(yb.reshape(n_rows, ROW_TILE, LANES), dest.reshape(2 * T))

    out = _combine(x1, y.reshape(2 * T * ROW_TILE, LANES), wts)
    return out.reshape(B, S, D)
```

```python
import functools
import math

import jax
import jax.numpy as jnp
from jax import lax
from jax.experimental import pallas as pl
from jax.experimental.pallas import tpu as pltpu
from jax.experimental.pallas import tpu_sc as plsc

F32 = jnp.float32
BF16 = jnp.bfloat16

D_MODEL = 1024
CONV_CH = 512
CONV_WIDTH = 31
DA_HEADS = 4
DA_HEAD_DIM = 64
DA_V_DIM = 128
ATTN_W = 512
N_GROUPS = 4
EXPERTS_PER_GROUP = 8
N_EXPERTS = 32
D_EXPERT = 512
EPS = 1e-6
LAM_INIT = 0.8 - 0.6 * math.exp(-0.3 * 0)

COL_GLU = 2 * CONV_CH
COL_Q = 512
COL_K = 512
COL_V = 512
COL_GATE = 2 * D_MODEL
OFF_Q = COL_GLU
OFF_K = OFF_Q + COL_Q
OFF_V = OFF_K + COL_K
OFF_GATE = OFF_V + COL_V
IN_COLS = OFF_GATE + COL_GATE

ROUTER_LANES = 128
ROUTER_EXPERT_COL = 8
SUBLANES = 8
LANES = 128
ROW_TILE = D_MODEL // (2 * LANES)
NORM_BLOCK = 256
HALO = 32

TM_PROJ = 512
TM_MERGE = 1024
CONV_CHUNK = TM_PROJ
ATT_TQ = 1024
ATT_TK = 1024
ATT_RC = 256
ATT_HEADS = 2
RB = 512
TM_COMB = 1024
TM_ROUTE = 1024
SC_CORES = 2
SC_SUBCORES = 16
SC_ROWS = 64

VMEM_LIMIT = 48 * 1024 * 1024


def _sigmoid(x):
    return 1.0 / (1.0 + jnp.exp(-x))


def _const_spec(shape):
    nd = len(shape)
    return pl.BlockSpec(shape, lambda *_: (0,) * nd, pipeline_mode=pl.Buffered(1))


def _store_rows(ref, lead, val):
    rows, half = val.shape[0], val.shape[1] // 2
    hi = pltpu.bitcast(val[:, :half].astype(BF16).astype(F32), jnp.uint32)
    lo = pltpu.bitcast(val[:, half:].astype(BF16).astype(F32), jnp.uint32)
    words = hi | (lo >> 16)
    for j in range(half // LANES):
        ref[(*lead, pl.ds(j, rows, stride=ROW_TILE), slice(None))] = words[:, j * LANES:(j + 1) * LANES]


def _load_rows(ref, lead, rows):
    words = jnp.concatenate(
        [ref[(*lead, pl.ds(j, rows, stride=ROW_TILE), slice(None))] for j in range(ROW_TILE)], axis=1)
    hi = pltpu.bitcast(words & jnp.uint32(0xFFFF0000), F32).astype(BF16)
    lo = pltpu.bitcast(words << 16, F32).astype(BF16)
    return jnp.concatenate([hi, lo], axis=1)


def _group_mean_sq(t, bd_ref):
    sq = t * t
    hi = sq.astype(BF16)
    lo = (sq - hi.astype(F32)).astype(BF16)
    w = bd_ref.shape[0]
    parts = [jnp.dot(hi[:, c:c + w], bd_ref[...], preferred_element_type=F32)
             + jnp.dot(lo[:, c:c + w], bd_ref[...], preferred_element_type=F32)
             for c in range(0, t.shape[1], w)]
    return jnp.concatenate(parts, axis=1)


def _inproj_kernel(x_ref, g_ref, w_ref, qg_ref, kg_ref, bd_ref, cw_ref, cb_ref, lng_ref, lnb_ref,
                   conv_ref, q_ref, k_ref, v_ref, gate_ref, buf_ref, halo_ref, *, tiles_per_seq):
    i = pl.program_id(0)
    tm = x_ref.shape[0]
    x = x_ref[...]
    ms = jnp.mean(x * x, axis=-1, keepdims=True)
    h = (x * lax.rsqrt(ms + EPS) * g_ref[...]).astype(BF16)

    def proj(lo, width):
        return jnp.dot(h, w_ref[:, lo:lo + width], preferred_element_type=F32)

    a = proj(0, CONV_CH)
    gt = proj(CONV_CH, CONV_CH)
    glu = a * _sigmoid(gt)

    rows = HALO + tm
    buf_ref[0, 0:HALO, :] = jnp.where(i % tiles_per_seq == 0, 0.0, halo_ref[...])
    buf_ref[0, HALO:rows, :] = glu
    halo_ref[...] = glu[tm - HALO:tm, :]
    for r in range(1, SUBLANES):
        buf_ref[r, 0:rows - SUBLANES, :] = buf_ref[0, r:r + rows - SUBLANES, :]
    first = HALO - (CONV_WIDTH - 1)
    for c in range(tm // CONV_CHUNK):
        r0 = c * CONV_CHUNK
        acc = jnp.broadcast_to(cb_ref[...], (CONV_CHUNK, CONV_CH))
        for j in range(CONV_WIDTH):
            shift = (first + j) % SUBLANES
            lo = r0 + first + j - shift
            acc = acc + cw_ref[j:j + 1, :] * buf_ref[shift, lo:lo + CONV_CHUNK, :]
        mu = jnp.mean(acc, axis=-1, keepdims=True)
        d = acc - mu
        var = jnp.mean(d * d, axis=-1, keepdims=True)
        y = d * lax.rsqrt(var + EPS) * lng_ref[...] + lnb_ref[...]
        conv_ref[r0:r0 + CONV_CHUNK, :] = (y * _sigmoid(y)).astype(BF16)

    q = proj(OFF_Q, COL_Q)
    qn = q * lax.rsqrt(_group_mean_sq(q, bd_ref) + EPS) * qg_ref[...]
    q_ref[...] = (qn * (DA_HEAD_DIM ** -0.5)).astype(BF16)

    k = proj(OFF_K, COL_K)
    kn = k * lax.rsqrt(_group_mean_sq(k, bd_ref) + EPS) * kg_ref[...]
    k_ref[...] = kn.astype(BF16)

    v_ref[...] = proj(OFF_V, COL_V).astype(BF16)

    for c in range(COL_GATE // 512):
        gate_ref[:, c * 512:(c + 1) * 512] = _sigmoid(proj(OFF_GATE + c * 512, 512)).astype(BF16)


def _in_projection(x2, g, w_bf, qg, kg, bd, dw_w, dw_b, ln_g, ln_b, seq_len):
    T = x2.shape[0]
    tm = TM_PROJ
    row = lambda i: (i, 0)
    out_shape = (
        jax.ShapeDtypeStruct((T, CONV_CH), BF16),
        jax.ShapeDtypeStruct((T, COL_Q), BF16),
        jax.ShapeDtypeStruct((T, COL_K), BF16),
        jax.ShapeDtypeStruct((T, COL_V), BF16),
        jax.ShapeDtypeStruct((T, COL_GATE), BF16),
    )
    return pl.pallas_call(
        functools.partial(_inproj_kernel, tiles_per_seq=seq_len // tm),
        grid=(T // tm,),
        in_specs=[
            pl.BlockSpec((tm, D_MODEL), row),
            _const_spec((1, D_MODEL)),
            _const_spec((D_MODEL, IN_COLS)),
            _const_spec((1, COL_Q)),
            _const_spec((1, COL_K)),
            _const_spec((NORM_BLOCK, NORM_BLOCK)),
            _const_spec((CONV_WIDTH, CONV_CH)),
            _const_spec((1, CONV_CH)),
            _const_spec((1, CONV_CH)),
            _const_spec((1, CONV_CH)),
        ],
        out_specs=(
            pl.BlockSpec((tm, CONV_CH), row),
            pl.BlockSpec((tm, COL_Q), row),
            pl.BlockSpec((tm, COL_K), row),
            pl.BlockSpec((tm, COL_V), row),
            pl.BlockSpec((tm, COL_GATE), row),
        ),
        out_shape=out_shape,
        scratch_shapes=[pltpu.VMEM((SUBLANES, HALO + tm, CONV_CH), F32), pltpu.VMEM((HALO, CONV_CH), F32)],
        compiler_params=pltpu.CompilerParams(
            dimension_semantics=("arbitrary",), vmem_limit_bytes=VMEM_LIMIT),
        name="in_projection",
    )(x2, g, w_bf, qg, kg, bd, dw_w, dw_b, ln_g, ln_b)


def _attn_kernel(q_ref, k_ref, v_ref, lam_ref, sg_ref, o_ref, qs_ref, m_ref, acc_ref, s_ref, *, tq, tk):
    qi = pl.program_id(2)
    lane = lax.broadcasted_iota(jnp.int32, (tq, DA_V_DIM), 1)
    for hh in range(ATT_HEADS):
        q = q_ref[0, :, hh * DA_V_DIM:(hh + 1) * DA_V_DIM]
        zero = jnp.zeros_like(q)
        qs_ref[hh, 0:tq, :] = jnp.where(lane < DA_HEAD_DIM, q, zero)
        qs_ref[hh, tq:2 * tq, :] = jnp.where(lane >= DA_HEAD_DIM, q, zero)
    ones = jnp.ones((tk, LANES), BF16)
    rc = ATT_RC
    chunks = [slice(r0, r0 + rc) for r0 in range(0, 2 * tq, rc)]
    n_diag = tq // tk

    def scores(hh, j, rows):
        kj = k_ref[0, pl.ds(pl.multiple_of(j * tk, tk), tk), hh * DA_V_DIM:(hh + 1) * DA_V_DIM]
        return lax.dot_general(qs_ref[hh, rows, :], kj, (((1,), (1,)), ((), ())),
                               preferred_element_type=F32)

    def step(j, diag, prefetch, first):
        assert not first or diag in (None, 0)
        for hh in range(ATT_HEADS):
            vj = jnp.concatenate(
                [v_ref[0, pl.ds(pl.multiple_of(j * tk, tk), tk), hh * DA_V_DIM:(hh + 1) * DA_V_DIM], ones], axis=1)
            for rows in chunks:
                r_lo = rows.start % tq
                if diag is None:
                    kw, masked = tk, False
                else:
                    k_lo = diag * tk
                    if k_lo > r_lo + rc - 1:
                        if prefetch:
                            s_ref[hh, rows, :] = scores(hh, j + 1, rows)
                        continue
                    kw = min(tk, -(-(r_lo + rc - k_lo) // (2 * LANES)) * (2 * LANES))
                    masked = k_lo + kw - 1 > r_lo
                s = s_ref[hh, rows, 0:kw]
                if prefetch:
                    s_ref[hh, rows, :] = scores(hh, j + 1, rows)
                if masked:
                    r = lax.broadcasted_iota(jnp.int32, s.shape, 0) + r_lo
                    c = lax.broadcasted_iota(jnp.int32, s.shape, 1) + k_lo
                    s = jnp.where(c <= r, s, -jnp.inf)
                m_cur = jnp.max(s, axis=-1, keepdims=True)
                if first:
                    m_new = jnp.broadcast_to(m_cur, (rc, LANES))
                else:
                    m_old = m_ref[hh, rows, :]
                    m_new = jnp.maximum(m_old, m_cur)
                p = jnp.exp(s - jnp.tile(m_new, (1, kw // LANES)))
                pv = jnp.dot(p.astype(BF16), vj[0:kw, :], preferred_element_type=F32)
                if first:
                    acc_ref[hh, rows, :] = pv
                else:
                    alpha = jnp.exp(m_old - m_new)
                    acc_ref[hh, rows, :] = jnp.tile(alpha, (1, 2)) * acc_ref[hh, rows, :] + pv
                m_ref[hh, rows, :] = m_new

    for hh in range(ATT_HEADS):
        for rows in chunks:
            s_ref[hh, rows, :] = scores(hh, 0, rows)

    def body(j, carry):
        step(j, None, True, False)
        return carry

    n_full = qi * n_diag

    def diagonal_tiles(first):
        for t in range(n_diag):
            step(n_full + t, t, t + 1 < n_diag, first and t == 0)

    @pl.when(qi == 0)
    def _():
        diagonal_tiles(True)

    @pl.when(qi > 0)
    def _():
        step(0, None, True, True)
        lax.fori_loop(1, n_full, body, 0)
        diagonal_tiles(False)

    lam = (jnp.exp(jnp.sum(lam_ref[0:1, :] * lam_ref[1:2, :], axis=-1, keepdims=True))
           - jnp.exp(jnp.sum(lam_ref[2:3, :] * lam_ref[3:4, :], axis=-1, keepdims=True))
           + LAM_INIT)
    for hh in range(ATT_HEADS):
        o1 = acc_ref[hh, 0:tq, 0:LANES] / acc_ref[hh, 0:tq, LANES:2 * LANES]
        o2 = acc_ref[hh, tq:2 * tq, 0:LANES] / acc_ref[hh, tq:2 * tq, LANES:2 * LANES]
        o = o1 - lam * o2
        ms = jnp.mean(o * o, axis=-1, keepdims=True)
        on = o * lax.rsqrt(ms + EPS) * sg_ref[...]
        o_ref[0, :, hh * DA_V_DIM:(hh + 1) * DA_V_DIM] = (on * (1.0 - LAM_INIT)).astype(BF16)


def _diff_attention(q3, k3, v3, lam_vecs, subln_g):
    B, S, _ = q3.shape
    tq, tk = ATT_TQ, ATT_TK
    w = ATT_HEADS * DA_V_DIM
    return pl.pallas_call(
        functools.partial(_attn_kernel, tq=tq, tk=tk),
        grid=(B, DA_HEADS // ATT_HEADS, S // tq),
        in_specs=[
            pl.BlockSpec((1, tq, w), lambda b, h, i: (b, i, h)),
            pl.BlockSpec((1, S, w), lambda b, h, i: (b, 0, h)),
            pl.BlockSpec((1, S, w), lambda b, h, i: (b, 0, h)),
            _const_spec((4, DA_HEAD_DIM)),
            _const_spec((1, DA_V_DIM)),
        ],
        out_specs=pl.BlockSpec((1, tq, w), lambda b, h, i: (b, i, h)),
        out_shape=jax.ShapeDtypeStruct((B, S, ATTN_W), BF16),
        scratch_shapes=[
            pltpu.VMEM((ATT_HEADS, 2 * tq, DA_V_DIM), BF16),
            pltpu.VMEM((ATT_HEADS, 2 * tq, LANES), F32),
            pltpu.VMEM((ATT_HEADS, 2 * tq, 2 * LANES), F32),
            pltpu.VMEM((ATT_HEADS, 2 * tq, tk), F32),
        ],
        compiler_params=pltpu.CompilerParams(
            dimension_semantics=("parallel", "parallel", "parallel"),
            vmem_limit_bytes=VMEM_LIMIT),
        name="diff_attention",
    )(q3, k3, v3, lam_vecs, subln_g)


def _merge_kernel(conv_ref, attn_ref, gate_ref, x_ref, wc_ref, wa_ref, wo_ref, fg_ref,
                  wr_ref, br_ref, x1_ref, h2_ref, logit_ref):
    c = jnp.dot(conv_ref[...], wc_ref[...], preferred_element_type=F32)
    a = jnp.dot(attn_ref[...], wa_ref[...], preferred_element_type=F32)
    g0 = gate_ref[:, 0:D_MODEL].astype(F32)
    g1 = gate_ref[:, D_MODEL:2 * D_MODEL].astype(F32)
    merged = (g0 * c + g1 * a).astype(BF16)
    x1 = x_ref[...] + jnp.dot(merged, wo_ref[...], preferred_element_type=F32)
    x1_ref[...] = x1
    ms = jnp.mean(x1 * x1, axis=-1, keepdims=True)
    h2 = x1 * lax.rsqrt(ms + EPS) * fg_ref[...]
    _store_rows(h2_ref, (), h2)
    h2_hi = h2.astype(BF16)
    h2_lo = (h2 - h2_hi.astype(F32)).astype(BF16)
    half = h2.shape[0] // 2
    for rows in (slice(0, half), slice(half, 2 * half)):
        r = (jnp.dot(h2_hi[rows], wr_ref[...], preferred_element_type=F32)
             + jnp.dot(h2_lo[rows], wr_ref[...], preferred_element_type=F32))
        logit_ref[rows, :] = r[:, 0:ROUTER_LANES] + r[:, ROUTER_LANES:2 * ROUTER_LANES] + br_ref[...]


def _merge(conv_act, attn_o, gates, x2, wc, wa, wo, fg, wr, br):
    T = x2.shape[0]
    tm = TM_MERGE
    row = lambda i: (i, 0)
    return pl.pallas_call(
        _merge_kernel,
        grid=(T // tm,),
        in_specs=[
            pl.BlockSpec((tm, CONV_CH), row),
            pl.BlockSpec((tm, ATTN_W), row),
            pl.BlockSpec((tm, COL_GATE), row),
            pl.BlockSpec((tm, D_MODEL), row),
            _const_spec((CONV_CH, D_MODEL)),
            _const_spec((ATTN_W, D_MODEL)),
            _const_spec((D_MODEL, D_MODEL)),
            _const_spec((1, D_MODEL)),
            _const_spec((D_MODEL, 2 * ROUTER_LANES)),
            _const_spec((1, ROUTER_LANES)),
        ],
        out_specs=(
            pl.BlockSpec((tm, D_MODEL), row),
            pl.BlockSpec((tm * ROW_TILE, LANES), row),
            pl.BlockSpec((tm, ROUTER_LANES), row),
        ),
        out_shape=(
            jax.ShapeDtypeStruct((T, D_MODEL), F32),
            jax.ShapeDtypeStruct((T * ROW_TILE, LANES), jnp.uint32),
            jax.ShapeDtypeStruct((T, ROUTER_LANES), F32),
        ),
        compiler_params=pltpu.CompilerParams(
            dimension_semantics=("parallel",), vmem_limit_bytes=VMEM_LIMIT),
        name="merge_out_router",
    )(conv_act, attn_o, gates, x2, wc, wa, wo, fg, wr, br)


def _route_kernel(logit_ref, tri_ref, low_ref, dest_ref, wts_ref, blk_ref,
                  cnt_ref, base_ref, *, n_blk_lanes):
    ph = pl.program_id(0)
    i = pl.program_id(1)
    tm = logit_ref.shape[0]
    lt = logit_ref[...].T
    row8 = lax.broadcasted_iota(jnp.int32, (SUBLANES, tm), 0)

    g = jnp.where(row8 < N_GROUPS, lt[0:SUBLANES, :], -jnp.inf)
    g_max = jnp.max(g, axis=0, keepdims=True)
    g_sum = jnp.sum(jnp.exp(g - g_max), axis=0, keepdims=True)
    g_sel = jnp.min(jnp.where(g == g_max, row8, SUBLANES), axis=0, keepdims=True)
    g_w = 1.0 / g_sum

    e_sel = jnp.zeros((EXPERTS_PER_GROUP, tm), F32)
    for gi in range(N_GROUPS):
        lo = ROUTER_EXPERT_COL + gi * EXPERTS_PER_GROUP
        e_sel = jnp.where(g_sel == gi, lt[lo:lo + EXPERTS_PER_GROUP, :], e_sel)
    e_max = jnp.max(e_sel, axis=0, keepdims=True)
    e_exp = jnp.exp(e_sel - e_max)
    prob = e_exp / jnp.sum(e_exp, axis=0, keepdims=True)
    p1 = jnp.max(prob, axis=0, keepdims=True)
    i1 = jnp.min(jnp.where(prob == p1, row8, SUBLANES), axis=0, keepdims=True)
    rest = jnp.where(row8 == i1, -1.0, prob)
    p2 = jnp.max(rest, axis=0, keepdims=True)
    i2 = jnp.min(jnp.where(rest == p2, row8, SUBLANES), axis=0, keepdims=True)
    denom = p1 + p2
    e0 = g_sel * EXPERTS_PER_GROUP + i1
    e1 = g_sel * EXPERTS_PER_GROUP + i2

    row32 = lax.broadcasted_iota(jnp.int32, (N_EXPERTS, tm), 0)
    oh0 = row32 == e0
    oh1 = row32 == e1
    ohs = jnp.where(oh0 | oh1, 1.0, 0.0)
    tile_cnt = jnp.sum(ohs, axis=1, keepdims=True)

    @pl.when((ph == 0) & (i == 0))
    def _():
        cnt_ref[...] = jnp.zeros(cnt_ref.shape, F32)

    @pl.when(ph == 0)
    def _():
        cnt_ref[...] += tile_cnt

    @pl.when((ph == 1) & (i == 0))
    def _():
        cnt = cnt_ref[...]
        padded = jnp.ceil(cnt * (1.0 / RB)) * RB
        pstart = jnp.dot(low_ref[...], padded, preferred_element_type=F32,
                         precision=lax.Precision.HIGHEST)
        base_ref[...] = pstart
        pad_end = pstart + padded
        row0 = lax.broadcasted_iota(jnp.int32, (N_EXPERTS, n_blk_lanes), 1).astype(F32) * RB
        expert = lax.broadcasted_iota(jnp.int32, (N_EXPERTS, n_blk_lanes), 0).astype(F32)
        blk = jnp.minimum(jnp.sum(jnp.where(pad_end[:, 0:1] <= row0, 1.0, 0.0), axis=0, keepdims=True),
                          N_EXPERTS - 1.0)
        valid_end = jnp.sum(jnp.where(expert == blk, (pstart + cnt)[:, 0:1], 0.0), axis=0, keepdims=True)
        n_valid = jnp.clip(valid_end - row0[0:1, :], 0.0, float(RB))
        n_used = jnp.max(pad_end[:, 0:1], axis=0, keepdims=True) * (1.0 / RB) + jnp.zeros_like(blk)
        blk_ref[...] = jnp.zeros(blk_ref.shape, jnp.int32)
        blk_ref[0:1, :] = blk.astype(jnp.int32)
        blk_ref[1:2, :] = n_valid.astype(jnp.int32)
        blk_ref[2:3, :] = n_used.astype(jnp.int32)

    @pl.when(ph == 1)
    def _():
        before = jnp.dot(ohs.astype(BF16), tri_ref[...], preferred_element_type=F32)
        pos = base_ref[:, 0:1] + before
        d0 = jnp.sum(jnp.where(oh0, pos, 0.0), axis=0, keepdims=True)
        d1 = jnp.sum(jnp.where(oh1, pos, 0.0), axis=0, keepdims=True)
        dest_ref[0:1, :] = d0.astype(jnp.int32)
        dest_ref[1:2, :] = d1.astype(jnp.int32)
        wts_ref[...] = jnp.zeros(wts_ref.shape, F32)
        wts_ref[0:1, :] = p1 / denom * g_w
        wts_ref[1:2, :] = p2 / denom * g_w
        base_ref[...] += tile_cnt


def _route(logits):
    T = logits.shape[0]
    tm = TM_ROUTE
    n_rows = 2 * T + N_EXPERTS * RB
    n_blk = n_rows // RB
    n_blk_lanes = -(-n_blk // LANES) * LANES
    r = jnp.arange(tm, dtype=jnp.int32)
    tri = (r[:, None] < r[None, :]).astype(BF16)
    e = jnp.arange(N_EXPERTS, dtype=jnp.int32)
    low = (e[None, :] < e[:, None]).astype(F32)
    return pl.pallas_call(
        functools.partial(_route_kernel, n_blk_lanes=n_blk_lanes),
        grid=(2, T // tm),
        in_specs=[
            pl.BlockSpec((tm, ROUTER_LANES), lambda ph, i: (i, 0)),
            _const_spec((tm, tm)),
            _const_spec((N_EXPERTS, N_EXPERTS)),
        ],
        out_specs=(
            pl.BlockSpec((2, tm), lambda ph, i: (0, i * ph)),
            pl.BlockSpec((SUBLANES, tm), lambda ph, i: (0, i * ph)),
            pl.BlockSpec((SUBLANES, n_blk_lanes), lambda ph, i: (0, 0)),
        ),
        out_shape=(
            jax.ShapeDtypeStruct((2, T), jnp.int32),
            jax.ShapeDtypeStruct((SUBLANES, T), F32),
            jax.ShapeDtypeStruct((SUBLANES, n_blk_lanes), jnp.int32),
        ),
        scratch_shapes=[pltpu.VMEM((N_EXPERTS, LANES), F32), pltpu.VMEM((N_EXPERTS, LANES), F32)],
        compiler_params=pltpu.CompilerParams(
            dimension_semantics=("arbitrary", "arbitrary"), vmem_limit_bytes=VMEM_LIMIT),
        name="route_plan",
    )(logits, tri, low)


def _sc_worker():
    return lax.axis_index("s") * SC_CORES + lax.axis_index("c")


def _sc_scatter_rows(h, dest, n_rows):
    T = h.shape[0]
    n_workers = SC_CORES * SC_SUBCORES
    per_w = T // n_workers
    n_ch = per_w // SC_ROWS
    assert per_w * n_workers == T and n_ch * SC_ROWS == per_w and n_ch % 2 == 0
    mesh = plsc.VectorSubcoreMesh(core_axis_name="c", subcore_axis_name="s")

    @functools.partial(
        pl.kernel, mesh=mesh,
        out_type=jax.ShapeDtypeStruct((n_rows,) + h.shape[1:], h.dtype),
        scratch_types=[
            pltpu.VMEM((2, n_ch, SC_ROWS), jnp.int32),
            pltpu.VMEM((2, SC_ROWS) + h.shape[1:], h.dtype),
            pltpu.SemaphoreType.DMA((2,)),
            pltpu.SemaphoreType.DMA((2,)),
        ],
    )
    def k(h_hbm, idx_hbm, out_hbm, idx_v, rows_v, lsem, ssem):
        wid = _sc_worker()
        base = wid * per_w
        pltpu.sync_copy(idx_hbm.at[0, wid], idx_v.at[0])
        pltpu.sync_copy(idx_hbm.at[1, wid], idx_v.at[1])

        def load(c, slot):
            return pltpu.make_async_copy(h_hbm.at[pl.ds(base + c * SC_ROWS, SC_ROWS)], rows_v.at[slot],
                                         lsem.at[slot])

        def scatter(slot_k, c, slot):
            return pltpu.make_async_copy(rows_v.at[slot], out_hbm.at[idx_v.at[slot_k, c]], ssem.at[slot])

        load(0, 0).start()

        def body(g, carry):
            for slot in range(2):
                c = 2 * g + slot
                load(c, slot).wait()

                @pl.when(c >= 1)
                def _():
                    scatter(0, c - 1, 1 - slot).wait()
                    scatter(1, c - 1, 1 - slot).wait()

                @pl.when(c + 1 < n_ch)
                def _():
                    load(c + 1, 1 - slot).start()

                scatter(0, c, slot).start()
                scatter(1, c, slot).start()
            return carry

        lax.fori_loop(0, n_ch // 2, body, 0)
        scatter(0, n_ch - 1, 1).wait()
        scatter(1, n_ch - 1, 1).wait()

    return k(h, dest.reshape(2, n_workers, n_ch, SC_ROWS))


def _sc_gather_rows(table, idx):
    B = idx.shape[0]
    n_workers = SC_CORES * SC_SUBCORES
    per_w = B // n_workers
    n_ch = per_w // SC_ROWS
    assert per_w * n_workers == B and n_ch * SC_ROWS == per_w and n_ch % 2 == 0
    mesh = plsc.VectorSubcoreMesh(core_axis_name="c", subcore_axis_name="s")

    @functools.partial(
        pl.kernel, mesh=mesh,
        out_type=jax.ShapeDtypeStruct((B,) + table.shape[1:], table.dtype),
        scratch_types=[
            pltpu.VMEM((n_ch, SC_ROWS), jnp.int32),
            pltpu.VMEM((2, SC_ROWS) + table.shape[1:], table.dtype),
            pltpu.SemaphoreType.DMA((2,)),
            pltpu.SemaphoreType.DMA((2,)),
        ],
    )
    def k(table_hbm, idx_hbm, out_hbm, idx_v, rows_v, gsem, wsem):
        wid = _sc_worker()
        base = wid * per_w
        pltpu.sync_copy(idx_hbm.at[wid], idx_v)

        def gather(c, slot):
            return pltpu.make_async_copy(table_hbm.at[idx_v.at[c]], rows_v.at[slot], gsem.at[slot])

        def writeback(c, slot):
            return pltpu.make_async_copy(rows_v.at[slot], out_hbm.at[pl.ds(base + c * SC_ROWS, SC_ROWS)],
                                         wsem.at[slot])

        gather(0, 0).start()

        def body(g, carry):
            for slot in range(2):
                c = 2 * g + slot
                gather(c, slot).wait()

                @pl.when(c >= 1)
                def _():
                    writeback(c - 1, 1 - slot).wait()

                @pl.when(c + 1 < n_ch)
                def _():
                    gather(c + 1, 1 - slot).start()

                writeback(c, slot).start()
            return carry

        lax.fori_loop(0, n_ch // 2, body, 0)
        writeback(n_ch - 1, 1).wait()

    return k(table, idx.reshape(n_workers, n_ch, SC_ROWS))


def _expert_kernel(blk_exp_ref, n_used_ref, n_valid_ref, xs_ref, wg_ref, wu_ref, wd_ref, y_ref,
                   wg_bf, wu_bf, wd_bf):
    b = pl.program_id(0)
    used = b < n_used_ref[0]
    new_expert = (b == 0) | (blk_exp_ref[b] != blk_exp_ref[jnp.maximum(b - 1, 0)])

    @pl.when(used & new_expert)
    def _():
        wg_bf[...] = wg_ref[0].astype(BF16)
        wu_bf[...] = wu_ref[0].astype(BF16)
        wd_bf[...] = wd_ref[0].astype(BF16)

    @pl.when(used)
    def _():
        x = _load_rows(xs_ref, (), RB)
        row = lax.broadcasted_iota(jnp.int32, (RB, 1), 0)
        xb = jnp.where(row < n_valid_ref[b], x, jnp.zeros_like(x))
        g = jnp.dot(xb, wg_bf[...], preferred_element_type=F32)
        u = jnp.dot(xb, wu_bf[...], preferred_element_type=F32)
        hid = (g * _sigmoid(g) * u).astype(BF16)
        _store_rows(y_ref, (), jnp.dot(hid, wd_bf[...], preferred_element_type=F32))

    @pl.when(jnp.logical_not(used))
    def _():
        y_ref[...] = jnp.zeros(y_ref.shape, y_ref.dtype)


def _experts(blk_exp, n_used, n_valid, xs, wg, wu, wd):
    n_blk = blk_exp.shape[0]
    grid_spec = pltpu.PrefetchScalarGridSpec(
        num_scalar_prefetch=3,
        grid=(n_blk,),
        in_specs=[
            pl.BlockSpec((RB * ROW_TILE, LANES), lambda b, be, nu, nv: (b, 0)),
            pl.BlockSpec((1, D_MODEL, D_EXPERT), lambda b, be, nu, nv: (be[b], 0, 0)),
            pl.BlockSpec((1, D_MODEL, D_EXPERT), lambda b, be, nu, nv: (be[b], 0, 0)),
            pl.BlockSpec((1, D_EXPERT, D_MODEL), lambda b, be, nu, nv: (be[b], 0, 0)),
        ],
        out_specs=pl.BlockSpec((RB * ROW_TILE, LANES), lambda b, be, nu, nv: (b, 0)),
        scratch_shapes=[
            pltpu.VMEM((D_MODEL, D_EXPERT), BF16),
            pltpu.VMEM((D_MODEL, D_EXPERT), BF16),
            pltpu.VMEM((D_EXPERT, D_MODEL), BF16),
        ],
    )
    return pl.pallas_call(
        _expert_kernel,
        grid_spec=grid_spec,
        out_shape=jax.ShapeDtypeStruct(xs.shape, xs.dtype),
        compiler_params=pltpu.CompilerParams(
            dimension_semantics=("arbitrary",), vmem_limit_bytes=VMEM_LIMIT),
        name="expert_mlp",
    )(blk_exp, n_used, n_valid, xs, wg, wu, wd)


def _combine_kernel(x1_ref, y0_ref, y1_ref, w_ref, o_ref):
    tm = x1_ref.shape[0]
    y0 = _load_rows(y0_ref, (), tm).astype(F32)
    y1 = _load_rows(y1_ref, (), tm).astype(F32)
    w = w_ref[...].T
    o_ref[...] = x1_ref[...] + (y0 * w[:, 0:1] + y1 * w[:, 1:2])


def _combine(x1, y, wts):
    T = x1.shape[0]
    tm = TM_COMB
    row = lambda i: (i, 0)
    return pl.pallas_call(
        _combine_kernel,
        grid=(T // tm,),
        in_specs=[
            pl.BlockSpec((tm, D_MODEL), row),
            pl.BlockSpec((tm * ROW_TILE, LANES), row),
            pl.BlockSpec((tm * ROW_TILE, LANES), lambda i: (i + T // tm, 0)),
            pl.BlockSpec((SUBLANES, tm), lambda i: (0, i)),
        ],
        out_specs=pl.BlockSpec((tm, D_MODEL), row),
        out_shape=jax.ShapeDtypeStruct((T, D_MODEL), F32),
        compiler_params=pltpu.CompilerParams(
            dimension_semantics=("parallel",), vmem_limit_bytes=VMEM_LIMIT),
        name="moe_combine",
    )(x1, y, y, wts)


def kernel(x, attn_norm_g, w_in, conv_dw_w, conv_dw_b, conv_ln_g, conv_ln_b, w_conv_out,
           q_norm_g, k_norm_g, lambda_q1, lambda_k1, lambda_q2, lambda_k2, subln_g,
           w_attn_out, w_out, ffn_norm_g, w_router_group, b_router_group,
           w_router_expert, b_router_expert, w_gate_e, w_up_e, w_down_e):
    B, S, D = x.shape
    T = B * S
    l = 0
    x2 = x.reshape(T, D)

    reps = COL_Q // DA_HEAD_DIM
    qg = jnp.tile(q_norm_g[l], reps).reshape(1, COL_Q)
    kg = jnp.tile(k_norm_g[l], reps).reshape(1, COL_K)
    grp = jnp.arange(NORM_BLOCK, dtype=jnp.int32) // DA_HEAD_DIM
    bd = jnp.where(grp[:, None] == grp[None, :], 1.0 / DA_HEAD_DIM, 0.0).astype(BF16)

    conv_act, q, k, v, gates = _in_projection(
        x2, attn_norm_g[l].reshape(1, D), w_in[l].astype(BF16), qg, kg, bd, conv_dw_w[l],
        conv_dw_b[l].reshape(1, CONV_CH), conv_ln_g[l].reshape(1, CONV_CH), conv_ln_b[l].reshape(1, CONV_CH), S)

    lam_vecs = jnp.stack([lambda_q1[l], lambda_k1[l], lambda_q2[l], lambda_k2[l]]).astype(F32)
    attn_o = _diff_attention(q.reshape(B, S, COL_Q), k.reshape(B, S, COL_K),
                             v.reshape(B, S, COL_V), lam_vecs, subln_g[l].reshape(1, DA_V_DIM))

    e_lo, e_hi = ROUTER_EXPERT_COL, ROUTER_EXPERT_COL + N_EXPERTS
    wr = jnp.zeros((D, ROUTER_LANES), F32)
    wr = wr.at[:, :N_GROUPS].set(w_router_group[l]).at[:, e_lo:e_hi].set(w_router_expert[l])
    br = jnp.zeros((1, ROUTER_LANES), F32)
    br = br.at[0, :N_GROUPS].set(b_router_group[l]).at[0, e_lo:e_hi].set(b_router_expert[l])

    wr_hi = wr.astype(BF16)
    wr_lo = (wr - wr_hi.astype(F32)).astype(BF16)
    x1, h2, logits = _merge(conv_act, attn_o.reshape(T, ATTN_W), gates, x2,
                            w_conv_out[l].astype(BF16), w_attn_out[l].astype(BF16),
                            w_out[l].astype(BF16), ffn_norm_g[l].reshape(1, D),
                            jnp.concatenate([wr_hi, wr_lo], axis=1), br)

    dest, wts, blk = _route(logits)
    n_rows = 2 * T + N_EXPERTS * RB
    n_blk = n_rows // RB
    blk_exp, n_valid, n_used = blk[0, :n_blk], blk[1, :n_blk], blk[2, 0:1]

    xs = _sc_scatter_rows(h2.reshape(T, ROW_TILE, LANES), dest, n_rows)
    yb = _experts(blk_exp, n_used, n_valid, xs.reshape(n_rows * ROW_TILE, LANES),
                  w_gate_e[l], w_up_e[l], w_down_e[l])
    y = _sc_gather_rows(yb.reshape(n_rows, ROW_TILE, LANES), dest.reshape(2 * T))

    out = _combine(x1, y.reshape(2 * T * ROW_TILE, LANES), wts)
    return out.reshape(B, S, D)
```

```python
import functools
import math

import jax
import jax.numpy as jnp
from jax import lax
from jax.experimental import pallas as pl
from jax.experimental.pallas import tpu as pltpu
from jax.experimental.pallas import tpu_sc as plsc

F32 = jnp.float32
BF16 = jnp.bfloat16

D_MODEL = 1024
CONV_CH = 512
CONV_WIDTH = 31
DA_HEADS = 4
DA_HEAD_DIM = 64
DA_V_DIM = 128
ATTN_W = 512
N_GROUPS = 4
EXPERTS_PER_GROUP = 8
N_EXPERTS = 32
D_EXPERT = 512
EPS = 1e-6
LAM_INIT = 0.8 - 0.6 * math.exp(-0.3 * 0)

COL_GLU = 2 * CONV_CH
COL_Q = 512
COL_K = 512
COL_V = 512
COL_GATE = 2 * D_MODEL
OFF_Q = COL_GLU
OFF_K = OFF_Q + COL_Q
OFF_V = OFF_K + COL_K
OFF_GATE = OFF_V + COL_V
IN_COLS = OFF_GATE + COL_GATE

ROUTER_LANES = 128
ROUTER_EXPERT_COL = 8
SUBLANES = 8
LANES = 128
ROW_TILE = D_MODEL // (2 * LANES)
NORM_BLOCK = 256
HALO = 32

TM_PROJ = 512
TM_MERGE = 1024
CONV_STEP = 64
ATT_TQ = 1024
ATT_TK = 1024
ATT_RC = 256
ATT_HEADS = 2
RB = 512
TM_COMB = 1024
TM_ROUTE = 1024
SC_CORES = 2
SC_SUBCORES = 16
SC_ROWS = 64

VMEM_LIMIT = 48 * 1024 * 1024


def _sigmoid(x):
    return 1.0 / (1.0 + jnp.exp(-x))


def _const_spec(shape):
    nd = len(shape)
    return pl.BlockSpec(shape, lambda *_: (0,) * nd, pipeline_mode=pl.Buffered(1))


def _store_rows(ref, lead, val):
    rows, half = val.shape[0], val.shape[1] // 2
    hi = pltpu.bitcast(val[:, :half].astype(BF16).astype(F32), jnp.uint32)
    lo = pltpu.bitcast(val[:, half:].astype(BF16).astype(F32), jnp.uint32)
    words = hi | (lo >> 16)
    for j in range(half // LANES):
        ref[(*lead, pl.ds(j, rows, stride=ROW_TILE), slice(None))] = words[:, j * LANES:(j + 1) * LANES]


def _load_rows(ref, lead, rows):
    words = jnp.concatenate(
        [ref[(*lead, pl.ds(j, rows, stride=ROW_TILE), slice(None))] for j in range(ROW_TILE)], axis=1)
    hi = pltpu.bitcast(words & jnp.uint32(0xFFFF0000), F32).astype(BF16)
    lo = pltpu.bitcast(words << 16, F32).astype(BF16)
    return jnp.concatenate([hi, lo], axis=1)


def _group_mean_sq(t, bd_ref):
    sq = t * t
    hi = sq.astype(BF16)
    lo = (sq - hi.astype(F32)).astype(BF16)
    w = bd_ref.shape[0]
    parts = [jnp.dot(hi[:, c:c + w], bd_ref[...], preferred_element_type=F32)
             + jnp.dot(lo[:, c:c + w], bd_ref[...], preferred_element_type=F32)
             for c in range(0, t.shape[1], w)]
    return jnp.concatenate(parts, axis=1)


def _inproj_kernel(x_ref, g_ref, w_ref, qg_ref, kg_ref, bd_ref, cw_ref, cb_ref, lng_ref, lnb_ref,
                   conv_ref, q_ref, k_ref, v_ref, gate_ref, buf_ref, halo_ref, *, tiles_per_seq):
    i = pl.program_id(0)
    tm = x_ref.shape[0]
    x = x_ref[...]
    ms = jnp.mean(x * x, axis=-1, keepdims=True)
    h = (x * lax.rsqrt(ms + EPS) * g_ref[...]).astype(BF16)

    def proj(lo, width):
        return jnp.dot(h, w_ref[:, lo:lo + width], preferred_element_type=F32)

    a = proj(0, CONV_CH)
    gt = proj(CONV_CH, CONV_CH)
    glu = a * _sigmoid(gt)

    rows = HALO + tm
    buf_ref[0, 0:HALO, :] = jnp.where(i % tiles_per_seq == 0, 0.0, halo_ref[...])
    buf_ref[0, HALO:rows, :] = glu
    halo_ref[...] = glu[tm - HALO:tm, :]
    first = HALO - (CONV_WIDTH - 1)

    def zero_after(v):
        u = pltpu.bitcast(v, jnp.uint32)
        return pltpu.bitcast((u >> 16) >> 16, F32)

    def conv_chunk(c, gate):
        r0 = c * CONV_STEP
        acc = jnp.broadcast_to(cb_ref[...], (CONV_STEP, CONV_CH))
        if gate is not None:
            acc = acc + jnp.tile(gate, (CONV_STEP // SUBLANES, CONV_CH // LANES))
        for j in range(CONV_WIDTH):
            shift = (first + j) % SUBLANES
            lo = r0 + first + j - shift
            acc = acc + cw_ref[j:j + 1, :] * buf_ref[shift, lo:lo + CONV_STEP, :]
        mu = jnp.mean(acc, axis=-1, keepdims=True)
        d = acc - mu
        var = jnp.mean(d * d, axis=-1, keepdims=True)
        y = d * lax.rsqrt(var + EPS) * lng_ref[...] + lnb_ref[...]
        act = y * _sigmoid(y)
        conv_ref[r0:r0 + CONV_STEP, :] = act.astype(BF16)
        return zero_after(act[0:SUBLANES, 0:LANES])

    def gated_lhs(gate):
        if gate is None:
            return h
        z = jnp.tile(gate, (tm // SUBLANES, 1)).astype(BF16)
        return jnp.concatenate([h[:, 0:LANES] + z, h[:, LANES:]], axis=1)

    def q_cols(gate):
        q = jnp.dot(gated_lhs(gate), w_ref[:, OFF_Q:OFF_Q + COL_Q], preferred_element_type=F32)
        qn = q * lax.rsqrt(_group_mean_sq(q, bd_ref) + EPS) * qg_ref[...]
        q_ref[...] = (qn * (DA_HEAD_DIM ** -0.5)).astype(BF16)
        return zero_after(qn[0:SUBLANES, 0:LANES])

    def k_cols(gate):
        k = jnp.dot(gated_lhs(gate), w_ref[:, OFF_K:OFF_K + COL_K], preferred_element_type=F32)
        kn = k * lax.rsqrt(_group_mean_sq(k, bd_ref) + EPS) * kg_ref[...]
        k_ref[...] = kn.astype(BF16)
        return zero_after(kn[0:SUBLANES, 0:LANES])

    def v_cols(gate):
        v = jnp.dot(gated_lhs(gate), w_ref[:, OFF_V:OFF_V + COL_V], preferred_element_type=F32)
        v_ref[...] = v.astype(BF16)
        return zero_after(v[tm - SUBLANES:tm, COL_V - LANES:COL_V])

    def gate_cols(c, gate):
        g = _sigmoid(jnp.dot(gated_lhs(gate), w_ref[:, OFF_GATE + c * 512:OFF_GATE + (c + 1) * 512],
                             preferred_element_type=F32))
        gate_ref[:, c * 512:(c + 1) * 512] = g.astype(BF16)
        return zero_after(g[tm - SUBLANES:tm, 512 - LANES:512])

    q_cols(None)
    k_cols(None)
    for r in range(1, SUBLANES):
        buf_ref[r, 0:rows - SUBLANES, :] = buf_ref[0, r:r + rows - SUBLANES, :]
    col_tasks = [v_cols] + [functools.partial(gate_cols, c) for c in range(COL_GATE // 512)]
    conv_gates, cols_gates = [], []
    for c in range(max(tm // CONV_STEP, len(col_tasks))):
        if c < tm // CONV_STEP:
            conv_gates.append(conv_chunk(c, cols_gates[c - 2] if 2 <= c < len(cols_gates) + 2 else None))
        if c < len(col_tasks):
            cols_gates.append(col_tasks[c](conv_gates[c - 1] if 1 <= c <= len(conv_gates) else None))


def _in_projection(x2, g, w_bf, qg, kg, bd, dw_w, dw_b, ln_g, ln_b, seq_len):
    T = x2.shape[0]
    tm = TM_PROJ
    row = lambda i: (i, 0)
    out_shape = (
        jax.ShapeDtypeStruct((T, CONV_CH), BF16),
        jax.ShapeDtypeStruct((T, COL_Q), BF16),
        jax.ShapeDtypeStruct((T, COL_K), BF16),
        jax.ShapeDtypeStruct((T, COL_V), BF16),
        jax.ShapeDtypeStruct((T, COL_GATE), BF16),
    )
    return pl.pallas_call(
        functools.partial(_inproj_kernel, tiles_per_seq=seq_len // tm),
        grid=(T // tm,),
        in_specs=[
            pl.BlockSpec((tm, D_MODEL), row),
            _const_spec((1, D_MODEL)),
            _const_spec((D_MODEL, IN_COLS)),
            _const_spec((1, COL_Q)),
            _const_spec((1, COL_K)),
            _const_spec((NORM_BLOCK, NORM_BLOCK)),
            _const_spec((CONV_WIDTH, CONV_CH)),
            _const_spec((1, CONV_CH)),
            _const_spec((1, CONV_CH)),
            _const_spec((1, CONV_CH)),
        ],
        out_specs=(
            pl.BlockSpec((tm, CONV_CH), row),
            pl.BlockSpec((tm, COL_Q), row),
            pl.BlockSpec((tm, COL_K), row),
            pl.BlockSpec((tm, COL_V), row),
            pl.BlockSpec((tm, COL_GATE), row),
        ),
        out_shape=out_shape,
        scratch_shapes=[pltpu.VMEM((SUBLANES, HALO + tm, CONV_CH), F32), pltpu.VMEM((HALO, CONV_CH), F32)],
        compiler_params=pltpu.CompilerParams(
            dimension_semantics=("arbitrary",), vmem_limit_bytes=VMEM_LIMIT),
        name="in_projection",
    )(x2, g, w_bf, qg, kg, bd, dw_w, dw_b, ln_g, ln_b)


def _attn_kernel(q_ref, k_ref, v_ref, lam_ref, sg_ref, o_ref, qs_ref, m_ref, acc_ref, s_ref, *, tq, tk):
    qi = pl.program_id(2)
    lane = lax.broadcasted_iota(jnp.int32, (tq, DA_V_DIM), 1)
    for hh in range(ATT_HEADS):
        q = q_ref[0, :, hh * DA_V_DIM:(hh + 1) * DA_V_DIM]
        zero = jnp.zeros_like(q)
        qs_ref[hh, 0:tq, :] = jnp.where(lane < DA_HEAD_DIM, q, zero)
        qs_ref[hh, tq:2 * tq, :] = jnp.where(lane >= DA_HEAD_DIM, q, zero)
    ones = jnp.ones((tk, LANES), BF16)
    rc = ATT_RC
    chunks = [slice(r0, r0 + rc) for r0 in range(0, 2 * tq, rc)]
    n_diag = tq // tk

    def scores(hh, j, rows):
        kj = k_ref[0, pl.ds(pl.multiple_of(j * tk, tk), tk), hh * DA_V_DIM:(hh + 1) * DA_V_DIM]
        return lax.dot_general(qs_ref[hh, rows, :], kj, (((1,), (1,)), ((), ())),
                               preferred_element_type=F32)

    def step(j, diag, prefetch, first, heads=tuple(range(ATT_HEADS))):
        assert not first or diag in (None, 0)
        for hh in heads:
            vj = jnp.concatenate(
                [v_ref[0, pl.ds(pl.multiple_of(j * tk, tk), tk), hh * DA_V_DIM:(hh + 1) * DA_V_DIM], ones], axis=1)
            for rows in chunks:
                r_lo = rows.start % tq
                if diag is None:
                    kw, masked = tk, False
                else:
                    k_lo = diag * tk
                    if k_lo > r_lo + rc - 1:
                        if prefetch:
                            s_ref[hh, rows, :] = scores(hh, j + 1, rows)
                        continue
                    kw = min(tk, -(-(r_lo + rc - k_lo) // (2 * LANES)) * (2 * LANES))
                    masked = k_lo + kw - 1 > r_lo
                s = s_ref[hh, rows, 0:kw]
                if prefetch:
                    s_ref[hh, rows, :] = scores(hh, j + 1, rows)
                if masked:
                    r = lax.broadcasted_iota(jnp.int32, s.shape, 0) + r_lo
                    c = lax.broadcasted_iota(jnp.int32, s.shape, 1) + k_lo
                    s = jnp.where(c <= r, s, -jnp.inf)
                m_cur = jnp.max(s, axis=-1, keepdims=True)
                if first:
                    m_new = jnp.broadcast_to(m_cur, (rc, LANES))
                else:
                    m_old = m_ref[hh, rows, :]
                    m_new = jnp.maximum(m_old, m_cur)
                p = jnp.exp(s - jnp.tile(m_new, (1, kw // LANES)))
                pv = jnp.dot(p.astype(BF16), vj[0:kw, :], preferred_element_type=F32)
                if first:
                    acc_ref[hh, rows, :] = pv
                else:
                    alpha = jnp.exp(m_old - m_new)
                    acc_ref[hh, rows, :] = jnp.tile(alpha, (1, 2)) * acc_ref[hh, rows, :] + pv
                m_ref[hh, rows, :] = m_new

    for hh in range(ATT_HEADS):
        for rows in chunks:
            s_ref[hh, rows, :] = scores(hh, 0, rows)

    def body(j, carry):
        step(j, None, True, False)
        return carry

    n_full = qi * n_diag

    lam = (jnp.exp(jnp.sum(lam_ref[0:1, :] * lam_ref[1:2, :], axis=-1, keepdims=True))
           - jnp.exp(jnp.sum(lam_ref[2:3, :] * lam_ref[3:4, :], axis=-1, keepdims=True))
           + LAM_INIT)

    def finish(hh):
        o1 = acc_ref[hh, 0:tq, 0:LANES] / acc_ref[hh, 0:tq, LANES:2 * LANES]
        o2 = acc_ref[hh, tq:2 * tq, 0:LANES] / acc_ref[hh, tq:2 * tq, LANES:2 * LANES]
        o = o1 - lam * o2
        ms = jnp.mean(o * o, axis=-1, keepdims=True)
        on = o * lax.rsqrt(ms + EPS) * sg_ref[...]
        o_ref[0, :, hh * DA_V_DIM:(hh + 1) * DA_V_DIM] = (on * (1.0 - LAM_INIT)).astype(BF16)

    def diagonal_tiles(first):
        for hh in range(ATT_HEADS):
            for t in range(n_diag):
                step(n_full + t, t, t + 1 < n_diag, first and t == 0, heads=(hh,))
            finish(hh)

    @pl.when(qi == 0)
    def _():
        diagonal_tiles(True)

    @pl.when(qi > 0)
    def _():
        step(0, None, True, True)
        lax.fori_loop(1, n_full, body, 0)
        diagonal_tiles(False)


def _diff_attention(q3, k3, v3, lam_vecs, subln_g):
    B, S, _ = q3.shape
    tq, tk = ATT_TQ, ATT_TK
    w = ATT_HEADS * DA_V_DIM
    return pl.pallas_call(
        functools.partial(_attn_kernel, tq=tq, tk=tk),
        grid=(B, DA_HEADS // ATT_HEADS, S // tq),
        in_specs=[
            pl.BlockSpec((1, tq, w), lambda b, h, i: (b, i, h)),
            pl.BlockSpec((1, S, w), lambda b, h, i: (b, 0, h)),
            pl.BlockSpec((1, S, w), lambda b, h, i: (b, 0, h)),
            _const_spec((4, DA_HEAD_DIM)),
            _const_spec((1, DA_V_DIM)),
        ],
        out_specs=pl.BlockSpec((1, tq, w), lambda b, h, i: (b, i, h)),
        out_shape=jax.ShapeDtypeStruct((B, S, ATTN_W), BF16),
        scratch_shapes=[
            pltpu.VMEM((ATT_HEADS, 2 * tq, DA_V_DIM), BF16),
            pltpu.VMEM((ATT_HEADS, 2 * tq, LANES), F32),
            pltpu.VMEM((ATT_HEADS, 2 * tq, 2 * LANES), F32),
            pltpu.VMEM((ATT_HEADS, 2 * tq, tk), F32),
        ],
        compiler_params=pltpu.CompilerParams(
            dimension_semantics=("parallel", "parallel", "parallel"),
            vmem_limit_bytes=VMEM_LIMIT),
        name="diff_attention",
    )(q3, k3, v3, lam_vecs, subln_g)


def _merge_kernel(conv_ref, attn_ref, gate_ref, x_ref, wc_ref, wa_ref, wo_ref, fg_ref,
                  wr_ref, br_ref, x1_ref, h2_ref, logit_ref):
    c = jnp.dot(conv_ref[...], wc_ref[...], preferred_element_type=F32)
    a = jnp.dot(attn_ref[...], wa_ref[...], preferred_element_type=F32)
    g0 = gate_ref[:, 0:D_MODEL].astype(F32)
    g1 = gate_ref[:, D_MODEL:2 * D_MODEL].astype(F32)
    merged = (g0 * c + g1 * a).astype(BF16)
    x1 = x_ref[...] + jnp.dot(merged, wo_ref[...], preferred_element_type=F32)
    x1_ref[...] = x1
    ms = jnp.mean(x1 * x1, axis=-1, keepdims=True)
    h2 = x1 * lax.rsqrt(ms + EPS) * fg_ref[...]
    _store_rows(h2_ref, (), h2)
    h2_hi = h2.astype(BF16)
    h2_lo = (h2 - h2_hi.astype(F32)).astype(BF16)
    half = h2.shape[0] // 2
    for rows in (slice(0, half), slice(half, 2 * half)):
        r = (jnp.dot(h2_hi[rows], wr_ref[...], preferred_element_type=F32)
             + jnp.dot(h2_lo[rows], wr_ref[...], preferred_element_type=F32))
        logit_ref[rows, :] = r[:, 0:ROUTER_LANES] + r[:, ROUTER_LANES:2 * ROUTER_LANES] + br_ref[...]


def _merge(conv_act, attn_o, gates, x2, wc, wa, wo, fg, wr, br):
    T = x2.shape[0]
    tm = TM_MERGE
    row = lambda i: (i, 0)
    return pl.pallas_call(
        _merge_kernel,
        grid=(T // tm,),
        in_specs=[
            pl.BlockSpec((tm, CONV_CH), row),
            pl.BlockSpec((tm, ATTN_W), row),
            pl.BlockSpec((tm, COL_GATE), row),
            pl.BlockSpec((tm, D_MODEL), row),
            _const_spec((CONV_CH, D_MODEL)),
            _const_spec((ATTN_W, D_MODEL)),
            _const_spec((D_MODEL, D_MODEL)),
            _const_spec((1, D_MODEL)),
            _const_spec((D_MODEL, 2 * ROUTER_LANES)),
            _const_spec((1, ROUTER_LANES)),
        ],
        out_specs=(
            pl.BlockSpec((tm, D_MODEL), row),
            pl.BlockSpec((tm * ROW_TILE, LANES), row),
            pl.BlockSpec((tm, ROUTER_LANES), row),
        ),
        out_shape=(
            jax.ShapeDtypeStruct((T, D_MODEL), F32),
            jax.ShapeDtypeStruct((T * ROW_TILE, LANES), jnp.uint32),
            jax.ShapeDtypeStruct((T, ROUTER_LANES), F32),
        ),
        compiler_params=pltpu.CompilerParams(
            dimension_semantics=("parallel",), vmem_limit_bytes=VMEM_LIMIT),
        name="merge_out_router",
    )(conv_act, attn_o, gates, x2, wc, wa, wo, fg, wr, br)


def _route_kernel(logit_ref, tri_ref, low_ref, dest_ref, wts_ref, blk_ref,
                  cnt_ref, base_ref, *, n_blk_lanes):
    ph = pl.program_id(0)
    i = pl.program_id(1)
    tm = logit_ref.shape[0]
    lt = logit_ref[...].T
    row8 = lax.broadcasted_iota(jnp.int32, (SUBLANES, tm), 0)

    g = jnp.where(row8 < N_GROUPS, lt[0:SUBLANES, :], -jnp.inf)
    g_max = jnp.max(g, axis=0, keepdims=True)
    g_sum = jnp.sum(jnp.exp(g - g_max), axis=0, keepdims=True)
    g_sel = jnp.min(jnp.where(g == g_max, row8, SUBLANES), axis=0, keepdims=True)
    g_w = 1.0 / g_sum

    e_sel = jnp.zeros((EXPERTS_PER_GROUP, tm), F32)
    for gi in range(N_GROUPS):
        lo = ROUTER_EXPERT_COL + gi * EXPERTS_PER_GROUP
        e_sel = jnp.where(g_sel == gi, lt[lo:lo + EXPERTS_PER_GROUP, :], e_sel)
    e_max = jnp.max(e_sel, axis=0, keepdims=True)
    e_exp = jnp.exp(e_sel - e_max)
    prob = e_exp / jnp.sum(e_exp, axis=0, keepdims=True)
    p1 = jnp.max(prob, axis=0, keepdims=True)
    i1 = jnp.min(jnp.where(prob == p1, row8, SUBLANES), axis=0, keepdims=True)
    rest = jnp.where(row8 == i1, -1.0, prob)
    p2 = jnp.max(rest, axis=0, keepdims=True)
    i2 = jnp.min(jnp.where(rest == p2, row8, SUBLANES), axis=0, keepdims=True)
    denom = p1 + p2
    e0 = g_sel * EXPERTS_PER_GROUP + i1
    e1 = g_sel * EXPERTS_PER_GROUP + i2

    row32 = lax.broadcasted_iota(jnp.int32, (N_EXPERTS, tm), 0)
    oh0 = row32 == e0
    oh1 = row32 == e1
    ohs = jnp.where(oh0 | oh1, 1.0, 0.0)
    tile_cnt = jnp.sum(ohs, axis=1, keepdims=True)

    @pl.when((ph == 0) & (i == 0))
    def _():
        cnt_ref[...] = jnp.zeros(cnt_ref.shape, F32)

    @pl.when(ph == 0)
    def _():
        cnt_ref[...] += tile_cnt

    @pl.when((ph == 1) & (i == 0))
    def _():
        cnt = cnt_ref[...]
        padded = jnp.ceil(cnt * (1.0 / RB)) * RB
        pstart = jnp.dot(low_ref[...], padded, preferred_element_type=F32,
                         precision=lax.Precision.HIGHEST)
        base_ref[...] = pstart
        pad_end = pstart + padded
        row0 = lax.broadcasted_iota(jnp.int32, (N_EXPERTS, n_blk_lanes), 1).astype(F32) * RB
        expert = lax.broadcasted_iota(jnp.int32, (N_EXPERTS, n_blk_lanes), 0).astype(F32)
        blk = jnp.minimum(jnp.sum(jnp.where(pad_end[:, 0:1] <= row0, 1.0, 0.0), axis=0, keepdims=True),
                          N_EXPERTS - 1.0)
        valid_end = jnp.sum(jnp.where(expert == blk, (pstart + cnt)[:, 0:1], 0.0), axis=0, keepdims=True)
        n_valid = jnp.clip(valid_end - row0[0:1, :], 0.0, float(RB))
        n_used = jnp.max(pad_end[:, 0:1], axis=0, keepdims=True) * (1.0 / RB) + jnp.zeros_like(blk)
        blk_ref[...] = jnp.zeros(blk_ref.shape, jnp.int32)
        blk_ref[0:1, :] = blk.astype(jnp.int32)
        blk_ref[1:2, :] = n_valid.astype(jnp.int32)
        blk_ref[2:3, :] = n_used.astype(jnp.int32)

    @pl.when(ph == 1)
    def _():
        before = jnp.dot(ohs.astype(BF16), tri_ref[...], preferred_element_type=F32)
        pos = base_ref[:, 0:1] + before
        d0 = jnp.sum(jnp.where(oh0, pos, 0.0), axis=0, keepdims=True)
        d1 = jnp.sum(jnp.where(oh1, pos, 0.0), axis=0, keepdims=True)
        dest_ref[0:1, :] = d0.astype(jnp.int32)
        dest_ref[1:2, :] = d1.astype(jnp.int32)
        wts_ref[...] = jnp.zeros(wts_ref.shape, F32)
        wts_ref[0:1, :] = p1 / denom * g_w
        wts_ref[1:2, :] = p2 / denom * g_w
        base_ref[...] += tile_cnt


def _route(logits):
    T = logits.shape[0]
    tm = TM_ROUTE
    n_rows = 2 * T + N_EXPERTS * RB
    n_blk = n_rows // RB
    n_blk_lanes = -(-n_blk // LANES) * LANES
    r = jnp.arange(tm, dtype=jnp.int32)
    tri = (r[:, None] < r[None, :]).astype(BF16)
    e = jnp.arange(N_EXPERTS, dtype=jnp.int32)
    low = (e[None, :] < e[:, None]).astype(F32)
    return pl.pallas_call(
        functools.partial(_route_kernel, n_blk_lanes=n_blk_lanes),
        grid=(2, T // tm),
        in_specs=[
            pl.BlockSpec((tm, ROUTER_LANES), lambda ph, i: (i, 0)),
            _const_spec((tm, tm)),
            _const_spec((N_EXPERTS, N_EXPERTS)),
        ],
        out_specs=(
            pl.BlockSpec((2, tm), lambda ph, i: (0, i * ph)),
            pl.BlockSpec((SUBLANES, tm), lambda ph, i: (0, i * ph)),
            pl.BlockSpec((SUBLANES, n_blk_lanes), lambda ph, i: (0, 0)),
        ),
        out_shape=(
            jax.ShapeDtypeStruct((2, T), jnp.int32),
            jax.ShapeDtypeStruct((SUBLANES, T), F32),
            jax.ShapeDtypeStruct((SUBLANES, n_blk_lanes), jnp.int32),
        ),
        scratch_shapes=[pltpu.VMEM((N_EXPERTS, LANES), F32), pltpu.VMEM((N_EXPERTS, LANES), F32)],
        compiler_params=pltpu.CompilerParams(
            dimension_semantics=("arbitrary", "arbitrary"), vmem_limit_bytes=VMEM_LIMIT),
        name="route_plan",
    )(logits, tri, low)


def _sc_worker():
    return lax.axis_index("s") * SC_CORES + lax.axis_index("c")


def _sc_scatter_rows(h, dest, n_rows):
    T = h.shape[0]
    n_workers = SC_CORES * SC_SUBCORES
    per_w = T // n_workers
    n_ch = per_w // SC_ROWS
    assert per_w * n_workers == T and n_ch * SC_ROWS == per_w and n_ch % 2 == 0
    mesh = plsc.VectorSubcoreMesh(core_axis_name="c", subcore_axis_name="s")

    @functools.partial(
        pl.kernel, mesh=mesh,
        out_type=jax.ShapeDtypeStruct((n_rows,) + h.shape[1:], h.dtype),
        scratch_types=[
            pltpu.VMEM((2, n_ch, SC_ROWS), jnp.int32),
            pltpu.VMEM((2, SC_ROWS) + h.shape[1:], h.dtype),
            pltpu.SemaphoreType.DMA((2,)),
            pltpu.SemaphoreType.DMA((2,)),
        ],
    )
    def k(h_hbm, idx_hbm, out_hbm, idx_v, rows_v, lsem, ssem):
        wid = _sc_worker()
        base = wid * per_w
        pltpu.sync_copy(idx_hbm.at[0, wid], idx_v.at[0])
        pltpu.sync_copy(idx_hbm.at[1, wid], idx_v.at[1])

        def load(c, slot):
            return pltpu.make_async_copy(h_hbm.at[pl.ds(base + c * SC_ROWS, SC_ROWS)], rows_v.at[slot],
                                         lsem.at[slot])

        def scatter(slot_k, c, slot):
            return pltpu.make_async_copy(rows_v.at[slot], out_hbm.at[idx_v.at[slot_k, c]], ssem.at[slot])

        load(0, 0).start()

        def body(g, carry):
            for slot in range(2):
                c = 2 * g + slot
                load(c, slot).wait()

                @pl.when(c >= 1)
                def _():
                    scatter(0, c - 1, 1 - slot).wait()
                    scatter(1, c - 1, 1 - slot).wait()

                @pl.when(c + 1 < n_ch)
                def _():
                    load(c + 1, 1 - slot).start()

                scatter(0, c, slot).start()
                scatter(1, c, slot).start()
            return carry

        lax.fori_loop(0, n_ch // 2, body, 0)
        scatter(0, n_ch - 1, 1).wait()
        scatter(1, n_ch - 1, 1).wait()

    return k(h, dest.reshape(2, n_workers, n_ch, SC_ROWS))


def _sc_gather_rows(table, idx):
    B = idx.shape[0]
    n_workers = SC_CORES * SC_SUBCORES
    per_w = B // n_workers
    n_ch = per_w // SC_ROWS
    assert per_w * n_workers == B and n_ch * SC_ROWS == per_w and n_ch % 2 == 0
    mesh = plsc.VectorSubcoreMesh(core_axis_name="c", subcore_axis_name="s")

    @functools.partial(
        pl.kernel, mesh=mesh,
        out_type=jax.ShapeDtypeStruct((B,) + table.shape[1:], table.dtype),
        scratch_types=[
            pltpu.VMEM((n_ch, SC_ROWS), jnp.int32),
            pltpu.VMEM((2, SC_ROWS) + table.shape[1:], table.dtype),
            pltpu.SemaphoreType.DMA((2,)),
            pltpu.SemaphoreType.DMA((2,)),
        ],
    )
    def k(table_hbm, idx_hbm, out_hbm, idx_v, rows_v, gsem, wsem):
        wid = _sc_worker()
        base = wid * per_w
        pltpu.sync_copy(idx_hbm.at[wid], idx_v)

        def gather(c, slot):
            return pltpu.make_async_copy(table_hbm.at[idx_v.at[c]], rows_v.at[slot], gsem.at[slot])

        def writeback(c, slot):
            return pltpu.make_async_copy(rows_v.at[slot], out_hbm.at[pl.ds(base + c * SC_ROWS, SC_ROWS)],
                                         wsem.at[slot])

        gather(0, 0).start()

        def body(g, carry):
            for slot in range(2):
                c = 2 * g + slot
                gather(c, slot).wait()

                @pl.when(c >= 1)
                def _():
                    writeback(c - 1, 1 - slot).wait()

                @pl.when(c + 1 < n_ch)
                def _():
                    gather(c + 1, 1 - slot).start()

                writeback(c, slot).start()
            return carry

        lax.fori_loop(0, n_ch // 2, body, 0)
        writeback(n_ch - 1, 1).wait()

    return k(table, idx.reshape(n_workers, n_ch, SC_ROWS))


def _expert_kernel(blk_exp_ref, n_used_ref, n_valid_ref, xs_ref, wg_ref, wu_ref, wd_ref, y_ref,
                   wg_bf, wu_bf, wd_bf):
    b = pl.program_id(0)
    used = b < n_used_ref[0]
    new_expert = (b == 0) | (blk_exp_ref[b] != blk_exp_ref[jnp.maximum(b - 1, 0)])

    @pl.when(used & new_expert)
    def _():
        wg_bf[...] = wg_ref[0].astype(BF16)
        wu_bf[...] = wu_ref[0].astype(BF16)
        wd_bf[...] = wd_ref[0].astype(BF16)

    @pl.when(used)
    def _():
        x = _load_rows(xs_ref, (), RB)
        row = lax.broadcasted_iota(jnp.int32, (RB, 1), 0)
        xb = jnp.where(row < n_valid_ref[b], x, jnp.zeros_like(x))
        g = jnp.dot(xb, wg_bf[...], preferred_element_type=F32)
        u = jnp.dot(xb, wu_bf[...], preferred_element_type=F32)
        hid = (g * _sigmoid(g) * u).astype(BF16)
        _store_rows(y_ref, (), jnp.dot(hid, wd_bf[...], preferred_element_type=F32))

    @pl.when(jnp.logical_not(used))
    def _():
        y_ref[...] = jnp.zeros(y_ref.shape, y_ref.dtype)


def _experts(blk_exp, n_used, n_valid, xs, wg, wu, wd):
    n_blk = blk_exp.shape[0]
    grid_spec = pltpu.PrefetchScalarGridSpec(
        num_scalar_prefetch=3,
        grid=(n_blk,),
        in_specs=[
            pl.BlockSpec((RB * ROW_TILE, LANES), lambda b, be, nu, nv: (b, 0)),
            pl.BlockSpec((1, D_MODEL, D_EXPERT), lambda b, be, nu, nv: (be[b], 0, 0)),
            pl.BlockSpec((1, D_MODEL, D_EXPERT), lambda b, be, nu, nv: (be[b], 0, 0)),
            pl.BlockSpec((1, D_EXPERT, D_MODEL), lambda b, be, nu, nv: (be[b], 0, 0)),
        ],
        out_specs=pl.BlockSpec((RB * ROW_TILE, LANES), lambda b, be, nu, nv: (b, 0)),
        scratch_shapes=[
            pltpu.VMEM((D_MODEL, D_EXPERT), BF16),
            pltpu.VMEM((D_MODEL, D_EXPERT), BF16),
            pltpu.VMEM((D_EXPERT, D_MODEL), BF16),
        ],
    )
    return pl.pallas_call(
        _expert_kernel,
        grid_spec=grid_spec,
        out_shape=jax.ShapeDtypeStruct(xs.shape, xs.dtype),
        compiler_params=pltpu.CompilerParams(
            dimension_semantics=("arbitrary",), vmem_limit_bytes=VMEM_LIMIT),
        name="expert_mlp",
    )(blk_exp, n_used, n_valid, xs, wg, wu, wd)


def _combine_kernel(x1_ref, y0_ref, y1_ref, w_ref, o_ref):
    tm = x1_ref.shape[0]
    y0 = _load_rows(y0_ref, (), tm).astype(F32)
    y1 = _load_rows(y1_ref, (), tm).astype(F32)
    w = w_ref[...].T
    o_ref[...] = x1_ref[...] + (y0 * w[:, 0:1] + y1 * w[:, 1:2])


def _combine(x1, y, wts):
    T = x1.shape[0]
    tm = TM_COMB
    row = lambda i: (i, 0)
    return pl.pallas_call(
        _combine_kernel,
        grid=(T // tm,),
        in_specs=[
            pl.BlockSpec((tm, D_MODEL), row),
            pl.BlockSpec((tm * ROW_TILE, LANES), row),
            pl.BlockSpec((tm * ROW_TILE, LANES), lambda i: (i + T // tm, 0)),
            pl.BlockSpec((SUBLANES, tm), lambda i: (0, i)),
        ],
        out_specs=pl.BlockSpec((tm, D_MODEL), row),
        out_shape=jax.ShapeDtypeStruct((T, D_MODEL), F32),
        compiler_params=pltpu.CompilerParams(
            dimension_semantics=("parallel",), vmem_limit_bytes=VMEM_LIMIT),
        name="moe_combine",
    )(x1, y, y, wts)


def kernel(x, attn_norm_g, w_in, conv_dw_w, conv_dw_b, conv_ln_g, conv_ln_b, w_conv_out,
           q_norm_g, k_norm_g, lambda_q1, lambda_k1, lambda_q2, lambda_k2, subln_g,
           w_attn_out, w_out, ffn_norm_g, w_router_group, b_router_group,
           w_router_expert, b_router_expert, w_gate_e, w_up_e, w_down_e):
    B, S, D = x.shape
    T = B * S
    l = 0
    x2 = x.reshape(T, D)

    reps = COL_Q // DA_HEAD_DIM
    qg = jnp.tile(q_norm_g[l], reps).reshape(1, COL_Q)
    kg = jnp.tile(k_norm_g[l], reps).reshape(1, COL_K)
    grp = jnp.arange(NORM_BLOCK, dtype=jnp.int32) // DA_HEAD_DIM
    bd = jnp.where(grp[:, None] == grp[None, :], 1.0 / DA_HEAD_DIM, 0.0).astype(BF16)

    conv_act, q, k, v, gates = _in_projection(
        x2, attn_norm_g[l].reshape(1, D), w_in[l].astype(BF16), qg, kg, bd, conv_dw_w[l],
        conv_dw_b[l].reshape(1, CONV_CH), conv_ln_g[l].reshape(1, CONV_CH), conv_ln_b[l].reshape(1, CONV_CH), S)

    lam_vecs = jnp.stack([lambda_q1[l], lambda_k1[l], lambda_q2[l], lambda_k2[l]]).astype(F32)
    attn_o = _diff_attention(q.reshape(B, S, COL_Q), k.reshape(B, S, COL_K),
                             v.reshape(B, S, COL_V), lam_vecs, subln_g[l].reshape(1, DA_V_DIM))

    e_lo, e_hi = ROUTER_EXPERT_COL, ROUTER_EXPERT_COL + N_EXPERTS
    wr = jnp.zeros((D, ROUTER_LANES), F32)
    wr = wr.at[:, :N_GROUPS].set(w_router_group[l]).at[:, e_lo:e_hi].set(w_router_expert[l])
    br = jnp.zeros((1, ROUTER_LANES), F32)
    br = br.at[0, :N_GROUPS].set(b_router_group[l]).at[0, e_lo:e_hi].set(b_router_expert[l])

    wr_hi = wr.astype(BF16)
    wr_lo = (wr - wr_hi.astype(F32)).astype(BF16)
    x1, h2, logits = _merge(conv_act, attn_o.reshape(T, ATTN_W), gates, x2,
                            w_conv_out[l].astype(BF16), w_attn_out[l].astype(BF16),
                            w_out[l].astype(BF16), ffn_norm_g[l].reshape(1, D),
                            jnp.concatenate([wr_hi, wr_lo], axis=1), br)

    dest, wts, blk = _route(logits)
    n_rows = 2 * T + N_EXPERTS * RB
    n_blk = n_rows // RB
    blk_exp, n_valid, n_used = blk[0, :n_blk], blk[1, :n_blk], blk[2, 0:1]

    xs = _sc_scatter_rows(h2.reshape(T, ROW_TILE, LANES), dest, n_rows)
    yb = _experts(blk_exp, n_used, n_valid, xs.reshape(n_rows * ROW_TILE, LANES),
                  w_gate_e[l], w_up_e[l], w_down_e[l])
    y = _sc_gather_rows(yb.reshape(n_rows, ROW_TILE, LANES), dest.reshape(2 * T))

    out = _combine(x1, y.reshape(2 * T * ROW_TILE, LANES), wts)
    return out.reshape(B, S, D)
```
